```python
import math
import jax
import jax.numpy as jnp
from jax import lax
import numpy as np

D_MODEL = 1024
BATCH = 8
SEQ = 2048
DEPTH = 4

CTX_LEN = 256
GRID_W = 64
S5_WIDTH = 512
S5_GROUP = 16
S5_GROUPS = S5_WIDTH // S5_GROUP
S5_STATE = 64
DT_MIN = 1e-3
DT_MAX = 1e-1
MLA_HEADS = 8
MLA_NOPE = 64
MLA_ROPE = 32
MLA_V = 64
MLA_Q_RANK = 384
MLA_KV_RANK = 256
MLA_SCALE = (MLA_NOPE + MLA_ROPE) ** -0.5
WIN_Q_HEADS = 8
WIN_KV_HEADS = 2
WIN_GROUP = WIN_Q_HEADS // WIN_KV_HEADS
WIN_HEAD_DIM = 64
WINDOW = 128
BLOCK = 128
WIN_SCALE = WIN_HEAD_DIM ** -0.5
N_BRANCH = 3
BRANCH_WIDTH = 512
N_EXPERTS = 16
EXPERT_FF = 1024
CAPACITY_FACTOR = 2
ROPE_BASE = 10000.0
LN_EPS = 1e-6
NEG_INF = -1e30
ALPHA = (2 * DEPTH) ** 0.25
BETA = (8 * DEPTH) ** -0.25
IN_SIZES = (S5_WIDTH, MLA_Q_RANK, MLA_KV_RANK, MLA_ROPE,
            WIN_Q_HEADS * WIN_HEAD_DIM, WIN_KV_HEADS * WIN_HEAD_DIM, WIN_KV_HEADS * WIN_HEAD_DIM,
            N_BRANCH * D_MODEL)
D_IN = sum(IN_SIZES)
IN_POINTS = tuple(int(p) for p in np.cumsum(IN_SIZES)[:-1])

kernel_name = 'hybrid_s5_mla_swa_ecmoe_dit'


def layer_norm(x, gain=None, bias=None):
    xf = x.astype(jnp.float32)
    mu = jnp.mean(xf, axis=-1, keepdims=True)
    var = jnp.mean(jnp.square(xf - mu), axis=-1, keepdims=True)
    y = (xf - mu) * lax.rsqrt(var + LN_EPS)
    if gain is not None:
        y = y * gain.astype(jnp.float32) + bias.astype(jnp.float32)
    return y.astype(x.dtype)


def rms_norm(x, gain):
    xf = x.astype(jnp.float32)
    y = xf * lax.rsqrt(jnp.mean(xf * xf, axis=-1, keepdims=True) + LN_EPS) * gain.astype(jnp.float32)
    return y.astype(x.dtype)


def modulate(x, shift, scale):
    return layer_norm(x) * (1 + scale) + shift


def adaln_params(cond, lp):
    return jnp.split(jax.nn.silu(cond) @ lp['w_ada'] + lp['b_ada'], 6, axis=-1)


def axial_rope(x, row, col):
    d = x.shape[-1]
    nf = d // 4
    freqs = ROPE_BASE ** (-jnp.arange(nf, dtype=jnp.float32) / nf)
    ang = jnp.concatenate([row[:, None] * freqs, col[:, None] * freqs], axis=-1)[:, None, :]
    cos = jnp.cos(ang).astype(x.dtype)
    sin = jnp.sin(ang).astype(x.dtype)
    x1, x2 = x[..., : d // 2], x[..., d // 2:]
    return jnp.concatenate([x1 * cos - x2 * sin, x1 * sin + x2 * cos], axis=-1)


def dense_attention(q, k, v, scale, sink=None):
    s = jnp.einsum('bqkgd,bskd->bkgqs', q, k).astype(jnp.float32) * scale
    if sink is not None:
        sk = jnp.broadcast_to(sink.astype(jnp.float32)[None, :, :, None, None], s.shape[:-1] + (1,))
        s = jnp.concatenate([s, sk], axis=-1)
    p = jax.nn.softmax(s, axis=-1)[..., : k.shape[1]].astype(v.dtype)
    return jnp.einsum('bkgqs,bskd->bqkgd', p, v)


def cmul(ar, ai, br, bi):
    return ar * br - ai * bi, ar * bi + ai * br


def s5_discretise(lam_re, lam_im, log_dt, b_re, b_im):
    lam_re = lam_re.astype(jnp.float32)
    lam_im = lam_im.astype(jnp.float32)
    dt = jnp.exp(log_dt.astype(jnp.float32))[:, None]
    mag = jnp.exp(lam_re * dt)
    ar, ai = mag * jnp.cos(lam_im * dt), mag * jnp.sin(lam_im * dt)
    den = lam_re * lam_re + lam_im * lam_im
    qr = ((ar - 1) * lam_re + ai * lam_im) / den
    qi = (ai * lam_re - (ar - 1) * lam_im) / den
    bbr, bbi = cmul(qr[..., None], qi[..., None], b_re.astype(jnp.float32), b_im.astype(jnp.float32))
    return ar, ai, bbr, bbi


def _s5_combine(e1, e2):
    a1r, a1i, b1r, b1i = e1
    a2r, a2i, b2r, b2i = e2
    ar, ai = cmul(a2r, a2i, a1r, a1i)
    br, bi = cmul(a2r, a2i, b1r, b1i)
    return ar, ai, br + b2r, bi + b2i


def s5_scan(u, ar, ai, bbr, bbi, s0):
    bu_r = jnp.einsum('blgh,gph->blgp', u, bbr)
    bu_i = jnp.einsum('blgh,gph->blgp', u, bbi)
    if s0 is not None:
        sr, si = cmul(ar, ai, s0[0], s0[1])
        bu_r = bu_r.at[:, 0].add(sr)
        bu_i = bu_i.at[:, 0].add(si)
    L = u.shape[1]
    a_r = jnp.broadcast_to(ar, (1, L) + ar.shape)
    a_i = jnp.broadcast_to(ai, (1, L) + ai.shape)
    _, _, xr, xi = lax.associative_scan(_s5_combine, (a_r, a_i, bu_r, bu_i), axis=1)
    return xr, xi


def s5_readout(xr, xi, c_re, c_im):
    return (jnp.einsum('blgp,ghp->blgh', xr, c_re.astype(jnp.float32))
            - jnp.einsum('blgp,ghp->blgh', xi, c_im.astype(jnp.float32)))


def s5_branch(u_c, u_l, lp, update_ctx):
    B, Lc, _ = u_c.shape
    L = u_l.shape[1]
    uc = u_c.reshape(B, Lc, S5_GROUPS, S5_GROUP).astype(jnp.float32)
    ul = u_l.reshape(B, L, S5_GROUPS, S5_GROUP).astype(jnp.float32)
    d_skip = lp['s5_d'].astype(jnp.float32).reshape(S5_GROUPS, S5_GROUP)
    y_l = d_skip * ul
    y_c = d_skip * uc if update_ctx else None
    for d in range(2):
        ar, ai, bbr, bbi = s5_discretise(lp['s5_lam_re'][d], lp['s5_lam_im'][d], lp['s5_log_dt'][d],
                                         lp['s5_b_re'][d], lp['s5_b_im'][d])
        order = (lambda t: t[:, ::-1]) if d == 1 else (lambda t: t)
        xr_c, xi_c = s5_scan(order(uc), ar, ai, bbr, bbi, None)
        xr_l, xi_l = s5_scan(order(ul), ar, ai, bbr, bbi, (xr_c[:, -1], xi_c[:, -1]))
        y_l = y_l + order(s5_readout(xr_l, xi_l, lp['s5_c_re'][d], lp['s5_c_im'][d]))
        if update_ctx:
            y_c = y_c + order(s5_readout(xr_c, xi_c, lp['s5_c_re'][d], lp['s5_c_im'][d]))

    def glu(y, n):
        g = jax.nn.gelu(y.reshape(B, n, S5_WIDTH))
        return (g * jax.nn.sigmoid(g @ lp['s5_w_glu'].astype(jnp.float32) + lp['s5_b_glu'].astype(jnp.float32))).astype(u_l.dtype)

    out_l = glu(y_l, L)
    out_c = glu(y_c, Lc) if update_ctx else None
    return out_c, out_l


def mla_queries(qa, lp):
    B, N, _ = qa.shape
    q = (rms_norm(qa, lp['mla_q_norm']) @ lp['mla_w_uq']).reshape(B, N, MLA_HEADS, MLA_NOPE + MLA_ROPE)
    return q[..., :MLA_NOPE], q[..., MLA_NOPE:]


def mla_keys_values(kva, lp):
    B, N, _ = kva.shape
    kv = (rms_norm(kva, lp['mla_kv_norm']) @ lp['mla_w_ukv']).reshape(B, N, MLA_HEADS, MLA_NOPE + MLA_V)
    return kv[..., :MLA_NOPE], kv[..., MLA_NOPE:]


def mla_branch(qa_c, kva_c, kr_c, qa_l, kva_l, kr_l, lp, row, col, update_ctx):
    B, Lc, _ = kva_c.shape
    L = kva_l.shape[1]
    kn_c, v_c = mla_keys_values(kva_c, lp)
    k_ctx = jnp.concatenate([kn_c, jnp.broadcast_to(kr_c[:, :, None, :], (B, Lc, MLA_HEADS, MLA_ROPE))], axis=-1)
    kn_l, v_l = mla_keys_values(kva_l, lp)
    kr_rot = axial_rope(kr_l[:, :, None, :], row, col)
    k_lat = jnp.concatenate([kn_l, jnp.broadcast_to(kr_rot, (B, L, MLA_HEADS, MLA_ROPE))], axis=-1)
    qn_l, qr_l = mla_queries(qa_l, lp)
    q_rot = jnp.concatenate([qn_l, axial_rope(qr_l, row, col)], axis=-1)
    q_plain = jnp.concatenate([qn_l, qr_l], axis=-1)
    nb = L // BLOCK
    dk = MLA_NOPE + MLA_ROPE
    qr_b = q_rot.reshape(B, nb, BLOCK, MLA_HEADS, dk).swapaxes(0, 1)
    qp_b = q_plain.reshape(B, nb, BLOCK, MLA_HEADS, dk).swapaxes(0, 1)

    def block_attention(qs):
        qrb, qpb = qs
        s_lat = jnp.einsum('bqhd,bkhd->bhqk', qrb, k_lat).astype(jnp.float32)
        s_ctx = jnp.einsum('bqhd,bkhd->bhqk', qpb, k_ctx).astype(jnp.float32)
        p = jax.nn.softmax(jnp.concatenate([s_lat, s_ctx], axis=-1) * MLA_SCALE, axis=-1).astype(v_l.dtype)
        return (jnp.einsum('bhqk,bkhd->bqhd', p[..., :L], v_l)
                + jnp.einsum('bhqk,bkhd->bqhd', p[..., L:], v_c))

    o = lax.map(block_attention, (qr_b, qp_b))
    out_l = o.swapaxes(0, 1).reshape(B, L, MLA_HEADS * MLA_V)
    out_c = None
    if update_ctx:
        qn_c, qr_c = mla_queries(qa_c, lp)
        q_c = jnp.concatenate([qn_c, qr_c], axis=-1)[:, :, :, None, :]
        out_c = dense_attention(q_c, k_ctx, v_c, MLA_SCALE).reshape(B, Lc, MLA_HEADS * MLA_V)
    return out_c, out_l


def window_branch(qc, kc, vc, ql, kl, vl, sink, row, col, update_ctx):
    B, Lc, _ = kc.shape
    L = kl.shape[1]
    nb = L // BLOCK
    sink = sink.reshape(WIN_KV_HEADS, WIN_GROUP)
    kc = kc.reshape(B, Lc, WIN_KV_HEADS, WIN_HEAD_DIM)
    vc = vc.reshape(B, Lc, WIN_KV_HEADS, WIN_HEAD_DIM)
    q = ql.reshape(B, L, WIN_Q_HEADS, WIN_HEAD_DIM)
    qb = axial_rope(q, row, col).reshape(B, nb, BLOCK, WIN_KV_HEADS, WIN_GROUP, WIN_HEAD_DIM)
    qpb = q.reshape(B, nb, BLOCK, WIN_KV_HEADS, WIN_GROUP, WIN_HEAD_DIM)
    k_rot = axial_rope(kl.reshape(B, L, WIN_KV_HEADS, WIN_HEAD_DIM), row, col)
    v = vl.reshape(B, L, WIN_KV_HEADS, WIN_HEAD_DIM)

    def band(t):
        tp = jnp.pad(t, ((0, 0), (BLOCK, BLOCK), (0, 0), (0, 0))).reshape(B, nb + 2, BLOCK, WIN_KV_HEADS, WIN_HEAD_DIM)
        return jnp.concatenate([tp[:, :-2], tp[:, 1:-1], tp[:, 2:]], axis=2)

    kb, vb = band(k_rot), band(v)
    blk = jnp.arange(nb)[:, None, None]
    r = jnp.arange(BLOCK)[None, :, None]
    j = jnp.arange(3 * BLOCK)[None, None, :]
    key_pos = blk * BLOCK - BLOCK + j
    valid = (jnp.abs(j - BLOCK - r) <= WINDOW) & (key_pos >= 0) & (key_pos < L)
    s_band = jnp.einsum('bnqkgd,bnskd->bnkgqs', qb, kb).astype(jnp.float32) * WIN_SCALE
    s_band = jnp.where(valid[None, :, None, None], s_band, NEG_INF)
    s_ctx = jnp.einsum('bnqkgd,bskd->bnkgqs', qpb, kc).astype(jnp.float32) * WIN_SCALE
    sk = jnp.broadcast_to(sink.astype(jnp.float32)[None, None, :, :, None, None], s_ctx.shape[:-1] + (1,))
    p = jax.nn.softmax(jnp.concatenate([s_band, s_ctx, sk], axis=-1), axis=-1)
    nk = 3 * BLOCK
    o = (jnp.einsum('bnkgqs,bnskd->bnqkgd', p[..., :nk].astype(v.dtype), vb)
         + jnp.einsum('bnkgqs,bskd->bnqkgd', p[..., nk:nk + Lc].astype(v.dtype), vc))
    out_l = o.reshape(B, L, WIN_Q_HEADS * WIN_HEAD_DIM)
    out_c = None
    if update_ctx:
        q_c = qc.reshape(B, Lc, WIN_KV_HEADS, WIN_GROUP, WIN_HEAD_DIM)
        out_c = dense_attention(q_c, kc, vc, WIN_SCALE, sink).reshape(B, Lc, WIN_Q_HEADS * WIN_HEAD_DIM)
    return out_c, out_l


def merge_branches(outs, gate_logits, lp):
    B, N, _ = gate_logits.shape
    o = jnp.stack(outs, axis=2)
    proj = jnp.einsum('bnkw,kwd->bnkd', o, lp['w_branch'])
    g = jax.nn.sigmoid(gate_logits.reshape(B, N, N_BRANCH, D_MODEL))
    return jnp.sum(g * proj, axis=2) @ lp['w_out']


def token_mixer(hc, hl, lp, row, col, update_ctx):
    zc = jnp.split(hc @ lp['w_in'], IN_POINTS, axis=-1)
    zl = jnp.split(hl @ lp['w_in'], IN_POINTS, axis=-1)
    s5_c, s5_l = s5_branch(zc[0], zl[0], lp, update_ctx)
    mla_c, mla_l = mla_branch(zc[1], zc[2], zc[3], zl[1], zl[2], zl[3], lp, row, col, update_ctx)
    win_c, win_l = window_branch(zc[4], zc[5], zc[6], zl[4], zl[5], zl[6], lp['win_sink'], row, col, update_ctx)
    yl = merge_branches((s5_l, mla_l, win_l), zl[7], lp)
    yc = merge_branches((s5_c, mla_c, win_c), zc[7], lp) if update_ctx else None
    return yc, yl


def expert_choice_ffn(h, lp):
    B, N, D = h.shape
    cap = CAPACITY_FACTOR * N // N_EXPERTS
    aff = jax.nn.softmax((h @ lp['w_router']).astype(jnp.float32), axis=-1)
    g, idx = lax.top_k(aff.swapaxes(1, 2), cap)
    xs = jax.vmap(lambda hb, ib: hb[ib])(h, idx)
    a = jnp.einsum('becd,edf->becf', xs, lp['w_gate'])
    u = jnp.einsum('becd,edf->becf', xs, lp['w_up'])
    y = jnp.einsum('becf,efd->becd', jax.nn.silu(a) * u, lp['w_down']) * g[..., None].astype(h.dtype)
    return jax.vmap(lambda ib, yb: jnp.zeros((N, D), yb.dtype).at[ib.reshape(-1)].add(yb.reshape(-1, D)))(idx, y)


def trunk_layer(xc, xl, c, c_ctx, lp, row, col, update_ctx):
    mod_l = [m[:, None, :] for m in adaln_params(c, lp)]
    mod_c = adaln_params(c_ctx, lp)
    hc = modulate(xc, mod_c[0], mod_c[1])
    hl = modulate(xl, mod_l[0], mod_l[1])
    yc, yl = token_mixer(hc, hl, lp, row, col, update_ctx)
    xl = layer_norm(ALPHA * xl + mod_l[2] * yl, lp['ln1_g'], lp['ln1_b'])
    fl = expert_choice_ffn(modulate(xl, mod_l[3], mod_l[4]), lp)
    xl = layer_norm(ALPHA * xl + mod_l[5] * fl, lp['ln2_g'], lp['ln2_b'])
    if update_ctx:
        xc = layer_norm(ALPHA * xc + mod_c[2] * yc, lp['ln1_g'], lp['ln1_b'])
        fc = expert_choice_ffn(modulate(xc, mod_c[3], mod_c[4]), lp)
        xc = layer_norm(ALPHA * xc + mod_c[5] * fc, lp['ln2_g'], lp['ln2_b'])
    return xc, xl


def setup_inputs(seed: int = 0) -> dict:
    key = jax.random.key(seed)
    k = jax.random.split(key, 32)
    f = jnp.float32
    G, P, HG = S5_GROUPS, S5_STATE, S5_GROUP

    def nrm(i, shape, scale):
        return scale * jax.random.normal(k[i], shape, f)

    lam_im = jnp.broadcast_to(jnp.pi * jnp.arange(P, dtype=f), (DEPTH, 2, G, P))
    return {
        'x': nrm(0, (BATCH, SEQ, D_MODEL), 1.0),
        'c': nrm(1, (BATCH, D_MODEL), 1.0),
        'ctx': nrm(2, (BATCH, CTX_LEN, D_MODEL), 1.0),
        'c_ctx': nrm(3, (D_MODEL,), 1.0),
        'w_ada': nrm(4, (DEPTH, D_MODEL, 6 * D_MODEL), D_MODEL ** -0.5),
        'b_ada': nrm(5, (DEPTH, 6 * D_MODEL), 0.01),
        'w_in': nrm(6, (DEPTH, D_MODEL, D_IN), D_MODEL ** -0.5),
        's5_lam_re': -0.5 + nrm(7, (DEPTH, 2, G, P), 0.01),
        's5_lam_im': lam_im,
        's5_log_dt': jax.random.uniform(k[8], (DEPTH, 2, G), f, math.log(DT_MIN), math.log(DT_MAX)),
        's5_b_re': nrm(9, (DEPTH, 2, G, P, HG), (2 * HG) ** -0.5),
        's5_b_im': nrm(10, (DEPTH, 2, G, P, HG), (2 * HG) ** -0.5),
        's5_c_re': nrm(11, (DEPTH, 2, G, HG, P), (2 * P) ** -0.5),
        's5_c_im': nrm(12, (DEPTH, 2, G, HG, P), (2 * P) ** -0.5),
        's5_d': nrm(13, (DEPTH, S5_WIDTH), 1.0),
        's5_w_glu': nrm(14, (DEPTH, S5_WIDTH, S5_WIDTH), S5_WIDTH ** -0.5),
        's5_b_glu': nrm(15, (DEPTH, S5_WIDTH), 0.01),
        'mla_q_norm': 1.0 + nrm(16, (DEPTH, MLA_Q_RANK), 0.02),
        'mla_w_uq': nrm(17, (DEPTH, MLA_Q_RANK, MLA_HEADS * (MLA_NOPE + MLA_ROPE)), MLA_Q_RANK ** -0.5),
        'mla_kv_norm': 1.0 + nrm(18, (DEPTH, MLA_KV_RANK), 0.02),
        'mla_w_ukv': nrm(19, (DEPTH, MLA_KV_RANK, MLA_HEADS * (MLA_NOPE + MLA_V)), MLA_KV_RANK ** -0.5),
        'win_sink': nrm(20, (DEPTH, WIN_Q_HEADS), 0.5),
        'w_branch': nrm(21, (DEPTH, N_BRANCH, BRANCH_WIDTH, D_MODEL), BRANCH_WIDTH ** -0.5),
        'w_out': nrm(22, (DEPTH, D_MODEL, D_MODEL), BETA * D_MODEL ** -0.5),
        'ln1_g': 1.0 + nrm(23, (DEPTH, D_MODEL), 0.02),
        'ln1_b': nrm(24, (DEPTH, D_MODEL), 0.01),
        'ln2_g': 1.0 + nrm(25, (DEPTH, D_MODEL), 0.02),
        'ln2_b': nrm(26, (DEPTH, D_MODEL), 0.01),
        'w_router': nrm(27, (DEPTH, D_MODEL, N_EXPERTS), D_MODEL ** -0.5),
        'w_gate': nrm(28, (DEPTH, N_EXPERTS, D_MODEL, EXPERT_FF), D_MODEL ** -0.5),
        'w_up': nrm(29, (DEPTH, N_EXPERTS, D_MODEL, EXPERT_FF), D_MODEL ** -0.5),
        'w_down': nrm(30, (DEPTH, N_EXPERTS, EXPERT_FF, D_MODEL), BETA * EXPERT_FF ** -0.5),
    }


def reference(x, c, ctx, c_ctx, w_ada, b_ada, w_in, s5_lam_re, s5_lam_im, s5_log_dt, s5_b_re, s5_b_im,
              s5_c_re, s5_c_im, s5_d, s5_w_glu, s5_b_glu, mla_q_norm, mla_w_uq, mla_kv_norm, mla_w_ukv,
              win_sink, w_branch, w_out, ln1_g, ln1_b, ln2_g, ln2_b, w_router, w_gate, w_up, w_down):
    L = x.shape[1]
    rows = L // GRID_W
    row = jnp.repeat(jnp.arange(rows, dtype=jnp.float32), GRID_W)
    col = jnp.tile(jnp.arange(GRID_W, dtype=jnp.float32), rows)
    xc, xl = ctx, x
    for i in range(DEPTH):
        lp = {
            'w_ada': w_ada[i], 'b_ada': b_ada[i], 'w_in': w_in[i],
            's5_lam_re': s5_lam_re[i], 's5_lam_im': s5_lam_im[i], 's5_log_dt': s5_log_dt[i],
            's5_b_re': s5_b_re[i], 's5_b_im': s5_b_im[i], 's5_c_re': s5_c_re[i], 's5_c_im': s5_c_im[i],
            's5_d': s5_d[i], 's5_w_glu': s5_w_glu[i], 's5_b_glu': s5_b_glu[i],
            'mla_q_norm': mla_q_norm[i], 'mla_w_uq': mla_w_uq[i], 'mla_kv_norm': mla_kv_norm[i],
            'mla_w_ukv': mla_w_ukv[i], 'win_sink': win_sink[i], 'w_branch': w_branch[i], 'w_out': w_out[i],
            'ln1_g': ln1_g[i], 'ln1_b': ln1_b[i], 'ln2_g': ln2_g[i], 'ln2_b': ln2_b[i],
            'w_router': w_router[i], 'w_gate': w_gate[i], 'w_up': w_up[i], 'w_down': w_down[i],
        }
        xc, xl = trunk_layer(xc, xl, c, c_ctx, lp, row, col, update_ctx=(i < DEPTH - 1))
    return xl
```

```python
import functools
import math

import jax
import jax.numpy as jnp
import numpy as np
from jax import lax
from jax.experimental import pallas as pl
from jax.experimental.pallas import tpu as pltpu

F32 = jnp.float32
BF16 = jnp.bfloat16
HIGHEST = lax.Precision.HIGHEST

GRID_W = 64
S5_WIDTH = 512
S5_GROUP = 16
S5_GROUPS = S5_WIDTH // S5_GROUP
S5_STATE = 64
S5_CHUNK = 16
S5_GROUPS_PER_STEP = 4
MLA_HEADS = 8
MLA_NOPE = 64
MLA_ROPE = 32
MLA_V = 64
MLA_Q_RANK = 384
MLA_KV_RANK = 256
MLA_PAD = 128
MLA_SCALE = (MLA_NOPE + MLA_ROPE) ** -0.5
WIN_Q_HEADS = 8
WIN_KV_HEADS = 2
WIN_GROUP = WIN_Q_HEADS // WIN_KV_HEADS
WIN_HEAD_DIM = 64
WINDOW = 128
BLOCK = 128
WIN_SCALE = WIN_HEAD_DIM ** -0.5
N_BRANCH = 3
BRANCH_WIDTH = 512
N_EXPERTS = 16
CAPACITY_FACTOR = 2
ROPE_BASE = 10000.0
LN_EPS = 1e-6
NEG_INF = -1e30
TOKEN_TILE = 256
VMEM_LIMIT = 56 * 1024 * 1024


def _cparams(n_axes):
    return pltpu.CompilerParams(dimension_semantics=("arbitrary",) * n_axes, vmem_limit_bytes=VMEM_LIMIT)


def _bdot(a, b):
    return jnp.dot(a, b, preferred_element_type=F32)


def _dot_nt(a, b):
    return lax.dot_general(a, b, (((1,), (1,)), ((), ())), preferred_element_type=F32)


def _layer_norm(x):
    mu = jnp.mean(x, axis=-1, keepdims=True)
    xc = x - mu
    var = jnp.mean(xc * xc, axis=-1, keepdims=True)
    return xc * lax.rsqrt(var + LN_EPS)


def _ada_kernel(cond_ref, w_ref, b_ref, o_ref):
    s = cond_ref[...]
    s = s * jax.nn.sigmoid(s)
    o_ref[0] = jnp.dot(s, w_ref[0], precision=HIGHEST, preferred_element_type=F32) + b_ref[0]


def _ada_call(cond, w_ada, b_ada):
    depth, d, d6 = w_ada.shape
    tn = 1536
    rows = cond.shape[0]
    return pl.pallas_call(
        _ada_kernel,
        grid=(depth, d6 // tn),
        in_specs=[
            pl.BlockSpec((rows, d), lambda i, j: (0, 0)),
            pl.BlockSpec((1, d, tn), lambda i, j: (i, 0, j)),
            pl.BlockSpec((1, 1, tn), lambda i, j: (i, 0, j)),
        ],
        out_specs=pl.BlockSpec((1, rows, tn), lambda i, j: (i, 0, j)),
        out_shape=jax.ShapeDtypeStruct((depth, rows, d6), F32),
        compiler_params=_cparams(2),
    )(cond, w_ada, b_ada.reshape(depth, 1, d6))


IN_WIDTHS = (S5_WIDTH, MLA_Q_RANK, MLA_KV_RANK, MLA_PAD, WIN_Q_HEADS * WIN_HEAD_DIM,
             WIN_KV_HEADS * WIN_HEAD_DIM, WIN_KV_HEADS * WIN_HEAD_DIM)
IN_OFFSETS = tuple(int(v) for v in np.cumsum((0,) + IN_WIDTHS))


def _in_kernel(x_ref, mod_ref, w_ref, *out_refs):
    xn = _layer_norm(x_ref[0])
    shift = mod_ref[0, 0, 0:1, :]
    scale = mod_ref[0, 0, 1:2, :]
    h = (xn * (1.0 + scale) + shift).astype(BF16)
    widths = IN_WIDTHS + (out_refs[-1].shape[-1],)
    for off, width, o_ref in zip(IN_OFFSETS, widths, out_refs):
        o_ref[0] = _bdot(h, w_ref[:, off:off + width]).astype(o_ref.dtype)


def _in_call(xall, mod, w_cat, n_ctx_tiles):
    b, n, d = xall.shape
    tm = TOKEN_TILE
    gate_w = w_cat.shape[1] - IN_OFFSETS[-1]
    widths = IN_WIDTHS + (gate_w,)
    return pl.pallas_call(
        _in_kernel,
        grid=(b, n // tm),
        in_specs=[
            pl.BlockSpec((1, tm, d), lambda i, t: (i, t, 0)),
            pl.BlockSpec((1, 1, 6, d), lambda i, t: (i, jnp.where(t < n_ctx_tiles, 0, 1), 0, 0)),
            pl.BlockSpec(w_cat.shape, lambda i, t: (0, 0)),
        ],
        out_specs=[pl.BlockSpec((1, tm, w), lambda i, t: (i, t, 0)) for w in widths],
        out_shape=[jax.ShapeDtypeStruct((b, n, w), F32) for w in widths],
        compiler_params=_cparams(2),
    )(xall, mod, w_cat)


def _s5_kernel(u_ref, tz_ref, bc_ref, cc_ref, coef_ref, d_ref, y_ref, loc_ref, sp_ref, *, nc_ctx, nc):
    gb = u_ref.shape[0]
    for g in range(gb):
        loc_ref[g] = _bdot(u_ref[g].astype(BF16), bc_ref[g])

    def coef(g, r):
        return jnp.broadcast_to(coef_ref[g, r:r + 1, :], (8, 128))

    def step(i, carry):
        rf = pl.multiple_of(i * 8, 8)
        cb = jnp.where(i < nc_ctx, nc_ctx - 1 - i, nc + nc_ctx - 1 - i)
        rb = pl.multiple_of(cb * 8, 8)
        out = []
        for g in range(gb):
            v0f, v1f, v0b, v1b = carry[g]
            sp_ref[g, pl.ds(rf, 8), 0:128] = v0f
            sp_ref[g, pl.ds(rb, 8), 128:256] = v0b
            lf0 = loc_ref[g, pl.ds(rf, 8), 0:128]
            lf1 = loc_ref[g, pl.ds(rf, 8), 128:256]
            lb0 = loc_ref[g, pl.ds(rb, 8), 256:384]
            lb1 = loc_ref[g, pl.ds(rb, 8), 384:512]
            n0f = coef(g, 0) * v0f + coef(g, 1) * v1f + lf0
            n1f = coef(g, 0) * v1f + coef(g, 2) * v0f + lf1
            n0b = coef(g, 3) * v0b + coef(g, 4) * v1b + lb0
            n1b = coef(g, 3) * v1b + coef(g, 5) * v0b + lb1
            out.append((n0f, n1f, n0b, n1b))
        return tuple(out)

    z = jnp.zeros((8, 128), F32)
    lax.fori_loop(0, nc, step, tuple((z, z, z, z) for _ in range(gb)))
    for g in range(gb):
        u = u_ref[g]
        y_ref[g] = (_bdot(u.astype(BF16), tz_ref[g]) + _bdot(sp_ref[g].astype(BF16), cc_ref[g])
                    + d_ref[g] * u)


def _s5_call(ug, tz, bc, cc, coef, dsk, nc_ctx):
    g, r, w = ug.shape
    gb = S5_GROUPS_PER_STEP
    nc = r // 8
    spec3 = lambda a: pl.BlockSpec((gb,) + a.shape[1:], lambda i: (i, 0, 0))
    return pl.pallas_call(
        functools.partial(_s5_kernel, nc_ctx=nc_ctx, nc=nc),
        grid=(g // gb,),
        in_specs=[spec3(ug), spec3(tz), spec3(bc), spec3(cc), spec3(coef), spec3(dsk)],
        out_specs=pl.BlockSpec((gb, r, w), lambda i: (i, 0, 0)),
        out_shape=jax.ShapeDtypeStruct((g, r, w), F32),
        scratch_shapes=[pltpu.VMEM((gb, r, 2 * w), F32), pltpu.VMEM((gb, r, w), F32)],
        compiler_params=_cparams(1),
    )(ug, tz, bc, cc, coef, dsk)


def _s5_prepare(lam_re, lam_im, log_dt, b_re, b_im, c_re, c_im, s5_d):
    t = S5_CHUNK
    g, p, hg = S5_GROUPS, S5_STATE, S5_GROUP
    lam_re = lam_re.astype(F32)
    lam_im = lam_im.astype(F32)
    dt = jnp.exp(log_dt.astype(F32))[..., None]
    k = jnp.arange(t + 1, dtype=F32)[:, None, None, None]
    mag = jnp.exp(lam_re * dt * k)
    ang = lam_im * dt * k
    pr, pi = mag * jnp.cos(ang), mag * jnp.sin(ang)
    ar, ai = pr[1], pi[1]
    den = lam_re * lam_re + lam_im * lam_im
    qr = ((ar - 1) * lam_re + ai * lam_im) / den
    qi = (ai * lam_re - (ar - 1) * lam_im) / den
    b_re = b_re.astype(F32)
    b_im = b_im.astype(F32)
    bbr = qr[..., None] * b_re - qi[..., None] * b_im
    bbi = qr[..., None] * b_im + qi[..., None] * b_re
    c_re = c_re.astype(F32)
    c_im = c_im.astype(F32)
    car = c_re[None] * pr[:, :, :, None, :] - c_im[None] * pi[:, :, :, None, :]
    cai = c_re[None] * pi[:, :, :, None, :] + c_im[None] * pr[:, :, :, None, :]
    kern = (jnp.einsum('kdghp,dgpj->kdghj', car[:t], bbr, precision=HIGHEST)
            - jnp.einsum('kdghp,dgpj->kdghj', cai[:t], bbi, precision=HIGHEST))
    tau = np.stack([np.arange(t), t - 1 - np.arange(t)])
    tz_dirs, bc_dirs, cc_dirs, coef_rows = [], [], [], []
    for d in range(2):
        lag = tau[d][None, :] - tau[d][:, None]
        valid = jnp.asarray(lag >= 0, F32)
        kd = kern[:, d][np.clip(lag, 0, t - 1)]
        kd = kd * valid[:, :, None, None, None]
        tz_dirs.append(kd.transpose(2, 0, 4, 1, 3).reshape(g, t * hg, t * hg))
        pw = t - 1 - tau[d]
        br = pr[pw, d][..., None] * bbr[d][None] - pi[pw, d][..., None] * bbi[d][None]
        bi = pr[pw, d][..., None] * bbi[d][None] + pi[pw, d][..., None] * bbr[d][None]
        br = br.transpose(1, 0, 3, 2).reshape(g, t * hg, p)
        bi = bi.transpose(1, 0, 3, 2).reshape(g, t * hg, p)
        bc_dirs.append(jnp.concatenate([br, bi, bi, br], axis=-1))
        cr = car[tau[d] + 1, d]
        ci = cai[tau[d] + 1, d]
        cr = cr.transpose(1, 3, 0, 2).reshape(g, p, t * hg)
        ci = ci.transpose(1, 3, 0, 2).reshape(g, p, t * hg)
        cc_dirs.append(jnp.concatenate([cr, -ci], axis=1))
        er, ei = pr[t, d], pi[t, d]
        coef_rows += [jnp.concatenate([er, er], -1), jnp.concatenate([-ei, ei], -1),
                      jnp.concatenate([ei, -ei], -1)]
    tz = (tz_dirs[0] + tz_dirs[1]).astype(BF16)
    bc = jnp.concatenate(bc_dirs, axis=-1).astype(BF16)
    cc = jnp.concatenate(cc_dirs, axis=1).astype(BF16)
    zero = jnp.zeros_like(coef_rows[0])
    coef = jnp.stack(coef_rows + [zero, zero], axis=1)
    dsk = jnp.tile(s5_d.astype(F32).reshape(g, 1, hg), (1, 1, t))
    return tz, bc, cc, coef, dsk


def _s5_branch(u, prep, n_ctx):
    b, n, _ = u.shape
    t = S5_CHUNK
    nc = n // t
    ug = u.reshape(b, nc, t, S5_GROUPS, S5_GROUP).transpose(3, 1, 0, 2, 4).reshape(S5_GROUPS, nc * b, t * S5_GROUP)
    yg = _s5_call(ug, *prep, n_ctx // t)
    return yg.reshape(S5_GROUPS, nc, b, t, S5_GROUP).transpose(2, 1, 3, 0, 4).reshape(b, n, S5_WIDTH)


def _rope_lanes(x, cos, sa, sb, shift):
    return x * cos + pltpu.roll(x, 128 - shift, 1) * sa + pltpu.roll(x, shift, 1) * sb


def _mla_prep_kernel(qa_ref, kva_ref, kr_ref, qg_ref, kvg_ref, wq_ref, wk_ref, wv_ref,
                     cos_ref, sa_ref, sb_ref, qp_ref, qr_ref, k_ref, v_ref):
    def rms(x, gain):
        return (x * lax.rsqrt(jnp.mean(x * x, axis=-1, keepdims=True) + LN_EPS) * gain).astype(BF16)

    cos, sa, sb = cos_ref[...], sa_ref[...], sb_ref[...]
    qn = rms(qa_ref[0], qg_ref[...])
    kvn = rms(kva_ref[0], kvg_ref[...])
    q = _bdot(qn, wq_ref[...]) * MLA_SCALE
    k = _bdot(kvn, wk_ref[...])
    v_ref[0] = _bdot(kvn, wv_ref[...]).astype(BF16)
    kr_rot = _rope_lanes(kr_ref[0], cos, sa, sb, MLA_ROPE // 2)
    qp_ref[0] = q.astype(BF16)
    for h in range(MLA_HEADS):
        sl = slice(h * MLA_PAD, (h + 1) * MLA_PAD)
        qr_ref[0, :, sl] = _rope_lanes(q[:, sl], cos, sa, sb, MLA_ROPE // 2).astype(BF16)
        k_ref[0, :, sl] = (k[:, sl] + kr_rot).astype(BF16)


def _mla_prep_call(qa, kva, kr, qg, kvg, wq, wk, wv, tabs):
    b, n, _ = qa.shape
    tm = TOKEN_TILE
    tok = lambda w: pl.BlockSpec((1, tm, w), lambda i, t: (i, t, 0))
    full = lambda a: pl.BlockSpec(a.shape, lambda i, t: (0,) * a.ndim)
    tab = pl.BlockSpec((tm, 128), lambda i, t: (t, 0))
    hw = MLA_HEADS * MLA_PAD
    return pl.pallas_call(
        _mla_prep_kernel,
        grid=(b, n // tm),
        in_specs=[tok(MLA_Q_RANK), tok(MLA_KV_RANK), tok(MLA_PAD), full(qg), full(kvg), full(wq), full(wk),
                  full(wv), tab, tab, tab],
        out_specs=[tok(hw), tok(hw), tok(hw), tok(MLA_HEADS * MLA_V)],
        out_shape=[jax.ShapeDtypeStruct((b, n, hw), BF16)] * 3
        + [jax.ShapeDtypeStruct((b, n, MLA_HEADS * MLA_V), BF16)],
        compiler_params=_cparams(2),
    )(qa, kva, kr, qg, kvg, wq, wk, wv, *tabs)


def _softmax_av(scores, values, extra=None):
    m = functools.reduce(jnp.maximum, [jnp.max(s, axis=-1, keepdims=True) for s in scores])
    if extra is not None:
        m = jnp.maximum(m, extra)
    ps = [jnp.exp(s - m) for s in scores]
    l = functools.reduce(jnp.add, [jnp.sum(p, axis=-1, keepdims=True) for p in ps])
    if extra is not None:
        l = l + jnp.exp(extra - m)
    o = functools.reduce(jnp.add, [_bdot(p.astype(BF16), v) for p, v in zip(ps, values)])
    return o / l


def _mla_attn_kernel(qp_ref, qr_ref, k_ref, v_ref, o_ref, *, n_ctx, n_ctx_tiles):
    heads = qp_ref.shape[-1] // MLA_PAD
    t = pl.program_id(2)

    def run(latent):
        outs = []
        for h in range(heads):
            sl = slice(h * MLA_PAD, (h + 1) * MLA_PAD)
            vs = slice(h * MLA_V, (h + 1) * MLA_V)
            scores = [_dot_nt(qp_ref[0, :, sl], k_ref[0, :n_ctx, sl])]
            values = [v_ref[0, :n_ctx, vs]]
            if latent:
                scores.append(_dot_nt(qr_ref[0, :, sl], k_ref[0, n_ctx:, sl]))
                values.append(v_ref[0, n_ctx:, vs])
            outs.append(_softmax_av(scores, values))
        o_ref[0] = jnp.concatenate(outs, axis=-1).astype(o_ref.dtype)

    pl.when(t < n_ctx_tiles)(lambda: run(False))
    pl.when(t >= n_ctx_tiles)(lambda: run(True))


def _mla_attn_call(qp, qr, k, v, n_ctx):
    b, n, _ = qp.shape
    tq = TOKEN_TILE
    hp = 2
    qspec = pl.BlockSpec((1, tq, hp * MLA_PAD), lambda i, h, t: (i, t, h))
    return pl.pallas_call(
        functools.partial(_mla_attn_kernel, n_ctx=n_ctx, n_ctx_tiles=n_ctx // tq),
        grid=(b, MLA_HEADS // hp, n // tq),
        in_specs=[qspec, qspec,
                  pl.BlockSpec((1, n, hp * MLA_PAD), lambda i, h, t: (i, 0, h)),
                  pl.BlockSpec((1, n, hp * MLA_V), lambda i, h, t: (i, 0, h))],
        out_specs=pl.BlockSpec((1, tq, hp * MLA_V), lambda i, h, t: (i, t, h)),
        out_shape=jax.ShapeDtypeStruct((b, n, MLA_HEADS * MLA_V), BF16),
        compiler_params=_cparams(3),
    )(qp, qr, k, v)


def _win_kernel(sink_ref, q_ref, kp_ref, kc_ref, kn_ref, vp_ref, vc_ref, vn_ref, kctx_ref, vctx_ref,
                cq_ref, saq_ref, sbq_ref, cp_ref, sap_ref, sbp_ref, cn_ref, san_ref, sbn_ref, o_ref,
                *, n_ctx_blocks, n_blocks):
    j = pl.program_id(1)
    half = WIN_HEAD_DIM // 2
    hd = WIN_HEAD_DIM

    def rows_of_heads(x, kh):
        return jnp.concatenate([x[:, (kh * WIN_GROUP + g) * hd:(kh * WIN_GROUP + g + 1) * hd]
                                for g in range(WIN_GROUP)], axis=0)

    def finish(per_kv):
        cols = []
        for kh in range(WIN_KV_HEADS):
            o = per_kv[kh]
            cols += [o[g * BLOCK:(g + 1) * BLOCK] for g in range(WIN_GROUP)]
        o_ref[0] = jnp.concatenate(cols, axis=-1).astype(o_ref.dtype)

    def sink_col(kh):
        return jnp.concatenate([jnp.full((BLOCK, 1), sink_ref[kh * WIN_GROUP + g], F32) for g in range(WIN_GROUP)],
                               axis=0)

    q = q_ref[0] * WIN_SCALE
    kctx = kctx_ref[0].astype(BF16)
    vctx = vctx_ref[0].astype(BF16)

    def ctx_path():
        per_kv = []
        for kh in range(WIN_KV_HEADS):
            ks = slice(kh * hd, (kh + 1) * hd)
            s = _dot_nt(rows_of_heads(q, kh).astype(BF16), kctx[:, ks])
            per_kv.append(_softmax_av([s], [vctx[:, ks]], sink_col(kh)))
        finish(per_kv)

    def lat_path():
        q_rot = jnp.concatenate(
            [_rope_lanes(q[:, c * 128:(c + 1) * 128], cq_ref[...], saq_ref[...], sbq_ref[...], half)
             for c in range(q.shape[1] // 128)], axis=-1)
        kp =_rope_lanes(kp_ref[0], cp_ref[...], sap_ref[...], sbp_ref[...], half).astype(BF16)
        kc = _rope_lanes(kc_ref[0], cq_ref[...], saq_ref[...], sbq_ref[...], half).astype(BF16)
        kn = _rope_lanes(kn_ref[0], cn_ref[...], san_ref[...], sbn_ref[...], half).astype(BF16)
        kband = jnp.concatenate([kp, kc, kn], axis=0)
        vband = jnp.concatenate([vp_ref[0], vc_ref[0], vn_ref[0]], axis=0).astype(BF16)
        blk = j - n_ctx_blocks
        r = lax.broadcasted_iota(jnp.int32, (BLOCK, 3 * BLOCK), 0)
        c = lax.broadcasted_iota(jnp.int32, (BLOCK, 3 * BLOCK), 1)
        valid = (jnp.abs(c - BLOCK - r) <= WINDOW)
        valid = valid & ((c >= BLOCK) | (blk > 0)) & ((c < 2 * BLOCK) | (blk < n_blocks - 1))
        valid = jnp.concatenate([valid] * WIN_GROUP, axis=0)
        per_kv = []
        for kh in range(WIN_KV_HEADS):
            ks = slice(kh * hd, (kh + 1) * hd)
            sb = _dot_nt(rows_of_heads(q_rot, kh).astype(BF16), kband[:, ks])
            sb = jnp.where(valid, sb, NEG_INF)
            sc = _dot_nt(rows_of_heads(q, kh).astype(BF16), kctx[:, ks])
            per_kv.append(_softmax_av([sb, sc], [vband[:, ks], vctx[:, ks]], sink_col(kh)))
        finish(per_kv)

    pl.when(j < n_ctx_blocks)(ctx_path)
    pl.when(j >= n_ctx_blocks)(lat_path)


def _win_call(sink, wq, wk, wv, tabs, n_ctx):
    b, n, _ = wq.shape
    ncb = n_ctx // BLOCK
    nb = n // BLOCK
    kvw = WIN_KV_HEADS * WIN_HEAD_DIM
    lo, hi = ncb, nb - 1
    prev = lambda i, j: (i, jnp.clip(j - 1, lo, hi), 0)
    cur = lambda i, j: (i, j, 0)
    nxt = lambda i, j: (i, jnp.clip(j + 1, lo, hi), 0)
    kv = lambda f: pl.BlockSpec((1, BLOCK, kvw), f)
    tab = lambda f: pl.BlockSpec((BLOCK, 128), lambda i, j: f(i, j)[1:])
    ctx = pl.BlockSpec((1, n_ctx, kvw), lambda i, j: (i, 0, 0))
    return pl.pallas_call(
        functools.partial(_win_kernel, n_ctx_blocks=ncb, n_blocks=nb - ncb),
        grid=(b, nb),
        in_specs=[pl.BlockSpec(memory_space=pltpu.SMEM),
                  pl.BlockSpec((1, BLOCK, WIN_Q_HEADS * WIN_HEAD_DIM), cur),
                  kv(prev), kv(cur), kv(nxt), kv(prev), kv(cur), kv(nxt), ctx, ctx,
                  tab(cur), tab(cur), tab(cur), tab(prev), tab(prev), tab(prev), tab(nxt), tab(nxt), tab(nxt)],
        out_specs=pl.BlockSpec((1, BLOCK, WIN_Q_HEADS * WIN_HEAD_DIM), cur),
        out_shape=jax.ShapeDtypeStruct((b, n, WIN_Q_HEADS * WIN_HEAD_DIM), BF16),
        compiler_params=_cparams(2),
    )(sink, wq, wk, wk, wk, wv, wv, wv, wk, wv, *tabs, *tabs, *tabs)


def _merge_kernel(x_ref, s5_ref, mla_ref, win_ref, gate_ref, mod_ref, wglu_ref, bglu_ref, wbr_ref, wout_ref,
                  g1_ref, b1_ref, wr_ref, x1_ref, h2_ref, lg_ref, *, alpha):
    d = x_ref.shape[-1]
    g = jax.nn.gelu(s5_ref[0])
    s5o = g * jax.nn.sigmoid(_bdot(g.astype(BF16), wglu_ref[...]) + bglu_ref[...])
    branches = (s5o.astype(BF16), mla_ref[0], win_ref[0])
    mix = None
    for kk, o in enumerate(branches):
        term = jax.nn.sigmoid(gate_ref[0, :, kk * d:(kk + 1) * d]) * _bdot(o, wbr_ref[kk])
        mix = term if mix is None else mix + term
    y = _bdot(mix.astype(BF16), wout_ref[...])
    mod = lambda r: mod_ref[0, 0, r:r + 1, :]
    x1 = _layer_norm(alpha * x_ref[0] + mod(2) * y) * g1_ref[...] + b1_ref[...]
    x1_ref[0] = x1
    h2 = (_layer_norm(x1) * (1.0 + mod(4)) + mod(3)).astype(BF16)
    h2_ref[0] = h2
    lg_ref[0] = _dot_nt(wr_ref[...], h2)


def _merge_call(xall, s5y, mla_o, win_o, gates, mod, wglu, bglu, wbr, wout, g1, b1, wr_t, n_ctx_tiles, alpha):
    b, n, d = xall.shape
    tm = TOKEN_TILE
    tok = lambda w: pl.BlockSpec((1, tm, w), lambda i, t: (i, t, 0))
    full = lambda a: pl.BlockSpec(a.shape, lambda i, t: (0,) * a.ndim)
    return pl.pallas_call(
        functools.partial(_merge_kernel, alpha=alpha),
        grid=(b, n // tm),
        in_specs=[tok(d), tok(BRANCH_WIDTH), tok(BRANCH_WIDTH), tok(BRANCH_WIDTH), tok(N_BRANCH * d),
                  pl.BlockSpec((1, 1, 6, d), lambda i, t: (i, jnp.where(t < n_ctx_tiles, 0, 1), 0, 0)),
                  full(wglu), full(bglu), full(wbr), full(wout), full(g1), full(b1), full(wr_t)],
        out_specs=[tok(d), tok(d), pl.BlockSpec((1, N_EXPERTS, tm), lambda i, t: (i, 0, t))],
        out_shape=[jax.ShapeDtypeStruct((b, n, d), F32), jax.ShapeDtypeStruct((b, n, d), BF16),
                   jax.ShapeDtypeStruct((b, N_EXPERTS, n), F32)],
        compiler_params=_cparams(2),
    )(xall, s5y, mla_o, win_o, gates, mod, wglu, bglu, wbr, wout, g1, b1, wr_t)


def _excl_cumsum_lanes(m):
    rows, n = m.shape
    r = lax.broadcasted_iota(jnp.int32, (128, 128), 0)
    c = lax.broadcasted_iota(jnp.int32, (128, 128), 1)
    tri = jnp.where(r < c, 1.0, 0.0).astype(BF16)
    off = jnp.zeros((rows, 1), F32)
    outs = []
    for jb in range(n // 128):
        blk = m[:, jb * 128:(jb + 1) * 128]
        outs.append(_bdot(blk.astype(BF16), tri) + off)
        off = off + jnp.sum(blk, axis=1, keepdims=True)
    return jnp.concatenate(outs, axis=1)


def _topk_slots(aff, cap):
    bits = pltpu.bitcast(aff, jnp.int32)

    def body(i, thr):
        cand = thr | (jnp.int32(1) << (30 - i))
        cnt = jnp.sum(jnp.where(bits >= cand, 1.0, 0.0), axis=1, keepdims=True)
        return jnp.where(cnt >= cap, cand, thr)

    thr = lax.fori_loop(0, 31, body, jnp.zeros((aff.shape[0], 1), jnp.int32))
    gt = jnp.where(bits > thr, 1.0, 0.0)
    eq = jnp.where(bits == thr, 1.0, 0.0)
    need = cap - jnp.sum(gt, axis=1, keepdims=True)
    sel = gt + eq * jnp.where(_excl_cumsum_lanes(eq) < need, 1.0, 0.0)
    rank = _excl_cumsum_lanes(sel)
    return jnp.where(sel > 0.5, rank, -1.0).astype(jnp.int32)


def _route_kernel(lg_ref, slot_ref, aff_ref, *, n_ctx, cap_ctx, cap_lat):
    lg = lg_ref[0]
    m = jnp.max(lg, axis=0, keepdims=True)
    ex = jnp.exp(lg - m)
    aff = ex / jnp.sum(ex, axis=0, keepdims=True)
    aff_ref[0] = aff
    slot_ref[0, :, :n_ctx] = _topk_slots(aff[:, :n_ctx], cap_ctx)
    slot_ref[0, :, n_ctx:] = _topk_slots(aff[:, n_ctx:], cap_lat)


def _route_call(logits_t, n_ctx, cap_ctx, cap_lat):
    b, e, n = logits_t.shape
    spec = pl.BlockSpec((1, e, n), lambda i: (i, 0, 0))
    return pl.pallas_call(
        functools.partial(_route_kernel, n_ctx=n_ctx, cap_ctx=cap_ctx, cap_lat=cap_lat),
        grid=(b,),
        in_specs=[spec],
        out_specs=[spec, spec],
        out_shape=[jax.ShapeDtypeStruct((b, e, n), jnp.int32), jax.ShapeDtypeStruct((b, e, n), F32)],
        compiler_params=_cparams(1),
    )(logits_t)


def _expert_kernel(slot_ref, aff_ref, h_ref, wg_ref, wu_ref, wd_ref, yl_ref, yc_ref, *, n_ctx):
    cap_lat = yl_ref.shape[2]
    cap_ctx = yc_ref.shape[2]
    slot = slot_ref[0, 0]
    aff = aff_ref[0, 0]

    def pick(lo, hi, cap):
        iota = lax.broadcasted_iota(jnp.int32, (cap, hi - lo), 0)
        mask = slot[:, lo:hi] == iota
        gate = jnp.sum(jnp.where(mask, aff[:, lo:hi], 0.0), axis=1, keepdims=True)
        onehot = jnp.where(mask, 1.0, 0.0).astype(BF16)
        return _bdot(onehot, h_ref[0, lo:hi, :]), gate

    n = h_ref.shape[1]
    xl, gl = pick(n_ctx, n, cap_lat)
    xc, gc = pick(0, n_ctx, cap_ctx)
    xs = jnp.concatenate([xl, xc], axis=0).astype(BF16)
    gate = jnp.concatenate([gl, gc], axis=0)
    a = _bdot(xs, wg_ref[0])
    u = _bdot(xs, wu_ref[0])
    hm = (a * jax.nn.sigmoid(a) * u).astype(BF16)
    y = _bdot(hm, wd_ref[0]) * gate
    yl_ref[0, 0] = y[:cap_lat].astype(yl_ref.dtype)
    yc_ref[0, 0] = y[cap_lat:].astype(yc_ref.dtype)


def _expert_call(slot, aff, h2, wg, wu, wd, n_ctx, cap_ctx, cap_lat):
    b, e, n = slot.shape
    d = h2.shape[-1]
    f = wg.shape[-1]
    row = pl.BlockSpec((1, 1, 1, n), lambda ie, ib: (ib, ie, 0, 0))
    return pl.pallas_call(
        functools.partial(_expert_kernel, n_ctx=n_ctx),
        grid=(e, b),
        in_specs=[row, row,
                  pl.BlockSpec((1, n, d), lambda ie, ib: (ib, 0, 0)),
                  pl.BlockSpec((1, d, f), lambda ie, ib: (ie, 0, 0)),
                  pl.BlockSpec((1, d, f), lambda ie, ib: (ie, 0, 0)),
                  pl.BlockSpec((1, f, d), lambda ie, ib: (ie, 0, 0))],
        out_specs=[pl.BlockSpec((1, 1, cap_lat, d), lambda ie, ib: (ib, ie, 0, 0)),
                   pl.BlockSpec((1, 1, cap_ctx, d), lambda ie, ib: (ib, ie, 0, 0))],
        out_shape=[jax.ShapeDtypeStruct((b, e, cap_lat, d), BF16), jax.ShapeDtypeStruct((b, e, cap_ctx, d), BF16)],
        compiler_params=_cparams(2),
    )(slot.reshape(b, e, 1, n), aff.reshape(b, e, 1, n), h2, wg, wu, wd)


def _combine_kernel(slot_ref, yl_ref, yc_ref, x1_ref, mod_ref, g2_ref, b2_ref, o_ref, *, n_ctx_tiles, alpha):
    t = pl.program_id(1)
    tm = x1_ref.shape[1]
    slot = slot_ref[0]

    def finish(fl):
        x1 = x1_ref[0]
        o_ref[0] = _layer_norm(alpha * x1 + mod_ref[0, 0, 5:6, :] * fl) * g2_ref[...] + b2_ref[...]

    def onehot(e, cap):
        iota = lax.broadcasted_iota(jnp.int32, (tm, cap), 1)
        return jnp.where(slot[:, e:e + 1] == iota, 1.0, 0.0).astype(BF16)

    def ctx_path():
        cap = yc_ref.shape[2]
        fl = None
        for e in range(N_EXPERTS):
            term = _bdot(onehot(e, cap), yc_ref[0, e])
            fl = term if fl is None else fl + term
        finish(fl)

    def lat_path():
        cap = yl_ref.shape[2]
        pt = jnp.concatenate([onehot(e, cap) for e in range(N_EXPERTS)], axis=1)
        finish(_bdot(pt, yl_ref[0].reshape(N_EXPERTS * cap, yl_ref.shape[3])))

    pl.when(t < n_ctx_tiles)(ctx_path)
    pl.when(t >= n_ctx_tiles)(lat_path)


def _combine_call(slot_t, yl, yc, x1, mod, g2, b2, n_ctx_tiles, alpha):
    b, n, d = x1.shape
    tm = TOKEN_TILE
    e = N_EXPERTS
    full = lambda a: pl.BlockSpec(a.shape, lambda i, t: (0,) * a.ndim)
    return pl.pallas_call(
        functools.partial(_combine_kernel, n_ctx_tiles=n_ctx_tiles, alpha=alpha),
        grid=(b, n // tm),
        in_specs=[pl.BlockSpec((1, tm, e), lambda i, t: (i, t, 0)),
                  pl.BlockSpec((1,) + yl.shape[1:], lambda i, t: (i, 0, 0, 0)),
                  pl.BlockSpec((1,) + yc.shape[1:], lambda i, t: (i, 0, 0, 0)),
                  pl.BlockSpec((1, tm, d), lambda i, t: (i, t, 0)),
                  pl.BlockSpec((1, 1, 6, d), lambda i, t: (i, jnp.where(t < n_ctx_tiles, 0, 1), 0, 0)),
                  full(g2), full(b2)],
        out_specs=pl.BlockSpec((1, tm, d), lambda i, t: (i, t, 0)),
        out_shape=jax.ShapeDtypeStruct((b, n, d), F32),
        compiler_params=_cparams(2),
    )(slot_t, yl, yc, x1, mod, g2, b2)


def _rope_tables(n_ctx, seq, head_dim, lane_offset):
    half = head_dim // 2
    nf = head_dim // 4
    t = jnp.arange(seq, dtype=F32)
    row = jnp.floor(t / GRID_W)
    col = t - row * GRID_W
    freqs = ROPE_BASE ** (-jnp.arange(nf, dtype=F32) / nf)
    ang = jnp.concatenate([row[:, None] * freqs, col[:, None] * freqs], axis=-1)
    cos, sin = jnp.cos(ang), jnp.sin(ang)
    zeros = jnp.zeros_like(sin)
    n_heads = (128 - lane_offset) // head_dim if lane_offset == 0 else 1
    c = jnp.concatenate([jnp.ones((seq, lane_offset), F32)] + [cos, cos] * n_heads, axis=-1)
    sa = jnp.concatenate([jnp.zeros((seq, lane_offset), F32)] + [-sin, zeros] * n_heads, axis=-1)
    sb = jnp.concatenate([jnp.zeros((seq, lane_offset), F32)] + [zeros, sin] * n_heads, axis=-1)
    pad = 128 - c.shape[1]
    c = jnp.pad(c, ((n_ctx, 0), (0, pad)), constant_values=1.0)
    sa = jnp.pad(sa, ((n_ctx, 0), (0, pad)))
    sb = jnp.pad(sb, ((n_ctx, 0), (0, pad)))
    return c, sa, sb


def _layer_weights(i, p):
    d = p['w_in'].shape[1]
    pts = np.cumsum((S5_WIDTH, MLA_Q_RANK, MLA_KV_RANK, MLA_ROPE, WIN_Q_HEADS * WIN_HEAD_DIM,
                     WIN_KV_HEADS * WIN_HEAD_DIM, WIN_KV_HEADS * WIN_HEAD_DIM))
    cols = jnp.split(p['w_in'][i], [int(v) for v in pts], axis=1)
    kr = jnp.pad(cols[3], ((0, 0), (MLA_NOPE, MLA_PAD - MLA_NOPE - MLA_ROPE)))
    w_cat = jnp.concatenate([cols[0], cols[1], cols[2], kr, cols[4], cols[5], cols[6], cols[7]], axis=1)
    dq = MLA_NOPE + MLA_ROPE
    wq = p['mla_w_uq'][i].reshape(MLA_Q_RANK, MLA_HEADS, dq)
    wq = jnp.pad(wq, ((0, 0), (0, 0), (0, MLA_PAD - dq))).reshape(MLA_Q_RANK, MLA_HEADS * MLA_PAD)
    wkv = p['mla_w_ukv'][i].reshape(MLA_KV_RANK, MLA_HEADS, MLA_NOPE + MLA_V)
    wk = jnp.pad(wkv[:, :, :MLA_NOPE], ((0, 0), (0, 0), (0, MLA_PAD - MLA_NOPE)))
    wk = wk.reshape(MLA_KV_RANK, MLA_HEADS * MLA_PAD)
    wv = wkv[:, :, MLA_NOPE:].reshape(MLA_KV_RANK, MLA_HEADS * MLA_V)
    row = lambda a: a[i].astype(F32).reshape(1, -1)
    return dict(
        w_cat=w_cat.astype(BF16), wq=wq.astype(BF16), wk=wk.astype(BF16), wv=wv.astype(BF16),
        qg=row(p['mla_q_norm']), kvg=row(p['mla_kv_norm']),
        s5=_s5_prepare(p['s5_lam_re'][i], p['s5_lam_im'][i], p['s5_log_dt'][i], p['s5_b_re'][i], p['s5_b_im'][i],
                       p['s5_c_re'][i], p['s5_c_im'][i], p['s5_d'][i]),
        wglu=p['s5_w_glu'][i].astype(BF16), bglu=row(p['s5_b_glu']),
        sink=p['win_sink'][i].astype(F32),
        wbr=p['w_branch'][i].astype(BF16), wout=p['w_out'][i].astype(BF16),
        g1=row(p['ln1_g']), b1=row(p['ln1_b']), g2=row(p['ln2_g']), b2=row(p['ln2_b']),
        wr_t=p['w_router'][i].T.astype(BF16),
        wg=p['w_gate'][i].astype(BF16), wu=p['w_up'][i].astype(BF16), wd=p['w_down'][i].astype(BF16),
    )


def _forward(p):
    x, c, ctx, c_ctx = p['x'], p['c'], p['ctx'], p['c_ctx']
    b, seq, d = x.shape
    n_ctx = ctx.shape[1]
    depth = p['w_ada'].shape[0]
    assert b == 8 and seq % TOKEN_TILE == 0 and n_ctx % TOKEN_TILE == 0 and seq % GRID_W == 0
    alpha = float((2 * depth) ** 0.25)
    n_ctx_tiles = n_ctx // TOKEN_TILE
    cap_lat = CAPACITY_FACTOR * seq // N_EXPERTS
    cap_ctx = CAPACITY_FACTOR * n_ctx // N_EXPERTS

    cond = jnp.concatenate([c, c_ctx[None], jnp.zeros((16 - b - 1, d), F32)], axis=0)
    mods = _ada_call(cond, p['w_ada'], p['b_ada'])
    mods = mods.reshape(depth, 16, 6, d)
    tabs_mla = _rope_tables(n_ctx, seq, MLA_ROPE, MLA_NOPE)
    tabs_win = _rope_tables(n_ctx, seq, WIN_HEAD_DIM, 0)

    xall = jnp.concatenate([ctx, x], axis=1)
    for i in range(depth):
        w = _layer_weights(i, p)
        mod = jnp.stack([jnp.broadcast_to(mods[i, b], (b, 6, d)), mods[i, :b]], axis=1)
        u, qa, kva, kr, wq, wk, wv, gates = _in_call(xall, mod, w['w_cat'], n_ctx_tiles)
        s5y = _s5_branch(u, w['s5'], n_ctx)
        qp, qr, kk, vv = _mla_prep_call(qa, kva, kr, w['qg'], w['kvg'], w['wq'], w['wk'], w['wv'], tabs_mla)
        mla_o = _mla_attn_call(qp, qr, kk, vv, n_ctx)
        win_o = _win_call(w['sink'], wq, wk, wv, tabs_win, n_ctx)
        x1, h2, logits_t = _merge_call(xall, s5y, mla_o, win_o, gates, mod, w['wglu'], w['bglu'], w['wbr'],
                                       w['wout'], w['g1'], w['b1'], w['wr_t'], n_ctx_tiles, alpha)
        slot, aff = _route_call(logits_t, n_ctx, cap_ctx, cap_lat)
        yl, yc = _expert_call(slot, aff, h2, w['wg'], w['wu'], w['wd'], n_ctx, cap_ctx, cap_lat)
        slot_t = jnp.swapaxes(slot, 1, 2)
        xall = _combine_call(slot_t, yl, yc, x1, mod, w['g2'], w['b2'], n_ctx_tiles, alpha)
    return xall[:, n_ctx:]


def kernel(x, c, ctx, c_ctx, w_ada, b_ada, w_in, s5_lam_re, s5_lam_im, s5_log_dt, s5_b_re, s5_b_im, s5_c_re, s5_c_im, s5_d, s5_w_glu, s5_b_glu, mla_q_norm, mla_w_uq, mla_kv_norm, mla_w_ukv, win_sink, w_branch, w_out, ln1_g, ln1_b, ln2_g, ln2_b, w_router, w_gate, w_up, w_down):
    return _forward(dict(
        x=x, c=c, ctx=ctx, c_ctx=c_ctx, w_ada=w_ada, b_ada=b_ada, w_in=w_in, s5_lam_re=s5_lam_re,
        s5_lam_im=s5_lam_im, s5_log_dt=s5_log_dt, s5_b_re=s5_b_re, s5_b_im=s5_b_im, s5_c_re=s5_c_re,
        s5_c_im=s5_c_im, s5_d=s5_d, s5_w_glu=s5_w_glu, s5_b_glu=s5_b_glu, mla_q_norm=mla_q_norm,
        mla_w_uq=mla_w_uq, mla_kv_norm=mla_kv_norm, mla_w_ukv=mla_w_ukv, win_sink=win_sink, w_branch=w_branch,
        w_out=w_out, ln1_g=ln1_g, ln1_b=ln1_b, ln2_g=ln2_g, ln2_b=ln2_b, w_router=w_router, w_gate=w_gate,
        w_up=w_up, w_down=w_down))
```

```python
import functools
import math

import jax
import jax.numpy as jnp
import numpy as np
from jax import lax
from jax.experimental import pallas as pl
from jax.experimental.pallas import tpu as pltpu

F32 = jnp.float32
BF16 = jnp.bfloat16
HIGHEST = lax.Precision.HIGHEST

GRID_W = 64
S5_WIDTH = 512
S5_GROUP = 16
S5_GROUPS = S5_WIDTH // S5_GROUP
S5_STATE = 64
S5_CHUNK = 16
S5_GROUPS_PER_STEP = 4
MLA_HEADS = 8
MLA_NOPE = 64
MLA_ROPE = 32
MLA_V = 64
MLA_Q_RANK = 384
MLA_KV_RANK = 256
MLA_PAD = 128
MLA_SCALE = (MLA_NOPE + MLA_ROPE) ** -0.5
WIN_Q_HEADS = 8
WIN_KV_HEADS = 2
WIN_GROUP = WIN_Q_HEADS // WIN_KV_HEADS
WIN_HEAD_DIM = 64
WINDOW = 128
BLOCK = 128
WIN_SCALE = WIN_HEAD_DIM ** -0.5
N_BRANCH = 3
BRANCH_WIDTH = 512
N_EXPERTS = 16
CAPACITY_FACTOR = 2
ROPE_BASE = 10000.0
LN_EPS = 1e-6
NEG_INF = -1e30
TOKEN_TILE = 256
VMEM_LIMIT = 56 * 1024 * 1024


def _cparams(n_axes):
    return pltpu.CompilerParams(dimension_semantics=("arbitrary",) * n_axes, vmem_limit_bytes=VMEM_LIMIT)


def _bdot(a, b):
    return jnp.dot(a, b, preferred_element_type=F32)


def _dot_nt(a, b):
    return lax.dot_general(a, b, (((1,), (1,)), ((), ())), preferred_element_type=F32)


def _layer_norm(x):
    mu = jnp.mean(x, axis=-1, keepdims=True)
    xc = x - mu
    var = jnp.mean(xc * xc, axis=-1, keepdims=True)
    return xc * lax.rsqrt(var + LN_EPS)


def _ada_kernel(cond_ref, w_ref, b_ref, o_ref):
    s = cond_ref[...]
    s = s * jax.nn.sigmoid(s)
    o_ref[0] = jnp.dot(s, w_ref[0], precision=HIGHEST, preferred_element_type=F32) + b_ref[0]


def _ada_call(cond, w_ada, b_ada):
    depth, d, d6 = w_ada.shape
    tn = 1536
    rows = cond.shape[0]
    return pl.pallas_call(
        _ada_kernel,
        grid=(depth, d6 // tn),
        in_specs=[
            pl.BlockSpec((rows, d), lambda i, j: (0, 0)),
            pl.BlockSpec((1, d, tn), lambda i, j: (i, 0, j)),
            pl.BlockSpec((1, 1, tn), lambda i, j: (i, 0, j)),
        ],
        out_specs=pl.BlockSpec((1, rows, tn), lambda i, j: (i, 0, j)),
        out_shape=jax.ShapeDtypeStruct((depth, rows, d6), F32),
        compiler_params=_cparams(2),
    )(cond, w_ada, b_ada.reshape(depth, 1, d6))


IN_WIDTHS = (S5_WIDTH, MLA_Q_RANK, MLA_KV_RANK, MLA_PAD, WIN_Q_HEADS * WIN_HEAD_DIM,
             WIN_KV_HEADS * WIN_HEAD_DIM, WIN_KV_HEADS * WIN_HEAD_DIM)
IN_OFFSETS = tuple(int(v) for v in np.cumsum((0,) + IN_WIDTHS))


def _in_kernel(x_ref, mod_ref, w_ref, *out_refs):
    xn = _layer_norm(x_ref[0])
    shift = mod_ref[0, 0, 0:1, :]
    scale = mod_ref[0, 0, 1:2, :]
    h = (xn * (1.0 + scale) + shift).astype(BF16)
    widths = IN_WIDTHS + (out_refs[-1].shape[-1],)
    for off, width, o_ref in zip(IN_OFFSETS, widths, out_refs):
        o_ref[0] = _bdot(h, w_ref[:, off:off + width]).astype(o_ref.dtype)


def _in_call(xall, mod, w_cat, n_ctx_tiles):
    b, n, d = xall.shape
    tm = TOKEN_TILE
    gate_w = w_cat.shape[1] - IN_OFFSETS[-1]
    widths = IN_WIDTHS + (gate_w,)
    return pl.pallas_call(
        _in_kernel,
        grid=(b, n // tm),
        in_specs=[
            pl.BlockSpec((1, tm, d), lambda i, t: (i, t, 0)),
            pl.BlockSpec((1, 1, 6, d), lambda i, t: (i, jnp.where(t < n_ctx_tiles, 0, 1), 0, 0)),
            pl.BlockSpec(w_cat.shape, lambda i, t: (0, 0)),
        ],
        out_specs=[pl.BlockSpec((1, tm, w), lambda i, t: (i, t, 0)) for w in widths],
        out_shape=[jax.ShapeDtypeStruct((b, n, w), F32) for w in IN_WIDTHS]
        + [jax.ShapeDtypeStruct((b, n, gate_w), BF16)],
        compiler_params=_cparams(2),
    )(xall, mod, w_cat)


S5_LANE_GROUPS = 128 // S5_GROUP
S5_SCAN_GROUPS = 4
S5_PITCH_PAD = 8


def _s5_param_kernel(cre_ref, cim_ref, bre_ref, bim_ref, pr_ref, pi_ref, tz_ref, bc_ref, cc_ref, coef_ref):
    t = S5_CHUNK
    w = t * S5_GROUP
    nt = (((1,), (1,)), ((), ()))
    tz = None
    bcs, ccs, coefs = [], [], []
    for d in range(2):
        cre, cim = cre_ref[0, d, 0], cim_ref[0, d, 0]
        bre, bim = bre_ref[0, d, 0], bim_ref[0, d, 0]
        power = lambda k: (pr_ref[0, d, 0, k:k + 1, :], pi_ref[0, d, 0, k:k + 1, :])
        rt = []
        for k in range(t + 1):
            prk, pik = power(k)
            rt.append(jnp.concatenate([cre * prk - cim * pik, -(cre * pik + cim * prk)], axis=1))
        bt = jnp.concatenate([bre, bim], axis=1)
        zeros = jnp.zeros((S5_GROUP, w), F32)
        if d == 0:
            kt = lax.dot_general(bt, jnp.concatenate(rt[:t], axis=0), nt, precision=HIGHEST,
                                 preferred_element_type=F32)
            pad = jnp.concatenate([zeros, kt], axis=1)
            rows = [kt] + [pltpu.roll(pad, S5_GROUP * s, 1)[:, w:] for s in range(1, t)]
        else:
            kt = lax.dot_general(bt, jnp.concatenate(rt[t - 1::-1], axis=0), nt, precision=HIGHEST,
                                 preferred_element_type=F32)
            pad = jnp.concatenate([kt, zeros], axis=1)
            rows = [pltpu.roll(pad, 2 * w - S5_GROUP * (t - 1 - s), 1)[:, :w] for s in range(t - 1)] + [kt]
        tz_d = jnp.concatenate(rows, axis=0)
        tz = tz_d if tz is None else tz + tz_d
        bc_rows = []
        for s in range(t):
            prk, pik = power(t - 1 - s if d == 0 else s)
            br = bre * prk - bim * pik
            bi = bim * prk + bre * pik
            bc_rows.append(jnp.concatenate([br, bi, bi, br], axis=1))
        bcs.append(jnp.concatenate(bc_rows, axis=0))
        ccs.append(jnp.concatenate(rt[1:] if d == 0 else rt[t:0:-1], axis=0))
        er, ei = power(t)
        coefs += [jnp.concatenate([er, er], axis=1), jnp.concatenate([-ei, ei], axis=1),
                  jnp.concatenate([ei, -ei], axis=1)]
    tz_ref[0, 0] = tz.astype(BF16)
    bc_ref[0, 0] = jnp.concatenate(bcs, axis=1).astype(BF16)
    cc_ref[0, 0] = jnp.concatenate(ccs, axis=1).astype(BF16)
    coef_ref[0, 0] = jnp.concatenate(coefs + [jnp.zeros((2, 2 * S5_STATE), F32)], axis=0)


def _s5_param_call(p):
    t = S5_CHUNK
    f = lambda name: p[name].astype(F32)
    lam_re, lam_im = f('s5_lam_re'), f('s5_lam_im')
    depth = lam_re.shape[0]
    dt = jnp.exp(f('s5_log_dt'))[..., None]
    k = jnp.arange(t + 1, dtype=F32)[:, None]
    mag = jnp.exp((lam_re * dt)[..., None, :] * k)
    ang = (lam_im * dt)[..., None, :] * k
    pr, pi = mag * jnp.cos(ang), mag * jnp.sin(ang)
    ar, ai = pr[..., 1, :], pi[..., 1, :]
    den = lam_re * lam_re + lam_im * lam_im
    qr = (((ar - 1) * lam_re + ai * lam_im) / den)[..., None, :]
    qi = ((ai * lam_re - (ar - 1) * lam_im) / den)[..., None, :]
    b_re = jnp.swapaxes(f('s5_b_re'), -1, -2)
    b_im = jnp.swapaxes(f('s5_b_im'), -1, -2)
    bbr = qr * b_re - qi * b_im
    bbi = qr * b_im + qi * b_re
    g, hg, ps = S5_GROUPS, S5_GROUP, S5_STATE
    w = t * hg
    small = lambda rows: pl.BlockSpec((1, 2, 1, rows, ps), lambda i, j: (i, 0, j, 0, 0))
    out = lambda cols: pl.BlockSpec((1, 1, w, cols), lambda i, j: (i, j, 0, 0))
    return pl.pallas_call(
        _s5_param_kernel,
        grid=(depth, g),
        in_specs=[small(hg)] * 4 + [small(t + 1)] * 2,
        out_specs=[out(w), out(2 * w), out(w), pl.BlockSpec((1, 1, 8, 2 * ps), lambda i, j: (i, j, 0, 0))],
        out_shape=[jax.ShapeDtypeStruct((depth, g, w, w), BF16), jax.ShapeDtypeStruct((depth, g, w, 2 * w), BF16),
                   jax.ShapeDtypeStruct((depth, g, w, w), BF16), jax.ShapeDtypeStruct((depth, g, 8, 2 * ps), F32)],
        compiler_params=_cparams(2),
    )(f('s5_c_re'), f('s5_c_im'), bbr, bbi, pr, pi)


def _s5_kernel(u_ref, tz_ref, bc_ref, cc_ref, coef_ref, d_ref, y_ref, uy_ref, loc_ref, sp_ref, slab_ref,
               *, n_ctx, n_batch):
    ph, b = pl.program_id(1), pl.program_id(2)
    t, hg, ng = S5_CHUNK, S5_GROUP, S5_LANE_GROUPS
    ncc = n_ctx // t
    ncl = (u_ref.shape[1] - n_ctx) // t
    nc = ncc + ncl
    pitch = nc + S5_PITCH_PAD

    def to_chunk_rows(slabs):
        tr = [s.T for s in slabs]
        return [jnp.concatenate([x[g * hg:(g + 1) * hg] for x in tr], axis=0).T for g in range(ng)]

    def to_token_slabs(rows):
        tr = [r.T for r in rows]
        return [jnp.concatenate([x[tau * hg:(tau + 1) * hg] for x in tr], axis=0).T for tau in range(t)]

    base = pl.multiple_of(b * pitch, 8)
    cbase = pl.multiple_of(b * ncc, 8)

    @pl.when(ph == 0)
    def _():
        rows = to_chunk_rows([u_ref[0, pl.ds(n_ctx + tau, ncl, stride=t), :] for tau in range(t)])
        for g in range(ng):
            uy_ref[g, pl.ds(base + ncc, ncl), :] = rows[g]
            uy_ref[g, pl.ds(base + nc, S5_PITCH_PAD), :] = jnp.zeros((S5_PITCH_PAD, t * hg), F32)
        for tau in range(t):
            slab_ref[tau, pl.ds(cbase, ncc), :] = u_ref[0, pl.ds(tau, ncc, stride=t), :]

    @pl.when((ph == 1) & (b == 0))
    def _():
        rows = to_chunk_rows([slab_ref[tau] for tau in range(t)])
        for g in range(ng):
            for s in range(n_batch):
                uy_ref[g, s * pitch:s * pitch + ncc, :] = rows[g][s * ncc:(s + 1) * ncc]
        for part in range(ng // S5_SCAN_GROUPS):
            gs = [part * S5_SCAN_GROUPS + gl for gl in range(S5_SCAN_GROUPS)]
            for gl, g in enumerate(gs):
                ub = uy_ref[g].astype(BF16)
                loc = _bdot(ub, bc_ref[0, g])
                for q in range(4):
                    loc_ref[gl, q] = loc[:, q * 128:(q + 1) * 128]
                uy_ref[g] = _bdot(ub, tz_ref[0, g])
                for s in range(n_batch):
                    for d in range(2):
                        sp_ref[gl, d, s * pitch + nc:(s + 1) * pitch, :] = jnp.zeros((S5_PITCH_PAD, 128), F32)

            def coef(g, r):
                return jnp.broadcast_to(coef_ref[0, g, r:r + 1, :], (n_batch, 128))

            def step(i, carry):
                cb = jnp.where(i < ncc, ncc - 1 - i, nc + ncc - 1 - i)
                fwd = pl.ds(i, n_batch, stride=pitch)
                bwd = pl.ds(cb, n_batch, stride=pitch)
                out = []
                for gl, g in enumerate(gs):
                    v0f, v1f, v0b, v1b = carry[gl]
                    sp_ref[gl, 0, fwd, :] = v0f
                    sp_ref[gl, 1, bwd, :] = v0b
                    n0f = coef(g, 0) * v0f + coef(g, 1) * v1f + loc_ref[gl, 0, fwd, :]
                    n1f = coef(g, 0) * v1f + coef(g, 2) * v0f + loc_ref[gl, 1, fwd, :]
                    n0b = coef(g, 3) * v0b + coef(g, 4) * v1b + loc_ref[gl, 2, bwd, :]
                    n1b = coef(g, 3) * v1b + coef(g, 5) * v0b + loc_ref[gl, 3, bwd, :]
                    out.append((n0f, n1f, n0b, n1b))
                return tuple(out)

            z = jnp.zeros((n_batch, 128), F32)
            lax.fori_loop(0, nc, step, tuple((z, z, z, z) for _ in gs))
            for gl, g in enumerate(gs):
                sp = jnp.concatenate([sp_ref[gl, 0], sp_ref[gl, 1]], axis=1).astype(BF16)
                uy_ref[g] = uy_ref[g] + _dot_nt(sp, cc_ref[0, g])
        rows = [jnp.concatenate([uy_ref[g, s * pitch:s * pitch + ncc, :] for s in range(n_batch)], axis=0)
                for g in range(ng)]
        for tau, slab in enumerate(to_token_slabs(rows)):
            slab_ref[tau] = slab

    @pl.when(ph == 1)
    def _():
        slabs = to_token_slabs([uy_ref[g, pl.ds(base + ncc, ncl), :] for g in range(ng)])
        for tau in range(t):
            y_ref[0, pl.ds(n_ctx + tau, ncl, stride=t), :] = slabs[tau]
            y_ref[0, pl.ds(tau, ncc, stride=t), :] = slab_ref[tau, pl.ds(cbase, ncc), :]
        y_ref[0] = y_ref[0] + d_ref[0] * u_ref[0]


def _s5_call(layer, u, tz, bc, cc, coef, s5_d, n_ctx):
    b, n, width = u.shape
    t, ng = S5_CHUNK, S5_LANE_GROUPS
    rows = b * (n // t + S5_PITCH_PAD)
    assert (n - n_ctx) // t == 128 and b * (n_ctx // t) == 128
    wspec = lambda a: pl.BlockSpec((1, ng) + a.shape[2:], lambda g, ph, i: (layer, g, 0, 0))
    return pl.pallas_call(
        functools.partial(_s5_kernel, n_ctx=n_ctx, n_batch=b),
        grid=(width // 128, 2, b),
        in_specs=[pl.BlockSpec((1, n, 128), lambda g, ph, i: (i, 0, g)),
                  wspec(tz), wspec(bc), wspec(cc), wspec(coef),
                  pl.BlockSpec((1, 1, 128), lambda g, ph, i: (layer, 0, g))],
        out_specs=pl.BlockSpec((1, n, 128), lambda g, ph, i: (i * ph, 0, g)),
        out_shape=jax.ShapeDtypeStruct((b, n, width), F32),
        scratch_shapes=[pltpu.VMEM((ng, rows, t * S5_GROUP), F32),
                        pltpu.VMEM((S5_SCAN_GROUPS, 4, rows, 128), F32),
                        pltpu.VMEM((S5_SCAN_GROUPS, 2, rows, 128), F32),
                        pltpu.VMEM((t, 128, 128), F32)],
        compiler_params=_cparams(3),
    )(u, tz, bc, cc, coef, s5_d.astype(F32).reshape(s5_d.shape[0], 1, width))


def _rope_lanes(x, cos, sa, sb, shift):
    return x * cos + pltpu.roll(x, 128 - shift, 1) * sa + pltpu.roll(x, shift, 1) * sb


def _mla_prep_kernel(qa_ref, kva_ref, kr_ref, qg_ref, kvg_ref, wq_ref, wk_ref, wv_ref,
                     cos_ref, sa_ref, sb_ref, qp_ref, qr_ref, k_ref, v_ref):
    def rms(x, gain):
        return (x * lax.rsqrt(jnp.mean(x * x, axis=-1, keepdims=True) + LN_EPS) * gain).astype(BF16)

    cos, sa, sb = cos_ref[...], sa_ref[...], sb_ref[...]
    qn = rms(qa_ref[0], qg_ref[...])
    kvn = rms(kva_ref[0], kvg_ref[...])
    q = _bdot(qn, wq_ref[...]) * MLA_SCALE
    k = _bdot(kvn, wk_ref[...])
    v_ref[0] = _bdot(kvn, wv_ref[...]).astype(BF16)
    kr_rot = _rope_lanes(kr_ref[0], cos, sa, sb, MLA_ROPE // 2)
    qp_ref[0] = q.astype(BF16)
    for h in range(MLA_HEADS):
        sl = slice(h * MLA_PAD, (h + 1) * MLA_PAD)
        qr_ref[0, :, sl] = _rope_lanes(q[:, sl], cos, sa, sb, MLA_ROPE // 2).astype(BF16)
        k_ref[0, :, sl] = (k[:, sl] + kr_rot).astype(BF16)


def _mla_prep_call(qa, kva, kr, qg, kvg, wq, wk, wv, tabs):
    b, n, _ = qa.shape
    tm = TOKEN_TILE
    tok = lambda w: pl.BlockSpec((1, tm, w), lambda i, t: (i, t, 0))
    full = lambda a: pl.BlockSpec(a.shape, lambda i, t: (0,) * a.ndim)
    tab = pl.BlockSpec((tm, 128), lambda i, t: (t, 0))
    hw = MLA_HEADS * MLA_PAD
    return pl.pallas_call(
        _mla_prep_kernel,
        grid=(b, n // tm),
        in_specs=[tok(MLA_Q_RANK), tok(MLA_KV_RANK), tok(MLA_PAD), full(qg), full(kvg), full(wq), full(wk),
                  full(wv), tab, tab, tab],
        out_specs=[tok(hw), tok(hw), tok(hw), tok(MLA_HEADS * MLA_V)],
        out_shape=[jax.ShapeDtypeStruct((b, n, hw), BF16)] * 3
        + [jax.ShapeDtypeStruct((b, n, MLA_HEADS * MLA_V), BF16)],
        compiler_params=_cparams(2),
    )(qa, kva, kr, qg, kvg, wq, wk, wv, *tabs)


def _softmax_av(scores, values, extra=None):
    m = functools.reduce(jnp.maximum, [jnp.max(s, axis=-1, keepdims=True) for s in scores])
    if extra is not None:
        m = jnp.maximum(m, extra)
    ps = [jnp.exp(s - m) for s in scores]
    l = functools.reduce(jnp.add, [jnp.sum(p, axis=-1, keepdims=True) for p in ps])
    if extra is not None:
        l = l + jnp.exp(extra - m)
    o = functools.reduce(jnp.add, [_bdot(p.astype(BF16), v) for p, v in zip(ps, values)])
    return o / l


def _mla_attn_kernel(qp_ref, qr_ref, k_ref, v_ref, o_ref, *, n_ctx, n_ctx_tiles):
    heads = qp_ref.shape[-1] // MLA_PAD
    t = pl.program_id(2)
    lane = lax.broadcasted_iota(jnp.int32, (1, heads * MLA_V), 1)

    def run(latent):
        acc = None
        for h in range(heads):
            sl = slice(h * MLA_PAD, (h + 1) * MLA_PAD)
            own = (lane >= h * MLA_V) & (lane < (h + 1) * MLA_V)
            zero = jnp.zeros((), BF16)
            scores = [_dot_nt(qp_ref[0, :, sl], k_ref[0, :n_ctx, sl])]
            values = [jnp.where(own, v_ref[0, :n_ctx, :], zero)]
            if latent:
                scores.append(_dot_nt(qr_ref[0, :, sl], k_ref[0, n_ctx:, sl]))
                values.append(jnp.where(own, v_ref[0, n_ctx:, :], zero))
            o = _softmax_av(scores, values)
            acc = o if acc is None else acc + o
        o_ref[0] = acc.astype(o_ref.dtype)

    pl.when(t < n_ctx_tiles)(lambda: run(False))
    pl.when(t >= n_ctx_tiles)(lambda: run(True))


def _mla_attn_call(qp, qr, k, v, n_ctx):
    b, n, _ = qp.shape
    tq = TOKEN_TILE
    hp = 2
    qspec = pl.BlockSpec((1, tq, hp * MLA_PAD), lambda i, h, t: (i, t, h))
    return pl.pallas_call(
        functools.partial(_mla_attn_kernel, n_ctx=n_ctx, n_ctx_tiles=n_ctx // tq),
        grid=(b, MLA_HEADS // hp, n // tq),
        in_specs=[qspec, qspec,
                  pl.BlockSpec((1, n, hp * MLA_PAD), lambda i, h, t: (i, 0, h)),
                  pl.BlockSpec((1, n, hp * MLA_V), lambda i, h, t: (i, 0, h))],
        out_specs=pl.BlockSpec((1, tq, hp * MLA_V), lambda i, h, t: (i, t, h)),
        out_shape=jax.ShapeDtypeStruct((b, n, MLA_HEADS * MLA_V), BF16),
        compiler_params=_cparams(3),
    )(qp, qr, k, v)


def _win_kernel(sink_ref, q_ref, kp_ref, kc_ref, kn_ref, vp_ref, vc_ref, vn_ref, kctx_ref, vctx_ref,
                cq_ref, saq_ref, sbq_ref, cp_ref, sap_ref, sbp_ref, cn_ref, san_ref, sbn_ref, o_ref,
                *, n_ctx_blocks, n_blocks):
    j = pl.program_id(1)
    hd = WIN_HEAD_DIM
    half = hd // 2
    lane = lax.broadcasted_iota(jnp.int32, (1, 128), 1)
    lo = jnp.where(lane < hd, 1.0, 0.0)
    hi = 1.0 - lo

    def lane_halves(x):
        xr = pltpu.roll(x, hd, 1)
        return {(0, 0): (x * lo).astype(BF16), (0, 1): (xr * hi).astype(BF16),
                (1, 0): (xr * lo).astype(BF16), (1, 1): (x * hi).astype(BF16)}

    def attend(queries, keys, values, masks):
        for kh in range(WIN_KV_HEADS):
            stacked = [jnp.concatenate([qs[:, (2 * kh) * 128:(2 * kh + 1) * 128],
                                        qs[:, (2 * kh + 1) * 128:(2 * kh + 2) * 128]], axis=0).astype(BF16)
                       for qs in queries]
            acc = None
            for par in range(2):
                scores = []
                for qst, ks, msk in zip(stacked, keys, masks):
                    s = _dot_nt(qst, ks[(kh, par)])
                    scores.append(s if msk is None else jnp.where(msk, s, NEG_INF))
                sink = jnp.concatenate([jnp.full((BLOCK, 1), sink_ref[4 * kh + par], F32),
                                        jnp.full((BLOCK, 1), sink_ref[4 * kh + 2 + par], F32)], axis=0)
                o = _softmax_av(scores, [vs[(kh, par)] for vs in values], sink)
                acc = o if acc is None else acc + o
            o_ref[0, :, (2 * kh) * 128:(2 * kh + 1) * 128] = acc[:BLOCK].astype(o_ref.dtype)
            o_ref[0, :, (2 * kh + 1) * 128:(2 * kh + 2) * 128] = acc[BLOCK:].astype(o_ref.dtype)

    q = q_ref[0] * WIN_SCALE
    kctx = lane_halves(kctx_ref[0])
    vctx = lane_halves(vctx_ref[0])

    def ctx_path():
        attend([q], [kctx], [vctx], [None])

    def lat_path():
        q_rot = jnp.concatenate(
            [_rope_lanes(q[:, c * 128:(c + 1) * 128], cq_ref[...], saq_ref[...], sbq_ref[...], half)
             for c in range(q.shape[1] // 128)], axis=-1)
        kband = jnp.concatenate([
            _rope_lanes(kp_ref[0], cp_ref[...], sap_ref[...], sbp_ref[...], half),
            _rope_lanes(kc_ref[0], cq_ref[...], saq_ref[...], sbq_ref[...], half),
            _rope_lanes(kn_ref[0], cn_ref[...], san_ref[...], sbn_ref[...], half)], axis=0)
        vband = jnp.concatenate([vp_ref[0], vc_ref[0], vn_ref[0]], axis=0)
        blk = j - n_ctx_blocks
        r = lax.broadcasted_iota(jnp.int32, (2 * BLOCK, 3 * BLOCK), 0) % BLOCK
        c = lax.broadcasted_iota(jnp.int32, (2 * BLOCK, 3 * BLOCK), 1)
        first = jnp.where(blk > 0, 0, BLOCK)
        last = jnp.where(blk < n_blocks - 1, 3 * BLOCK, 2 * BLOCK)
        valid = (jnp.abs(c - BLOCK - r) <= WINDOW) & (c >= first) & (c < last)
        attend([q_rot, q], [lane_halves(kband), kctx], [lane_halves(vband), vctx], [valid, None])

    pl.when(j < n_ctx_blocks)(ctx_path)
    pl.when(j >= n_ctx_blocks)(lat_path)


def _win_call(sink, wq, wk, wv, tabs, n_ctx):
    b, n, _ = wq.shape
    ncb = n_ctx // BLOCK
    nb = n // BLOCK
    kvw = WIN_KV_HEADS * WIN_HEAD_DIM
    lo, hi = ncb, nb - 1
    prev = lambda i, j: (i, jnp.clip(j - 1, lo, hi), 0)
    cur = lambda i, j: (i, j, 0)
    nxt = lambda i, j: (i, jnp.clip(j + 1, lo, hi), 0)
    kv = lambda f: pl.BlockSpec((1, BLOCK, kvw), f)
    tab = lambda f: pl.BlockSpec((BLOCK, 128), lambda i, j: f(i, j)[1:])
    ctx = pl.BlockSpec((1, n_ctx, kvw), lambda i, j: (i, 0, 0))
    return pl.pallas_call(
        functools.partial(_win_kernel, n_ctx_blocks=ncb, n_blocks=nb - ncb),
        grid=(b, nb),
        in_specs=[pl.BlockSpec(memory_space=pltpu.SMEM),
                  pl.BlockSpec((1, BLOCK, WIN_Q_HEADS * WIN_HEAD_DIM), cur),
                  kv(prev), kv(cur), kv(nxt), kv(prev), kv(cur), kv(nxt), ctx, ctx,
                  tab(cur), tab(cur), tab(cur), tab(prev), tab(prev), tab(prev), tab(nxt), tab(nxt), tab(nxt)],
        out_specs=pl.BlockSpec((1, BLOCK, WIN_Q_HEADS * WIN_HEAD_DIM), cur),
        out_shape=jax.ShapeDtypeStruct((b, n, WIN_Q_HEADS * WIN_HEAD_DIM), BF16),
        compiler_params=_cparams(2),
    )(sink, wq, wk, wk, wk, wv, wv, wv, wk, wv, *tabs, *tabs, *tabs)


def _merge_kernel(x_ref, s5_ref, mla_ref, win_ref, gate_ref, mod_ref, wglu_ref, bglu_ref, wbr_ref, wout_ref,
                  g1_ref, b1_ref, wr_ref, x1_ref, h2_ref, lg_ref, *, alpha):
    d = x_ref.shape[-1]
    g = jax.nn.gelu(s5_ref[0])
    s5o = g * jax.nn.sigmoid(_bdot(g.astype(BF16), wglu_ref[...]) + bglu_ref[...])
    branches = (s5o.astype(BF16), mla_ref[0], win_ref[0])
    mix = None
    for kk, o in enumerate(branches):
        term = jax.nn.sigmoid(gate_ref[0, :, kk * d:(kk + 1) * d].astype(F32)) * _bdot(o, wbr_ref[kk])
        mix = term if mix is None else mix + term
    y = _bdot(mix.astype(BF16), wout_ref[...])
    mod = lambda r: mod_ref[0, 0, r:r + 1, :]
    x1 = _layer_norm(alpha * x_ref[0] + mod(2) * y) * g1_ref[...] + b1_ref[...]
    x1_ref[0] = x1
    h2 = (_layer_norm(x1) * (1.0 + mod(4)) + mod(3)).astype(BF16)
    h2_ref[0] = h2
    lg_ref[0] = _dot_nt(wr_ref[...], h2)


def _merge_call(xall, s5y, mla_o, win_o, gates, mod, wglu, bglu, wbr, wout, g1, b1, wr_t, n_ctx_tiles, alpha):
    b, n, d = xall.shape
    tm = TOKEN_TILE
    tok = lambda w: pl.BlockSpec((1, tm, w), lambda i, t: (i, t, 0))
    full = lambda a: pl.BlockSpec(a.shape, lambda i, t: (0,) * a.ndim)
    return pl.pallas_call(
        functools.partial(_merge_kernel, alpha=alpha),
        grid=(b, n // tm),
        in_specs=[tok(d), tok(BRANCH_WIDTH), tok(BRANCH_WIDTH), tok(BRANCH_WIDTH), tok(N_BRANCH * d),
                  pl.BlockSpec((1, 1, 6, d), lambda i, t: (i, jnp.where(t < n_ctx_tiles, 0, 1), 0, 0)),
                  full(wglu), full(bglu), full(wbr), full(wout), full(g1), full(b1), full(wr_t)],
        out_specs=[tok(d), tok(d), pl.BlockSpec((1, N_EXPERTS, tm), lambda i, t: (i, 0, t))],
        out_shape=[jax.ShapeDtypeStruct((b, n, d), F32), jax.ShapeDtypeStruct((b, n, d), BF16),
                   jax.ShapeDtypeStruct((b, N_EXPERTS, n), F32)],
        compiler_params=_cparams(2),
    )(xall, s5y, mla_o, win_o, gates, mod, wglu, bglu, wbr, wout, g1, b1, wr_t)


def _excl_cumsum_lanes(m):
    rows, n = m.shape
    r = lax.broadcasted_iota(jnp.int32, (128, 128), 0)
    c = lax.broadcasted_iota(jnp.int32, (128, 128), 1)
    tri = jnp.where(r < c, 1.0, 0.0).astype(BF16)
    off = jnp.zeros((rows, 1), F32)
    outs = []
    for jb in range(n // 128):
        blk = m[:, jb * 128:(jb + 1) * 128]
        outs.append(_bdot(blk.astype(BF16), tri) + off)
        off = off + jnp.sum(blk, axis=1, keepdims=True)
    return jnp.concatenate(outs, axis=1)


def _topk_slots(aff, cap):
    bits = pltpu.bitcast(aff, jnp.int32)

    def body(i, thr):
        cand = thr | (jnp.int32(1) << (30 - i))
        cnt = jnp.sum(jnp.where(bits >= cand, 1.0, 0.0), axis=1, keepdims=True)
        return jnp.where(cnt >= cap, cand, thr)

    thr = lax.fori_loop(0, 31, body, jnp.zeros((aff.shape[0], 1), jnp.int32))
    gt = jnp.where(bits > thr, 1.0, 0.0)
    eq = jnp.where(bits == thr, 1.0, 0.0)
    need = cap - jnp.sum(gt, axis=1, keepdims=True)
    sel = gt + eq * jnp.where(_excl_cumsum_lanes(eq) < need, 1.0, 0.0)
    rank = _excl_cumsum_lanes(sel)
    return jnp.where(sel > 0.5, rank, -1.0).astype(jnp.int32)


def _route_kernel(lg_ref, slot_ref, aff_ref, *, n_ctx, cap_ctx, cap_lat):
    lg = lg_ref[0]
    m = jnp.max(lg, axis=0, keepdims=True)
    ex = jnp.exp(lg - m)
    aff = ex / jnp.sum(ex, axis=0, keepdims=True)
    aff_ref[0] = aff
    slot_ref[0, :, :n_ctx] = _topk_slots(aff[:, :n_ctx], cap_ctx)
    slot_ref[0, :, n_ctx:] = _topk_slots(aff[:, n_ctx:], cap_lat)


def _route_call(logits_t, n_ctx, cap_ctx, cap_lat):
    b, e, n = logits_t.shape
    spec = pl.BlockSpec((1, e, n), lambda i: (i, 0, 0))
    return pl.pallas_call(
        functools.partial(_route_kernel, n_ctx=n_ctx, cap_ctx=cap_ctx, cap_lat=cap_lat),
        grid=(b,),
        in_specs=[spec],
        out_specs=[spec, spec],
        out_shape=[jax.ShapeDtypeStruct((b, e, n), jnp.int32), jax.ShapeDtypeStruct((b, e, n), F32)],
        compiler_params=_cparams(1),
    )(logits_t)


def _expert_kernel(slot_ref, aff_ref, h_ref, wg_ref, wu_ref, wd_ref, yl_ref, yc_ref, wg_s, wu_s, wd_s, *, n_ctx):
    cap_lat = yl_ref.shape[2]
    cap_ctx = yc_ref.shape[2]
    slot = slot_ref[0, 0]
    aff = aff_ref[0, 0]

    @pl.when(pl.program_id(1) == 0)
    def _():
        wg_s[...] = wg_ref[0, 0].astype(BF16)
        wu_s[...] = wu_ref[0, 0].astype(BF16)
        wd_s[...] = wd_ref[0, 0].astype(BF16)

    def pick(lo, hi, cap):
        iota = lax.broadcasted_iota(jnp.int32, (cap, hi - lo), 0)
        mask = slot[:, lo:hi] == iota
        gate = jnp.sum(jnp.where(mask, aff[:, lo:hi], 0.0), axis=1, keepdims=True)
        onehot = jnp.where(mask, 1.0, 0.0).astype(BF16)
        return _bdot(onehot, h_ref[0, lo:hi, :]), gate

    n = h_ref.shape[1]
    xl, gl = pick(n_ctx, n, cap_lat)
    xc, gc = pick(0, n_ctx, cap_ctx)
    xs = jnp.concatenate([xl, xc], axis=0).astype(BF16)
    gate = jnp.concatenate([gl, gc], axis=0)
    a = _bdot(xs, wg_s[...])
    u = _bdot(xs, wu_s[...])
    hm = (a * jax.nn.sigmoid(a) * u).astype(BF16)
    y = _bdot(hm, wd_s[...]) * gate
    yl_ref[0, 0] = y[:cap_lat].astype(yl_ref.dtype)
    yc_ref[0, 0] = y[cap_lat:].astype(yc_ref.dtype)


def _expert_call(layer, slot, aff, h2, w_gate, w_up, w_down, n_ctx, cap_ctx, cap_lat):
    b, e, n = slot.shape
    d = h2.shape[-1]
    f = w_gate.shape[-1]
    row = pl.BlockSpec((1, 1, 1, n), lambda ie, ib: (ib, ie, 0, 0))
    return pl.pallas_call(
        functools.partial(_expert_kernel, n_ctx=n_ctx),
        grid=(e, b),
        in_specs=[row, row,
                  pl.BlockSpec((1, n, d), lambda ie, ib: (ib, 0, 0)),
                  pl.BlockSpec((1, 1, d, f), lambda ie, ib: (layer, ie, 0, 0)),
                  pl.BlockSpec((1, 1, d, f), lambda ie, ib: (layer, ie, 0, 0)),
                  pl.BlockSpec((1, 1, f, d), lambda ie, ib: (layer, ie, 0, 0))],
        out_specs=[pl.BlockSpec((1, 1, cap_lat, d), lambda ie, ib: (ib, ie, 0, 0)),
                   pl.BlockSpec((1, 1, cap_ctx, d), lambda ie, ib: (ib, ie, 0, 0))],
        out_shape=[jax.ShapeDtypeStruct((b, e, cap_lat, d), BF16), jax.ShapeDtypeStruct((b, e, cap_ctx, d), BF16)],
        scratch_shapes=[pltpu.VMEM((d, f), BF16), pltpu.VMEM((d, f), BF16), pltpu.VMEM((f, d), BF16)],
        compiler_params=_cparams(2),
    )(slot.reshape(b, e, 1, n), aff.reshape(b, e, 1, n), h2, w_gate, w_up, w_down)


def _combine_kernel(slot_ref, yl_ref, yc_ref, x1_ref, mod_ref, g2_ref, b2_ref, o_ref, *, n_ctx_tiles, alpha):
    t = pl.program_id(1)
    tm = x1_ref.shape[1]
    slot = slot_ref[0]

    def finish(fl):
        x1 = x1_ref[0]
        o_ref[0] = _layer_norm(alpha * x1 + mod_ref[0, 0, 5:6, :] * fl) * g2_ref[...] + b2_ref[...]

    def onehot(e, cap):
        iota = lax.broadcasted_iota(jnp.int32, (tm, cap), 1)
        return jnp.where(slot[:, e:e + 1] == iota, 1.0, 0.0).astype(BF16)

    def ctx_path():
        cap = yc_ref.shape[2]
        fl = None
        for e in range(N_EXPERTS):
            term = _bdot(onehot(e, cap), yc_ref[0, e])
            fl = term if fl is None else fl + term
        finish(fl)

    def lat_path():
        cap = yl_ref.shape[2]
        pt = jnp.concatenate([onehot(e, cap) for e in range(N_EXPERTS)], axis=1)
        finish(_bdot(pt, yl_ref[0].reshape(N_EXPERTS * cap, yl_ref.shape[3])))

    pl.when(t < n_ctx_tiles)(ctx_path)
    pl.when(t >= n_ctx_tiles)(lat_path)


def _combine_call(slot_t, yl, yc, x1, mod, g2, b2, n_ctx_tiles, alpha):
    b, n, d = x1.shape
    tm = TOKEN_TILE
    e = N_EXPERTS
    full = lambda a: pl.BlockSpec(a.shape, lambda i, t: (0,) * a.ndim)
    return pl.pallas_call(
        functools.partial(_combine_kernel, n_ctx_tiles=n_ctx_tiles, alpha=alpha),
        grid=(b, n // tm),
        in_specs=[pl.BlockSpec((1, tm, e), lambda i, t: (i, t, 0)),
                  pl.BlockSpec((1,) + yl.shape[1:], lambda i, t: (i, 0, 0, 0)),
                  pl.BlockSpec((1,) + yc.shape[1:], lambda i, t: (i, 0, 0, 0)),
                  pl.BlockSpec((1, tm, d), lambda i, t: (i, t, 0)),
                  pl.BlockSpec((1, 1, 6, d), lambda i, t: (i, jnp.where(t < n_ctx_tiles, 0, 1), 0, 0)),
                  full(g2), full(b2)],
        out_specs=pl.BlockSpec((1, tm, d), lambda i, t: (i, t, 0)),
        out_shape=jax.ShapeDtypeStruct((b, n, d), F32),
        compiler_params=_cparams(2),
    )(slot_t, yl, yc, x1, mod, g2, b2)


def _rope_tables(n_ctx, seq, head_dim, lane_offset):
    half = head_dim // 2
    nf = head_dim // 4
    t = jnp.arange(seq, dtype=F32)
    row = jnp.floor(t / GRID_W)
    col = t - row * GRID_W
    freqs = ROPE_BASE ** (-jnp.arange(nf, dtype=F32) / nf)
    ang = jnp.concatenate([row[:, None] * freqs, col[:, None] * freqs], axis=-1)
    cos, sin = jnp.cos(ang), jnp.sin(ang)
    zeros = jnp.zeros_like(sin)
    n_heads = (128 - lane_offset) // head_dim if lane_offset == 0 else 1
    c = jnp.concatenate([jnp.ones((seq, lane_offset), F32)] + [cos, cos] * n_heads, axis=-1)
    sa = jnp.concatenate([jnp.zeros((seq, lane_offset), F32)] + [-sin, zeros] * n_heads, axis=-1)
    sb = jnp.concatenate([jnp.zeros((seq, lane_offset), F32)] + [zeros, sin] * n_heads, axis=-1)
    pad = 128 - c.shape[1]
    c = jnp.pad(c, ((n_ctx, 0), (0, pad)), constant_values=1.0)
    sa = jnp.pad(sa, ((n_ctx, 0), (0, pad)))
    sb = jnp.pad(sb, ((n_ctx, 0), (0, pad)))
    return c, sa, sb


def _layer_weights(i, p):
    d = p['w_in'].shape[1]
    pts = np.cumsum((S5_WIDTH, MLA_Q_RANK, MLA_KV_RANK, MLA_ROPE, WIN_Q_HEADS * WIN_HEAD_DIM,
                     WIN_KV_HEADS * WIN_HEAD_DIM, WIN_KV_HEADS * WIN_HEAD_DIM))
    cols = jnp.split(p['w_in'][i], [int(v) for v in pts], axis=1)
    kr = jnp.pad(cols[3], ((0, 0), (MLA_NOPE, MLA_PAD - MLA_NOPE - MLA_ROPE)))
    w_cat = jnp.concatenate([cols[0], cols[1], cols[2], kr, cols[4], cols[5], cols[6], cols[7]], axis=1)
    dq = MLA_NOPE + MLA_ROPE
    wq = p['mla_w_uq'][i].reshape(MLA_Q_RANK, MLA_HEADS, dq)
    wq = jnp.pad(wq, ((0, 0), (0, 0), (0, MLA_PAD - dq))).reshape(MLA_Q_RANK, MLA_HEADS * MLA_PAD)
    wkv = p['mla_w_ukv'][i].reshape(MLA_KV_RANK, MLA_HEADS, MLA_NOPE + MLA_V)
    wk = jnp.pad(wkv[:, :, :MLA_NOPE], ((0, 0), (0, 0), (0, MLA_PAD - MLA_NOPE)))
    wk = wk.reshape(MLA_KV_RANK, MLA_HEADS * MLA_PAD)
    wv = wkv[:, :, MLA_NOPE:].reshape(MLA_KV_RANK, MLA_HEADS * MLA_V)
    row = lambda a: a[i].astype(F32).reshape(1, -1)
    return dict(
        w_cat=w_cat.astype(BF16), wq=wq.astype(BF16), wk=wk.astype(BF16), wv=wv.astype(BF16),
        qg=row(p['mla_q_norm']), kvg=row(p['mla_kv_norm']),
        wglu=p['s5_w_glu'][i].astype(BF16), bglu=row(p['s5_b_glu']),
        sink=p['win_sink'][i].astype(F32),
        wbr=p['w_branch'][i].astype(BF16), wout=p['w_out'][i].astype(BF16),
        g1=row(p['ln1_g']), b1=row(p['ln1_b']), g2=row(p['ln2_g']), b2=row(p['ln2_b']),
        wr_t=p['w_router'][i].T.astype(BF16),
    )


def _forward(p):
    x, c, ctx, c_ctx = p['x'], p['c'], p['ctx'], p['c_ctx']
    b, seq, d = x.shape
    n_ctx = ctx.shape[1]
    depth = p['w_ada'].shape[0]
    assert b == 8 and seq % TOKEN_TILE == 0 and n_ctx % TOKEN_TILE == 0 and seq % GRID_W == 0
    alpha = float((2 * depth) ** 0.25)
    n_ctx_tiles = n_ctx // TOKEN_TILE
    cap_lat = CAPACITY_FACTOR * seq // N_EXPERTS
    cap_ctx = CAPACITY_FACTOR * n_ctx // N_EXPERTS

    cond = jnp.concatenate([c, c_ctx[None], jnp.zeros((16 - b - 1, d), F32)], axis=0)
    mods = _ada_call(cond, p['w_ada'], p['b_ada'])
    mods = mods.reshape(depth, 16, 6, d)
    tabs_mla = _rope_tables(n_ctx, seq, MLA_ROPE, MLA_NOPE)
    tabs_win = _rope_tables(n_ctx, seq, WIN_HEAD_DIM, 0)
    s5w = _s5_param_call(p)

    xall = jnp.concatenate([ctx, x], axis=1)
    for i in range(depth):
        w = _layer_weights(i, p)
        mod = jnp.stack([jnp.broadcast_to(mods[i, b], (b, 6, d)), mods[i, :b]], axis=1)
        u, qa, kva, kr, wq, wk, wv, gates = _in_call(xall, mod, w['w_cat'], n_ctx_tiles)
        s5y = _s5_call(i, u, *s5w, p['s5_d'], n_ctx)
        qp, qr, kk, vv = _mla_prep_call(qa, kva, kr, w['qg'], w['kvg'], w['wq'], w['wk'], w['wv'], tabs_mla)
        mla_o = _mla_attn_call(qp, qr, kk, vv, n_ctx)
        win_o = _win_call(w['sink'], wq, wk, wv, tabs_win, n_ctx)
        x1, h2, logits_t = _merge_call(xall, s5y, mla_o, win_o, gates, mod, w['wglu'], w['bglu'], w['wbr'],
                                       w['wout'], w['g1'], w['b1'], w['wr_t'], n_ctx_tiles, alpha)
        slot, aff = _route_call(logits_t, n_ctx, cap_ctx, cap_lat)
        yl, yc = _expert_call(i, slot, aff, h2, p['w_gate'], p['w_up'], p['w_down'], n_ctx, cap_ctx, cap_lat)
        slot_t = jnp.swapaxes(slot, 1, 2)
        xall = _combine_call(slot_t, yl, yc, x1, mod, w['g2'], w['b2'], n_ctx_tiles, alpha)
    return xall[:, n_ctx:]


def kernel(x, c, ctx, c_ctx, w_ada, b_ada, w_in, s5_lam_re, s5_lam_im, s5_log_dt, s5_b_re, s5_b_im, s5_c_re, s5_c_im, s5_d, s5_w_glu, s5_b_glu, mla_q_norm, mla_w_uq, mla_kv_norm, mla_w_ukv, win_sink, w_branch, w_out, ln1_g, ln1_b, ln2_g, ln2_b, w_router, w_gate, w_up, w_down):
    return _forward(dict(
        x=x, c=c, ctx=ctx, c_ctx=c_ctx, w_ada=w_ada, b_ada=b_ada, w_in=w_in, s5_lam_re=s5_lam_re,
        s5_lam_im=s5_lam_im, s5_log_dt=s5_log_dt, s5_b_re=s5_b_re, s5_b_im=s5_b_im, s5_c_re=s5_c_re,
        s5_c_im=s5_c_im, s5_d=s5_d, s5_w_glu=s5_w_glu, s5_b_glu=s5_b_glu, mla_q_norm=mla_q_norm,
        mla_w_uq=mla_w_uq, mla_kv_norm=mla_kv_norm, mla_w_ukv=mla_w_ukv, win_sink=win_sink, w_branch=w_branch,
        w_out=w_out, ln1_g=ln1_g, ln1_b=ln1_b, ln2_g=ln2_g, ln2_b=ln2_b, w_router=w_router, w_gate=w_gate,
        w_up=w_up, w_down=w_down))
```

```python
import functools
import math

import jax
import jax.numpy as jnp
import numpy as np
from jax import lax
from jax.experimental import pallas as pl
from jax.experimental.pallas import tpu as pltpu

F32 = jnp.float32
BF16 = jnp.bfloat16
HIGHEST = lax.Precision.HIGHEST

GRID_W = 64
S5_WIDTH = 512
S5_GROUP = 16
S5_GROUPS = S5_WIDTH // S5_GROUP
S5_STATE = 64
S5_CHUNK = 16
S5_GROUPS_PER_STEP = 4
MLA_HEADS = 8
MLA_NOPE = 64
MLA_ROPE = 32
MLA_V = 64
MLA_Q_RANK = 384
MLA_KV_RANK = 256
MLA_PAD = 128
MLA_HEADS_PER_STEP = 4
MLA_SCALE = (MLA_NOPE + MLA_ROPE) ** -0.5
WIN_Q_HEADS = 8
WIN_KV_HEADS = 2
WIN_GROUP = WIN_Q_HEADS // WIN_KV_HEADS
WIN_HEAD_DIM = 64
WINDOW = 128
BLOCK = 128
WIN_SCALE = WIN_HEAD_DIM ** -0.5
N_BRANCH = 3
BRANCH_WIDTH = 512
N_EXPERTS = 16
CAPACITY_FACTOR = 2
ROPE_BASE = 10000.0
LN_EPS = 1e-6
NEG_INF = -1e30
TOKEN_TILE = 256
IN_TILE = 768
VMEM_LIMIT = 56 * 1024 * 1024


def _cparams(n_axes):
    return pltpu.CompilerParams(dimension_semantics=("arbitrary",) * n_axes, vmem_limit_bytes=VMEM_LIMIT)


def _bdot(a, b):
    return jnp.dot(a, b, preferred_element_type=F32)


def _dot_nt(a, b):
    return lax.dot_general(a, b, (((1,), (1,)), ((), ())), preferred_element_type=F32)


def _layer_norm(x):
    mu = jnp.mean(x, axis=-1, keepdims=True)
    xc = x - mu
    var = jnp.mean(xc * xc, axis=-1, keepdims=True)
    return xc * lax.rsqrt(var + LN_EPS)


def _ada_kernel(cond_ref, w_ref, b_ref, o_ref):
    s = cond_ref[...]
    s = s * jax.nn.sigmoid(s)
    o_ref[0] = jnp.dot(s, w_ref[0], precision=HIGHEST, preferred_element_type=F32) + b_ref[0]


def _ada_call(cond, w_ada, b_ada):
    depth, d, d6 = w_ada.shape
    tn = 1536
    rows = cond.shape[0]
    return pl.pallas_call(
        _ada_kernel,
        grid=(depth, d6 // tn),
        in_specs=[
            pl.BlockSpec((rows, d), lambda i, j: (0, 0)),
            pl.BlockSpec((1, d, tn), lambda i, j: (i, 0, j)),
            pl.BlockSpec((1, 1, tn), lambda i, j: (i, 0, j)),
        ],
        out_specs=pl.BlockSpec((1, rows, tn), lambda i, j: (i, 0, j)),
        out_shape=jax.ShapeDtypeStruct((depth, rows, d6), F32),
        compiler_params=_cparams(2),
    )(cond, w_ada, b_ada.reshape(depth, 1, d6))


IN_WIDTHS = (S5_WIDTH, MLA_Q_RANK, MLA_KV_RANK, MLA_PAD, WIN_Q_HEADS * WIN_HEAD_DIM,
             WIN_KV_HEADS * WIN_HEAD_DIM, WIN_KV_HEADS * WIN_HEAD_DIM)
IN_OFFSETS = tuple(int(v) for v in np.cumsum((0,) + IN_WIDTHS))


def _mod_rows(mod_ref, r, tile, n_ctx):
    row = pl.program_id(1) * tile + lax.broadcasted_iota(jnp.int32, (tile, 1), 0)
    return jnp.where(row < n_ctx, mod_ref[0, 0, r:r + 1, :], mod_ref[0, 1, r:r + 1, :])


def _in_kernel(x_ref, mod_ref, w_ref, *out_refs, n_ctx):
    tile = x_ref.shape[1]
    xn = _layer_norm(x_ref[0])
    h = (xn * (1.0 + _mod_rows(mod_ref, 1, tile, n_ctx)) + _mod_rows(mod_ref, 0, tile, n_ctx)).astype(BF16)
    widths = IN_WIDTHS + (out_refs[-1].shape[-1],)
    for off, width, o_ref in zip(IN_OFFSETS, widths, out_refs):
        o_ref[0] = _bdot(h, w_ref[:, off:off + width]).astype(o_ref.dtype)


_sigmoid = jax.nn.sigmoid


def _in_call(xall, mod, w_cat, n_ctx):
    b, n, d = xall.shape
    tm = IN_TILE
    gate_w = w_cat.shape[1] - IN_OFFSETS[-1]
    widths = IN_WIDTHS + (gate_w,)
    return pl.pallas_call(
        functools.partial(_in_kernel, n_ctx=n_ctx),
        grid=(b, n // tm),
        in_specs=[
            pl.BlockSpec((1, tm, d), lambda i, t: (i, t, 0)),
            pl.BlockSpec((1, 2, 6, d), lambda i, t: (i, 0, 0, 0)),
            pl.BlockSpec(w_cat.shape, lambda i, t: (0, 0), pipeline_mode=pl.Buffered(1)),
        ],
        out_specs=[pl.BlockSpec((1, tm, w), lambda i, t: (i, t, 0)) for w in widths],
        out_shape=[jax.ShapeDtypeStruct((b, n, w), F32) for w in IN_WIDTHS]
        + [jax.ShapeDtypeStruct((b, n, gate_w), BF16)],
        compiler_params=_cparams(2),
    )(xall, mod, w_cat)


S5_LANE_GROUPS = 128 // S5_GROUP
S5_SCAN_GROUPS = 4
S5_PITCH_PAD = 8


def _s5_param_kernel(cre_ref, cim_ref, bre_ref, bim_ref, pr_ref, pi_ref, tz_ref, bc_ref, cc_ref, coef_ref):
    t = S5_CHUNK
    w = t * S5_GROUP
    nt = (((1,), (1,)), ((), ()))
    tz = None
    bcs, ccs, coefs = [], [], []
    for d in range(2):
        cre, cim = cre_ref[0, d, 0], cim_ref[0, d, 0]
        bre, bim = bre_ref[0, d, 0], bim_ref[0, d, 0]
        power = lambda k: (pr_ref[0, d, 0, k:k + 1, :], pi_ref[0, d, 0, k:k + 1, :])
        rt = []
        for k in range(t + 1):
            prk, pik = power(k)
            rt.append(jnp.concatenate([cre * prk - cim * pik, -(cre * pik + cim * prk)], axis=1))
        bt = jnp.concatenate([bre, bim], axis=1)
        zeros = jnp.zeros((S5_GROUP, w), F32)
        if d == 0:
            kt = lax.dot_general(bt, jnp.concatenate(rt[:t], axis=0), nt, precision=HIGHEST,
                                 preferred_element_type=F32)
            pad = jnp.concatenate([zeros, kt], axis=1)
            rows = [kt] + [pltpu.roll(pad, S5_GROUP * s, 1)[:, w:] for s in range(1, t)]
        else:
            kt = lax.dot_general(bt, jnp.concatenate(rt[t - 1::-1], axis=0), nt, precision=HIGHEST,
                                 preferred_element_type=F32)
            pad = jnp.concatenate([kt, zeros], axis=1)
            rows = [pltpu.roll(pad, 2 * w - S5_GROUP * (t - 1 - s), 1)[:, :w] for s in range(t - 1)] + [kt]
        tz_d = jnp.concatenate(rows, axis=0)
        tz = tz_d if tz is None else tz + tz_d
        bc_rows = []
        for s in range(t):
            prk, pik = power(t - 1 - s if d == 0 else s)
            br = bre * prk - bim * pik
            bi = bim * prk + bre * pik
            bc_rows.append(jnp.concatenate([br, bi, bi, br], axis=1))
        bcs.append(jnp.concatenate(bc_rows, axis=0))
        ccs.append(jnp.concatenate(rt[1:] if d == 0 else rt[t:0:-1], axis=0))
        er, ei = power(t)
        coefs += [jnp.concatenate([er, er], axis=1), jnp.concatenate([-ei, ei], axis=1),
                  jnp.concatenate([ei, -ei], axis=1)]
    tz_ref[0, 0] = tz.astype(BF16)
    bc_ref[0, 0] = jnp.concatenate(bcs, axis=1).astype(BF16)
    cc_ref[0, 0] = jnp.concatenate(ccs, axis=1).astype(BF16)
    coef_ref[0, 0] = jnp.concatenate(coefs + [jnp.zeros((2, 2 * S5_STATE), F32)], axis=0)


def _s5_param_call(p):
    t = S5_CHUNK
    f = lambda name: p[name].astype(F32)
    lam_re, lam_im = f('s5_lam_re'), f('s5_lam_im')
    depth = lam_re.shape[0]
    dt = jnp.exp(f('s5_log_dt'))[..., None]
    k = jnp.arange(t + 1, dtype=F32)[:, None]
    mag = jnp.exp((lam_re * dt)[..., None, :] * k)
    ang = (lam_im * dt)[..., None, :] * k
    pr, pi = mag * jnp.cos(ang), mag * jnp.sin(ang)
    ar, ai = pr[..., 1, :], pi[..., 1, :]
    den = lam_re * lam_re + lam_im * lam_im
    qr = (((ar - 1) * lam_re + ai * lam_im) / den)[..., None, :]
    qi = ((ai * lam_re - (ar - 1) * lam_im) / den)[..., None, :]
    b_re = jnp.swapaxes(f('s5_b_re'), -1, -2)
    b_im = jnp.swapaxes(f('s5_b_im'), -1, -2)
    bbr = qr * b_re - qi * b_im
    bbi = qr * b_im + qi * b_re
    g, hg, ps = S5_GROUPS, S5_GROUP, S5_STATE
    w = t * hg
    small = lambda rows: pl.BlockSpec((1, 2, 1, rows, ps), lambda i, j: (i, 0, j, 0, 0))
    out = lambda cols: pl.BlockSpec((1, 1, w, cols), lambda i, j: (i, j, 0, 0))
    return pl.pallas_call(
        _s5_param_kernel,
        grid=(depth, g),
        in_specs=[small(hg)] * 4 + [small(t + 1)] * 2,
        out_specs=[out(w), out(2 * w), out(w), pl.BlockSpec((1, 1, 8, 2 * ps), lambda i, j: (i, j, 0, 0))],
        out_shape=[jax.ShapeDtypeStruct((depth, g, w, w), BF16), jax.ShapeDtypeStruct((depth, g, w, 2 * w), BF16),
                   jax.ShapeDtypeStruct((depth, g, w, w), BF16), jax.ShapeDtypeStruct((depth, g, 8, 2 * ps), F32)],
        compiler_params=_cparams(2),
    )(f('s5_c_re'), f('s5_c_im'), bbr, bbi, pr, pi)


def _s5_kernel(u_ref, tz_ref, bc_ref, cc_ref, coef_ref, d_ref, y_ref, uy_ref, loc_ref, sp_ref, slab_ref,
               *, n_ctx, n_batch):
    ph, b = pl.program_id(1), pl.program_id(2)
    t, hg, ng = S5_CHUNK, S5_GROUP, S5_LANE_GROUPS
    ncc = n_ctx // t
    ncl = (u_ref.shape[1] - n_ctx) // t
    nc = ncc + ncl
    pitch = nc + S5_PITCH_PAD

    def to_chunk_rows(slabs):
        tr = [s.T for s in slabs]
        return [jnp.concatenate([x[g * hg:(g + 1) * hg] for x in tr], axis=0).T for g in range(ng)]

    def to_token_slabs(rows):
        tr = [r.T for r in rows]
        return [jnp.concatenate([x[tau * hg:(tau + 1) * hg] for x in tr], axis=0).T for tau in range(t)]

    base = pl.multiple_of(b * pitch, 8)
    cbase = pl.multiple_of(b * ncc, 8)

    @pl.when(ph == 0)
    def _():
        rows = to_chunk_rows([u_ref[0, pl.ds(n_ctx + tau, ncl, stride=t), :] for tau in range(t)])
        for g in range(ng):
            uy_ref[g, pl.ds(base + ncc, ncl), :] = rows[g]
            uy_ref[g, pl.ds(base + nc, S5_PITCH_PAD), :] = jnp.zeros((S5_PITCH_PAD, t * hg), F32)
        for tau in range(t):
            slab_ref[tau, pl.ds(cbase, ncc), :] = u_ref[0, pl.ds(tau, ncc, stride=t), :]

    @pl.when((ph == 1) & (b == 0))
    def _():
        rows = to_chunk_rows([slab_ref[tau] for tau in range(t)])
        for g in range(ng):
            for s in range(n_batch):
                uy_ref[g, s * pitch:s * pitch + ncc, :] = rows[g][s * ncc:(s + 1) * ncc]
        for part in range(ng // S5_SCAN_GROUPS):
            gs = [part * S5_SCAN_GROUPS + gl for gl in range(S5_SCAN_GROUPS)]
            for gl, g in enumerate(gs):
                ub = uy_ref[g].astype(BF16)
                loc = _bdot(ub, bc_ref[0, g])
                for q in range(4):
                    loc_ref[gl, q] = loc[:, q * 128:(q + 1) * 128]
                uy_ref[g] = _bdot(ub, tz_ref[0, g])
                for s in range(n_batch):
                    for d in range(2):
                        sp_ref[gl, d, s * pitch + nc:(s + 1) * pitch, :] = jnp.zeros((S5_PITCH_PAD, 128), F32)

            def coef(g, r):
                return jnp.broadcast_to(coef_ref[0, g, r:r + 1, :], (n_batch, 128))

            def step(i, carry):
                cb = jnp.where(i < ncc, ncc - 1 - i, nc + ncc - 1 - i)
                fwd = pl.ds(i, n_batch, stride=pitch)
                bwd = pl.ds(cb, n_batch, stride=pitch)
                out = []
                for gl, g in enumerate(gs):
                    v0f, v1f, v0b, v1b = carry[gl]
                    sp_ref[gl, 0, fwd, :] = v0f
                    sp_ref[gl, 1, bwd, :] = v0b
                    n0f = coef(g, 0) * v0f + coef(g, 1) * v1f + loc_ref[gl, 0, fwd, :]
                    n1f = coef(g, 0) * v1f + coef(g, 2) * v0f + loc_ref[gl, 1, fwd, :]
                    n0b = coef(g, 3) * v0b + coef(g, 4) * v1b + loc_ref[gl, 2, bwd, :]
                    n1b = coef(g, 3) * v1b + coef(g, 5) * v0b + loc_ref[gl, 3, bwd, :]
                    out.append((n0f, n1f, n0b, n1b))
                return tuple(out)

            z = jnp.zeros((n_batch, 128), F32)
            lax.fori_loop(0, nc, step, tuple((z, z, z, z) for _ in gs))
            for gl, g in enumerate(gs):
                sp = jnp.concatenate([sp_ref[gl, 0], sp_ref[gl, 1]], axis=1).astype(BF16)
                uy_ref[g] = uy_ref[g] + _dot_nt(sp, cc_ref[0, g])
        rows = [jnp.concatenate([uy_ref[g, s * pitch:s * pitch + ncc, :] for s in range(n_batch)], axis=0)
                for g in range(ng)]
        for tau, slab in enumerate(to_token_slabs(rows)):
            slab_ref[tau] = slab

    @pl.when(ph == 1)
    def _():
        slabs = to_token_slabs([uy_ref[g, pl.ds(base + ncc, ncl), :] for g in range(ng)])
        for tau in range(t):
            y_ref[0, pl.ds(n_ctx + tau, ncl, stride=t), :] = slabs[tau]
            y_ref[0, pl.ds(tau, ncc, stride=t), :] = slab_ref[tau, pl.ds(cbase, ncc), :]
        y_ref[0] = y_ref[0] + d_ref[0] * u_ref[0]


def _s5_call(layer, u, tz, bc, cc, coef, s5_d, n_ctx):
    b, n, width = u.shape
    t, ng = S5_CHUNK, S5_LANE_GROUPS
    rows = b * (n // t + S5_PITCH_PAD)
    assert (n - n_ctx) // t == 128 and b * (n_ctx // t) == 128
    wspec = lambda a: pl.BlockSpec((1, ng) + a.shape[2:], lambda g, ph, i: (layer, g, 0, 0))
    return pl.pallas_call(
        functools.partial(_s5_kernel, n_ctx=n_ctx, n_batch=b),
        grid=(width // 128, 2, b),
        in_specs=[pl.BlockSpec((1, n, 128), lambda g, ph, i: (i, 0, g)),
                  wspec(tz), wspec(bc), wspec(cc), wspec(coef),
                  pl.BlockSpec((1, 1, 128), lambda g, ph, i: (layer, 0, g))],
        out_specs=pl.BlockSpec((1, n, 128), lambda g, ph, i: (i * ph, 0, g)),
        out_shape=jax.ShapeDtypeStruct((b, n, width), F32),
        scratch_shapes=[pltpu.VMEM((ng, rows, t * S5_GROUP), F32),
                        pltpu.VMEM((S5_SCAN_GROUPS, 4, rows, 128), F32),
                        pltpu.VMEM((S5_SCAN_GROUPS, 2, rows, 128), F32),
                        pltpu.VMEM((t, 128, 128), F32)],
        compiler_params=_cparams(3),
    )(u, tz, bc, cc, coef, s5_d.astype(F32).reshape(s5_d.shape[0], 1, width))


def _rope_lanes(x, cos, sa, sb, shift):
    return x * cos + pltpu.roll(x, 128 - shift, 1) * sa + pltpu.roll(x, shift, 1) * sb


def _mla_prep_kernel(qa_ref, kva_ref, kr_ref, qg_ref, kvg_ref, wq_ref, wk_ref, wv_ref,
                     cos_ref, sa_ref, sb_ref, qp_ref, qr_ref, k_ref, v_ref):
    def rms(x, gain):
        return (x * lax.rsqrt(jnp.mean(x * x, axis=-1, keepdims=True) + LN_EPS) * gain).astype(BF16)

    cos, sa, sb = cos_ref[...], sa_ref[...], sb_ref[...]
    qn = rms(qa_ref[0], qg_ref[...])
    kvn = rms(kva_ref[0], kvg_ref[...])
    q = _bdot(qn, wq_ref[...]) * MLA_SCALE
    k = _bdot(kvn, wk_ref[...])
    v_ref[0] = _bdot(kvn, wv_ref[...]).astype(BF16)
    kr_rot = _rope_lanes(kr_ref[0], cos, sa, sb, MLA_ROPE // 2)
    qp_ref[0] = q.astype(BF16)
    for h in range(MLA_HEADS):
        sl = slice(h * MLA_PAD, (h + 1) * MLA_PAD)
        qr_ref[0, :, sl] = _rope_lanes(q[:, sl], cos, sa, sb, MLA_ROPE // 2).astype(BF16)
        k_ref[0, :, sl] = (k[:, sl] + kr_rot).astype(BF16)


def _mla_prep_call(qa, kva, kr, qg, kvg, wq, wk, wv, tabs):
    b, n, _ = qa.shape
    tm = TOKEN_TILE
    tok = lambda w: pl.BlockSpec((1, tm, w), lambda i, t: (i, t, 0))
    full = lambda a: pl.BlockSpec(a.shape, lambda i, t: (0,) * a.ndim)
    tab = pl.BlockSpec((tm, 128), lambda i, t: (t, 0))
    hw = MLA_HEADS * MLA_PAD
    return pl.pallas_call(
        _mla_prep_kernel,
        grid=(b, n // tm),
        in_specs=[tok(MLA_Q_RANK), tok(MLA_KV_RANK), tok(MLA_PAD), full(qg), full(kvg), full(wq), full(wk),
                  full(wv), tab, tab, tab],
        out_specs=[tok(hw), tok(hw), tok(hw), tok(MLA_HEADS * MLA_V)],
        out_shape=[jax.ShapeDtypeStruct((b, n, hw), BF16)] * 3
        + [jax.ShapeDtypeStruct((b, n, MLA_HEADS * MLA_V), BF16)],
        compiler_params=_cparams(2),
    )(qa, kva, kr, qg, kvg, wq, wk, wv, *tabs)


def _lane_chunks(xs):
    return [x[:, i * 128:(i + 1) * 128] for x in xs for i in range(x.shape[1] // 128)]


def _row_max(scores, floor=None):
    mm = functools.reduce(jnp.maximum, _lane_chunks(scores))
    if floor is not None:
        mm = jnp.maximum(mm, floor)
    return jnp.max(mm, axis=-1, keepdims=True)


def _softmax_av(scores, values, sink=None):
    m = _row_max(scores, sink)
    ps = [jnp.exp(s - m) for s in scores]
    ll = functools.reduce(jnp.add, _lane_chunks(ps))
    if sink is not None:
        lane = lax.broadcasted_iota(jnp.int32, sink.shape, 1)
        ll = ll + jnp.where(lane == 0, jnp.exp(sink - m), 0.0)
    l = jnp.sum(ll, axis=-1, keepdims=True)
    o = functools.reduce(jnp.add, [_bdot(p.astype(BF16), v) for p, v in zip(ps, values)])
    return o / l


def _mla_attn_kernel(qp_ref, qr_ref, k_ref, v_ref, o_ref, *, n_ctx, n_ctx_tiles):
    heads = qp_ref.shape[-1] // MLA_PAD
    t = pl.program_id(2)
    lane = lax.broadcasted_iota(jnp.int32, (1, heads * MLA_V), 1)

    def run(latent):
        acc = None
        for h in range(heads):
            sl = slice(h * MLA_PAD, (h + 1) * MLA_PAD)
            own = (lane >= h * MLA_V) & (lane < (h + 1) * MLA_V)
            zero = jnp.zeros((), BF16)
            scores = [_dot_nt(qp_ref[0, :, sl], k_ref[0, :n_ctx, sl])]
            values = [jnp.where(own, v_ref[0, :n_ctx, :], zero)]
            if latent:
                scores.append(_dot_nt(qr_ref[0, :, sl], k_ref[0, n_ctx:, sl]))
                values.append(jnp.where(own, v_ref[0, n_ctx:, :], zero))
            o = _softmax_av(scores, values)
            acc = o if acc is None else acc + o
        o_ref[0] = acc.astype(o_ref.dtype)

    pl.when(t < n_ctx_tiles)(lambda: run(False))
    pl.when(t >= n_ctx_tiles)(lambda: run(True))


def _mla_attn_call(qp, qr, k, v, n_ctx):
    b, n, _ = qp.shape
    tq = TOKEN_TILE
    hp = MLA_HEADS_PER_STEP
    qspec = pl.BlockSpec((1, tq, hp * MLA_PAD), lambda i, h, t: (i, t, h))
    return pl.pallas_call(
        functools.partial(_mla_attn_kernel, n_ctx=n_ctx, n_ctx_tiles=n_ctx // tq),
        grid=(b, MLA_HEADS // hp, n // tq),
        in_specs=[qspec, qspec,
                  pl.BlockSpec((1, n, hp * MLA_PAD), lambda i, h, t: (i, 0, h)),
                  pl.BlockSpec((1, n, hp * MLA_V), lambda i, h, t: (i, 0, h))],
        out_specs=pl.BlockSpec((1, tq, hp * MLA_V), lambda i, h, t: (i, t, h)),
        out_shape=jax.ShapeDtypeStruct((b, n, MLA_HEADS * MLA_V), BF16),
        compiler_params=_cparams(3),
    )(qp, qr, k, v)


def _win_kernel(sink_ref, q_ref, kp_ref, kc_ref, kn_ref, vp_ref, vc_ref, vn_ref, kctx_ref, vctx_ref,
                cq_ref, saq_ref, sbq_ref, cp_ref, sap_ref, sbp_ref, cn_ref, san_ref, sbn_ref, o_ref,
                *, n_ctx_blocks, n_blocks):
    j = pl.program_id(1)
    hd = WIN_HEAD_DIM
    half = hd // 2
    lane = lax.broadcasted_iota(jnp.int32, (1, 128), 1)
    lo = jnp.where(lane < hd, 1.0, 0.0)
    hi = 1.0 - lo
    upper_rows = lax.broadcasted_iota(jnp.int32, (2 * BLOCK, 128), 0) < BLOCK

    def lane_halves(x):
        xr = pltpu.roll(x, hd, 1)
        return {(0, 0): (x * lo).astype(BF16), (0, 1): (xr * hi).astype(BF16),
                (1, 0): (xr * lo).astype(BF16), (1, 1): (x * hi).astype(BF16)}

    def attend(queries, keys, values, masks):
        for kh in range(WIN_KV_HEADS):
            stacked = [jnp.concatenate([qs[:, (2 * kh) * 128:(2 * kh + 1) * 128],
                                        qs[:, (2 * kh + 1) * 128:(2 * kh + 2) * 128]], axis=0).astype(BF16)
                       for qs in queries]
            acc = None
            for par in range(2):
                scores = []
                for qst, ks, msk in zip(stacked, keys, masks):
                    s = _dot_nt(qst, ks[(kh, par)])
                    scores.append(s if msk is None else jnp.where(msk, s, NEG_INF))
                sink = jnp.where(upper_rows, sink_ref[4 * kh + par], sink_ref[4 * kh + 2 + par])
                o = _softmax_av(scores, [vs[(kh, par)] for vs in values], sink)
                acc = o if acc is None else acc + o
            o_ref[0, :, (2 * kh) * 128:(2 * kh + 1) * 128] = acc[:BLOCK].astype(o_ref.dtype)
            o_ref[0, :, (2 * kh + 1) * 128:(2 * kh + 2) * 128] = acc[BLOCK:].astype(o_ref.dtype)

    q = q_ref[0] * WIN_SCALE
    kctx = lane_halves(kctx_ref[0])
    vctx = lane_halves(vctx_ref[0])

    def ctx_path():
        attend([q], [kctx], [vctx], [None])

    def lat_path():
        q_rot = jnp.concatenate(
            [_rope_lanes(q[:, c * 128:(c + 1) * 128], cq_ref[...], saq_ref[...], sbq_ref[...], half)
             for c in range(q.shape[1] // 128)], axis=-1)
        kband = jnp.concatenate([
            _rope_lanes(kp_ref[0], cp_ref[...], sap_ref[...], sbp_ref[...], half),
            _rope_lanes(kc_ref[0], cq_ref[...], saq_ref[...], sbq_ref[...], half),
            _rope_lanes(kn_ref[0], cn_ref[...], san_ref[...], sbn_ref[...], half)], axis=0)
        vband = jnp.concatenate([vp_ref[0], vc_ref[0], vn_ref[0]], axis=0)
        blk = j - n_ctx_blocks
        r = lax.broadcasted_iota(jnp.int32, (2 * BLOCK, 3 * BLOCK), 0) % BLOCK
        c = lax.broadcasted_iota(jnp.int32, (2 * BLOCK, 3 * BLOCK), 1)
        first = jnp.where(blk > 0, 0, BLOCK)
        last = jnp.where(blk < n_blocks - 1, 3 * BLOCK, 2 * BLOCK)
        valid = (jnp.abs(c - BLOCK - r) <= WINDOW) & (c >= first) & (c < last)
        attend([q_rot, q], [lane_halves(kband), kctx], [lane_halves(vband), vctx], [valid, None])

    pl.when(j < n_ctx_blocks)(ctx_path)
    pl.when(j >= n_ctx_blocks)(lat_path)


def _win_call(sink, wq, wk, wv, tabs, n_ctx):
    b, n, _ = wq.shape
    ncb = n_ctx // BLOCK
    nb = n // BLOCK
    kvw = WIN_KV_HEADS * WIN_HEAD_DIM
    lo, hi = ncb, nb - 1
    prev = lambda i, j: (i, jnp.clip(j - 1, lo, hi), 0)
    cur = lambda i, j: (i, j, 0)
    nxt = lambda i, j: (i, jnp.clip(j + 1, lo, hi), 0)
    kv = lambda f: pl.BlockSpec((1, BLOCK, kvw), f)
    tab = lambda f: pl.BlockSpec((BLOCK, 128), lambda i, j: f(i, j)[1:])
    ctx = pl.BlockSpec((1, n_ctx, kvw), lambda i, j: (i, 0, 0))
    return pl.pallas_call(
        functools.partial(_win_kernel, n_ctx_blocks=ncb, n_blocks=nb - ncb),
        grid=(b, nb),
        in_specs=[pl.BlockSpec(memory_space=pltpu.SMEM),
                  pl.BlockSpec((1, BLOCK, WIN_Q_HEADS * WIN_HEAD_DIM), cur),
                  kv(prev), kv(cur), kv(nxt), kv(prev), kv(cur), kv(nxt), ctx, ctx,
                  tab(cur), tab(cur), tab(cur), tab(prev), tab(prev), tab(prev), tab(nxt), tab(nxt), tab(nxt)],
        out_specs=pl.BlockSpec((1, BLOCK, WIN_Q_HEADS * WIN_HEAD_DIM), cur),
        out_shape=jax.ShapeDtypeStruct((b, n, WIN_Q_HEADS * WIN_HEAD_DIM), BF16),
        compiler_params=_cparams(2),
    )(sink, wq, wk, wk, wk, wv, wv, wv, wk, wv, *tabs, *tabs, *tabs)


def _merge_kernel(x_ref, s5_ref, mla_ref, win_ref, gate_ref, mod_ref, wglu_ref, bglu_ref, wbr_ref, wout_ref,
                  g1_ref, b1_ref, wr_ref, x1_ref, h2_ref, lg_ref, *, alpha):
    d = x_ref.shape[-1]
    g = jax.nn.gelu(s5_ref[0])
    s5o = g * _sigmoid(_bdot(g.astype(BF16), wglu_ref[...]) + bglu_ref[...])
    branches = (s5o.astype(BF16), mla_ref[0], win_ref[0])
    mix = None
    for kk, o in enumerate(branches):
        term = _sigmoid(gate_ref[0, :, kk * d:(kk + 1) * d].astype(F32)) * _bdot(o, wbr_ref[kk])
        mix = term if mix is None else mix + term
    y = _bdot(mix.astype(BF16), wout_ref[...])
    mod = lambda r: mod_ref[0, 0, r:r + 1, :]
    x1 = _layer_norm(alpha * x_ref[0] + mod(2) * y) * g1_ref[...] + b1_ref[...]
    x1_ref[0] = x1
    h2 = (_layer_norm(x1) * (1.0 + mod(4)) + mod(3)).astype(BF16)
    h2_ref[0] = h2
    lg_ref[0] = _dot_nt(wr_ref[...], h2)


def _merge_call(xall, s5y, mla_o, win_o, gates, mod, wglu, bglu, wbr, wout, g1, b1, wr_t, n_ctx_tiles, alpha):
    b, n, d = xall.shape
    tm = TOKEN_TILE
    tok = lambda w: pl.BlockSpec((1, tm, w), lambda i, t: (i, t, 0))
    full = lambda a: pl.BlockSpec(a.shape, lambda i, t: (0,) * a.ndim)
    return pl.pallas_call(
        functools.partial(_merge_kernel, alpha=alpha),
        grid=(b, n // tm),
        in_specs=[tok(d), tok(BRANCH_WIDTH), tok(BRANCH_WIDTH), tok(BRANCH_WIDTH), tok(N_BRANCH * d),
                  pl.BlockSpec((1, 1, 6, d), lambda i, t: (i, jnp.where(t < n_ctx_tiles, 0, 1), 0, 0)),
                  full(wglu), full(bglu), full(wbr), full(wout), full(g1), full(b1), full(wr_t)],
        out_specs=[tok(d), tok(d), pl.BlockSpec((1, N_EXPERTS, tm), lambda i, t: (i, 0, t))],
        out_shape=[jax.ShapeDtypeStruct((b, n, d), F32), jax.ShapeDtypeStruct((b, n, d), BF16),
                   jax.ShapeDtypeStruct((b, N_EXPERTS, n), F32)],
        compiler_params=_cparams(2),
    )(xall, s5y, mla_o, win_o, gates, mod, wglu, bglu, wbr, wout, g1, b1, wr_t)


def _excl_cumsum_lanes(m):
    rows, n = m.shape
    r = lax.broadcasted_iota(jnp.int32, (128, 128), 0)
    c = lax.broadcasted_iota(jnp.int32, (128, 128), 1)
    tri = jnp.where(r < c, 1.0, 0.0).astype(BF16)
    off = jnp.zeros((rows, 1), F32)
    outs = []
    for jb in range(n // 128):
        blk = m[:, jb * 128:(jb + 1) * 128]
        outs.append(_bdot(blk.astype(BF16), tri) + off)
        off = off + jnp.sum(blk, axis=1, keepdims=True)
    return jnp.concatenate(outs, axis=1)


def _topk_slots(aff, cap):
    bits = pltpu.bitcast(aff, jnp.int32)

    def body(i, thr):
        cand = thr | (jnp.int32(1) << (30 - i))
        cnt = jnp.sum(jnp.where(bits >= cand, 1.0, 0.0), axis=1, keepdims=True)
        return jnp.where(cnt >= cap, cand, thr)

    thr = lax.fori_loop(0, 31, body, jnp.zeros((aff.shape[0], 1), jnp.int32))
    gt = jnp.where(bits > thr, 1.0, 0.0)
    eq = jnp.where(bits == thr, 1.0, 0.0)
    need = cap - jnp.sum(gt, axis=1, keepdims=True)
    sel = gt + eq * jnp.where(_excl_cumsum_lanes(eq) < need, 1.0, 0.0)
    rank = _excl_cumsum_lanes(sel)
    return jnp.where(sel > 0.5, rank, -1.0).astype(jnp.int32)


def _route_kernel(lg_ref, slot_ref, aff_ref, *, n_ctx, cap_ctx, cap_lat):
    lg = lg_ref[0]
    m = jnp.max(lg, axis=0, keepdims=True)
    ex = jnp.exp(lg - m)
    aff = ex / jnp.sum(ex, axis=0, keepdims=True)
    aff_ref[0] = aff
    slot_ref[0, :, :n_ctx] = _topk_slots(aff[:, :n_ctx], cap_ctx)
    slot_ref[0, :, n_ctx:] = _topk_slots(aff[:, n_ctx:], cap_lat)


def _route_call(logits_t, n_ctx, cap_ctx, cap_lat):
    b, e, n = logits_t.shape
    spec = pl.BlockSpec((1, e, n), lambda i: (i, 0, 0))
    return pl.pallas_call(
        functools.partial(_route_kernel, n_ctx=n_ctx, cap_ctx=cap_ctx, cap_lat=cap_lat),
        grid=(b,),
        in_specs=[spec],
        out_specs=[spec, spec],
        out_shape=[jax.ShapeDtypeStruct((b, e, n), jnp.int32), jax.ShapeDtypeStruct((b, e, n), F32)],
        compiler_params=_cparams(1),
    )(logits_t)


def _expert_kernel(slot_ref, aff_ref, h_ref, wg_ref, wu_ref, wd_ref, yl_ref, yc_ref, wg_s, wu_s, wd_s, *, n_ctx):
    cap_lat = yl_ref.shape[2]
    cap_ctx = yc_ref.shape[2]
    slot = slot_ref[0, 0]
    aff = aff_ref[0, 0]

    @pl.when(pl.program_id(1) == 0)
    def _():
        wg_s[...] = wg_ref[0, 0].astype(BF16)
        wu_s[...] = wu_ref[0, 0].astype(BF16)
        wd_s[...] = wd_ref[0, 0].astype(BF16)

    def pick(lo, hi, cap):
        iota = lax.broadcasted_iota(jnp.int32, (cap, hi - lo), 0)
        mask = slot[:, lo:hi] == iota
        gate = jnp.sum(jnp.where(mask, aff[:, lo:hi], 0.0), axis=1, keepdims=True)
        onehot = jnp.where(mask, 1.0, 0.0).astype(BF16)
        return _bdot(onehot, h_ref[0, lo:hi, :]), gate

    n = h_ref.shape[1]
    xl, gl = pick(n_ctx, n, cap_lat)
    xc, gc = pick(0, n_ctx, cap_ctx)
    xs = jnp.concatenate([xl, xc], axis=0).astype(BF16)
    gate = jnp.concatenate([gl, gc], axis=0)
    a = _bdot(xs, wg_s[...])
    u = _bdot(xs, wu_s[...])
    hm = (a * _sigmoid(a) * u).astype(BF16)
    y = _bdot(hm, wd_s[...]) * gate
    yl_ref[0, 0] = y[:cap_lat].astype(yl_ref.dtype)
    yc_ref[0, 0] = y[cap_lat:].astype(yc_ref.dtype)


def _expert_call(layer, slot, aff, h2, w_gate, w_up, w_down, n_ctx, cap_ctx, cap_lat):
    b, e, n = slot.shape
    d = h2.shape[-1]
    f = w_gate.shape[-1]
    row = pl.BlockSpec((1, 1, 1, n), lambda ie, ib: (ib, ie, 0, 0))
    return pl.pallas_call(
        functools.partial(_expert_kernel, n_ctx=n_ctx),
        grid=(e, b),
        in_specs=[row, row,
                  pl.BlockSpec((1, n, d), lambda ie, ib: (ib, 0, 0)),
                  pl.BlockSpec((1, 1, d, f), lambda ie, ib: (layer, ie, 0, 0)),
                  pl.BlockSpec((1, 1, d, f), lambda ie, ib: (layer, ie, 0, 0)),
                  pl.BlockSpec((1, 1, f, d), lambda ie, ib: (layer, ie, 0, 0))],
        out_specs=[pl.BlockSpec((1, 1, cap_lat, d), lambda ie, ib: (ib, ie, 0, 0)),
                   pl.BlockSpec((1, 1, cap_ctx, d), lambda ie, ib: (ib, ie, 0, 0))],
        out_shape=[jax.ShapeDtypeStruct((b, e, cap_lat, d), BF16), jax.ShapeDtypeStruct((b, e, cap_ctx, d), BF16)],
        scratch_shapes=[pltpu.VMEM((d, f), BF16), pltpu.VMEM((d, f), BF16), pltpu.VMEM((f, d), BF16)],
        compiler_params=_cparams(2),
    )(slot.reshape(b, e, 1, n), aff.reshape(b, e, 1, n), h2, w_gate, w_up, w_down)


def _combine_kernel(slot_ref, yl_ref, yc_ref, x1_ref, mod_ref, g2_ref, b2_ref, o_ref, *, n_ctx_tiles, alpha):
    t = pl.program_id(1)
    tm = x1_ref.shape[1]
    slot = slot_ref[0]

    def finish(fl):
        x1 = x1_ref[0]
        o_ref[0] = _layer_norm(alpha * x1 + mod_ref[0, 0, 5:6, :] * fl) * g2_ref[...] + b2_ref[...]

    def onehot(e, cap):
        iota = lax.broadcasted_iota(jnp.int32, (tm, cap), 1)
        return jnp.where(slot[:, e:e + 1] == iota, 1.0, 0.0).astype(BF16)

    def ctx_path():
        cap = yc_ref.shape[2]
        fl = None
        for e in range(N_EXPERTS):
            term = _bdot(onehot(e, cap), yc_ref[0, e])
            fl = term if fl is None else fl + term
        finish(fl)

    def lat_path():
        cap = yl_ref.shape[2]
        pt = jnp.concatenate([onehot(e, cap) for e in range(N_EXPERTS)], axis=1)
        finish(_bdot(pt, yl_ref[0].reshape(N_EXPERTS * cap, yl_ref.shape[3])))

    pl.when(t < n_ctx_tiles)(ctx_path)
    pl.when(t >= n_ctx_tiles)(lat_path)


def _combine_call(slot_t, yl, yc, x1, mod, g2, b2, n_ctx_tiles, alpha):
    b, n, d = x1.shape
    tm = TOKEN_TILE
    e = N_EXPERTS
    full = lambda a: pl.BlockSpec(a.shape, lambda i, t: (0,) * a.ndim)
    return pl.pallas_call(
        functools.partial(_combine_kernel, n_ctx_tiles=n_ctx_tiles, alpha=alpha),
        grid=(b, n // tm),
        in_specs=[pl.BlockSpec((1, tm, e), lambda i, t: (i, t, 0)),
                  pl.BlockSpec((1,) + yl.shape[1:], lambda i, t: (i, 0, 0, 0)),
                  pl.BlockSpec((1,) + yc.shape[1:], lambda i, t: (i, 0, 0, 0)),
                  pl.BlockSpec((1, tm, d), lambda i, t: (i, t, 0)),
                  pl.BlockSpec((1, 1, 6, d), lambda i, t: (i, jnp.where(t < n_ctx_tiles, 0, 1), 0, 0)),
                  full(g2), full(b2)],
        out_specs=pl.BlockSpec((1, tm, d), lambda i, t: (i, t, 0)),
        out_shape=jax.ShapeDtypeStruct((b, n, d), F32),
        compiler_params=_cparams(2),
    )(slot_t, yl, yc, x1, mod, g2, b2)


def _rope_tables(n_ctx, seq, head_dim, lane_offset):
    half = head_dim // 2
    nf = head_dim // 4
    t = jnp.arange(seq, dtype=F32)
    row = jnp.floor(t / GRID_W)
    col = t - row * GRID_W
    freqs = ROPE_BASE ** (-jnp.arange(nf, dtype=F32) / nf)
    ang = jnp.concatenate([row[:, None] * freqs, col[:, None] * freqs], axis=-1)
    cos, sin = jnp.cos(ang), jnp.sin(ang)
    zeros = jnp.zeros_like(sin)
    n_heads = (128 - lane_offset) // head_dim if lane_offset == 0 else 1
    c = jnp.concatenate([jnp.ones((seq, lane_offset), F32)] + [cos, cos] * n_heads, axis=-1)
    sa = jnp.concatenate([jnp.zeros((seq, lane_offset), F32)] + [-sin, zeros] * n_heads, axis=-1)
    sb = jnp.concatenate([jnp.zeros((seq, lane_offset), F32)] + [zeros, sin] * n_heads, axis=-1)
    pad = 128 - c.shape[1]
    c = jnp.pad(c, ((n_ctx, 0), (0, pad)), constant_values=1.0)
    sa = jnp.pad(sa, ((n_ctx, 0), (0, pad)))
    sb = jnp.pad(sb, ((n_ctx, 0), (0, pad)))
    return c, sa, sb


def _layer_weights(i, p):
    d = p['w_in'].shape[1]
    pts = np.cumsum((S5_WIDTH, MLA_Q_RANK, MLA_KV_RANK, MLA_ROPE, WIN_Q_HEADS * WIN_HEAD_DIM,
                     WIN_KV_HEADS * WIN_HEAD_DIM, WIN_KV_HEADS * WIN_HEAD_DIM))
    cols = jnp.split(p['w_in'][i], [int(v) for v in pts], axis=1)
    kr = jnp.pad(cols[3], ((0, 0), (MLA_NOPE, MLA_PAD - MLA_NOPE - MLA_ROPE)))
    w_cat = jnp.concatenate([cols[0], cols[1], cols[2], kr, cols[4], cols[5], cols[6], cols[7]], axis=1)
    dq = MLA_NOPE + MLA_ROPE
    wq = p['mla_w_uq'][i].reshape(MLA_Q_RANK, MLA_HEADS, dq)
    wq = jnp.pad(wq, ((0, 0), (0, 0), (0, MLA_PAD - dq))).reshape(MLA_Q_RANK, MLA_HEADS * MLA_PAD)
    wkv = p['mla_w_ukv'][i].reshape(MLA_KV_RANK, MLA_HEADS, MLA_NOPE + MLA_V)
    wk = jnp.pad(wkv[:, :, :MLA_NOPE], ((0, 0), (0, 0), (0, MLA_PAD - MLA_NOPE)))
    wk = wk.reshape(MLA_KV_RANK, MLA_HEADS * MLA_PAD)
    wv = wkv[:, :, MLA_NOPE:].reshape(MLA_KV_RANK, MLA_HEADS * MLA_V)
    row = lambda a: a[i].astype(F32).reshape(1, -1)
    return dict(
        w_cat=w_cat.astype(BF16), wq=wq.astype(BF16), wk=wk.astype(BF16), wv=wv.astype(BF16),
        qg=row(p['mla_q_norm']), kvg=row(p['mla_kv_norm']),
        wglu=p['s5_w_glu'][i].astype(BF16), bglu=row(p['s5_b_glu']),
        sink=p['win_sink'][i].astype(F32),
        wbr=p['w_branch'][i].astype(BF16), wout=p['w_out'][i].astype(BF16),
        g1=row(p['ln1_g']), b1=row(p['ln1_b']), g2=row(p['ln2_g']), b2=row(p['ln2_b']),
        wr_t=p['w_router'][i].T.astype(BF16),
    )


def _forward(p):
    x, c, ctx, c_ctx = p['x'], p['c'], p['ctx'], p['c_ctx']
    b, seq, d = x.shape
    n_ctx = ctx.shape[1]
    depth = p['w_ada'].shape[0]
    assert b == 8 and seq % TOKEN_TILE == 0 and n_ctx % TOKEN_TILE == 0 and seq % GRID_W == 0
    alpha = float((2 * depth) ** 0.25)
    n_ctx_tiles = n_ctx // TOKEN_TILE
    cap_lat = CAPACITY_FACTOR * seq // N_EXPERTS
    cap_ctx = CAPACITY_FACTOR * n_ctx // N_EXPERTS

    cond = jnp.concatenate([c, c_ctx[None], jnp.zeros((16 - b - 1, d), F32)], axis=0)
    mods = _ada_call(cond, p['w_ada'], p['b_ada'])
    mods = mods.reshape(depth, 16, 6, d)
    tabs_mla = _rope_tables(n_ctx, seq, MLA_ROPE, MLA_NOPE)
    tabs_win = _rope_tables(n_ctx, seq, WIN_HEAD_DIM, 0)
    s5w = _s5_param_call(p)

    xall = jnp.concatenate([ctx, x], axis=1)
    for i in range(depth):
        w = _layer_weights(i, p)
        mod = jnp.stack([jnp.broadcast_to(mods[i, b], (b, 6, d)), mods[i, :b]], axis=1)
        u, qa, kva, kr, wq, wk, wv, gates = _in_call(xall, mod, w['w_cat'], n_ctx)
        s5y = _s5_call(i, u, *s5w, p['s5_d'], n_ctx)
        qp, qr, kk, vv = _mla_prep_call(qa, kva, kr, w['qg'], w['kvg'], w['wq'], w['wk'], w['wv'], tabs_mla)
        mla_o = _mla_attn_call(qp, qr, kk, vv, n_ctx)
        win_o = _win_call(w['sink'], wq, wk, wv, tabs_win, n_ctx)
        x1, h2, logits_t = _merge_call(xall, s5y, mla_o, win_o, gates, mod, w['wglu'], w['bglu'], w['wbr'],
                                       w['wout'], w['g1'], w['b1'], w['wr_t'], n_ctx_tiles, alpha)
        slot, aff = _route_call(logits_t, n_ctx, cap_ctx, cap_lat)
        yl, yc = _expert_call(i, slot, aff, h2, p['w_gate'], p['w_up'], p['w_down'], n_ctx, cap_ctx, cap_lat)
        slot_t = jnp.swapaxes(slot, 1, 2)
        xall = _combine_call(slot_t, yl, yc, x1, mod, w['g2'], w['b2'], n_ctx_tiles, alpha)
    return xall[:, n_ctx:]


def kernel(x, c, ctx, c_ctx, w_ada, b_ada, w_in, s5_lam_re, s5_lam_im, s5_log_dt, s5_b_re, s5_b_im, s5_c_re, s5_c_im, s5_d, s5_w_glu, s5_b_glu, mla_q_norm, mla_w_uq, mla_kv_norm, mla_w_ukv, win_sink, w_branch, w_out, ln1_g, ln1_b, ln2_g, ln2_b, w_router, w_gate, w_up, w_down):
    return _forward(dict(
        x=x, c=c, ctx=ctx, c_ctx=c_ctx, w_ada=w_ada, b_ada=b_ada, w_in=w_in, s5_lam_re=s5_lam_re,
        s5_lam_im=s5_lam_im, s5_log_dt=s5_log_dt, s5_b_re=s5_b_re, s5_b_im=s5_b_im, s5_c_re=s5_c_re,
        s5_c_im=s5_c_im, s5_d=s5_d, s5_w_glu=s5_w_glu, s5_b_glu=s5_b_glu, mla_q_norm=mla_q_norm,
        mla_w_uq=mla_w_uq, mla_kv_norm=mla_kv_norm, mla_w_ukv=mla_w_ukv, win_sink=win_sink, w_branch=w_branch,
        w_out=w_out, ln1_g=ln1_g, ln1_b=ln1_b, ln2_g=ln2_g, ln2_b=ln2_b, w_router=w_router, w_gate=w_gate,
        w_up=w_up, w_down=w_down))
```

```python
import functools
import math

import jax
import jax.numpy as jnp
import numpy as np
from jax import lax
from jax.experimental import pallas as pl
from jax.experimental.pallas import tpu as pltpu

F32 = jnp.float32
BF16 = jnp.bfloat16
HIGHEST = lax.Precision.HIGHEST

GRID_W = 64
S5_WIDTH = 512
S5_GROUP = 16
S5_GROUPS = S5_WIDTH // S5_GROUP
S5_STATE = 64
S5_CHUNK = 16
S5_GROUPS_PER_STEP = 4
MLA_HEADS = 8
MLA_NOPE = 64
MLA_ROPE = 32
MLA_V = 64
MLA_Q_RANK = 384
MLA_KV_RANK = 256
MLA_PAD = 128
MLA_HEADS_PER_STEP = 4
MLA_SCALE = (MLA_NOPE + MLA_ROPE) ** -0.5
WIN_Q_HEADS = 8
WIN_KV_HEADS = 2
WIN_GROUP = WIN_Q_HEADS // WIN_KV_HEADS
WIN_HEAD_DIM = 64
WINDOW = 128
BLOCK = 128
WIN_SCALE = WIN_HEAD_DIM ** -0.5
N_BRANCH = 3
BRANCH_WIDTH = 512
N_EXPERTS = 16
CAPACITY_FACTOR = 2
ROPE_BASE = 10000.0
LN_EPS = 1e-6
NEG_INF = -1e30
TOKEN_TILE = 256
IN_TILE = 768
MOE_TILE = 256
MOE_WINDOW = 64
MOE_BOUNDS = 16
VMEM_LIMIT = 56 * 1024 * 1024


def _cparams(n_axes):
    return pltpu.CompilerParams(dimension_semantics=("arbitrary",) * n_axes, vmem_limit_bytes=VMEM_LIMIT)


def _bdot(a, b):
    return jnp.dot(a, b, preferred_element_type=F32)


def _dot_nt(a, b):
    return lax.dot_general(a, b, (((1,), (1,)), ((), ())), preferred_element_type=F32)


def _layer_norm(x):
    mu = jnp.mean(x, axis=-1, keepdims=True)
    xc = x - mu
    var = jnp.mean(xc * xc, axis=-1, keepdims=True)
    return xc * lax.rsqrt(var + LN_EPS)


def _ada_kernel(cond_ref, w_ref, b_ref, o_ref):
    s = cond_ref[...]
    s = s * jax.nn.sigmoid(s)
    o_ref[0] = jnp.dot(s, w_ref[0], precision=HIGHEST, preferred_element_type=F32) + b_ref[0]


def _ada_call(cond, w_ada, b_ada):
    depth, d, d6 = w_ada.shape
    tn = 1536
    rows = cond.shape[0]
    return pl.pallas_call(
        _ada_kernel,
        grid=(depth, d6 // tn),
        in_specs=[
            pl.BlockSpec((rows, d), lambda i, j: (0, 0)),
            pl.BlockSpec((1, d, tn), lambda i, j: (i, 0, j)),
            pl.BlockSpec((1, 1, tn), lambda i, j: (i, 0, j)),
        ],
        out_specs=pl.BlockSpec((1, rows, tn), lambda i, j: (i, 0, j)),
        out_shape=jax.ShapeDtypeStruct((depth, rows, d6), F32),
        compiler_params=_cparams(2),
    )(cond, w_ada, b_ada.reshape(depth, 1, d6))


IN_WIDTHS = (S5_WIDTH, MLA_Q_RANK, MLA_KV_RANK, MLA_PAD, WIN_Q_HEADS * WIN_HEAD_DIM,
             WIN_KV_HEADS * WIN_HEAD_DIM, WIN_KV_HEADS * WIN_HEAD_DIM)
IN_OFFSETS = tuple(int(v) for v in np.cumsum((0,) + IN_WIDTHS))


def _mod_rows(mod_ref, r, tile, n_ctx):
    row = pl.program_id(1) * tile + lax.broadcasted_iota(jnp.int32, (tile, 1), 0)
    return jnp.where(row < n_ctx, mod_ref[0, 0, r:r + 1, :], mod_ref[0, 1, r:r + 1, :])


def _in_kernel(x_ref, mod_ref, w_ref, *out_refs, n_ctx):
    tile = x_ref.shape[1]
    xn = _layer_norm(x_ref[0])
    h = (xn * (1.0 + _mod_rows(mod_ref, 1, tile, n_ctx)) + _mod_rows(mod_ref, 0, tile, n_ctx)).astype(BF16)
    widths = IN_WIDTHS + (out_refs[-1].shape[-1],)
    for off, width, o_ref in zip(IN_OFFSETS, widths, out_refs):
        o_ref[0] = _bdot(h, w_ref[:, off:off + width]).astype(o_ref.dtype)


_sigmoid = jax.nn.sigmoid


def _in_call(xall, mod, w_cat, n_ctx):
    b, n, d = xall.shape
    tm = IN_TILE
    gate_w = w_cat.shape[1] - IN_OFFSETS[-1]
    widths = IN_WIDTHS + (gate_w,)
    return pl.pallas_call(
        functools.partial(_in_kernel, n_ctx=n_ctx),
        grid=(b, n // tm),
        in_specs=[
            pl.BlockSpec((1, tm, d), lambda i, t: (i, t, 0)),
            pl.BlockSpec((1, 2, 6, d), lambda i, t: (i, 0, 0, 0)),
            pl.BlockSpec(w_cat.shape, lambda i, t: (0, 0), pipeline_mode=pl.Buffered(1)),
        ],
        out_specs=[pl.BlockSpec((1, tm, w), lambda i, t: (i, t, 0)) for w in widths],
        out_shape=[jax.ShapeDtypeStruct((b, n, w), F32) for w in IN_WIDTHS]
        + [jax.ShapeDtypeStruct((b, n, gate_w), BF16)],
        compiler_params=_cparams(2),
    )(xall, mod, w_cat)


S5_LANE_GROUPS = 128 // S5_GROUP
S5_SCAN_GROUPS = 4
S5_PITCH_PAD = 8


def _s5_param_kernel(cre_ref, cim_ref, bre_ref, bim_ref, pr_ref, pi_ref, tz_ref, bc_ref, cc_ref, coef_ref):
    t = S5_CHUNK
    w = t * S5_GROUP
    nt = (((1,), (1,)), ((), ()))
    tz = None
    bcs, ccs, coefs = [], [], []
    for d in range(2):
        cre, cim = cre_ref[0, d, 0], cim_ref[0, d, 0]
        bre, bim = bre_ref[0, d, 0], bim_ref[0, d, 0]
        power = lambda k: (pr_ref[0, d, 0, k:k + 1, :], pi_ref[0, d, 0, k:k + 1, :])
        rt = []
        for k in range(t + 1):
            prk, pik = power(k)
            rt.append(jnp.concatenate([cre * prk - cim * pik, -(cre * pik + cim * prk)], axis=1))
        bt = jnp.concatenate([bre, bim], axis=1)
        zeros = jnp.zeros((S5_GROUP, w), F32)
        if d == 0:
            kt = lax.dot_general(bt, jnp.concatenate(rt[:t], axis=0), nt, precision=HIGHEST,
                                 preferred_element_type=F32)
            pad = jnp.concatenate([zeros, kt], axis=1)
            rows = [kt] + [pltpu.roll(pad, S5_GROUP * s, 1)[:, w:] for s in range(1, t)]
        else:
            kt = lax.dot_general(bt, jnp.concatenate(rt[t - 1::-1], axis=0), nt, precision=HIGHEST,
                                 preferred_element_type=F32)
            pad = jnp.concatenate([kt, zeros], axis=1)
            rows = [pltpu.roll(pad, 2 * w - S5_GROUP * (t - 1 - s), 1)[:, :w] for s in range(t - 1)] + [kt]
        tz_d = jnp.concatenate(rows, axis=0)
        tz = tz_d if tz is None else tz + tz_d
        bc_rows = []
        for s in range(t):
            prk, pik = power(t - 1 - s if d == 0 else s)
            br = bre * prk - bim * pik
            bi = bim * prk + bre * pik
            bc_rows.append(jnp.concatenate([br, bi, bi, br], axis=1))
        bcs.append(jnp.concatenate(bc_rows, axis=0))
        ccs.append(jnp.concatenate(rt[1:] if d == 0 else rt[t:0:-1], axis=0))
        er, ei = power(t)
        coefs += [jnp.concatenate([er, er], axis=1), jnp.concatenate([-ei, ei], axis=1),
                  jnp.concatenate([ei, -ei], axis=1)]
    tz_ref[0, 0] = tz.astype(BF16)
    bc_ref[0, 0] = jnp.concatenate(bcs, axis=1).astype(BF16)
    cc_ref[0, 0] = jnp.concatenate(ccs, axis=1).astype(BF16)
    coef_ref[0, 0] = jnp.concatenate(coefs + [jnp.zeros((2, 2 * S5_STATE), F32)], axis=0)


def _s5_param_call(p):
    t = S5_CHUNK
    f = lambda name: p[name].astype(F32)
    lam_re, lam_im = f('s5_lam_re'), f('s5_lam_im')
    depth = lam_re.shape[0]
    dt = jnp.exp(f('s5_log_dt'))[..., None]
    k = jnp.arange(t + 1, dtype=F32)[:, None]
    mag = jnp.exp((lam_re * dt)[..., None, :] * k)
    ang = (lam_im * dt)[..., None, :] * k
    pr, pi = mag * jnp.cos(ang), mag * jnp.sin(ang)
    ar, ai = pr[..., 1, :], pi[..., 1, :]
    den = lam_re * lam_re + lam_im * lam_im
    qr = (((ar - 1) * lam_re + ai * lam_im) / den)[..., None, :]
    qi = ((ai * lam_re - (ar - 1) * lam_im) / den)[..., None, :]
    b_re = jnp.swapaxes(f('s5_b_re'), -1, -2)
    b_im = jnp.swapaxes(f('s5_b_im'), -1, -2)
    bbr = qr * b_re - qi * b_im
    bbi = qr * b_im + qi * b_re
    g, hg, ps = S5_GROUPS, S5_GROUP, S5_STATE
    w = t * hg
    small = lambda rows: pl.BlockSpec((1, 2, 1, rows, ps), lambda i, j: (i, 0, j, 0, 0))
    out = lambda cols: pl.BlockSpec((1, 1, w, cols), lambda i, j: (i, j, 0, 0))
    return pl.pallas_call(
        _s5_param_kernel,
        grid=(depth, g),
        in_specs=[small(hg)] * 4 + [small(t + 1)] * 2,
        out_specs=[out(w), out(2 * w), out(w), pl.BlockSpec((1, 1, 8, 2 * ps), lambda i, j: (i, j, 0, 0))],
        out_shape=[jax.ShapeDtypeStruct((depth, g, w, w), BF16), jax.ShapeDtypeStruct((depth, g, w, 2 * w), BF16),
                   jax.ShapeDtypeStruct((depth, g, w, w), BF16), jax.ShapeDtypeStruct((depth, g, 8, 2 * ps), F32)],
        compiler_params=_cparams(2),
    )(f('s5_c_re'), f('s5_c_im'), bbr, bbi, pr, pi)


def _s5_kernel(u_ref, tz_ref, bc_ref, cc_ref, coef_ref, d_ref, y_ref, uy_ref, loc_ref, sp_ref, slab_ref,
               *, n_ctx, n_batch):
    ph, b = pl.program_id(1), pl.program_id(2)
    t, hg, ng = S5_CHUNK, S5_GROUP, S5_LANE_GROUPS
    ncc = n_ctx // t
    ncl = (u_ref.shape[1] - n_ctx) // t
    nc = ncc + ncl
    pitch = nc + S5_PITCH_PAD

    def to_chunk_rows(slabs):
        tr = [s.T for s in slabs]
        return [jnp.concatenate([x[g * hg:(g + 1) * hg] for x in tr], axis=0).T for g in range(ng)]

    def to_token_slabs(rows):
        tr = [r.T for r in rows]
        return [jnp.concatenate([x[tau * hg:(tau + 1) * hg] for x in tr], axis=0).T for tau in range(t)]

    base = pl.multiple_of(b * pitch, 8)
    cbase = pl.multiple_of(b * ncc, 8)

    @pl.when(ph == 0)
    def _():
        rows = to_chunk_rows([u_ref[0, pl.ds(n_ctx + tau, ncl, stride=t), :] for tau in range(t)])
        for g in range(ng):
            uy_ref[g, pl.ds(base + ncc, ncl), :] = rows[g]
            uy_ref[g, pl.ds(base + nc, S5_PITCH_PAD), :] = jnp.zeros((S5_PITCH_PAD, t * hg), F32)
        for tau in range(t):
            slab_ref[tau, pl.ds(cbase, ncc), :] = u_ref[0, pl.ds(tau, ncc, stride=t), :]

    @pl.when((ph == 1) & (b == 0))
    def _():
        rows = to_chunk_rows([slab_ref[tau] for tau in range(t)])
        for g in range(ng):
            for s in range(n_batch):
                uy_ref[g, s * pitch:s * pitch + ncc, :] = rows[g][s * ncc:(s + 1) * ncc]
        for part in range(ng // S5_SCAN_GROUPS):
            gs = [part * S5_SCAN_GROUPS + gl for gl in range(S5_SCAN_GROUPS)]
            for gl, g in enumerate(gs):
                ub = uy_ref[g].astype(BF16)
                loc = _bdot(ub, bc_ref[0, g])
                for q in range(4):
                    loc_ref[gl, q] = loc[:, q * 128:(q + 1) * 128]
                uy_ref[g] = _bdot(ub, tz_ref[0, g])
                for s in range(n_batch):
                    for d in range(2):
                        sp_ref[gl, d, s * pitch + nc:(s + 1) * pitch, :] = jnp.zeros((S5_PITCH_PAD, 128), F32)

            def coef(g, r):
                return jnp.broadcast_to(coef_ref[0, g, r:r + 1, :], (n_batch, 128))

            def step(i, carry):
                cb = jnp.where(i < ncc, ncc - 1 - i, nc + ncc - 1 - i)
                fwd = pl.ds(i, n_batch, stride=pitch)
                bwd = pl.ds(cb, n_batch, stride=pitch)
                out = []
                for gl, g in enumerate(gs):
                    v0f, v1f, v0b, v1b = carry[gl]
                    sp_ref[gl, 0, fwd, :] = v0f
                    sp_ref[gl, 1, bwd, :] = v0b
                    n0f = coef(g, 0) * v0f + coef(g, 1) * v1f + loc_ref[gl, 0, fwd, :]
                    n1f = coef(g, 0) * v1f + coef(g, 2) * v0f + loc_ref[gl, 1, fwd, :]
                    n0b = coef(g, 3) * v0b + coef(g, 4) * v1b + loc_ref[gl, 2, bwd, :]
                    n1b = coef(g, 3) * v1b + coef(g, 5) * v0b + loc_ref[gl, 3, bwd, :]
                    out.append((n0f, n1f, n0b, n1b))
                return tuple(out)

            z = jnp.zeros((n_batch, 128), F32)
            lax.fori_loop(0, nc, step, tuple((z, z, z, z) for _ in gs))
            for gl, g in enumerate(gs):
                sp = jnp.concatenate([sp_ref[gl, 0], sp_ref[gl, 1]], axis=1).astype(BF16)
                uy_ref[g] = uy_ref[g] + _dot_nt(sp, cc_ref[0, g])
        rows = [jnp.concatenate([uy_ref[g, s * pitch:s * pitch + ncc, :] for s in range(n_batch)], axis=0)
                for g in range(ng)]
        for tau, slab in enumerate(to_token_slabs(rows)):
            slab_ref[tau] = slab

    @pl.when(ph == 1)
    def _():
        slabs = to_token_slabs([uy_ref[g, pl.ds(base + ncc, ncl), :] for g in range(ng)])
        for tau in range(t):
            y_ref[0, pl.ds(n_ctx + tau, ncl, stride=t), :] = slabs[tau]
            y_ref[0, pl.ds(tau, ncc, stride=t), :] = slab_ref[tau, pl.ds(cbase, ncc), :]
        y_ref[0] = y_ref[0] + d_ref[0] * u_ref[0]


def _s5_call(layer, u, tz, bc, cc, coef, s5_d, n_ctx):
    b, n, width = u.shape
    t, ng = S5_CHUNK, S5_LANE_GROUPS
    rows = b * (n // t + S5_PITCH_PAD)
    assert (n - n_ctx) // t == 128 and b * (n_ctx // t) == 128
    wspec = lambda a: pl.BlockSpec((1, ng) + a.shape[2:], lambda g, ph, i: (layer, g, 0, 0))
    return pl.pallas_call(
        functools.partial(_s5_kernel, n_ctx=n_ctx, n_batch=b),
        grid=(width // 128, 2, b),
        in_specs=[pl.BlockSpec((1, n, 128), lambda g, ph, i: (i, 0, g)),
                  wspec(tz), wspec(bc), wspec(cc), wspec(coef),
                  pl.BlockSpec((1, 1, 128), lambda g, ph, i: (layer, 0, g))],
        out_specs=pl.BlockSpec((1, n, 128), lambda g, ph, i: (i * ph, 0, g)),
        out_shape=jax.ShapeDtypeStruct((b, n, width), F32),
        scratch_shapes=[pltpu.VMEM((ng, rows, t * S5_GROUP), F32),
                        pltpu.VMEM((S5_SCAN_GROUPS, 4, rows, 128), F32),
                        pltpu.VMEM((S5_SCAN_GROUPS, 2, rows, 128), F32),
                        pltpu.VMEM((t, 128, 128), F32)],
        compiler_params=_cparams(3),
    )(u, tz, bc, cc, coef, s5_d.astype(F32).reshape(s5_d.shape[0], 1, width))


def _rope_lanes(x, cos, sa, sb, shift):
    return x * cos + pltpu.roll(x, 128 - shift, 1) * sa + pltpu.roll(x, shift, 1) * sb


def _mla_prep_kernel(qa_ref, kva_ref, kr_ref, qg_ref, kvg_ref, wq_ref, wk_ref, wv_ref,
                     cos_ref, sa_ref, sb_ref, qp_ref, qr_ref, k_ref, v_ref):
    def rms(x, gain):
        return (x * lax.rsqrt(jnp.mean(x * x, axis=-1, keepdims=True) + LN_EPS) * gain).astype(BF16)

    cos, sa, sb = cos_ref[...], sa_ref[...], sb_ref[...]
    qn = rms(qa_ref[0], qg_ref[...])
    kvn = rms(kva_ref[0], kvg_ref[...])
    q = _bdot(qn, wq_ref[...]) * MLA_SCALE
    k = _bdot(kvn, wk_ref[...])
    v_ref[0] = _bdot(kvn, wv_ref[...]).astype(BF16)
    kr_rot = _rope_lanes(kr_ref[0], cos, sa, sb, MLA_ROPE // 2)
    qp_ref[0] = q.astype(BF16)
    for h in range(MLA_HEADS):
        sl = slice(h * MLA_PAD, (h + 1) * MLA_PAD)
        qr_ref[0, :, sl] = _rope_lanes(q[:, sl], cos, sa, sb, MLA_ROPE // 2).astype(BF16)
        k_ref[0, :, sl] = (k[:, sl] + kr_rot).astype(BF16)


def _mla_prep_call(qa, kva, kr, qg, kvg, wq, wk, wv, tabs):
    b, n, _ = qa.shape
    tm = TOKEN_TILE
    tok = lambda w: pl.BlockSpec((1, tm, w), lambda i, t: (i, t, 0))
    full = lambda a: pl.BlockSpec(a.shape, lambda i, t: (0,) * a.ndim)
    tab = pl.BlockSpec((tm, 128), lambda i, t: (t, 0))
    hw = MLA_HEADS * MLA_PAD
    return pl.pallas_call(
        _mla_prep_kernel,
        grid=(b, n // tm),
        in_specs=[tok(MLA_Q_RANK), tok(MLA_KV_RANK), tok(MLA_PAD), full(qg), full(kvg), full(wq), full(wk),
                  full(wv), tab, tab, tab],
        out_specs=[tok(hw), tok(hw), tok(hw), tok(MLA_HEADS * MLA_V)],
        out_shape=[jax.ShapeDtypeStruct((b, n, hw), BF16)] * 3
        + [jax.ShapeDtypeStruct((b, n, MLA_HEADS * MLA_V), BF16)],
        compiler_params=_cparams(2),
    )(qa, kva, kr, qg, kvg, wq, wk, wv, *tabs)


def _lane_chunks(xs):
    return [x[:, i * 128:(i + 1) * 128] for x in xs for i in range(x.shape[1] // 128)]


def _row_max(scores, floor=None):
    mm = functools.reduce(jnp.maximum, _lane_chunks(scores))
    if floor is not None:
        mm = jnp.maximum(mm, floor)
    return jnp.max(mm, axis=-1, keepdims=True)


def _softmax_av(scores, values, sink=None):
    m = _row_max(scores, sink)
    ps = [jnp.exp(s - m) for s in scores]
    ll = functools.reduce(jnp.add, _lane_chunks(ps))
    if sink is not None:
        lane = lax.broadcasted_iota(jnp.int32, sink.shape, 1)
        ll = ll + jnp.where(lane == 0, jnp.exp(sink - m), 0.0)
    l = jnp.sum(ll, axis=-1, keepdims=True)
    o = functools.reduce(jnp.add, [_bdot(p.astype(BF16), v) for p, v in zip(ps, values)])
    return o / l


def _mla_attn_kernel(qp_ref, qr_ref, k_ref, v_ref, o_ref, *, n_ctx, n_ctx_tiles):
    heads = qp_ref.shape[-1] // MLA_PAD
    t = pl.program_id(2)
    lane = lax.broadcasted_iota(jnp.int32, (1, heads * MLA_V), 1)

    def run(latent):
        acc = None
        for h in range(heads):
            sl = slice(h * MLA_PAD, (h + 1) * MLA_PAD)
            own = (lane >= h * MLA_V) & (lane < (h + 1) * MLA_V)
            zero = jnp.zeros((), BF16)
            scores = [_dot_nt(qp_ref[0, :, sl], k_ref[0, :n_ctx, sl])]
            values = [jnp.where(own, v_ref[0, :n_ctx, :], zero)]
            if latent:
                scores.append(_dot_nt(qr_ref[0, :, sl], k_ref[0, n_ctx:, sl]))
                values.append(jnp.where(own, v_ref[0, n_ctx:, :], zero))
            o = _softmax_av(scores, values)
            acc = o if acc is None else acc + o
        o_ref[0] = acc.astype(o_ref.dtype)

    pl.when(t < n_ctx_tiles)(lambda: run(False))
    pl.when(t >= n_ctx_tiles)(lambda: run(True))


def _mla_attn_call(qp, qr, k, v, n_ctx):
    b, n, _ = qp.shape
    tq = TOKEN_TILE
    hp = MLA_HEADS_PER_STEP
    qspec = pl.BlockSpec((1, tq, hp * MLA_PAD), lambda i, h, t: (i, t, h))
    return pl.pallas_call(
        functools.partial(_mla_attn_kernel, n_ctx=n_ctx, n_ctx_tiles=n_ctx // tq),
        grid=(b, MLA_HEADS // hp, n // tq),
        in_specs=[qspec, qspec,
                  pl.BlockSpec((1, n, hp * MLA_PAD), lambda i, h, t: (i, 0, h)),
                  pl.BlockSpec((1, n, hp * MLA_V), lambda i, h, t: (i, 0, h))],
        out_specs=pl.BlockSpec((1, tq, hp * MLA_V), lambda i, h, t: (i, t, h)),
        out_shape=jax.ShapeDtypeStruct((b, n, MLA_HEADS * MLA_V), BF16),
        compiler_params=_cparams(3),
    )(qp, qr, k, v)


def _win_kernel(sink_ref, q_ref, k_ref, v_ref, cos_ref, sa_ref, sb_ref, o_ref, *, n_ctx_blocks, n_blocks):
    j = pl.program_id(1)
    hd = WIN_HEAD_DIM
    half = hd // 2
    lane = lax.broadcasted_iota(jnp.int32, (1, 128), 1)
    lo = jnp.where(lane < hd, 1.0, 0.0)
    hi = 1.0 - lo
    upper_rows = lax.broadcasted_iota(jnp.int32, (2 * BLOCK, 128), 0) < BLOCK

    def lane_halves(x):
        xr = pltpu.roll(x, hd, 1)
        return {(0, 0): (x * lo).astype(BF16), (0, 1): (xr * hi).astype(BF16),
                (1, 0): (xr * lo).astype(BF16), (1, 1): (x * hi).astype(BF16)}

    def attend(queries, keys, values, masks):
        for kh in range(WIN_KV_HEADS):
            stacked = [jnp.concatenate([qs[:, (2 * kh) * 128:(2 * kh + 1) * 128],
                                        qs[:, (2 * kh + 1) * 128:(2 * kh + 2) * 128]], axis=0).astype(BF16)
                       for qs in queries]
            acc = None
            for par in range(2):
                scores = []
                for qst, ks, msk in zip(stacked, keys, masks):
                    s = _dot_nt(qst, ks[(kh, par)])
                    scores.append(s if msk is None else jnp.where(msk, s, NEG_INF))
                sink = jnp.where(upper_rows, sink_ref[4 * kh + par], sink_ref[4 * kh + 2 + par])
                o = _softmax_av(scores, [vs[(kh, par)] for vs in values], sink)
                acc = o if acc is None else acc + o
            o_ref[0, :, (2 * kh) * 128:(2 * kh + 1) * 128] = acc[:BLOCK].astype(o_ref.dtype)
            o_ref[0, :, (2 * kh + 1) * 128:(2 * kh + 2) * 128] = acc[BLOCK:].astype(o_ref.dtype)

    q = q_ref[0] * WIN_SCALE
    n_ctx = n_ctx_blocks * BLOCK
    kctx = lane_halves(k_ref[0, :n_ctx, :])
    vctx = lane_halves(v_ref[0, :n_ctx, :])

    def ctx_path():
        attend([q], [kctx], [vctx], [None])

    def lat_path():
        blk = j - n_ctx_blocks
        band = [pl.ds(pl.multiple_of((n_ctx_blocks + jnp.clip(blk + d, 0, n_blocks - 1)) * BLOCK, BLOCK), BLOCK)
                for d in (-1, 0, 1)]
        rope = lambda x, rows: _rope_lanes(x, cos_ref[rows, :], sa_ref[rows, :], sb_ref[rows, :], half)
        q_rot = jnp.concatenate([rope(q[:, c * 128:(c + 1) * 128], band[1]) for c in range(q.shape[1] // 128)],
                                axis=-1)
        kband = jnp.concatenate([rope(k_ref[0, rows, :], rows) for rows in band], axis=0)
        vband = jnp.concatenate([v_ref[0, rows, :] for rows in band], axis=0)
        r = lax.broadcasted_iota(jnp.int32, (2 * BLOCK, 3 * BLOCK), 0) % BLOCK
        c = lax.broadcasted_iota(jnp.int32, (2 * BLOCK, 3 * BLOCK), 1)
        first = jnp.where(blk > 0, 0, BLOCK)
        last = jnp.where(blk < n_blocks - 1, 3 * BLOCK, 2 * BLOCK)
        valid = (jnp.abs(c - BLOCK - r) <= WINDOW) & (c >= first) & (c < last)
        attend([q_rot, q], [lane_halves(kband), kctx], [lane_halves(vband), vctx], [valid, None])

    pl.when(j < n_ctx_blocks)(ctx_path)
    pl.when(j >= n_ctx_blocks)(lat_path)


def _win_call(sink, wq, wk, wv, tabs, n_ctx):
    b, n, _ = wq.shape
    ncb = n_ctx // BLOCK
    nb = n // BLOCK
    kvw = WIN_KV_HEADS * WIN_HEAD_DIM
    cur = lambda i, j: (i, j, 0)
    kv = pl.BlockSpec((1, n, kvw), lambda i, j: (i, 0, 0))
    tab = pl.BlockSpec((n, 128), lambda i, j: (0, 0))
    return pl.pallas_call(
        functools.partial(_win_kernel, n_ctx_blocks=ncb, n_blocks=nb - ncb),
        grid=(b, nb),
        in_specs=[pl.BlockSpec(memory_space=pltpu.SMEM),
                  pl.BlockSpec((1, BLOCK, WIN_Q_HEADS * WIN_HEAD_DIM), cur), kv, kv, tab, tab, tab],
        out_specs=pl.BlockSpec((1, BLOCK, WIN_Q_HEADS * WIN_HEAD_DIM), cur),
        out_shape=jax.ShapeDtypeStruct((b, n, WIN_Q_HEADS * WIN_HEAD_DIM), BF16),
        compiler_params=_cparams(2),
    )(sink, wq, wk, wv, *tabs)


def _merge_kernel(x_ref, s5_ref, mla_ref, win_ref, gate_ref, mod_ref, wglu_ref, bglu_ref, wbr_ref, wout_ref,
                  g1_ref, b1_ref, wr_ref, x1_ref, h2_ref, lg_ref, *, alpha):
    d = x_ref.shape[-1]
    g = jax.nn.gelu(s5_ref[0])
    s5o = g * _sigmoid(_bdot(g.astype(BF16), wglu_ref[...]) + bglu_ref[...])
    branches = (s5o.astype(BF16), mla_ref[0], win_ref[0])
    mix = None
    for kk, o in enumerate(branches):
        term = _sigmoid(gate_ref[0, :, kk * d:(kk + 1) * d].astype(F32)) * _bdot(o, wbr_ref[kk])
        mix = term if mix is None else mix + term
    y = _bdot(mix.astype(BF16), wout_ref[...])
    mod = lambda r: mod_ref[0, 0, r:r + 1, :]
    x1 = _layer_norm(alpha * x_ref[0] + mod(2) * y) * g1_ref[...] + b1_ref[...]
    x1_ref[0] = x1
    h2 = (_layer_norm(x1) * (1.0 + mod(4)) + mod(3)).astype(BF16)
    h2_ref[0] = h2
    lg_ref[0] = _dot_nt(wr_ref[...], h2)


def _merge_call(xall, s5y, mla_o, win_o, gates, mod, wglu, bglu, wbr, wout, g1, b1, wr_t, n_ctx_tiles, alpha):
    b, n, d = xall.shape
    tm = TOKEN_TILE
    tok = lambda w: pl.BlockSpec((1, tm, w), lambda i, t: (i, t, 0))
    full = lambda a: pl.BlockSpec(a.shape, lambda i, t: (0,) * a.ndim)
    return pl.pallas_call(
        functools.partial(_merge_kernel, alpha=alpha),
        grid=(b, n // tm),
        in_specs=[tok(d), tok(BRANCH_WIDTH), tok(BRANCH_WIDTH), tok(BRANCH_WIDTH), tok(N_BRANCH * d),
                  pl.BlockSpec((1, 1, 6, d), lambda i, t: (i, jnp.where(t < n_ctx_tiles, 0, 1), 0, 0)),
                  full(wglu), full(bglu), full(wbr), full(wout), full(g1), full(b1), full(wr_t)],
        out_specs=[tok(d), tok(d), pl.BlockSpec((1, N_EXPERTS, tm), lambda i, t: (i, 0, t))],
        out_shape=[jax.ShapeDtypeStruct((b, n, d), F32), jax.ShapeDtypeStruct((b, n, d), BF16),
                   jax.ShapeDtypeStruct((b, N_EXPERTS, n), F32)],
        compiler_params=_cparams(2),
    )(xall, s5y, mla_o, win_o, gates, mod, wglu, bglu, wbr, wout, g1, b1, wr_t)


def _excl_cumsum_lanes(m):
    rows, n = m.shape
    r = lax.broadcasted_iota(jnp.int32, (128, 128), 0)
    c = lax.broadcasted_iota(jnp.int32, (128, 128), 1)
    tri = jnp.where(r < c, 1.0, 0.0).astype(BF16)
    off = jnp.zeros((rows, 1), F32)
    outs, offs = [], []
    for jb in range(n // 128):
        blk = m[:, jb * 128:(jb + 1) * 128]
        offs.append(off)
        outs.append(_bdot(blk.astype(BF16), tri) + off)
        off = off + jnp.sum(blk, axis=1, keepdims=True)
    return jnp.concatenate(outs, axis=1), offs + [off]


def _topk_slots(aff, cap):
    bits = pltpu.bitcast(aff, jnp.int32)

    def body(i, thr):
        cand = thr | (jnp.int32(1) << (30 - i))
        cnt = jnp.sum(jnp.where(bits >= cand, 1.0, 0.0), axis=1, keepdims=True)
        return jnp.where(cnt >= cap, cand, thr)

    thr = lax.fori_loop(0, 31, body, jnp.zeros((aff.shape[0], 1), jnp.int32))
    gt = jnp.where(bits > thr, 1.0, 0.0)
    eq = jnp.where(bits == thr, 1.0, 0.0)
    need = cap - jnp.sum(gt, axis=1, keepdims=True)
    sel = gt + eq * jnp.where(_excl_cumsum_lanes(eq)[0] < need, 1.0, 0.0)
    rank, offs = _excl_cumsum_lanes(sel)
    return jnp.where(sel > 0.5, rank, -1.0).astype(jnp.int32), offs[::MOE_TILE // 128]


def _route_kernel(lg_ref, slot_ref, aff_ref, bnd_ref, *, n_ctx, cap_ctx, cap_lat):
    lg = lg_ref[0]
    m = jnp.max(lg, axis=0, keepdims=True)
    ex = jnp.exp(lg - m)
    aff = ex / jnp.sum(ex, axis=0, keepdims=True)
    aff_ref[0] = aff
    slot_ref[0, :, :n_ctx] = _topk_slots(aff[:, :n_ctx], cap_ctx)[0]
    slots, counts = _topk_slots(aff[:, n_ctx:], cap_lat)
    slot_ref[0, :, n_ctx:] = slots
    lane = lax.broadcasted_iota(jnp.int32, bnd_ref.shape[1:], 1)
    bnd = jnp.zeros(bnd_ref.shape[1:], F32)
    for k, cnt in enumerate(counts):
        bnd = jnp.where(lane == k, cnt, bnd)
    bnd_ref[0] = bnd.astype(jnp.int32)


def _route_call(logits_t, n_ctx, cap_ctx, cap_lat):
    b, e, n = logits_t.shape
    assert (n - n_ctx) % MOE_TILE == 0 and (n - n_ctx) // MOE_TILE < MOE_BOUNDS
    spec = pl.BlockSpec((1, e, n), lambda i: (i, 0, 0))
    return pl.pallas_call(
        functools.partial(_route_kernel, n_ctx=n_ctx, cap_ctx=cap_ctx, cap_lat=cap_lat),
        grid=(b,),
        in_specs=[spec],
        out_specs=[spec, spec, pl.BlockSpec((1, e, MOE_BOUNDS), lambda i: (i, 0, 0))],
        out_shape=[jax.ShapeDtypeStruct((b, e, n), jnp.int32), jax.ShapeDtypeStruct((b, e, n), F32),
                   jax.ShapeDtypeStruct((b, e, MOE_BOUNDS), jnp.int32)],
        compiler_params=_cparams(1),
    )(logits_t)


def _expert_kernel(bnd_ref, slot_ref, aff_ref, h_ref, wg_ref, wu_ref, wd_ref, yl_ref, yc_ref,
                   wg_s, wu_s, wd_s, xl_ref, gl_ref, *, n_ctx):
    cap_lat = yl_ref.shape[2] - MOE_WINDOW
    cap_ctx = yc_ref.shape[2]
    slot = slot_ref[0, 0]
    aff = aff_ref[0, 0]

    @pl.when(pl.program_id(1) == 0)
    def _():
        wg_s[...] = wg_ref[0, 0].astype(BF16)
        wu_s[...] = wu_ref[0, 0].astype(BF16)
        wd_s[...] = wd_ref[0, 0].astype(BF16)

    def slot_gate(lo, hi, cap):
        iota = lax.broadcasted_iota(jnp.int32, (cap, hi - lo), 0)
        return jnp.sum(jnp.where(slot[:, lo:hi] == iota, aff[:, lo:hi], 0.0), axis=1, keepdims=True)

    n = h_ref.shape[1]
    ie, ib = pl.program_id(0), pl.program_id(1)
    xl_ref[...] = jnp.zeros(xl_ref.shape, F32)
    gl_ref[...] = jnp.zeros(gl_ref.shape, F32)
    tiles =range((n - n_ctx) // MOE_TILE)
    firsts = [(bnd_ref[ib, ie, kt] // 16) * 16 for kt in tiles]

    def add_window(kt, start):
        tok = slice(n_ctx + kt * MOE_TILE, n_ctx + (kt + 1) * MOE_TILE)
        start = pl.multiple_of(start, 16)
        rows = start + lax.broadcasted_iota(jnp.int32, (MOE_WINDOW, MOE_TILE), 0)
        hit = slot[:, tok] == rows
        xl_ref[pl.ds(start, MOE_WINDOW), :] += _bdot(jnp.where(hit, 1.0, 0.0).astype(BF16), h_ref[0, tok, :])
        picked = jnp.sum(jnp.where(hit, aff[:, tok], 0.0), axis=1, keepdims=True)
        gl_ref[pl.ds(start, MOE_WINDOW), :] += jnp.broadcast_to(picked, (MOE_WINDOW, 128))

    for kt in tiles:
        add_window(kt, firsts[kt])
    for kt in tiles:
        def more(w, carry, kt=kt):
            add_window(kt, firsts[kt] + (w + 1) * MOE_WINDOW)
            return carry

        n_win = (bnd_ref[ib, ie, kt + 1] - firsts[kt] + MOE_WINDOW - 1) // MOE_WINDOW
        lax.fori_loop(0, jnp.maximum(n_win - 1, 0), more, 0)
    iota = lax.broadcasted_iota(jnp.int32, (cap_ctx, n_ctx), 0)
    xc = _bdot(jnp.where(slot[:, :n_ctx] == iota, 1.0, 0.0).astype(BF16), h_ref[0, :n_ctx, :])
    xs = jnp.concatenate([xl_ref[:cap_lat], xc], axis=0).astype(BF16)
    gate = jnp.concatenate([gl_ref[:cap_lat, 0:1], slot_gate(0, n_ctx, cap_ctx)], axis=0)
    a = _bdot(xs, wg_s[...])
    u = _bdot(xs, wu_s[...])
    hm = (a * _sigmoid(a) * u).astype(BF16)
    y = _bdot(hm, wd_s[...]) * gate
    yl_ref[0, 0, :cap_lat] = y[:cap_lat].astype(yl_ref.dtype)
    yl_ref[0, 0, cap_lat:] = jnp.zeros((MOE_WINDOW, y.shape[1]), yl_ref.dtype)
    yc_ref[0, 0] = y[cap_lat:].astype(yc_ref.dtype)


def _expert_call(layer, bounds, slot, aff, h2, w_gate, w_up, w_down, n_ctx, cap_ctx, cap_lat):
    b, e, n = slot.shape
    d = h2.shape[-1]
    f = w_gate.shape[-1]
    rows_lat = cap_lat + MOE_WINDOW
    row = pl.BlockSpec((1, 1, 1, n), lambda ie, ib: (ib, ie, 0, 0))
    return pl.pallas_call(
        functools.partial(_expert_kernel, n_ctx=n_ctx),
        grid=(e, b),
        in_specs=[pl.BlockSpec(memory_space=pltpu.SMEM), row, row,
                  pl.BlockSpec((1, n, d), lambda ie, ib: (ib, 0, 0)),
                  pl.BlockSpec((1, 1, d, f), lambda ie, ib: (layer, ie, 0, 0)),
                  pl.BlockSpec((1, 1, d, f), lambda ie, ib: (layer, ie, 0, 0)),
                  pl.BlockSpec((1, 1, f, d), lambda ie, ib: (layer, ie, 0, 0))],
        out_specs=[pl.BlockSpec((1, 1, rows_lat, d), lambda ie, ib: (ib, ie, 0, 0)),
                   pl.BlockSpec((1, 1, cap_ctx, d), lambda ie, ib: (ib, ie, 0, 0))],
        out_shape=[jax.ShapeDtypeStruct((b, e, rows_lat, d), BF16), jax.ShapeDtypeStruct((b, e, cap_ctx, d), BF16)],
        scratch_shapes=[pltpu.VMEM((d, f), BF16), pltpu.VMEM((d, f), BF16), pltpu.VMEM((f, d), BF16),
                        pltpu.VMEM((rows_lat, d), F32), pltpu.VMEM((rows_lat, 128), F32)],
        compiler_params=_cparams(2),
    )(bounds, slot.reshape(b, e, 1, n), aff.reshape(b, e, 1, n), h2, w_gate, w_up, w_down)


def _combine_kernel(bnd_ref, slot_ref, yl_ref, yc_ref, x1_ref, mod_ref, g2_ref, b2_ref, o_ref, fl_ref,
                    *, n_ctx_tiles, alpha):
    ib, t = pl.program_id(0), pl.program_id(1)
    tm = x1_ref.shape[1]
    slot = slot_ref[0]
    win = MOE_WINDOW

    def finish(fl):
        x1 = x1_ref[0]
        o_ref[0] = _layer_norm(alpha * x1 + mod_ref[0, 0, 5:6, :] * fl) * g2_ref[...] + b2_ref[...]

    def onehot(e, first, width):
        iota = lax.broadcasted_iota(jnp.int32, (tm, width), 1)
        return jnp.where(slot[:, e:e + 1] - first == iota, 1.0, 0.0).astype(BF16)

    def ctx_path():
        cap = yc_ref.shape[2]
        fl = None
        for e in range(N_EXPERTS):
            term = _bdot(onehot(e, 0, cap), yc_ref[0, e])
            fl = term if fl is None else fl + term
        finish(fl)

    def lat_path():
        kt = t - n_ctx_tiles
        firsts = [pl.multiple_of((bnd_ref[ib, e, kt] // 16) * 16, 16) for e in range(N_EXPERTS)]
        lane = lax.broadcasted_iota(jnp.int32, (tm, 2 * win), 1)
        pieces, ywins = [], []
        for e in range(0, N_EXPERTS, 2):
            rel = jnp.where(lane < win, slot[:, e:e + 1] - firsts[e], slot[:, e + 1:e + 2] - firsts[e + 1] + win)
            pieces.append(jnp.where(rel == lane, 1.0, 0.0).astype(BF16))
            ywins += [yl_ref[0, e, pl.ds(firsts[e], win), :], yl_ref[0, e + 1, pl.ds(firsts[e + 1], win), :]]
        fl_ref[...] = _bdot(jnp.concatenate(pieces, axis=1), jnp.concatenate(ywins, axis=0))
        for e in range(N_EXPERTS):
            def window(w, carry, e=e):
                first = pl.multiple_of(firsts[e] + (w + 1) * win, 16)
                fl_ref[...] += _bdot(onehot(e, first, win), yl_ref[0, e, pl.ds(first, win), :])
                return carry

            n_win = (bnd_ref[ib, e, kt + 1] - firsts[e] + win - 1) // win
            lax.fori_loop(0, jnp.maximum(n_win - 1, 0), window, 0)
        finish(fl_ref[...])

    pl.when(t < n_ctx_tiles)(ctx_path)
    pl.when(t >= n_ctx_tiles)(lat_path)


def _combine_call(bounds, slot_t, yl, yc, x1, mod, g2, b2, n_ctx_tiles, alpha):
    b, n, d = x1.shape
    tm = MOE_TILE
    e = N_EXPERTS
    full = lambda a: pl.BlockSpec(a.shape, lambda i, t: (0,) * a.ndim)
    return pl.pallas_call(
        functools.partial(_combine_kernel, n_ctx_tiles=n_ctx_tiles, alpha=alpha),
        grid=(b, n // tm),
        scratch_shapes=[pltpu.VMEM((tm, d), F32)],
        in_specs=[pl.BlockSpec(memory_space=pltpu.SMEM),
                  pl.BlockSpec((1, tm, e), lambda i, t: (i, t, 0)),
                  pl.BlockSpec((1,) + yl.shape[1:], lambda i, t: (i, 0, 0, 0)),
                  pl.BlockSpec((1,) + yc.shape[1:], lambda i, t: (i, 0, 0, 0)),
                  pl.BlockSpec((1, tm, d), lambda i, t: (i, t, 0)),
                  pl.BlockSpec((1, 1, 6, d), lambda i, t: (i, jnp.where(t < n_ctx_tiles, 0, 1), 0, 0)),
                  full(g2), full(b2)],
        out_specs=pl.BlockSpec((1, tm, d), lambda i, t: (i, t, 0)),
        out_shape=jax.ShapeDtypeStruct((b, n, d), F32),
        compiler_params=_cparams(2),
    )(bounds, slot_t, yl, yc, x1, mod, g2, b2)


def _rope_tables(n_ctx, seq, head_dim, lane_offset):
    half = head_dim // 2
    nf = head_dim // 4
    t = jnp.arange(seq, dtype=F32)
    row = jnp.floor(t / GRID_W)
    col = t - row * GRID_W
    freqs = ROPE_BASE ** (-jnp.arange(nf, dtype=F32) / nf)
    ang = jnp.concatenate([row[:, None] * freqs, col[:, None] * freqs], axis=-1)
    cos, sin = jnp.cos(ang), jnp.sin(ang)
    zeros = jnp.zeros_like(sin)
    n_heads = (128 - lane_offset) // head_dim if lane_offset == 0 else 1
    c = jnp.concatenate([jnp.ones((seq, lane_offset), F32)] + [cos, cos] * n_heads, axis=-1)
    sa = jnp.concatenate([jnp.zeros((seq, lane_offset), F32)] + [-sin, zeros] * n_heads, axis=-1)
    sb = jnp.concatenate([jnp.zeros((seq, lane_offset), F32)] + [zeros, sin] * n_heads, axis=-1)
    pad = 128 - c.shape[1]
    c = jnp.pad(c, ((n_ctx, 0), (0, pad)), constant_values=1.0)
    sa = jnp.pad(sa, ((n_ctx, 0), (0, pad)))
    sb = jnp.pad(sb, ((n_ctx, 0), (0, pad)))
    return c, sa, sb


def _layer_weights(i, p):
    d = p['w_in'].shape[1]
    pts = np.cumsum((S5_WIDTH, MLA_Q_RANK, MLA_KV_RANK, MLA_ROPE, WIN_Q_HEADS * WIN_HEAD_DIM,
                     WIN_KV_HEADS * WIN_HEAD_DIM, WIN_KV_HEADS * WIN_HEAD_DIM))
    cols = jnp.split(p['w_in'][i], [int(v) for v in pts], axis=1)
    kr = jnp.pad(cols[3], ((0, 0), (MLA_NOPE, MLA_PAD - MLA_NOPE - MLA_ROPE)))
    w_cat = jnp.concatenate([cols[0], cols[1], cols[2], kr, cols[4], cols[5], cols[6], cols[7]], axis=1)
    dq = MLA_NOPE + MLA_ROPE
    wq = p['mla_w_uq'][i].reshape(MLA_Q_RANK, MLA_HEADS, dq)
    wq = jnp.pad(wq, ((0, 0), (0, 0), (0, MLA_PAD - dq))).reshape(MLA_Q_RANK, MLA_HEADS * MLA_PAD)
    wkv = p['mla_w_ukv'][i].reshape(MLA_KV_RANK, MLA_HEADS, MLA_NOPE + MLA_V)
    wk = jnp.pad(wkv[:, :, :MLA_NOPE], ((0, 0), (0, 0), (0, MLA_PAD - MLA_NOPE)))
    wk = wk.reshape(MLA_KV_RANK, MLA_HEADS * MLA_PAD)
    wv = wkv[:, :, MLA_NOPE:].reshape(MLA_KV_RANK, MLA_HEADS * MLA_V)
    row = lambda a: a[i].astype(F32).reshape(1, -1)
    return dict(
        w_cat=w_cat.astype(BF16), wq=wq.astype(BF16), wk=wk.astype(BF16), wv=wv.astype(BF16),
        qg=row(p['mla_q_norm']), kvg=row(p['mla_kv_norm']),
        wglu=p['s5_w_glu'][i].astype(BF16), bglu=row(p['s5_b_glu']),
        sink=p['win_sink'][i].astype(F32),
        wbr=p['w_branch'][i].astype(BF16), wout=p['w_out'][i].astype(BF16),
        g1=row(p['ln1_g']), b1=row(p['ln1_b']), g2=row(p['ln2_g']), b2=row(p['ln2_b']),
        wr_t=p['w_router'][i].T.astype(BF16),
    )


def _forward(p):
    x, c, ctx, c_ctx = p['x'], p['c'], p['ctx'], p['c_ctx']
    b, seq, d = x.shape
    n_ctx = ctx.shape[1]
    depth = p['w_ada'].shape[0]
    assert b == 8 and seq % TOKEN_TILE == 0 and n_ctx % TOKEN_TILE == 0 and seq % GRID_W == 0
    alpha = float((2 * depth) ** 0.25)
    n_ctx_tiles = n_ctx // TOKEN_TILE
    cap_lat = CAPACITY_FACTOR * seq // N_EXPERTS
    cap_ctx = CAPACITY_FACTOR * n_ctx // N_EXPERTS

    cond = jnp.concatenate([c, c_ctx[None], jnp.zeros((16 - b - 1, d), F32)], axis=0)
    mods = _ada_call(cond, p['w_ada'], p['b_ada'])
    mods = mods.reshape(depth, 16, 6, d)
    tabs_mla = _rope_tables(n_ctx, seq, MLA_ROPE, MLA_NOPE)
    tabs_win = _rope_tables(n_ctx, seq, WIN_HEAD_DIM, 0)
    s5w = _s5_param_call(p)

    xall = jnp.concatenate([ctx, x], axis=1)
    for i in range(depth):
        w = _layer_weights(i, p)
        mod = jnp.stack([jnp.broadcast_to(mods[i, b], (b, 6, d)), mods[i, :b]], axis=1)
        u, qa, kva, kr, wq, wk, wv, gates = _in_call(xall, mod, w['w_cat'], n_ctx)
        s5y = _s5_call(i, u, *s5w, p['s5_d'], n_ctx)
        qp, qr, kk, vv = _mla_prep_call(qa, kva, kr, w['qg'], w['kvg'], w['wq'], w['wk'], w['wv'], tabs_mla)
        mla_o = _mla_attn_call(qp, qr, kk, vv, n_ctx)
        win_o = _win_call(w['sink'], wq, wk, wv, tabs_win, n_ctx)
        x1, h2, logits_t = _merge_call(xall, s5y, mla_o, win_o, gates, mod, w['wglu'], w['bglu'], w['wbr'],
                                       w['wout'], w['g1'], w['b1'], w['wr_t'], n_ctx_tiles, alpha)
        slot, aff, bounds = _route_call(logits_t, n_ctx, cap_ctx, cap_lat)
        yl, yc = _expert_call(i, bounds, slot, aff, h2, p['w_gate'], p['w_up'], p['w_down'], n_ctx, cap_ctx,
                              cap_lat)
        slot_t = jnp.swapaxes(slot, 1, 2)
        xall = _combine_call(bounds, slot_t, yl, yc, x1, mod, w['g2'], w['b2'], n_ctx_tiles, alpha)
    return xall[:, n_ctx:]


def kernel(x, c, ctx, c_ctx, w_ada, b_ada, w_in, s5_lam_re, s5_lam_im, s5_log_dt, s5_b_re, s5_b_im, s5_c_re, s5_c_im, s5_d, s5_w_glu, s5_b_glu, mla_q_norm, mla_w_uq, mla_kv_norm, mla_w_ukv, win_sink, w_branch, w_out, ln1_g, ln1_b, ln2_g, ln2_b, w_router, w_gate, w_up, w_down):
    return _forward(dict(
        x=x, c=c, ctx=ctx, c_ctx=c_ctx, w_ada=w_ada, b_ada=b_ada, w_in=w_in, s5_lam_re=s5_lam_re,
        s5_lam_im=s5_lam_im, s5_log_dt=s5_log_dt, s5_b_re=s5_b_re, s5_b_im=s5_b_im, s5_c_re=s5_c_re,
        s5_c_im=s5_c_im, s5_d=s5_d, s5_w_glu=s5_w_glu, s5_b_glu=s5_b_glu, mla_q_norm=mla_q_norm,
        mla_w_uq=mla_w_uq, mla_kv_norm=mla_kv_norm, mla_w_ukv=mla_w_ukv, win_sink=win_sink, w_branch=w_branch,
        w_out=w_out, ln1_g=ln1_g, ln1_b=ln1_b, ln2_g=ln2_g, ln2_b=ln2_b, w_router=w_router, w_gate=w_gate,
        w_up=w_up, w_down=w_down))
```

```python
import functools
import math

import jax
import jax.numpy as jnp
import numpy as np
from jax import lax
from jax.experimental import pallas as pl
from jax.experimental.pallas import tpu as pltpu

F32 = jnp.float32
BF16 = jnp.bfloat16
HIGHEST = lax.Precision.HIGHEST

GRID_W = 64
S5_WIDTH = 512
S5_GROUP = 16
S5_GROUPS = S5_WIDTH // S5_GROUP
S5_STATE = 64
S5_CHUNK = 16
S5_GROUPS_PER_STEP = 4
MLA_HEADS = 8
MLA_NOPE = 64
MLA_ROPE = 32
MLA_V = 64
MLA_Q_RANK = 384
MLA_KV_RANK = 256
MLA_PAD = 128
MLA_HEADS_PER_STEP = 4
MLA_SCALE = (MLA_NOPE + MLA_ROPE) ** -0.5
WIN_Q_HEADS = 8
WIN_KV_HEADS = 2
WIN_GROUP = WIN_Q_HEADS // WIN_KV_HEADS
WIN_HEAD_DIM = 64
WINDOW = 128
BLOCK = 128
WIN_SCALE = WIN_HEAD_DIM ** -0.5
N_BRANCH = 3
BRANCH_WIDTH = 512
N_EXPERTS = 16
CAPACITY_FACTOR = 2
ROPE_BASE = 10000.0
LN_EPS = 1e-6
NEG_INF = -1e30
TOKEN_TILE = 256
IN_TILE = 768
MOE_TILE = 256
MOE_WINDOW = 64
MOE_BOUNDS = 16
FFN_SAMPLES = 2
VMEM_LIMIT = 56 * 1024 * 1024


def _cparams(n_axes):
    return pltpu.CompilerParams(dimension_semantics=("arbitrary",) * n_axes, vmem_limit_bytes=VMEM_LIMIT)


def _bdot(a, b):
    return jnp.dot(a, b, preferred_element_type=F32)


def _dot_nt(a, b):
    return lax.dot_general(a, b, (((1,), (1,)), ((), ())), preferred_element_type=F32)


def _layer_norm(x):
    mu = jnp.mean(x, axis=-1, keepdims=True)
    xc = x - mu
    var = jnp.mean(xc * xc, axis=-1, keepdims=True)
    return xc * lax.rsqrt(var + LN_EPS)


def _ada_kernel(cond_ref, w_ref, b_ref, o_ref):
    s = cond_ref[...]
    s = s * jax.nn.sigmoid(s)
    o_ref[0] = jnp.dot(s, w_ref[0], precision=HIGHEST, preferred_element_type=F32) + b_ref[0]


def _ada_call(cond, w_ada, b_ada):
    depth, d, d6 = w_ada.shape
    tn = 1536
    rows = cond.shape[0]
    return pl.pallas_call(
        _ada_kernel,
        grid=(depth, d6 // tn),
        in_specs=[
            pl.BlockSpec((rows, d), lambda i, j: (0, 0)),
            pl.BlockSpec((1, d, tn), lambda i, j: (i, 0, j)),
            pl.BlockSpec((1, 1, tn), lambda i, j: (i, 0, j)),
        ],
        out_specs=pl.BlockSpec((1, rows, tn), lambda i, j: (i, 0, j)),
        out_shape=jax.ShapeDtypeStruct((depth, rows, d6), F32),
        compiler_params=_cparams(2),
    )(cond, w_ada, b_ada.reshape(depth, 1, d6))


IN_WIDTHS = (S5_WIDTH, MLA_Q_RANK, MLA_KV_RANK, MLA_PAD, WIN_Q_HEADS * WIN_HEAD_DIM,
             WIN_KV_HEADS * WIN_HEAD_DIM, WIN_KV_HEADS * WIN_HEAD_DIM)
IN_OFFSETS = tuple(int(v) for v in np.cumsum((0,) + IN_WIDTHS))


def _mod_rows(mod_ref, r, tile, n_ctx):
    row = pl.program_id(1) * tile + lax.broadcasted_iota(jnp.int32, (tile, 1), 0)
    return jnp.where(row < n_ctx, mod_ref[0, 0, r:r + 1, :], mod_ref[0, 1, r:r + 1, :])


def _in_kernel(x_ref, mod_ref, w_ref, *out_refs, n_ctx):
    tile = x_ref.shape[1]
    xn = _layer_norm(x_ref[0])
    h = (xn * (1.0 + _mod_rows(mod_ref, 1, tile, n_ctx)) + _mod_rows(mod_ref, 0, tile, n_ctx)).astype(BF16)
    widths = IN_WIDTHS + (out_refs[-1].shape[-1],)
    for off, width, o_ref in zip(IN_OFFSETS, widths, out_refs):
        o_ref[0] = _bdot(h, w_ref[:, off:off + width]).astype(o_ref.dtype)


_sigmoid = jax.nn.sigmoid


def _in_call(xall, mod, w_cat, n_ctx):
    b, n, d = xall.shape
    tm = IN_TILE
    gate_w = w_cat.shape[1] - IN_OFFSETS[-1]
    widths = IN_WIDTHS + (gate_w,)
    return pl.pallas_call(
        functools.partial(_in_kernel, n_ctx=n_ctx),
        grid=(b, n // tm),
        in_specs=[
            pl.BlockSpec((1, tm, d), lambda i, t: (i, t, 0)),
            pl.BlockSpec((1, 2, 6, d), lambda i, t: (i, 0, 0, 0)),
            pl.BlockSpec(w_cat.shape, lambda i, t: (0, 0), pipeline_mode=pl.Buffered(1)),
        ],
        out_specs=[pl.BlockSpec((1, tm, w), lambda i, t: (i, t, 0)) for w in widths],
        out_shape=[jax.ShapeDtypeStruct((b, n, w), F32) for w in IN_WIDTHS]
        + [jax.ShapeDtypeStruct((b, n, gate_w), BF16)],
        compiler_params=_cparams(2),
    )(xall, mod, w_cat)


S5_LANE_GROUPS = 128 // S5_GROUP
S5_SCAN_GROUPS = 4
S5_PITCH_PAD = 8


def _s5_param_kernel(cre_ref, cim_ref, bre_ref, bim_ref, pr_ref, pi_ref, tz_ref, bc_ref, cc_ref, coef_ref):
    t = S5_CHUNK
    w = t * S5_GROUP
    nt = (((1,), (1,)), ((), ()))
    tz = None
    bcs, ccs, coefs = [], [], []
    for d in range(2):
        cre, cim = cre_ref[0, d, 0], cim_ref[0, d, 0]
        bre, bim = bre_ref[0, d, 0], bim_ref[0, d, 0]
        power = lambda k: (pr_ref[0, d, 0, k:k + 1, :], pi_ref[0, d, 0, k:k + 1, :])
        rt = []
        for k in range(t + 1):
            prk, pik = power(k)
            rt.append(jnp.concatenate([cre * prk - cim * pik, -(cre * pik + cim * prk)], axis=1))
        bt = jnp.concatenate([bre, bim], axis=1)
        zeros = jnp.zeros((S5_GROUP, w), F32)
        if d == 0:
            kt = lax.dot_general(bt, jnp.concatenate(rt[:t], axis=0), nt, precision=HIGHEST,
                                 preferred_element_type=F32)
            pad = jnp.concatenate([zeros, kt], axis=1)
            rows = [kt] + [pltpu.roll(pad, S5_GROUP * s, 1)[:, w:] for s in range(1, t)]
        else:
            kt = lax.dot_general(bt, jnp.concatenate(rt[t - 1::-1], axis=0), nt, precision=HIGHEST,
                                 preferred_element_type=F32)
            pad = jnp.concatenate([kt, zeros], axis=1)
            rows = [pltpu.roll(pad, 2 * w - S5_GROUP * (t - 1 - s), 1)[:, :w] for s in range(t - 1)] + [kt]
        tz_d = jnp.concatenate(rows, axis=0)
        tz = tz_d if tz is None else tz + tz_d
        bc_rows = []
        for s in range(t):
            prk, pik = power(t - 1 - s if d == 0 else s)
            br = bre * prk - bim * pik
            bi = bim * prk + bre * pik
            bc_rows.append(jnp.concatenate([br, bi, bi, br], axis=1))
        bcs.append(jnp.concatenate(bc_rows, axis=0))
        ccs.append(jnp.concatenate(rt[1:] if d == 0 else rt[t:0:-1], axis=0))
        er, ei = power(t)
        coefs += [jnp.concatenate([er, er], axis=1), jnp.concatenate([-ei, ei], axis=1),
                  jnp.concatenate([ei, -ei], axis=1)]
    tz_ref[0, 0] = tz.astype(BF16)
    bc_ref[0, 0] = jnp.concatenate(bcs, axis=1).astype(BF16)
    cc_ref[0, 0] = jnp.concatenate(ccs, axis=1).astype(BF16)
    coef_ref[0, 0] = jnp.concatenate(coefs + [jnp.zeros((2, 2 * S5_STATE), F32)], axis=0)


def _s5_param_call(p):
    t = S5_CHUNK
    f = lambda name: p[name].astype(F32)
    lam_re, lam_im = f('s5_lam_re'), f('s5_lam_im')
    depth = lam_re.shape[0]
    dt = jnp.exp(f('s5_log_dt'))[..., None]
    k = jnp.arange(t + 1, dtype=F32)[:, None]
    mag = jnp.exp((lam_re * dt)[..., None, :] * k)
    ang = (lam_im * dt)[..., None, :] * k
    pr, pi = mag * jnp.cos(ang), mag * jnp.sin(ang)
    ar, ai = pr[..., 1, :], pi[..., 1, :]
    den = lam_re * lam_re + lam_im * lam_im
    qr = (((ar - 1) * lam_re + ai * lam_im) / den)[..., None, :]
    qi = ((ai * lam_re - (ar - 1) * lam_im) / den)[..., None, :]
    b_re = jnp.swapaxes(f('s5_b_re'), -1, -2)
    b_im = jnp.swapaxes(f('s5_b_im'), -1, -2)
    bbr = qr * b_re - qi * b_im
    bbi = qr * b_im + qi * b_re
    g, hg, ps = S5_GROUPS, S5_GROUP, S5_STATE
    w = t * hg
    small = lambda rows: pl.BlockSpec((1, 2, 1, rows, ps), lambda i, j: (i, 0, j, 0, 0))
    out = lambda cols: pl.BlockSpec((1, 1, w, cols), lambda i, j: (i, j, 0, 0))
    return pl.pallas_call(
        _s5_param_kernel,
        grid=(depth, g),
        in_specs=[small(hg)] * 4 + [small(t + 1)] * 2,
        out_specs=[out(w), out(2 * w), out(w), pl.BlockSpec((1, 1, 8, 2 * ps), lambda i, j: (i, j, 0, 0))],
        out_shape=[jax.ShapeDtypeStruct((depth, g, w, w), BF16), jax.ShapeDtypeStruct((depth, g, w, 2 * w), BF16),
                   jax.ShapeDtypeStruct((depth, g, w, w), BF16), jax.ShapeDtypeStruct((depth, g, 8, 2 * ps), F32)],
        compiler_params=_cparams(2),
    )(f('s5_c_re'), f('s5_c_im'), bbr, bbi, pr, pi)


def _s5_kernel(u_ref, tz_ref, bc_ref, cc_ref, coef_ref, d_ref, y_ref, uy_ref, loc_ref, sp_ref, slab_ref,
               *, n_ctx, n_batch):
    ph, b = pl.program_id(1), pl.program_id(2)
    t, hg, ng = S5_CHUNK, S5_GROUP, S5_LANE_GROUPS
    ncc = n_ctx // t
    ncl = (u_ref.shape[1] - n_ctx) // t
    nc = ncc + ncl
    pitch = nc + S5_PITCH_PAD

    def to_chunk_rows(slabs):
        tr = [s.T for s in slabs]
        return [jnp.concatenate([x[g * hg:(g + 1) * hg] for x in tr], axis=0).T for g in range(ng)]

    def to_token_slabs(rows):
        tr = [r.T for r in rows]
        return [jnp.concatenate([x[tau * hg:(tau + 1) * hg] for x in tr], axis=0).T for tau in range(t)]

    base = pl.multiple_of(b * pitch, 8)
    cbase = pl.multiple_of(b * ncc, 8)

    @pl.when(ph == 0)
    def _():
        rows = to_chunk_rows([u_ref[0, pl.ds(n_ctx + tau, ncl, stride=t), :] for tau in range(t)])
        for g in range(ng):
            uy_ref[g, pl.ds(base + ncc, ncl), :] = rows[g]
            uy_ref[g, pl.ds(base + nc, S5_PITCH_PAD), :] = jnp.zeros((S5_PITCH_PAD, t * hg), F32)
        for tau in range(t):
            slab_ref[tau, pl.ds(cbase, ncc), :] = u_ref[0, pl.ds(tau, ncc, stride=t), :]

    @pl.when((ph == 1) & (b == 0))
    def _():
        rows = to_chunk_rows([slab_ref[tau] for tau in range(t)])
        for g in range(ng):
            for s in range(n_batch):
                uy_ref[g, s * pitch:s * pitch + ncc, :] = rows[g][s * ncc:(s + 1) * ncc]
        for part in range(ng // S5_SCAN_GROUPS):
            gs = [part * S5_SCAN_GROUPS + gl for gl in range(S5_SCAN_GROUPS)]
            for gl, g in enumerate(gs):
                ub = uy_ref[g].astype(BF16)
                loc = _bdot(ub, bc_ref[0, g])
                for q in range(4):
                    loc_ref[gl, q] = loc[:, q * 128:(q + 1) * 128]
                uy_ref[g] = _bdot(ub, tz_ref[0, g])
                for s in range(n_batch):
                    for d in range(2):
                        sp_ref[gl, d, s * pitch + nc:(s + 1) * pitch, :] = jnp.zeros((S5_PITCH_PAD, 128), F32)

            def coef(g, r):
                return jnp.broadcast_to(coef_ref[0, g, r:r + 1, :], (n_batch, 128))

            def step(i, carry):
                cb = jnp.where(i < ncc, ncc - 1 - i, nc + ncc - 1 - i)
                fwd = pl.ds(i, n_batch, stride=pitch)
                bwd = pl.ds(cb, n_batch, stride=pitch)
                out = []
                for gl, g in enumerate(gs):
                    v0f, v1f, v0b, v1b = carry[gl]
                    sp_ref[gl, 0, fwd, :] = v0f
                    sp_ref[gl, 1, bwd, :] = v0b
                    n0f = coef(g, 0) * v0f + coef(g, 1) * v1f + loc_ref[gl, 0, fwd, :]
                    n1f = coef(g, 0) * v1f + coef(g, 2) * v0f + loc_ref[gl, 1, fwd, :]
                    n0b = coef(g, 3) * v0b + coef(g, 4) * v1b + loc_ref[gl, 2, bwd, :]
                    n1b = coef(g, 3) * v1b + coef(g, 5) * v0b + loc_ref[gl, 3, bwd, :]
                    out.append((n0f, n1f, n0b, n1b))
                return tuple(out)

            z = jnp.zeros((n_batch, 128), F32)
            lax.fori_loop(0, nc, step, tuple((z, z, z, z) for _ in gs))
            for gl, g in enumerate(gs):
                sp = jnp.concatenate([sp_ref[gl, 0], sp_ref[gl, 1]], axis=1).astype(BF16)
                uy_ref[g] = uy_ref[g] + _dot_nt(sp, cc_ref[0, g])
        rows = [jnp.concatenate([uy_ref[g, s * pitch:s * pitch + ncc, :] for s in range(n_batch)], axis=0)
                for g in range(ng)]
        for tau, slab in enumerate(to_token_slabs(rows)):
            slab_ref[tau] = slab

    @pl.when(ph == 1)
    def _():
        slabs = to_token_slabs([uy_ref[g, pl.ds(base + ncc, ncl), :] for g in range(ng)])
        for tau in range(t):
            y_ref[0, pl.ds(n_ctx + tau, ncl, stride=t), :] = slabs[tau]
            y_ref[0, pl.ds(tau, ncc, stride=t), :] = slab_ref[tau, pl.ds(cbase, ncc), :]
        y_ref[0] = y_ref[0] + d_ref[0] * u_ref[0]


def _s5_call(layer, u, tz, bc, cc, coef, s5_d, n_ctx):
    b, n, width = u.shape
    t, ng = S5_CHUNK, S5_LANE_GROUPS
    rows = b * (n // t + S5_PITCH_PAD)
    assert (n - n_ctx) // t == 128 and b * (n_ctx // t) == 128
    wspec = lambda a: pl.BlockSpec((1, ng) + a.shape[2:], lambda g, ph, i: (layer, g, 0, 0))
    return pl.pallas_call(
        functools.partial(_s5_kernel, n_ctx=n_ctx, n_batch=b),
        grid=(width // 128, 2, b),
        in_specs=[pl.BlockSpec((1, n, 128), lambda g, ph, i: (i, 0, g)),
                  wspec(tz), wspec(bc), wspec(cc), wspec(coef),
                  pl.BlockSpec((1, 1, 128), lambda g, ph, i: (layer, 0, g))],
        out_specs=pl.BlockSpec((1, n, 128), lambda g, ph, i: (i * ph, 0, g)),
        out_shape=jax.ShapeDtypeStruct((b, n, width), F32),
        scratch_shapes=[pltpu.VMEM((ng, rows, t * S5_GROUP), F32),
                        pltpu.VMEM((S5_SCAN_GROUPS, 4, rows, 128), F32),
                        pltpu.VMEM((S5_SCAN_GROUPS, 2, rows, 128), F32),
                        pltpu.VMEM((t, 128, 128), F32)],
        compiler_params=_cparams(3),
    )(u, tz, bc, cc, coef, s5_d.astype(F32).reshape(s5_d.shape[0], 1, width))


def _rope_lanes(x, cos, sa, sb, shift):
    return x * cos + pltpu.roll(x, 128 - shift, 1) * sa + pltpu.roll(x, shift, 1) * sb


def _mla_prep_kernel(qa_ref, kva_ref, kr_ref, qg_ref, kvg_ref, wq_ref, wk_ref, wv_ref,
                     cos_ref, sa_ref, sb_ref, qp_ref, qr_ref, k_ref, v_ref):
    def rms(x, gain):
        return (x * lax.rsqrt(jnp.mean(x * x, axis=-1, keepdims=True) + LN_EPS) * gain).astype(BF16)

    cos, sa, sb = cos_ref[...], sa_ref[...], sb_ref[...]
    qn = rms(qa_ref[0], qg_ref[...])
    kvn = rms(kva_ref[0], kvg_ref[...])
    q = _bdot(qn, wq_ref[...]) * MLA_SCALE
    k = _bdot(kvn, wk_ref[...])
    v_ref[0] = _bdot(kvn, wv_ref[...]).astype(BF16)
    kr_rot = _rope_lanes(kr_ref[0], cos, sa, sb, MLA_ROPE // 2)
    qp_ref[0] = q.astype(BF16)
    for h in range(MLA_HEADS):
        sl = slice(h * MLA_PAD, (h + 1) * MLA_PAD)
        qr_ref[0, :, sl] = _rope_lanes(q[:, sl], cos, sa, sb, MLA_ROPE // 2).astype(BF16)
        k_ref[0, :, sl] = (k[:, sl] + kr_rot).astype(BF16)


def _mla_prep_call(qa, kva, kr, qg, kvg, wq, wk, wv, tabs):
    b, n, _ = qa.shape
    tm = TOKEN_TILE
    tok = lambda w: pl.BlockSpec((1, tm, w), lambda i, t: (i, t, 0))
    full = lambda a: pl.BlockSpec(a.shape, lambda i, t: (0,) * a.ndim)
    tab = pl.BlockSpec((tm, 128), lambda i, t: (t, 0))
    hw = MLA_HEADS * MLA_PAD
    return pl.pallas_call(
        _mla_prep_kernel,
        grid=(b, n // tm),
        in_specs=[tok(MLA_Q_RANK), tok(MLA_KV_RANK), tok(MLA_PAD), full(qg), full(kvg), full(wq), full(wk),
                  full(wv), tab, tab, tab],
        out_specs=[tok(hw), tok(hw), tok(hw), tok(MLA_HEADS * MLA_V)],
        out_shape=[jax.ShapeDtypeStruct((b, n, hw), BF16)] * 3
        + [jax.ShapeDtypeStruct((b, n, MLA_HEADS * MLA_V), BF16)],
        compiler_params=_cparams(2),
    )(qa, kva, kr, qg, kvg, wq, wk, wv, *tabs)


def _lane_chunks(xs):
    return [x[:, i * 128:(i + 1) * 128] for x in xs for i in range(x.shape[1] // 128)]


def _row_max(scores, floor=None):
    mm = functools.reduce(jnp.maximum, _lane_chunks(scores))
    if floor is not None:
        mm = jnp.maximum(mm, floor)
    return jnp.max(mm, axis=-1, keepdims=True)


def _softmax_av(scores, values, sink=None):
    m = _row_max(scores, sink)
    ps = [jnp.exp(s - m) for s in scores]
    ll = functools.reduce(jnp.add, _lane_chunks(ps))
    if sink is not None:
        lane = lax.broadcasted_iota(jnp.int32, sink.shape, 1)
        ll = ll + jnp.where(lane == 0, jnp.exp(sink - m), 0.0)
    l = jnp.sum(ll, axis=-1, keepdims=True)
    o = functools.reduce(jnp.add, [_bdot(p.astype(BF16), v) for p, v in zip(ps, values)])
    return o / l


def _mla_attn_kernel(qp_ref, qr_ref, k_ref, v_ref, o_ref, *, n_ctx, n_ctx_tiles):
    heads = qp_ref.shape[-1] // MLA_PAD
    t = pl.program_id(2)
    lane = lax.broadcasted_iota(jnp.int32, (1, heads * MLA_V), 1)

    def run(latent):
        acc = None
        for h in range(heads):
            sl = slice(h * MLA_PAD, (h + 1) * MLA_PAD)
            own = (lane >= h * MLA_V) & (lane < (h + 1) * MLA_V)
            zero = jnp.zeros((), BF16)
            scores = [_dot_nt(qp_ref[0, :, sl], k_ref[0, :n_ctx, sl])]
            values = [jnp.where(own, v_ref[0, :n_ctx, :], zero)]
            if latent:
                scores.append(_dot_nt(qr_ref[0, :, sl], k_ref[0, n_ctx:, sl]))
                values.append(jnp.where(own, v_ref[0, n_ctx:, :], zero))
            o = _softmax_av(scores, values)
            acc = o if acc is None else acc + o
        o_ref[0] = acc.astype(o_ref.dtype)

    pl.when(t < n_ctx_tiles)(lambda: run(False))
    pl.when(t >= n_ctx_tiles)(lambda: run(True))


def _mla_attn_call(qp, qr, k, v, n_ctx):
    b, n, _ = qp.shape
    tq = TOKEN_TILE
    hp = MLA_HEADS_PER_STEP
    qspec = pl.BlockSpec((1, tq, hp * MLA_PAD), lambda i, h, t: (i, t, h))
    return pl.pallas_call(
        functools.partial(_mla_attn_kernel, n_ctx=n_ctx, n_ctx_tiles=n_ctx // tq),
        grid=(b, MLA_HEADS // hp, n // tq),
        in_specs=[qspec, qspec,
                  pl.BlockSpec((1, n, hp * MLA_PAD), lambda i, h, t: (i, 0, h)),
                  pl.BlockSpec((1, n, hp * MLA_V), lambda i, h, t: (i, 0, h))],
        out_specs=pl.BlockSpec((1, tq, hp * MLA_V), lambda i, h, t: (i, t, h)),
        out_shape=jax.ShapeDtypeStruct((b, n, MLA_HEADS * MLA_V), BF16),
        compiler_params=_cparams(3),
    )(qp, qr, k, v)


def _win_kernel(sink_ref, q_ref, k_ref, v_ref, cos_ref, sa_ref, sb_ref, o_ref, *, n_ctx_blocks, n_blocks):
    j = pl.program_id(1)
    hd = WIN_HEAD_DIM
    half = hd // 2
    lane = lax.broadcasted_iota(jnp.int32, (1, 128), 1)
    lo = jnp.where(lane < hd, 1.0, 0.0)
    hi = 1.0 - lo
    upper_rows = lax.broadcasted_iota(jnp.int32, (2 * BLOCK, 128), 0) < BLOCK

    def lane_halves(x):
        xr = pltpu.roll(x, hd, 1)
        return {(0, 0): (x * lo).astype(BF16), (0, 1): (xr * hi).astype(BF16),
                (1, 0): (xr * lo).astype(BF16), (1, 1): (x * hi).astype(BF16)}

    def attend(queries, keys, values, masks):
        for kh in range(WIN_KV_HEADS):
            stacked = [jnp.concatenate([qs[:, (2 * kh) * 128:(2 * kh + 1) * 128],
                                        qs[:, (2 * kh + 1) * 128:(2 * kh + 2) * 128]], axis=0).astype(BF16)
                       for qs in queries]
            acc = None
            for par in range(2):
                scores = []
                for qst, ks, msk in zip(stacked, keys, masks):
                    s = _dot_nt(qst, ks[(kh, par)])
                    scores.append(s if msk is None else jnp.where(msk, s, NEG_INF))
                sink = jnp.where(upper_rows, sink_ref[4 * kh + par], sink_ref[4 * kh + 2 + par])
                o = _softmax_av(scores, [vs[(kh, par)] for vs in values], sink)
                acc = o if acc is None else acc + o
            o_ref[0, :, (2 * kh) * 128:(2 * kh + 1) * 128] = acc[:BLOCK].astype(o_ref.dtype)
            o_ref[0, :, (2 * kh + 1) * 128:(2 * kh + 2) * 128] = acc[BLOCK:].astype(o_ref.dtype)

    q = q_ref[0] * WIN_SCALE
    n_ctx = n_ctx_blocks * BLOCK
    kctx = lane_halves(k_ref[0, :n_ctx, :])
    vctx = lane_halves(v_ref[0, :n_ctx, :])

    def ctx_path():
        attend([q], [kctx], [vctx], [None])

    def lat_path():
        blk = j - n_ctx_blocks
        band = [pl.ds(pl.multiple_of((n_ctx_blocks + jnp.clip(blk + d, 0, n_blocks - 1)) * BLOCK, BLOCK), BLOCK)
                for d in (-1, 0, 1)]
        rope = lambda x, rows: _rope_lanes(x, cos_ref[rows, :], sa_ref[rows, :], sb_ref[rows, :], half)
        q_rot = jnp.concatenate([rope(q[:, c * 128:(c + 1) * 128], band[1]) for c in range(q.shape[1] // 128)],
                                axis=-1)
        kband = jnp.concatenate([rope(k_ref[0, rows, :], rows) for rows in band], axis=0)
        vband = jnp.concatenate([v_ref[0, rows, :] for rows in band], axis=0)
        r = lax.broadcasted_iota(jnp.int32, (2 * BLOCK, 3 * BLOCK), 0) % BLOCK
        c = lax.broadcasted_iota(jnp.int32, (2 * BLOCK, 3 * BLOCK), 1)
        first = jnp.where(blk > 0, 0, BLOCK)
        last = jnp.where(blk < n_blocks - 1, 3 * BLOCK, 2 * BLOCK)
        valid = (jnp.abs(c - BLOCK - r) <= WINDOW) & (c >= first) & (c < last)
        attend([q_rot, q], [lane_halves(kband), kctx], [lane_halves(vband), vctx], [valid, None])

    pl.when(j < n_ctx_blocks)(ctx_path)
    pl.when(j >= n_ctx_blocks)(lat_path)


def _win_call(sink, wq, wk, wv, tabs, n_ctx):
    b, n, _ = wq.shape
    ncb = n_ctx // BLOCK
    nb = n // BLOCK
    kvw = WIN_KV_HEADS * WIN_HEAD_DIM
    cur = lambda i, j: (i, j, 0)
    kv = pl.BlockSpec((1, n, kvw), lambda i, j: (i, 0, 0))
    tab = pl.BlockSpec((n, 128), lambda i, j: (0, 0))
    return pl.pallas_call(
        functools.partial(_win_kernel, n_ctx_blocks=ncb, n_blocks=nb - ncb),
        grid=(b, nb),
        in_specs=[pl.BlockSpec(memory_space=pltpu.SMEM),
                  pl.BlockSpec((1, BLOCK, WIN_Q_HEADS * WIN_HEAD_DIM), cur), kv, kv, tab, tab, tab],
        out_specs=pl.BlockSpec((1, BLOCK, WIN_Q_HEADS * WIN_HEAD_DIM), cur),
        out_shape=jax.ShapeDtypeStruct((b, n, WIN_Q_HEADS * WIN_HEAD_DIM), BF16),
        compiler_params=_cparams(2),
    )(sink, wq, wk, wv, *tabs)


def _merge_kernel(x_ref, s5_ref, mla_ref, win_ref, gate_ref, mod_ref, wglu_ref, bglu_ref, wbr_ref, wout_ref,
                  g1_ref, b1_ref, wr_ref, x1_ref, h2_ref, lg_ref, *, alpha):
    d = x_ref.shape[-1]
    g = jax.nn.gelu(s5_ref[0])
    s5o = g * _sigmoid(_bdot(g.astype(BF16), wglu_ref[...]) + bglu_ref[...])
    branches = (s5o.astype(BF16), mla_ref[0], win_ref[0])
    mix = None
    for kk, o in enumerate(branches):
        term = _sigmoid(gate_ref[0, :, kk * d:(kk + 1) * d].astype(F32)) * _bdot(o, wbr_ref[kk])
        mix = term if mix is None else mix + term
    y = _bdot(mix.astype(BF16), wout_ref[...])
    mod = lambda r: mod_ref[0, 0, r:r + 1, :]
    x1 = _layer_norm(alpha * x_ref[0] + mod(2) * y) * g1_ref[...] + b1_ref[...]
    x1_ref[0] = x1
    h2 = (_layer_norm(x1) * (1.0 + mod(4)) + mod(3)).astype(BF16)
    h2_ref[0] = h2
    lg_ref[0] = _dot_nt(wr_ref[...], h2)


def _merge_call(xall, s5y, mla_o, win_o, gates, mod, wglu, bglu, wbr, wout, g1, b1, wr_t, n_ctx_tiles, alpha):
    b, n, d = xall.shape
    tm = TOKEN_TILE
    tok = lambda w: pl.BlockSpec((1, tm, w), lambda i, t: (i, t, 0))
    full = lambda a: pl.BlockSpec(a.shape, lambda i, t: (0,) * a.ndim)
    return pl.pallas_call(
        functools.partial(_merge_kernel, alpha=alpha),
        grid=(b, n // tm),
        in_specs=[tok(d), tok(BRANCH_WIDTH), tok(BRANCH_WIDTH), tok(BRANCH_WIDTH), tok(N_BRANCH * d),
                  pl.BlockSpec((1, 1, 6, d), lambda i, t: (i, jnp.where(t < n_ctx_tiles, 0, 1), 0, 0)),
                  full(wglu), full(bglu), full(wbr), full(wout), full(g1), full(b1), full(wr_t)],
        out_specs=[tok(d), tok(d), pl.BlockSpec((1, N_EXPERTS, tm), lambda i, t: (i, 0, t))],
        out_shape=[jax.ShapeDtypeStruct((b, n, d), F32), jax.ShapeDtypeStruct((b, n, d), BF16),
                   jax.ShapeDtypeStruct((b, N_EXPERTS, n), F32)],
        compiler_params=_cparams(2),
    )(xall, s5y, mla_o, win_o, gates, mod, wglu, bglu, wbr, wout, g1, b1, wr_t)


def _excl_cumsum_lanes(m):
    rows, n = m.shape
    r = lax.broadcasted_iota(jnp.int32, (128, 128), 0)
    c = lax.broadcasted_iota(jnp.int32, (128, 128), 1)
    tri = jnp.where(r < c, 1.0, 0.0).astype(BF16)
    off = jnp.zeros((rows, 1), F32)
    outs, offs = [], []
    for jb in range(n // 128):
        blk = m[:, jb * 128:(jb + 1) * 128]
        offs.append(off)
        outs.append(_bdot(blk.astype(BF16), tri) + off)
        off = off + jnp.sum(blk, axis=1, keepdims=True)
    return jnp.concatenate(outs, axis=1), offs + [off]


def _topk_slots(affs, caps):
    bits = [pltpu.bitcast(aff, jnp.int32) for aff in affs]

    def body(i, thrs):
        out = []
        for b, cap, thr in zip(bits, caps, thrs):
            cand = thr | (jnp.int32(1) << (30 - i))
            cnt = jnp.sum(jnp.where(b >= cand, 1.0, 0.0), axis=1, keepdims=True)
            out.append(jnp.where(cnt >= cap, cand, thr))
        return tuple(out)

    zero = jnp.zeros((affs[0].shape[0], 1), jnp.int32)
    thrs = lax.fori_loop(0, 31, body, tuple(zero for _ in affs))
    results = []
    for b, cap, thr in zip(bits, caps, thrs):
        gt = jnp.where(b > thr, 1.0, 0.0)
        eq = jnp.where(b == thr, 1.0, 0.0)
        need = cap - jnp.sum(gt, axis=1, keepdims=True)
        sel = gt + eq * jnp.where(_excl_cumsum_lanes(eq)[0] < need, 1.0, 0.0)
        rank, offs = _excl_cumsum_lanes(sel)
        results.append((jnp.where(sel > 0.5, rank, -1.0).astype(jnp.int32), offs[::MOE_TILE // 128]))
    return results


def _route_kernel(lg_ref, slot_ref, aff_ref, bnd_ref, *, n_ctx, cap_ctx, cap_lat):
    lg = lg_ref[0]
    m = jnp.max(lg, axis=0, keepdims=True)
    ex = jnp.exp(lg - m)
    aff = ex / jnp.sum(ex, axis=0, keepdims=True)
    aff_ref[0] = aff
    (slots_ctx, _), (slots, counts) = _topk_slots([aff[:, :n_ctx], aff[:, n_ctx:]], [cap_ctx, cap_lat])
    slot_ref[0, :, :n_ctx] = slots_ctx
    slot_ref[0, :, n_ctx:] = slots
    lane = lax.broadcasted_iota(jnp.int32, bnd_ref.shape[1:], 1)
    bnd = jnp.zeros(bnd_ref.shape[1:], F32)
    for k, cnt in enumerate(counts):
        bnd = jnp.where(lane == k, cnt, bnd)
    bnd_ref[0] = bnd.astype(jnp.int32)


def _route_call(logits_t, n_ctx, cap_ctx, cap_lat):
    b, e, n = logits_t.shape
    assert (n - n_ctx) % MOE_TILE == 0 and (n - n_ctx) // MOE_TILE < MOE_BOUNDS
    spec = pl.BlockSpec((1, e, n), lambda i: (i, 0, 0))
    return pl.pallas_call(
        functools.partial(_route_kernel, n_ctx=n_ctx, cap_ctx=cap_ctx, cap_lat=cap_lat),
        grid=(b,),
        in_specs=[spec],
        out_specs=[spec, spec, pl.BlockSpec((1, e, MOE_BOUNDS), lambda i: (i, 0, 0))],
        out_shape=[jax.ShapeDtypeStruct((b, e, n), jnp.int32), jax.ShapeDtypeStruct((b, e, n), F32),
                   jax.ShapeDtypeStruct((b, e, MOE_BOUNDS), jnp.int32)],
        compiler_params=_cparams(1),
    )(logits_t)


def _gather_kernel(bnd_ref, slot_ref, aff_ref, h_ref, xs_ref, gate_ref, xl_ref, gl_ref, *, n_ctx, cap_ctx):
    cap_lat = xs_ref.shape[2] - cap_ctx
    slot = slot_ref[0, 0]
    aff = aff_ref[0, 0]
    n = h_ref.shape[1]
    ib, ie = pl.program_id(0), pl.program_id(1)
    xl_ref[...] = jnp.zeros(xl_ref.shape, F32)
    gl_ref[...] = jnp.zeros(gl_ref.shape, F32)
    tiles = range((n - n_ctx) // MOE_TILE)
    firsts = [(bnd_ref[ib, ie, kt] // 16) * 16 for kt in tiles]

    def add_window(kt, start):
        tok = slice(n_ctx + kt * MOE_TILE, n_ctx + (kt + 1) * MOE_TILE)
        start = pl.multiple_of(start, 16)
        rows = start + lax.broadcasted_iota(jnp.int32, (MOE_WINDOW, MOE_TILE), 0)
        hit = slot[:, tok] == rows
        xl_ref[pl.ds(start, MOE_WINDOW), :] += _bdot(jnp.where(hit, 1.0, 0.0).astype(BF16), h_ref[0, tok, :])
        picked = jnp.sum(jnp.where(hit, aff[:, tok], 0.0), axis=1, keepdims=True)
        gl_ref[pl.ds(start, MOE_WINDOW), :] += jnp.broadcast_to(picked, (MOE_WINDOW, 128))

    for kt in tiles:
        add_window(kt, firsts[kt])
    for kt in tiles:
        def more(w, carry, kt=kt):
            add_window(kt, firsts[kt] + (w + 1) * MOE_WINDOW)
            return carry

        n_win = (bnd_ref[ib, ie, kt + 1] - firsts[kt] + MOE_WINDOW - 1) // MOE_WINDOW
        lax.fori_loop(0, jnp.maximum(n_win - 1, 0), more, 0)
    iota = lax.broadcasted_iota(jnp.int32, (cap_ctx, n_ctx), 0)
    hit = slot[:, :n_ctx] == iota
    xc = _bdot(jnp.where(hit, 1.0, 0.0).astype(BF16), h_ref[0, :n_ctx, :])
    gc = jnp.sum(jnp.where(hit, aff[:, :n_ctx], 0.0), axis=1, keepdims=True)
    xs_ref[0, 0, :cap_lat] = xl_ref[:cap_lat].astype(BF16)
    xs_ref[0, 0, cap_lat:] = xc.astype(BF16)
    gate_ref[0, 0, :cap_lat] = gl_ref[:cap_lat]
    gate_ref[0, 0, cap_lat:] = jnp.broadcast_to(gc, (cap_ctx, 128))


def _gather_call(bounds, slot, aff, h2, n_ctx, cap_ctx, cap_lat):
    b, e, n = slot.shape
    d = h2.shape[-1]
    cap = cap_lat + cap_ctx
    row = pl.BlockSpec((1, 1, 1, n), lambda ib, ie: (ib, ie, 0, 0))
    return pl.pallas_call(
        functools.partial(_gather_kernel, n_ctx=n_ctx, cap_ctx=cap_ctx),
        grid=(b, e),
        in_specs=[pl.BlockSpec(memory_space=pltpu.SMEM), row, row, pl.BlockSpec((1, n, d), lambda ib, ie: (ib, 0, 0))],
        out_specs=[pl.BlockSpec((1, 1, cap, d), lambda ib, ie: (ib, ie, 0, 0)),
                   pl.BlockSpec((1, 1, cap, 128), lambda ib, ie: (ib, ie, 0, 0))],
        out_shape=[jax.ShapeDtypeStruct((b, e, cap, d), BF16), jax.ShapeDtypeStruct((b, e, cap, 128), F32)],
        scratch_shapes=[pltpu.VMEM((cap_lat + MOE_WINDOW, d), F32), pltpu.VMEM((cap_lat + MOE_WINDOW, 128), F32)],
        compiler_params=_cparams(2),
    )(bounds, slot.reshape(b, e, 1, n), aff.reshape(b, e, 1, n), h2)


def _ffn_kernel(xs_ref, gate_ref, wg_ref, wu_ref, wd_ref, yl_ref, yc_ref, wg_s, wu_s, wd_s):
    ns, _, cap, d = xs_ref.shape
    cap_ctx = yc_ref.shape[2]
    cap_lat = cap - cap_ctx

    @pl.when(pl.program_id(1) == 0)
    def _():
        wg_s[...] = wg_ref[0, 0].astype(BF16)
        wu_s[...] = wu_ref[0, 0].astype(BF16)
        wd_s[...] = wd_ref[0, 0].astype(BF16)

    xs = xs_ref[:, 0].reshape(ns * cap, d)
    a = _bdot(xs, wg_s[...])
    u = _bdot(xs, wu_s[...])
    hm = (a * _sigmoid(a) * u).astype(BF16)
    y = _bdot(hm, wd_s[...]) * gate_ref[:, 0].reshape(ns * cap, 128)[:, 0:1]
    for s in range(ns):
        yl_ref[s, 0, :cap_lat] = y[s * cap:s * cap + cap_lat].astype(yl_ref.dtype)
        yl_ref[s, 0, cap_lat:] = jnp.zeros((MOE_WINDOW, d), yl_ref.dtype)
        yc_ref[s, 0] = y[s * cap + cap_lat:(s + 1) * cap].astype(yc_ref.dtype)


def _ffn_call(layer, xs, gate, w_gate, w_up, w_down, cap_ctx):
    b, e, cap, d = xs.shape
    f = w_gate.shape[-1]
    ns = FFN_SAMPLES
    rows_lat = cap - cap_ctx + MOE_WINDOW
    tok = lambda rows, w: pl.BlockSpec((ns, 1, rows, w), lambda ie, j: (j, ie, 0, 0))
    return pl.pallas_call(
        _ffn_kernel,
        grid=(e, b // ns),
        in_specs=[tok(cap, d), tok(cap, 128),
                  pl.BlockSpec((1, 1, d, f), lambda ie, j: (layer, ie, 0, 0)),
                  pl.BlockSpec((1, 1, d, f), lambda ie, j: (layer, ie, 0, 0)),
                  pl.BlockSpec((1, 1, f, d), lambda ie, j: (layer, ie, 0, 0))],
        out_specs=[tok(rows_lat, d), tok(cap_ctx, d)],
        out_shape=[jax.ShapeDtypeStruct((b, e, rows_lat, d), BF16), jax.ShapeDtypeStruct((b, e, cap_ctx, d), BF16)],
        scratch_shapes=[pltpu.VMEM((d, f), BF16), pltpu.VMEM((d, f), BF16), pltpu.VMEM((f, d), BF16)],
        compiler_params=_cparams(2),
    )(xs, gate, w_gate, w_up, w_down)


def _combine_kernel(bnd_ref, slot_ref, yl_ref, yc_ref, x1_ref, mod_ref, g2_ref, b2_ref, o_ref, fl_ref,
                    *, n_ctx_tiles, alpha):
    ib, t = pl.program_id(0), pl.program_id(1)
    tm = x1_ref.shape[1]
    slot = slot_ref[0]
    win = MOE_WINDOW

    def finish(fl):
        x1 = x1_ref[0]
        o_ref[0] = _layer_norm(alpha * x1 + mod_ref[0, 0, 5:6, :] * fl) * g2_ref[...] + b2_ref[...]

    def onehot(e, first, width):
        iota = lax.broadcasted_iota(jnp.int32, (tm, width), 1)
        return jnp.where(slot[:, e:e + 1] - first == iota, 1.0, 0.0).astype(BF16)

    def ctx_path():
        cap = yc_ref.shape[2]
        fl = None
        for e in range(N_EXPERTS):
            term = _bdot(onehot(e, 0, cap), yc_ref[0, e])
            fl = term if fl is None else fl + term
        finish(fl)

    def lat_path():
        kt = t - n_ctx_tiles
        firsts = [pl.multiple_of((bnd_ref[ib, e, kt] // 16) * 16, 16) for e in range(N_EXPERTS)]
        lane = lax.broadcasted_iota(jnp.int32, (tm, 2 * win), 1)
        pieces, ywins = [], []
        for e in range(0, N_EXPERTS, 2):
            rel = jnp.where(lane < win, slot[:, e:e + 1] - firsts[e], slot[:, e + 1:e + 2] - firsts[e + 1] + win)
            pieces.append(jnp.where(rel == lane, 1.0, 0.0).astype(BF16))
            ywins += [yl_ref[0, e, pl.ds(firsts[e], win), :], yl_ref[0, e + 1, pl.ds(firsts[e + 1], win), :]]
        fl_ref[...] = _bdot(jnp.concatenate(pieces, axis=1), jnp.concatenate(ywins, axis=0))
        for e in range(N_EXPERTS):
            def window(w, carry, e=e):
                first = pl.multiple_of(firsts[e] + (w + 1) * win, 16)
                fl_ref[...] += _bdot(onehot(e, first, win), yl_ref[0, e, pl.ds(first, win), :])
                return carry

            n_win = (bnd_ref[ib, e, kt + 1] - firsts[e] + win - 1) // win
            lax.fori_loop(0, jnp.maximum(n_win - 1, 0), window, 0)
        finish(fl_ref[...])

    pl.when(t < n_ctx_tiles)(ctx_path)
    pl.when(t >= n_ctx_tiles)(lat_path)


def _combine_call(bounds, slot_t, yl, yc, x1, mod, g2, b2, n_ctx_tiles, alpha):
    b, n, d = x1.shape
    tm = MOE_TILE
    e = N_EXPERTS
    full = lambda a: pl.BlockSpec(a.shape, lambda i, t: (0,) * a.ndim)
    return pl.pallas_call(
        functools.partial(_combine_kernel, n_ctx_tiles=n_ctx_tiles, alpha=alpha),
        grid=(b, n // tm),
        scratch_shapes=[pltpu.VMEM((tm, d), F32)],
        in_specs=[pl.BlockSpec(memory_space=pltpu.SMEM),
                  pl.BlockSpec((1, tm, e), lambda i, t: (i, t, 0)),
                  pl.BlockSpec((1,) + yl.shape[1:], lambda i, t: (i, 0, 0, 0)),
                  pl.BlockSpec((1,) + yc.shape[1:], lambda i, t: (i, 0, 0, 0)),
                  pl.BlockSpec((1, tm, d), lambda i, t: (i, t, 0)),
                  pl.BlockSpec((1, 1, 6, d), lambda i, t: (i, jnp.where(t < n_ctx_tiles, 0, 1), 0, 0)),
                  full(g2), full(b2)],
        out_specs=pl.BlockSpec((1, tm, d), lambda i, t: (i, t, 0)),
        out_shape=jax.ShapeDtypeStruct((b, n, d), F32),
        compiler_params=_cparams(2),
    )(bounds, slot_t, yl, yc, x1, mod, g2, b2)


def _rope_tables(n_ctx, seq, head_dim, lane_offset):
    half = head_dim // 2
    nf = head_dim // 4
    t = jnp.arange(seq, dtype=F32)
    row = jnp.floor(t / GRID_W)
    col = t - row * GRID_W
    freqs = ROPE_BASE ** (-jnp.arange(nf, dtype=F32) / nf)
    ang = jnp.concatenate([row[:, None] * freqs, col[:, None] * freqs], axis=-1)
    cos, sin = jnp.cos(ang), jnp.sin(ang)
    zeros = jnp.zeros_like(sin)
    n_heads = (128 - lane_offset) // head_dim if lane_offset == 0 else 1
    c = jnp.concatenate([jnp.ones((seq, lane_offset), F32)] + [cos, cos] * n_heads, axis=-1)
    sa = jnp.concatenate([jnp.zeros((seq, lane_offset), F32)] + [-sin, zeros] * n_heads, axis=-1)
    sb = jnp.concatenate([jnp.zeros((seq, lane_offset), F32)] + [zeros, sin] * n_heads, axis=-1)
    pad = 128 - c.shape[1]
    c = jnp.pad(c, ((n_ctx, 0), (0, pad)), constant_values=1.0)
    sa = jnp.pad(sa, ((n_ctx, 0), (0, pad)))
    sb = jnp.pad(sb, ((n_ctx, 0), (0, pad)))
    return c, sa, sb


def _layer_weights(i, p):
    d = p['w_in'].shape[1]
    pts = np.cumsum((S5_WIDTH, MLA_Q_RANK, MLA_KV_RANK, MLA_ROPE, WIN_Q_HEADS * WIN_HEAD_DIM,
                     WIN_KV_HEADS * WIN_HEAD_DIM, WIN_KV_HEADS * WIN_HEAD_DIM))
    cols = jnp.split(p['w_in'][i], [int(v) for v in pts], axis=1)
    kr = jnp.pad(cols[3], ((0, 0), (MLA_NOPE, MLA_PAD - MLA_NOPE - MLA_ROPE)))
    w_cat = jnp.concatenate([cols[0], cols[1], cols[2], kr, cols[4], cols[5], cols[6], cols[7]], axis=1)
    dq = MLA_NOPE + MLA_ROPE
    wq = p['mla_w_uq'][i].reshape(MLA_Q_RANK, MLA_HEADS, dq)
    wq = jnp.pad(wq, ((0, 0), (0, 0), (0, MLA_PAD - dq))).reshape(MLA_Q_RANK, MLA_HEADS * MLA_PAD)
    wkv = p['mla_w_ukv'][i].reshape(MLA_KV_RANK, MLA_HEADS, MLA_NOPE + MLA_V)
    wk = jnp.pad(wkv[:, :, :MLA_NOPE], ((0, 0), (0, 0), (0, MLA_PAD - MLA_NOPE)))
    wk = wk.reshape(MLA_KV_RANK, MLA_HEADS * MLA_PAD)
    wv = wkv[:, :, MLA_NOPE:].reshape(MLA_KV_RANK, MLA_HEADS * MLA_V)
    row = lambda a: a[i].astype(F32).reshape(1, -1)
    return dict(
        w_cat=w_cat.astype(BF16), wq=wq.astype(BF16), wk=wk.astype(BF16), wv=wv.astype(BF16),
        qg=row(p['mla_q_norm']), kvg=row(p['mla_kv_norm']),
        wglu=p['s5_w_glu'][i].astype(BF16), bglu=row(p['s5_b_glu']),
        sink=p['win_sink'][i].astype(F32),
        wbr=p['w_branch'][i].astype(BF16), wout=p['w_out'][i].astype(BF16),
        g1=row(p['ln1_g']), b1=row(p['ln1_b']), g2=row(p['ln2_g']), b2=row(p['ln2_b']),
        wr_t=p['w_router'][i].T.astype(BF16),
    )


def _forward(p):
    x, c, ctx, c_ctx = p['x'], p['c'], p['ctx'], p['c_ctx']
    b, seq, d = x.shape
    n_ctx = ctx.shape[1]
    depth = p['w_ada'].shape[0]
    assert b == 8 and seq % TOKEN_TILE == 0 and n_ctx % TOKEN_TILE == 0 and seq % GRID_W == 0
    alpha = float((2 * depth) ** 0.25)
    n_ctx_tiles = n_ctx // TOKEN_TILE
    cap_lat = CAPACITY_FACTOR * seq // N_EXPERTS
    cap_ctx = CAPACITY_FACTOR * n_ctx // N_EXPERTS

    cond = jnp.concatenate([c, c_ctx[None], jnp.zeros((16 - b - 1, d), F32)], axis=0)
    mods = _ada_call(cond, p['w_ada'], p['b_ada'])
    mods = mods.reshape(depth, 16, 6, d)
    tabs_mla = _rope_tables(n_ctx, seq, MLA_ROPE, MLA_NOPE)
    tabs_win = _rope_tables(n_ctx, seq, WIN_HEAD_DIM, 0)
    s5w = _s5_param_call(p)

    xall = jnp.concatenate([ctx, x], axis=1)
    for i in range(depth):
        w = _layer_weights(i, p)
        mod = jnp.stack([jnp.broadcast_to(mods[i, b], (b, 6, d)), mods[i, :b]], axis=1)
        u, qa, kva, kr, wq, wk, wv, gates = _in_call(xall, mod, w['w_cat'], n_ctx)
        s5y = _s5_call(i, u, *s5w, p['s5_d'], n_ctx)
        qp, qr, kk, vv = _mla_prep_call(qa, kva, kr, w['qg'], w['kvg'], w['wq'], w['wk'], w['wv'], tabs_mla)
        mla_o = _mla_attn_call(qp, qr, kk, vv, n_ctx)
        win_o = _win_call(w['sink'], wq, wk, wv, tabs_win, n_ctx)
        x1, h2, logits_t = _merge_call(xall, s5y, mla_o, win_o, gates, mod, w['wglu'], w['bglu'], w['wbr'],
                                       w['wout'], w['g1'], w['b1'], w['wr_t'], n_ctx_tiles, alpha)
        slot, aff, bounds = _route_call(logits_t, n_ctx, cap_ctx, cap_lat)
        xs, gate = _gather_call(bounds, slot, aff, h2, n_ctx, cap_ctx, cap_lat)
        yl, yc = _ffn_call(i, xs, gate, p['w_gate'], p['w_up'], p['w_down'], cap_ctx)
        slot_t = jnp.swapaxes(slot, 1, 2)
        xall = _combine_call(bounds, slot_t, yl, yc, x1, mod, w['g2'], w['b2'], n_ctx_tiles, alpha)
    return xall[:, n_ctx:]


def kernel(x, c, ctx, c_ctx, w_ada, b_ada, w_in, s5_lam_re, s5_lam_im, s5_log_dt, s5_b_re, s5_b_im, s5_c_re, s5_c_im, s5_d, s5_w_glu, s5_b_glu, mla_q_norm, mla_w_uq, mla_kv_norm, mla_w_ukv, win_sink, w_branch, w_out, ln1_g, ln1_b, ln2_g, ln2_b, w_router, w_gate, w_up, w_down):
    return _forward(dict(
        x=x, c=c, ctx=ctx, c_ctx=c_ctx, w_ada=w_ada, b_ada=b_ada, w_in=w_in, s5_lam_re=s5_lam_re,
        s5_lam_im=s5_lam_im, s5_log_dt=s5_log_dt, s5_b_re=s5_b_re, s5_b_im=s5_b_im, s5_c_re=s5_c_re,
        s5_c_im=s5_c_im, s5_d=s5_d, s5_w_glu=s5_w_glu, s5_b_glu=s5_b_glu, mla_q_norm=mla_q_norm,
        mla_w_uq=mla_w_uq, mla_kv_norm=mla_kv_norm, mla_w_ukv=mla_w_ukv, win_sink=win_sink, w_branch=w_branch,
        w_out=w_out, ln1_g=ln1_g, ln1_b=ln1_b, ln2_g=ln2_g, ln2_b=ln2_b, w_router=w_router, w_gate=w_gate,
        w_up=w_up, w_down=w_down))
```

```python
import functools
import math

import jax
import jax.numpy as jnp
import numpy as np
from jax import lax
from jax.experimental import pallas as pl
from jax.experimental.pallas import tpu as pltpu

F32 = jnp.float32
BF16 = jnp.bfloat16
HIGHEST = lax.Precision.HIGHEST

GRID_W = 64
S5_WIDTH = 512
S5_GROUP = 16
S5_GROUPS = S5_WIDTH // S5_GROUP
S5_STATE = 64
S5_CHUNK = 16
S5_GROUPS_PER_STEP = 4
MLA_HEADS = 8
MLA_NOPE = 64
MLA_ROPE = 32
MLA_V = 64
MLA_Q_RANK = 384
MLA_KV_RANK = 256
MLA_PAD = 128
MLA_HEADS_PER_STEP = 4
MLA_SCALE = (MLA_NOPE + MLA_ROPE) ** -0.5
WIN_Q_HEADS = 8
WIN_KV_HEADS = 2
WIN_GROUP = WIN_Q_HEADS // WIN_KV_HEADS
WIN_HEAD_DIM = 64
WINDOW = 128
BLOCK = 128
WIN_SCALE = WIN_HEAD_DIM ** -0.5
N_BRANCH = 3
BRANCH_WIDTH = 512
N_EXPERTS = 16
CAPACITY_FACTOR = 2
ROPE_BASE = 10000.0
LN_EPS = 1e-6
NEG_INF = -1e30
TOKEN_TILE = 256
IN_TILE = 384
MOE_TILE = 256
MOE_WINDOW = 64
MOE_BOUNDS = 16
FFN_SAMPLES = 2
VMEM_LIMIT = 56 * 1024 * 1024


def _cparams(n_axes):
    return pltpu.CompilerParams(dimension_semantics=("arbitrary",) * n_axes, vmem_limit_bytes=VMEM_LIMIT)


def _bdot(a, b):
    return jnp.dot(a, b, preferred_element_type=F32)


def _dot_nt(a, b):
    return lax.dot_general(a, b, (((1,), (1,)), ((), ())), preferred_element_type=F32)


def _layer_norm(x):
    mu = jnp.mean(x, axis=-1, keepdims=True)
    xc = x - mu
    var = jnp.mean(xc * xc, axis=-1, keepdims=True)
    return xc * lax.rsqrt(var + LN_EPS)


def _ada_kernel(cond_ref, w_ref, b_ref, o_ref):
    s = cond_ref[...]
    s = s * jax.nn.sigmoid(s)
    o_ref[0] = jnp.dot(s, w_ref[0], precision=HIGHEST, preferred_element_type=F32) + b_ref[0]


def _ada_call(cond, w_ada, b_ada):
    depth, d, d6 = w_ada.shape
    tn = 1536
    rows = cond.shape[0]
    return pl.pallas_call(
        _ada_kernel,
        grid=(depth, d6 // tn),
        in_specs=[
            pl.BlockSpec((rows, d), lambda i, j: (0, 0)),
            pl.BlockSpec((1, d, tn), lambda i, j: (i, 0, j)),
            pl.BlockSpec((1, 1, tn), lambda i, j: (i, 0, j)),
        ],
        out_specs=pl.BlockSpec((1, rows, tn), lambda i, j: (i, 0, j)),
        out_shape=jax.ShapeDtypeStruct((depth, rows, d6), F32),
        compiler_params=_cparams(2),
    )(cond, w_ada, b_ada.reshape(depth, 1, d6))


IN_WIDTHS = (S5_WIDTH, MLA_Q_RANK, MLA_KV_RANK, MLA_PAD, WIN_Q_HEADS * WIN_HEAD_DIM,
             WIN_KV_HEADS * WIN_HEAD_DIM, WIN_KV_HEADS * WIN_HEAD_DIM)
IN_OFFSETS = tuple(int(v) for v in np.cumsum((0,) + IN_WIDTHS))


def _mod_rows(mod_ref, r, tile, n_ctx):
    row = pl.program_id(1) * tile + lax.broadcasted_iota(jnp.int32, (tile, 1), 0)
    return jnp.where(row < n_ctx, mod_ref[0, 0, r:r + 1, :], mod_ref[0, 1, r:r + 1, :])


def _rope_lanes(x, cos, sa, sb, shift):
    return x * cos + pltpu.roll(x, 128 - shift, 1) * sa + pltpu.roll(x, shift, 1) * sb


def _mla_project(qa, kva, kr, qg, kvg, wq_ref, wk_ref, wv_ref, cos, sa, sb, qp_ref, qr_ref, k_ref, v_ref):
    def rms(x, gain):
        return (x * lax.rsqrt(jnp.mean(x * x, axis=-1, keepdims=True) + LN_EPS) * gain).astype(BF16)

    qn = rms(qa, qg)
    kvn = rms(kva, kvg)
    q = _bdot(qn, wq_ref[...]) * MLA_SCALE
    k = _bdot(kvn, wk_ref[...])
    v_ref[0] = _bdot(kvn, wv_ref[...]).astype(BF16)
    kr_rot = _rope_lanes(kr, cos, sa, sb, MLA_ROPE // 2)
    qp_ref[0] = q.astype(BF16)
    for h in range(MLA_HEADS):
        sl = slice(h * MLA_PAD, (h + 1) * MLA_PAD)
        qr_ref[0, :, sl] = _rope_lanes(q[:, sl], cos, sa, sb, MLA_ROPE // 2).astype(BF16)
        k_ref[0, :, sl] = (k[:, sl] + kr_rot).astype(BF16)


def _in_kernel(x_ref, mod_ref, w_ref, qg_ref, kvg_ref, wuq_ref, wuk_ref, wuv_ref, cos_ref, sa_ref, sb_ref,
               u_ref, wq_ref, wk_ref, wv_ref, gate_ref, qp_ref, qr_ref, k_ref, v_ref, *, n_ctx):
    tile = x_ref.shape[1]
    xn = _layer_norm(x_ref[0])
    h = (xn * (1.0 + _mod_rows(mod_ref, 1, tile, n_ctx)) + _mod_rows(mod_ref, 0, tile, n_ctx)).astype(BF16)
    widths = IN_WIDTHS + (gate_ref.shape[-1],)
    proj = lambda i: _bdot(h, w_ref[:, IN_OFFSETS[i]:IN_OFFSETS[i] + widths[i]])
    u_ref[0] = proj(0)
    wq_ref[0], wk_ref[0], wv_ref[0] = proj(4), proj(5), proj(6)
    gate_ref[0] = proj(7).astype(gate_ref.dtype)
    _mla_project(proj(1), proj(2), proj(3), qg_ref[...], kvg_ref[...], wuq_ref, wuk_ref, wuv_ref,
                 cos_ref[...], sa_ref[...], sb_ref[...], qp_ref, qr_ref, k_ref, v_ref)


_sigmoid = jax.nn.sigmoid


def _in_call(xall, mod, w_cat, qg, kvg, wuq, wuk, wuv, tabs, n_ctx):
    b, n, d = xall.shape
    tm = IN_TILE
    gate_w = w_cat.shape[1] - IN_OFFSETS[-1]
    hw = MLA_HEADS * MLA_PAD
    tok = lambda w: pl.BlockSpec((1, tm, w), lambda i, t: (i, t, 0))
    full = lambda a: pl.BlockSpec(a.shape, lambda i, t: (0,) * a.ndim)
    tab = pl.BlockSpec((tm, 128), lambda i, t: (t, 0))
    out_widths = (IN_WIDTHS[0],) + IN_WIDTHS[4:7] + (gate_w, hw, hw, hw, MLA_HEADS * MLA_V)
    out_dtypes = (F32,) * 4 + (BF16,) * 5
    return pl.pallas_call(
        functools.partial(_in_kernel, n_ctx=n_ctx),
        grid=(b, n // tm),
        in_specs=[tok(d), pl.BlockSpec((1, 2, 6, d), lambda i, t: (i, 0, 0, 0)),
                  pl.BlockSpec(w_cat.shape, lambda i, t: (0, 0), pipeline_mode=pl.Buffered(1)),
                  full(qg), full(kvg), full(wuq), full(wuk), full(wuv), tab, tab, tab],
        out_specs=[tok(w) for w in out_widths],
        out_shape=[jax.ShapeDtypeStruct((b, n, w), dt) for w, dt in zip(out_widths, out_dtypes)],
        compiler_params=_cparams(2),
    )(xall, mod, w_cat, qg, kvg, wuq, wuk, wuv, *tabs)


S5_LANE_GROUPS = 128 // S5_GROUP
S5_SCAN_GROUPS = 4
S5_PITCH_PAD = 8


def _s5_param_kernel(*refs):
    for g in range(refs[0].shape[2]):
        _s5_param_group(g, *refs)


def _s5_param_group(g, cre_ref, cim_ref, bre_ref, bim_ref, pr_ref, pi_ref, tz_ref, bc_ref, cc_ref, coef_ref):
    t = S5_CHUNK
    w = t * S5_GROUP
    nt = (((1,), (1,)), ((), ()))
    tz = None
    bcs, ccs, coefs = [], [], []
    for d in range(2):
        cre, cim = cre_ref[0, d, g], cim_ref[0, d, g]
        bre, bim = bre_ref[0, d, g], bim_ref[0, d, g]
        power = lambda k: (pr_ref[0, d, g, k:k + 1, :], pi_ref[0, d, g, k:k + 1, :])
        rt = []
        for k in range(t + 1):
            prk, pik = power(k)
            rt.append(jnp.concatenate([cre * prk - cim * pik, -(cre * pik + cim * prk)], axis=1))
        bt = jnp.concatenate([bre, bim], axis=1)
        zeros = jnp.zeros((S5_GROUP, w), F32)
        if d == 0:
            kt = lax.dot_general(bt, jnp.concatenate(rt[:t], axis=0), nt, precision=HIGHEST,
                                 preferred_element_type=F32)
            pad = jnp.concatenate([zeros, kt], axis=1)
            rows = [kt] + [pltpu.roll(pad, S5_GROUP * s, 1)[:, w:] for s in range(1, t)]
        else:
            kt = lax.dot_general(bt, jnp.concatenate(rt[t - 1::-1], axis=0), nt, precision=HIGHEST,
                                 preferred_element_type=F32)
            pad = jnp.concatenate([kt, zeros], axis=1)
            rows = [pltpu.roll(pad, 2 * w - S5_GROUP * (t - 1 - s), 1)[:, :w] for s in range(t - 1)] + [kt]
        tz_d = jnp.concatenate(rows, axis=0)
        tz = tz_d if tz is None else tz + tz_d
        bc_rows = []
        for s in range(t):
            prk, pik = power(t - 1 - s if d == 0 else s)
            br = bre * prk - bim * pik
            bi = bim * prk + bre * pik
            bc_rows.append(jnp.concatenate([br, bi, bi, br], axis=1))
        bcs.append(jnp.concatenate(bc_rows, axis=0))
        ccs.append(jnp.concatenate(rt[1:] if d == 0 else rt[t:0:-1], axis=0))
        er, ei = power(t)
        coefs += [jnp.concatenate([er, er], axis=1), jnp.concatenate([-ei, ei], axis=1),
                  jnp.concatenate([ei, -ei], axis=1)]
    tz_ref[0, g] = tz.astype(BF16)
    bc_ref[0, g] = jnp.concatenate(bcs, axis=1).astype(BF16)
    cc_ref[0, g] = jnp.concatenate(ccs, axis=1).astype(BF16)
    coef_ref[0, g] = jnp.concatenate(coefs + [jnp.zeros((2, 2 * S5_STATE), F32)], axis=0)


def _s5_param_call(p):
    t = S5_CHUNK
    f = lambda name: p[name].astype(F32)
    lam_re, lam_im = f('s5_lam_re'), f('s5_lam_im')
    depth = lam_re.shape[0]
    dt = jnp.exp(f('s5_log_dt'))[..., None]
    k = jnp.arange(t + 1, dtype=F32)[:, None]
    mag = jnp.exp((lam_re * dt)[..., None, :] * k)
    ang = (lam_im * dt)[..., None, :] * k
    pr, pi = mag * jnp.cos(ang), mag * jnp.sin(ang)
    ar, ai = pr[..., 1, :], pi[..., 1, :]
    den = lam_re * lam_re + lam_im * lam_im
    qr = (((ar - 1) * lam_re + ai * lam_im) / den)[..., None, :]
    qi = ((ai * lam_re - (ar - 1) * lam_im) / den)[..., None, :]
    b_re = jnp.swapaxes(f('s5_b_re'), -1, -2)
    b_im = jnp.swapaxes(f('s5_b_im'), -1, -2)
    bbr = qr * b_re - qi * b_im
    bbi = qr * b_im + qi * b_re
    g, hg, ps = S5_GROUPS, S5_GROUP, S5_STATE
    w = t * hg
    gs = S5_LANE_GROUPS
    small = lambda rows: pl.BlockSpec((1, 2, gs, rows, ps), lambda i, j: (i, 0, j, 0, 0))
    out = lambda cols: pl.BlockSpec((1, gs, w, cols), lambda i, j: (i, j, 0, 0))
    return pl.pallas_call(
        _s5_param_kernel,
        grid=(depth, g // gs),
        in_specs=[small(hg)] * 4 + [small(t + 1)] * 2,
        out_specs=[out(w), out(2 * w), out(w), pl.BlockSpec((1, gs, 8, 2 * ps), lambda i, j: (i, j, 0, 0))],
        out_shape=[jax.ShapeDtypeStruct((depth, g, w, w), BF16), jax.ShapeDtypeStruct((depth, g, w, 2 * w), BF16),
                   jax.ShapeDtypeStruct((depth, g, w, w), BF16), jax.ShapeDtypeStruct((depth, g, 8, 2 * ps), F32)],
        compiler_params=_cparams(2),
    )(f('s5_c_re'), f('s5_c_im'), bbr, bbi, pr, pi)


def _s5_kernel(u_ref, tz_ref, bc_ref, cc_ref, coef_ref, d_ref, y_ref, uy_ref, loc_ref, sp_ref, slab_ref,
               *, n_ctx, n_batch):
    ph, b = pl.program_id(1), pl.program_id(2)
    t, hg, ng = S5_CHUNK, S5_GROUP, S5_LANE_GROUPS
    ncc = n_ctx // t
    ncl = (u_ref.shape[1] - n_ctx) // t
    nc = ncc + ncl
    pitch = nc + S5_PITCH_PAD

    def to_chunk_rows(slabs):
        tr = [s.T for s in slabs]
        return [jnp.concatenate([x[g * hg:(g + 1) * hg] for x in tr], axis=0).T for g in range(ng)]

    def to_token_slabs(rows):
        tr = [r.T for r in rows]
        return [jnp.concatenate([x[tau * hg:(tau + 1) * hg] for x in tr], axis=0).T for tau in range(t)]

    base = pl.multiple_of(b * pitch, 8)
    cbase = pl.multiple_of(b * ncc, 8)

    @pl.when(ph == 0)
    def _():
        rows = to_chunk_rows([u_ref[0, pl.ds(n_ctx + tau, ncl, stride=t), :] for tau in range(t)])
        for g in range(ng):
            uy_ref[g, pl.ds(base + ncc, ncl), :] = rows[g]
            uy_ref[g, pl.ds(base + nc, S5_PITCH_PAD), :] = jnp.zeros((S5_PITCH_PAD, t * hg), F32)
        for tau in range(t):
            slab_ref[tau, pl.ds(cbase, ncc), :] = u_ref[0, pl.ds(tau, ncc, stride=t), :]

    @pl.when((ph == 1) & (b == 0))
    def _():
        rows = to_chunk_rows([slab_ref[tau] for tau in range(t)])
        for g in range(ng):
            for s in range(n_batch):
                uy_ref[g, s * pitch:s * pitch + ncc, :] = rows[g][s * ncc:(s + 1) * ncc]
        for part in range(ng // S5_SCAN_GROUPS):
            gs = [part * S5_SCAN_GROUPS + gl for gl in range(S5_SCAN_GROUPS)]
            for gl, g in enumerate(gs):
                ub = uy_ref[g].astype(BF16)
                loc = _bdot(ub, bc_ref[0, g])
                for q in range(4):
                    loc_ref[gl, q] = loc[:, q * 128:(q + 1) * 128]
                uy_ref[g] = _bdot(ub, tz_ref[0, g])
                for s in range(n_batch):
                    for d in range(2):
                        sp_ref[gl, d, s * pitch + nc:(s + 1) * pitch, :] = jnp.zeros((S5_PITCH_PAD, 128), F32)

            def coef(g, r):
                return jnp.broadcast_to(coef_ref[0, g, r:r + 1, :], (n_batch, 128))

            def step(i, carry):
                cb = jnp.where(i < ncc, ncc - 1 - i, nc + ncc - 1 - i)
                fwd = pl.ds(i, n_batch, stride=pitch)
                bwd = pl.ds(cb, n_batch, stride=pitch)
                out = []
                for gl, g in enumerate(gs):
                    v0f, v1f, v0b, v1b = carry[gl]
                    sp_ref[gl, 0, fwd, :] = v0f
                    sp_ref[gl, 1, bwd, :] = v0b
                    n0f = coef(g, 0) * v0f + coef(g, 1) * v1f + loc_ref[gl, 0, fwd, :]
                    n1f = coef(g, 0) * v1f + coef(g, 2) * v0f + loc_ref[gl, 1, fwd, :]
                    n0b = coef(g, 3) * v0b + coef(g, 4) * v1b + loc_ref[gl, 2, bwd, :]
                    n1b = coef(g, 3) * v1b + coef(g, 5) * v0b + loc_ref[gl, 3, bwd, :]
                    out.append((n0f, n1f, n0b, n1b))
                return tuple(out)

            z = jnp.zeros((n_batch, 128), F32)
            lax.fori_loop(0, nc, step, tuple((z, z, z, z) for _ in gs))
            for gl, g in enumerate(gs):
                sp = jnp.concatenate([sp_ref[gl, 0], sp_ref[gl, 1]], axis=1).astype(BF16)
                uy_ref[g] = uy_ref[g] + _dot_nt(sp, cc_ref[0, g])
        rows = [jnp.concatenate([uy_ref[g, s * pitch:s * pitch + ncc, :] for s in range(n_batch)], axis=0)
                for g in range(ng)]
        for tau, slab in enumerate(to_token_slabs(rows)):
            slab_ref[tau] = slab

    @pl.when(ph == 1)
    def _():
        slabs = to_token_slabs([uy_ref[g, pl.ds(base + ncc, ncl), :] for g in range(ng)])
        for tau in range(t):
            y_ref[0, pl.ds(n_ctx + tau, ncl, stride=t), :] = slabs[tau]
            y_ref[0, pl.ds(tau, ncc, stride=t), :] = slab_ref[tau, pl.ds(cbase, ncc), :]
        y_ref[0] = y_ref[0] + d_ref[0] * u_ref[0]


def _s5_call(layer, u, tz, bc, cc, coef, s5_d, n_ctx):
    b, n, width = u.shape
    t, ng = S5_CHUNK, S5_LANE_GROUPS
    rows = b * (n // t + S5_PITCH_PAD)
    assert (n - n_ctx) // t == 128 and b * (n_ctx // t) == 128
    wspec = lambda a: pl.BlockSpec((1, ng) + a.shape[2:], lambda g, ph, i: (layer, g, 0, 0))
    return pl.pallas_call(
        functools.partial(_s5_kernel, n_ctx=n_ctx, n_batch=b),
        grid=(width // 128, 2, b),
        in_specs=[pl.BlockSpec((1, n, 128), lambda g, ph, i: (i, 0, g)),
                  wspec(tz), wspec(bc), wspec(cc), wspec(coef),
                  pl.BlockSpec((1, 1, 128), lambda g, ph, i: (layer, 0, g))],
        out_specs=pl.BlockSpec((1, n, 128), lambda g, ph, i: (i * ph, 0, g)),
        out_shape=jax.ShapeDtypeStruct((b, n, width), F32),
        scratch_shapes=[pltpu.VMEM((ng, rows, t * S5_GROUP), F32),
                        pltpu.VMEM((S5_SCAN_GROUPS, 4, rows, 128), F32),
                        pltpu.VMEM((S5_SCAN_GROUPS, 2, rows, 128), F32),
                        pltpu.VMEM((t, 128, 128), F32)],
        compiler_params=_cparams(3),
    )(u, tz, bc, cc, coef, s5_d.astype(F32).reshape(s5_d.shape[0], 1, width))


def _lane_chunks(xs):
    return [x[:, i * 128:(i + 1) * 128] for x in xs for i in range(x.shape[1] // 128)]


def _row_max(scores, floor=None):
    mm = functools.reduce(jnp.maximum, _lane_chunks(scores))
    if floor is not None:
        mm = jnp.maximum(mm, floor)
    return jnp.max(mm, axis=-1, keepdims=True)


def _softmax_av(scores, values, sink=None):
    m = _row_max(scores, sink)
    ps = [jnp.exp(s - m) for s in scores]
    ll = functools.reduce(jnp.add, _lane_chunks(ps))
    if sink is not None:
        lane = lax.broadcasted_iota(jnp.int32, sink.shape, 1)
        ll = ll + jnp.where(lane == 0, jnp.exp(sink - m), 0.0)
    l = jnp.sum(ll, axis=-1, keepdims=True)
    o = functools.reduce(jnp.add, [_bdot(p.astype(BF16), v) for p, v in zip(ps, values)])
    return o / l


def _mla_attn_kernel(qp_ref, qr_ref, k_ref, v_ref, o_ref, *, n_ctx, n_ctx_tiles):
    heads = qp_ref.shape[-1] // MLA_PAD
    t = pl.program_id(2)
    lane = lax.broadcasted_iota(jnp.int32, (1, heads * MLA_V), 1)

    def run(latent):
        acc = None
        for h in range(heads):
            sl = slice(h * MLA_PAD, (h + 1) * MLA_PAD)
            own = (lane >= h * MLA_V) & (lane < (h + 1) * MLA_V)
            zero = jnp.zeros((), BF16)
            scores = [_dot_nt(qp_ref[0, :, sl], k_ref[0, :n_ctx, sl])]
            values = [jnp.where(own, v_ref[0, :n_ctx, :], zero)]
            if latent:
                scores.append(_dot_nt(qr_ref[0, :, sl], k_ref[0, n_ctx:, sl]))
                values.append(jnp.where(own, v_ref[0, n_ctx:, :], zero))
            o = _softmax_av(scores, values)
            acc = o if acc is None else acc + o
        o_ref[0] = acc.astype(o_ref.dtype)

    pl.when(t < n_ctx_tiles)(lambda: run(False))
    pl.when(t >= n_ctx_tiles)(lambda: run(True))


def _mla_attn_call(qp, qr, k, v, n_ctx):
    b, n, _ = qp.shape
    tq = TOKEN_TILE
    hp = MLA_HEADS_PER_STEP
    qspec = pl.BlockSpec((1, tq, hp * MLA_PAD), lambda i, h, t: (i, t, h))
    return pl.pallas_call(
        functools.partial(_mla_attn_kernel, n_ctx=n_ctx, n_ctx_tiles=n_ctx // tq),
        grid=(b, MLA_HEADS // hp, n // tq),
        in_specs=[qspec, qspec,
                  pl.BlockSpec((1, n, hp * MLA_PAD), lambda i, h, t: (i, 0, h)),
                  pl.BlockSpec((1, n, hp * MLA_V), lambda i, h, t: (i, 0, h))],
        out_specs=pl.BlockSpec((1, tq, hp * MLA_V), lambda i, h, t: (i, t, h)),
        out_shape=jax.ShapeDtypeStruct((b, n, MLA_HEADS * MLA_V), BF16),
        compiler_params=_cparams(3),
    )(qp, qr, k, v)


def _win_kernel(sink_ref, q_ref, k_ref, v_ref, cos_ref, sa_ref, sb_ref, o_ref, *, n_ctx_blocks, n_blocks):
    j = pl.program_id(1)
    hd = WIN_HEAD_DIM
    half = hd // 2
    lane = lax.broadcasted_iota(jnp.int32, (1, 128), 1)
    lo = jnp.where(lane < hd, 1.0, 0.0)
    hi = 1.0 - lo
    upper_rows = lax.broadcasted_iota(jnp.int32, (2 * BLOCK, 128), 0) < BLOCK

    def lane_halves(x):
        xr = pltpu.roll(x, hd, 1)
        return {(0, 0): (x * lo).astype(BF16), (0, 1): (xr * hi).astype(BF16),
                (1, 0): (xr * lo).astype(BF16), (1, 1): (x * hi).astype(BF16)}

    def attend(queries, keys, values, masks):
        for kh in range(WIN_KV_HEADS):
            stacked = [jnp.concatenate([qs[:, (2 * kh) * 128:(2 * kh + 1) * 128],
                                        qs[:, (2 * kh + 1) * 128:(2 * kh + 2) * 128]], axis=0).astype(BF16)
                       for qs in queries]
            acc = None
            for par in range(2):
                scores = []
                for qst, ks, msk in zip(stacked, keys, masks):
                    s = _dot_nt(qst, ks[(kh, par)])
                    scores.append(s if msk is None else jnp.where(msk, s, NEG_INF))
                sink = jnp.where(upper_rows, sink_ref[4 * kh + par], sink_ref[4 * kh + 2 + par])
                o = _softmax_av(scores, [vs[(kh, par)] for vs in values], sink)
                acc = o if acc is None else acc + o
            o_ref[0, :, (2 * kh) * 128:(2 * kh + 1) * 128] = acc[:BLOCK].astype(o_ref.dtype)
            o_ref[0, :, (2 * kh + 1) * 128:(2 * kh + 2) * 128] = acc[BLOCK:].astype(o_ref.dtype)

    q = q_ref[0] * WIN_SCALE
    n_ctx = n_ctx_blocks * BLOCK
    kctx = lane_halves(k_ref[0, :n_ctx, :])
    vctx = lane_halves(v_ref[0, :n_ctx, :])

    def ctx_path():
        attend([q], [kctx], [vctx], [None])

    def lat_path():
        blk = j - n_ctx_blocks
        band = [pl.ds(pl.multiple_of((n_ctx_blocks + jnp.clip(blk + d, 0, n_blocks - 1)) * BLOCK, BLOCK), BLOCK)
                for d in (-1, 0, 1)]
        rope = lambda x, rows: _rope_lanes(x, cos_ref[rows, :], sa_ref[rows, :], sb_ref[rows, :], half)
        q_rot = jnp.concatenate([rope(q[:, c * 128:(c + 1) * 128], band[1]) for c in range(q.shape[1] // 128)],
                                axis=-1)
        kband = jnp.concatenate([rope(k_ref[0, rows, :], rows) for rows in band], axis=0)
        vband = jnp.concatenate([v_ref[0, rows, :] for rows in band], axis=0)
        r = lax.broadcasted_iota(jnp.int32, (2 * BLOCK, 3 * BLOCK), 0) % BLOCK
        c = lax.broadcasted_iota(jnp.int32, (2 * BLOCK, 3 * BLOCK), 1)
        first = jnp.where(blk > 0, 0, BLOCK)
        last = jnp.where(blk < n_blocks - 1, 3 * BLOCK, 2 * BLOCK)
        valid = (jnp.abs(c - BLOCK - r) <= WINDOW) & (c >= first) & (c < last)
        attend([q_rot, q], [lane_halves(kband), kctx], [lane_halves(vband), vctx], [valid, None])

    pl.when(j < n_ctx_blocks)(ctx_path)
    pl.when(j >= n_ctx_blocks)(lat_path)


def _win_call(sink, wq, wk, wv, tabs, n_ctx):
    b, n, _ = wq.shape
    ncb = n_ctx // BLOCK
    nb = n // BLOCK
    kvw = WIN_KV_HEADS * WIN_HEAD_DIM
    cur = lambda i, j: (i, j, 0)
    kv = pl.BlockSpec((1, n, kvw), lambda i, j: (i, 0, 0))
    tab = pl.BlockSpec((n, 128), lambda i, j: (0, 0))
    return pl.pallas_call(
        functools.partial(_win_kernel, n_ctx_blocks=ncb, n_blocks=nb - ncb),
        grid=(b, nb),
        in_specs=[pl.BlockSpec(memory_space=pltpu.SMEM),
                  pl.BlockSpec((1, BLOCK, WIN_Q_HEADS * WIN_HEAD_DIM), cur), kv, kv, tab, tab, tab],
        out_specs=pl.BlockSpec((1, BLOCK, WIN_Q_HEADS * WIN_HEAD_DIM), cur),
        out_shape=jax.ShapeDtypeStruct((b, n, WIN_Q_HEADS * WIN_HEAD_DIM), BF16),
        compiler_params=_cparams(2),
    )(sink, wq, wk, wv, *tabs)


def _merge_kernel(x_ref, s5_ref, mla_ref, win_ref, gate_ref, mod_ref, wglu_ref, bglu_ref, wbr_ref, wout_ref,
                  g1_ref, b1_ref, wr_ref, x1_ref, h2_ref, lg_ref, *, alpha):
    d = x_ref.shape[-1]
    g = jax.nn.gelu(s5_ref[0])
    s5o = g * _sigmoid(_bdot(g.astype(BF16), wglu_ref[...]) + bglu_ref[...])
    branches = (s5o.astype(BF16), mla_ref[0], win_ref[0])
    mix = None
    for kk, o in enumerate(branches):
        term = _sigmoid(gate_ref[0, :, kk * d:(kk + 1) * d].astype(F32)) * _bdot(o, wbr_ref[kk])
        mix = term if mix is None else mix + term
    y = _bdot(mix.astype(BF16), wout_ref[...])
    mod = lambda r: mod_ref[0, 0, r:r + 1, :]
    x1 = _layer_norm(alpha * x_ref[0] + mod(2) * y) * g1_ref[...] + b1_ref[...]
    x1_ref[0] = x1
    h2 = (_layer_norm(x1) * (1.0 + mod(4)) + mod(3)).astype(BF16)
    h2_ref[0] = h2
    lg_ref[0] = _dot_nt(wr_ref[...], h2)


def _merge_call(xall, s5y, mla_o, win_o, gates, mod, wglu, bglu, wbr, wout, g1, b1, wr_t, n_ctx_tiles, alpha):
    b, n, d = xall.shape
    tm = TOKEN_TILE
    tok = lambda w: pl.BlockSpec((1, tm, w), lambda i, t: (i, t, 0))
    full = lambda a: pl.BlockSpec(a.shape, lambda i, t: (0,) * a.ndim)
    return pl.pallas_call(
        functools.partial(_merge_kernel, alpha=alpha),
        grid=(b, n // tm),
        in_specs=[tok(d), tok(BRANCH_WIDTH), tok(BRANCH_WIDTH), tok(BRANCH_WIDTH), tok(N_BRANCH * d),
                  pl.BlockSpec((1, 1, 6, d), lambda i, t: (i, jnp.where(t < n_ctx_tiles, 0, 1), 0, 0)),
                  full(wglu), full(bglu), full(wbr), full(wout), full(g1), full(b1), full(wr_t)],
        out_specs=[tok(d), tok(d), pl.BlockSpec((1, N_EXPERTS, tm), lambda i, t: (i, 0, t))],
        out_shape=[jax.ShapeDtypeStruct((b, n, d), F32), jax.ShapeDtypeStruct((b, n, d), BF16),
                   jax.ShapeDtypeStruct((b, N_EXPERTS, n), F32)],
        compiler_params=_cparams(2),
    )(xall, s5y, mla_o, win_o, gates, mod, wglu, bglu, wbr, wout, g1, b1, wr_t)


def _excl_cumsum_lanes(m):
    rows, n = m.shape
    r = lax.broadcasted_iota(jnp.int32, (128, 128), 0)
    c = lax.broadcasted_iota(jnp.int32, (128, 128), 1)
    tri = jnp.where(r < c, 1.0, 0.0).astype(BF16)
    off = jnp.zeros((rows, 1), F32)
    outs, offs = [], []
    for jb in range(n // 128):
        blk = m[:, jb * 128:(jb + 1) * 128]
        offs.append(off)
        outs.append(_bdot(blk.astype(BF16), tri) + off)
        off = off + jnp.sum(blk, axis=1, keepdims=True)
    return jnp.concatenate(outs, axis=1), offs + [off]


def _topk_slots(affs, caps):
    bits = [pltpu.bitcast(aff, jnp.int32) for aff in affs]

    def body(i, thrs):
        out = []
        for b, cap, thr in zip(bits, caps, thrs):
            cand = thr | (jnp.int32(1) << (30 - i))
            cnt = jnp.sum(jnp.where(b >= cand, 1.0, 0.0), axis=1, keepdims=True)
            out.append(jnp.where(cnt >= cap, cand, thr))
        return tuple(out)

    zero = jnp.zeros((affs[0].shape[0], 1), jnp.int32)
    thrs = lax.fori_loop(0, 31, body, tuple(zero for _ in affs))
    results = []
    for b, cap, thr in zip(bits, caps, thrs):
        gt = jnp.where(b > thr, 1.0, 0.0)
        eq = jnp.where(b == thr, 1.0, 0.0)
        need = cap - jnp.sum(gt, axis=1, keepdims=True)
        sel = gt + eq * jnp.where(_excl_cumsum_lanes(eq)[0] < need, 1.0, 0.0)
        rank, offs = _excl_cumsum_lanes(sel)
        results.append((jnp.where(sel > 0.5, rank, -1.0).astype(jnp.int32), offs[::MOE_TILE // 128]))
    return results


def _route_kernel(lg_ref, slot_ref, aff_ref, bnd_ref, *, n_ctx, cap_ctx, cap_lat):
    lg = lg_ref[0]
    m = jnp.max(lg, axis=0, keepdims=True)
    ex = jnp.exp(lg - m)
    aff = ex / jnp.sum(ex, axis=0, keepdims=True)
    aff_ref[0] = aff
    (slots_ctx, _), (slots, counts) = _topk_slots([aff[:, :n_ctx], aff[:, n_ctx:]], [cap_ctx, cap_lat])
    slot_ref[0, :, :n_ctx] = slots_ctx
    slot_ref[0, :, n_ctx:] = slots
    lane = lax.broadcasted_iota(jnp.int32, bnd_ref.shape[1:], 1)
    bnd = jnp.zeros(bnd_ref.shape[1:], F32)
    for k, cnt in enumerate(counts):
        bnd = jnp.where(lane == k, cnt, bnd)
    bnd_ref[0] = bnd.astype(jnp.int32)


def _route_call(logits_t, n_ctx, cap_ctx, cap_lat):
    b, e, n = logits_t.shape
    assert (n - n_ctx) % MOE_TILE == 0 and (n - n_ctx) // MOE_TILE < MOE_BOUNDS
    spec = pl.BlockSpec((1, e, n), lambda i: (i, 0, 0))
    return pl.pallas_call(
        functools.partial(_route_kernel, n_ctx=n_ctx, cap_ctx=cap_ctx, cap_lat=cap_lat),
        grid=(b,),
        in_specs=[spec],
        out_specs=[spec, spec, pl.BlockSpec((1, e, MOE_BOUNDS), lambda i: (i, 0, 0))],
        out_shape=[jax.ShapeDtypeStruct((b, e, n), jnp.int32), jax.ShapeDtypeStruct((b, e, n), F32),
                   jax.ShapeDtypeStruct((b, e, MOE_BOUNDS), jnp.int32)],
        compiler_params=_cparams(1),
    )(logits_t)


def _gather_kernel(bnd_ref, slot_ref, aff_ref, h_ref, xs_ref, gate_ref, xl_ref, gl_ref, *, n_ctx, cap_ctx):
    cap_lat = xs_ref.shape[2] - cap_ctx
    slot = slot_ref[0, 0]
    aff = aff_ref[0, 0]
    n = h_ref.shape[1]
    ib, ie = pl.program_id(0), pl.program_id(1)
    xl_ref[...] = jnp.zeros(xl_ref.shape, F32)
    gl_ref[...] = jnp.zeros(gl_ref.shape, F32)
    tiles = range((n - n_ctx) // MOE_TILE)
    firsts = [(bnd_ref[ib, ie, kt] // 16) * 16 for kt in tiles]

    def add_window(kt, start):
        tok = slice(n_ctx + kt * MOE_TILE, n_ctx + (kt + 1) * MOE_TILE)
        start = pl.multiple_of(start, 16)
        rows = start + lax.broadcasted_iota(jnp.int32, (MOE_WINDOW, MOE_TILE), 0)
        hit = slot[:, tok] == rows
        xl_ref[pl.ds(start, MOE_WINDOW), :] += _bdot(jnp.where(hit, 1.0, 0.0).astype(BF16), h_ref[0, tok, :])
        picked = jnp.sum(jnp.where(hit, aff[:, tok], 0.0), axis=1, keepdims=True)
        gl_ref[pl.ds(start, MOE_WINDOW), :] += jnp.broadcast_to(picked, (MOE_WINDOW, 128))

    for kt in tiles:
        add_window(kt, firsts[kt])
    for kt in tiles:
        def more(w, carry, kt=kt):
            add_window(kt, firsts[kt] + (w + 1) * MOE_WINDOW)
            return carry

        n_win = (bnd_ref[ib, ie, kt + 1] - firsts[kt] + MOE_WINDOW - 1) // MOE_WINDOW
        lax.fori_loop(0, jnp.maximum(n_win - 1, 0), more, 0)
    iota = lax.broadcasted_iota(jnp.int32, (cap_ctx, n_ctx), 0)
    hit = slot[:, :n_ctx] == iota
    xc = _bdot(jnp.where(hit, 1.0, 0.0).astype(BF16), h_ref[0, :n_ctx, :])
    gc = jnp.sum(jnp.where(hit, aff[:, :n_ctx], 0.0), axis=1, keepdims=True)
    xs_ref[0, 0, :cap_lat] = xl_ref[:cap_lat].astype(BF16)
    xs_ref[0, 0, cap_lat:] = xc.astype(BF16)
    gate_ref[0, 0, :cap_lat] = gl_ref[:cap_lat]
    gate_ref[0, 0, cap_lat:] = jnp.broadcast_to(gc, (cap_ctx, 128))


def _gather_call(bounds, slot, aff, h2, n_ctx, cap_ctx, cap_lat):
    b, e, n = slot.shape
    d = h2.shape[-1]
    cap = cap_lat + cap_ctx
    row = pl.BlockSpec((1, 1, 1, n), lambda ib, ie: (ib, ie, 0, 0))
    return pl.pallas_call(
        functools.partial(_gather_kernel, n_ctx=n_ctx, cap_ctx=cap_ctx),
        grid=(b, e),
        in_specs=[pl.BlockSpec(memory_space=pltpu.SMEM), row, row, pl.BlockSpec((1, n, d), lambda ib, ie: (ib, 0, 0))],
        out_specs=[pl.BlockSpec((1, 1, cap, d), lambda ib, ie: (ib, ie, 0, 0)),
                   pl.BlockSpec((1, 1, cap, 128), lambda ib, ie: (ib, ie, 0, 0))],
        out_shape=[jax.ShapeDtypeStruct((b, e, cap, d), BF16), jax.ShapeDtypeStruct((b, e, cap, 128), F32)],
        scratch_shapes=[pltpu.VMEM((cap_lat + MOE_WINDOW, d), F32), pltpu.VMEM((cap_lat + MOE_WINDOW, 128), F32)],
        compiler_params=_cparams(2),
    )(bounds, slot.reshape(b, e, 1, n), aff.reshape(b, e, 1, n), h2)


def _ffn_kernel(xs_ref, gate_ref, wg_ref, wu_ref, wd_ref, yl_ref, yc_ref, wg_s, wu_s, wd_s):
    ns, _, cap, d = xs_ref.shape
    cap_ctx = yc_ref.shape[2]
    cap_lat = cap - cap_ctx

    @pl.when(pl.program_id(1) == 0)
    def _():
        wg_s[...] = wg_ref[0, 0].astype(BF16)
        wu_s[...] = wu_ref[0, 0].astype(BF16)
        wd_s[...] = wd_ref[0, 0].astype(BF16)

    xs = xs_ref[:, 0].reshape(ns * cap, d)
    a = _bdot(xs, wg_s[...])
    u = _bdot(xs, wu_s[...])
    hm = (a * _sigmoid(a) * u).astype(BF16)
    y = _bdot(hm, wd_s[...]) * gate_ref[:, 0].reshape(ns * cap, 128)[:, 0:1]
    for s in range(ns):
        yl_ref[s, 0, :cap_lat] = y[s * cap:s * cap + cap_lat].astype(yl_ref.dtype)
        yl_ref[s, 0, cap_lat:] = jnp.zeros((MOE_WINDOW, d), yl_ref.dtype)
        yc_ref[s, 0] = y[s * cap + cap_lat:(s + 1) * cap].astype(yc_ref.dtype)


def _ffn_call(layer, xs, gate, w_gate, w_up, w_down, cap_ctx):
    b, e, cap, d = xs.shape
    f = w_gate.shape[-1]
    ns = FFN_SAMPLES
    rows_lat = cap - cap_ctx + MOE_WINDOW
    tok = lambda rows, w: pl.BlockSpec((ns, 1, rows, w), lambda ie, j: (j, ie, 0, 0))
    return pl.pallas_call(
        _ffn_kernel,
        grid=(e, b // ns),
        in_specs=[tok(cap, d), tok(cap, 128),
                  pl.BlockSpec((1, 1, d, f), lambda ie, j: (layer, ie, 0, 0)),
                  pl.BlockSpec((1, 1, d, f), lambda ie, j: (layer, ie, 0, 0)),
                  pl.BlockSpec((1, 1, f, d), lambda ie, j: (layer, ie, 0, 0))],
        out_specs=[tok(rows_lat, d), tok(cap_ctx, d)],
        out_shape=[jax.ShapeDtypeStruct((b, e, rows_lat, d), BF16), jax.ShapeDtypeStruct((b, e, cap_ctx, d), BF16)],
        scratch_shapes=[pltpu.VMEM((d, f), BF16), pltpu.VMEM((d, f), BF16), pltpu.VMEM((f, d), BF16)],
        compiler_params=_cparams(2),
    )(xs, gate, w_gate, w_up, w_down)


def _combine_kernel(bnd_ref, slot_ref, yl_ref, yc_ref, x1_ref, mod_ref, g2_ref, b2_ref, o_ref, fl_ref,
                    *, n_ctx_tiles, alpha):
    ib, t = pl.program_id(0), pl.program_id(1)
    tm = x1_ref.shape[1]
    slot = slot_ref[0]
    win = MOE_WINDOW

    def finish(fl):
        x1 = x1_ref[0]
        o_ref[0] = _layer_norm(alpha * x1 + mod_ref[0, 0, 5:6, :] * fl) * g2_ref[...] + b2_ref[...]

    def onehot(e, first, width):
        iota = lax.broadcasted_iota(jnp.int32, (tm, width), 1)
        return jnp.where(slot[:, e:e + 1] - first == iota, 1.0, 0.0).astype(BF16)

    def ctx_path():
        cap = yc_ref.shape[2]
        fl = None
        for e in range(N_EXPERTS):
            term = _bdot(onehot(e, 0, cap), yc_ref[0, e])
            fl = term if fl is None else fl + term
        finish(fl)

    def lat_path():
        kt = t - n_ctx_tiles
        firsts = [pl.multiple_of((bnd_ref[ib, e, kt] // 16) * 16, 16) for e in range(N_EXPERTS)]
        lane = lax.broadcasted_iota(jnp.int32, (tm, 2 * win), 1)
        pieces, ywins = [], []
        for e in range(0, N_EXPERTS, 2):
            rel = jnp.where(lane < win, slot[:, e:e + 1] - firsts[e], slot[:, e + 1:e + 2] - firsts[e + 1] + win)
            pieces.append(jnp.where(rel == lane, 1.0, 0.0).astype(BF16))
            ywins += [yl_ref[0, e, pl.ds(firsts[e], win), :], yl_ref[0, e + 1, pl.ds(firsts[e + 1], win), :]]
        fl_ref[...] = _bdot(jnp.concatenate(pieces, axis=1), jnp.concatenate(ywins, axis=0))
        for e in range(N_EXPERTS):
            def window(w, carry, e=e):
                first = pl.multiple_of(firsts[e] + (w + 1) * win, 16)
                fl_ref[...] += _bdot(onehot(e, first, win), yl_ref[0, e, pl.ds(first, win), :])
                return carry

            n_win = (bnd_ref[ib, e, kt + 1] - firsts[e] + win - 1) // win
            lax.fori_loop(0, jnp.maximum(n_win - 1, 0), window, 0)
        finish(fl_ref[...])

    pl.when(t < n_ctx_tiles)(ctx_path)
    pl.when(t >= n_ctx_tiles)(lat_path)


def _combine_call(bounds, slot_t, yl, yc, x1, mod, g2, b2, n_ctx_tiles, alpha, latent_only):
    b, n, d = x1.shape
    tm = MOE_TILE
    e = N_EXPERTS
    full = lambda a: pl.BlockSpec(a.shape, lambda i, t: (0,) * a.ndim)
    skip = n_ctx_tiles if latent_only else 0
    return pl.pallas_call(
        functools.partial(_combine_kernel, n_ctx_tiles=n_ctx_tiles, alpha=alpha),
        grid=(b, n // tm),
        scratch_shapes=[pltpu.VMEM((tm, d), F32)],
        in_specs=[pl.BlockSpec(memory_space=pltpu.SMEM),
                  pl.BlockSpec((1, tm, e), lambda i, t: (i, t, 0)),
                  pl.BlockSpec((1,) + yl.shape[1:], lambda i, t: (i, 0, 0, 0)),
                  pl.BlockSpec((1,) + yc.shape[1:], lambda i, t: (i, 0, 0, 0)),
                  pl.BlockSpec((1, tm, d), lambda i, t: (i, t, 0)),
                  pl.BlockSpec((1, 1, 6, d), lambda i, t: (i, jnp.where(t < n_ctx_tiles, 0, 1), 0, 0)),
                  full(g2), full(b2)],
        out_specs=pl.BlockSpec((1, tm, d), lambda i, t: (i, jnp.maximum(t - skip, 0), 0)),
        out_shape=jax.ShapeDtypeStruct((b, n - skip * tm, d), F32),
        compiler_params=_cparams(2),
    )(bounds, slot_t, yl, yc, x1, mod, g2, b2)


def _rope_tables(n_ctx, seq, head_dim, lane_offset):
    half = head_dim // 2
    nf = head_dim // 4
    t = jnp.arange(seq, dtype=F32)
    row = jnp.floor(t / GRID_W)
    col = t - row * GRID_W
    freqs = ROPE_BASE ** (-jnp.arange(nf, dtype=F32) / nf)
    ang = jnp.concatenate([row[:, None] * freqs, col[:, None] * freqs], axis=-1)
    cos, sin = jnp.cos(ang), jnp.sin(ang)
    zeros = jnp.zeros_like(sin)
    n_heads = (128 - lane_offset) // head_dim if lane_offset == 0 else 1
    c = jnp.concatenate([jnp.ones((seq, lane_offset), F32)] + [cos, cos] * n_heads, axis=-1)
    sa = jnp.concatenate([jnp.zeros((seq, lane_offset), F32)] + [-sin, zeros] * n_heads, axis=-1)
    sb = jnp.concatenate([jnp.zeros((seq, lane_offset), F32)] + [zeros, sin] * n_heads, axis=-1)
    pad = 128 - c.shape[1]
    c = jnp.pad(c, ((n_ctx, 0), (0, pad)), constant_values=1.0)
    sa = jnp.pad(sa, ((n_ctx, 0), (0, pad)))
    sb = jnp.pad(sb, ((n_ctx, 0), (0, pad)))
    return c, sa, sb


def _layer_weights(i, p):
    d = p['w_in'].shape[1]
    pts = np.cumsum((S5_WIDTH, MLA_Q_RANK, MLA_KV_RANK, MLA_ROPE, WIN_Q_HEADS * WIN_HEAD_DIM,
                     WIN_KV_HEADS * WIN_HEAD_DIM, WIN_KV_HEADS * WIN_HEAD_DIM))
    cols = jnp.split(p['w_in'][i], [int(v) for v in pts], axis=1)
    kr = jnp.pad(cols[3], ((0, 0), (MLA_NOPE, MLA_PAD - MLA_NOPE - MLA_ROPE)))
    w_cat = jnp.concatenate([cols[0], cols[1], cols[2], kr, cols[4], cols[5], cols[6], cols[7]], axis=1)
    dq = MLA_NOPE + MLA_ROPE
    wq = p['mla_w_uq'][i].reshape(MLA_Q_RANK, MLA_HEADS, dq)
    wq = jnp.pad(wq, ((0, 0), (0, 0), (0, MLA_PAD - dq))).reshape(MLA_Q_RANK, MLA_HEADS * MLA_PAD)
    wkv = p['mla_w_ukv'][i].reshape(MLA_KV_RANK, MLA_HEADS, MLA_NOPE + MLA_V)
    wk = jnp.pad(wkv[:, :, :MLA_NOPE], ((0, 0), (0, 0), (0, MLA_PAD - MLA_NOPE)))
    wk = wk.reshape(MLA_KV_RANK, MLA_HEADS * MLA_PAD)
    wv = wkv[:, :, MLA_NOPE:].reshape(MLA_KV_RANK, MLA_HEADS * MLA_V)
    row = lambda a: a[i].astype(F32).reshape(1, -1)
    return dict(
        w_cat=w_cat.astype(BF16), wq=wq.astype(BF16), wk=wk.astype(BF16), wv=wv.astype(BF16),
        qg=row(p['mla_q_norm']), kvg=row(p['mla_kv_norm']),
        wglu=p['s5_w_glu'][i].astype(BF16), bglu=row(p['s5_b_glu']),
        sink=p['win_sink'][i].astype(F32),
        wbr=p['w_branch'][i].astype(BF16), wout=p['w_out'][i].astype(BF16),
        g1=row(p['ln1_g']), b1=row(p['ln1_b']), g2=row(p['ln2_g']), b2=row(p['ln2_b']),
        wr_t=p['w_router'][i].T.astype(BF16),
    )


def _forward(p):
    x, c, ctx, c_ctx = p['x'], p['c'], p['ctx'], p['c_ctx']
    b, seq, d = x.shape
    n_ctx = ctx.shape[1]
    depth = p['w_ada'].shape[0]
    assert b == 8 and seq % TOKEN_TILE == 0 and n_ctx % TOKEN_TILE == 0 and seq % GRID_W == 0
    alpha = float((2 * depth) ** 0.25)
    n_ctx_tiles = n_ctx // TOKEN_TILE
    cap_lat = CAPACITY_FACTOR * seq // N_EXPERTS
    cap_ctx = CAPACITY_FACTOR * n_ctx // N_EXPERTS

    cond = jnp.concatenate([c, c_ctx[None], jnp.zeros((16 - b - 1, d), F32)], axis=0)
    mods = _ada_call(cond, p['w_ada'], p['b_ada'])
    mods = mods.reshape(depth, 16, 6, d)
    tabs_mla = _rope_tables(n_ctx, seq, MLA_ROPE, MLA_NOPE)
    tabs_win = _rope_tables(n_ctx, seq, WIN_HEAD_DIM, 0)
    s5w = _s5_param_call(p)

    xall = jnp.concatenate([ctx, x], axis=1)
    for i in range(depth):
        w = _layer_weights(i, p)
        mod = jnp.stack([jnp.broadcast_to(mods[i, b], (b, 6, d)), mods[i, :b]], axis=1)
        u, wq, wk, wv, gates, qp, qr, kk, vv = _in_call(xall, mod, w['w_cat'], w['qg'], w['kvg'], w['wq'], w['wk'],
                                                        w['wv'], tabs_mla, n_ctx)
        s5y = _s5_call(i, u, *s5w, p['s5_d'], n_ctx)
        mla_o = _mla_attn_call(qp, qr, kk, vv, n_ctx)
        win_o = _win_call(w['sink'], wq, wk, wv, tabs_win, n_ctx)
        x1, h2, logits_t = _merge_call(xall, s5y, mla_o, win_o, gates, mod, w['wglu'], w['bglu'], w['wbr'],
                                       w['wout'], w['g1'], w['b1'], w['wr_t'], n_ctx_tiles, alpha)
        slot, aff, bounds = _route_call(logits_t, n_ctx, cap_ctx, cap_lat)
        xs, gate = _gather_call(bounds, slot, aff, h2, n_ctx, cap_ctx, cap_lat)
        yl, yc = _ffn_call(i, xs, gate, p['w_gate'], p['w_up'], p['w_down'], cap_ctx)
        slot_t = jnp.swapaxes(slot, 1, 2)
        xall = _combine_call(bounds, slot_t, yl, yc, x1, mod, w['g2'], w['b2'], n_ctx_tiles, alpha,
                             latent_only=(i == depth - 1))
    return xall


def kernel(x, c, ctx, c_ctx, w_ada, b_ada, w_in, s5_lam_re, s5_lam_im, s5_log_dt, s5_b_re, s5_b_im, s5_c_re, s5_c_im, s5_d, s5_w_glu, s5_b_glu, mla_q_norm, mla_w_uq, mla_kv_norm, mla_w_ukv, win_sink, w_branch, w_out, ln1_g, ln1_b, ln2_g, ln2_b, w_router, w_gate, w_up, w_down):
    return _forward(dict(
        x=x, c=c, ctx=ctx, c_ctx=c_ctx, w_ada=w_ada, b_ada=b_ada, w_in=w_in, s5_lam_re=s5_lam_re,
        s5_lam_im=s5_lam_im, s5_log_dt=s5_log_dt, s5_b_re=s5_b_re, s5_b_im=s5_b_im, s5_c_re=s5_c_re,
        s5_c_im=s5_c_im, s5_d=s5_d, s5_w_glu=s5_w_glu, s5_b_glu=s5_b_glu, mla_q_norm=mla_q_norm,
        mla_w_uq=mla_w_uq, mla_kv_norm=mla_kv_norm, mla_w_ukv=mla_w_ukv, win_sink=win_sink, w_branch=w_branch,
        w_out=w_out, ln1_g=ln1_g, ln1_b=ln1_b, ln2_g=ln2_g, ln2_b=ln2_b, w_router=w_router, w_gate=w_gate,
        w_up=w_up, w_down=w_down))
```

```python
import functools
import math

import jax
import jax.numpy as jnp
import numpy as np
from jax import lax
from jax.experimental import pallas as pl
from jax.experimental.pallas import tpu as pltpu

F32 = jnp.float32
BF16 = jnp.bfloat16
HIGHEST = lax.Precision.HIGHEST

GRID_W = 64
S5_WIDTH = 512
S5_GROUP = 16
S5_GROUPS = S5_WIDTH // S5_GROUP
S5_STATE = 64
S5_CHUNK = 16
S5_GROUPS_PER_STEP = 4
MLA_HEADS = 8
MLA_NOPE = 64
MLA_ROPE = 32
MLA_V = 64
MLA_Q_RANK = 384
MLA_KV_RANK = 256
MLA_PAD = 128
MLA_HEADS_PER_STEP = 4
MLA_SCALE = (MLA_NOPE + MLA_ROPE) ** -0.5
WIN_Q_HEADS = 8
WIN_KV_HEADS = 2
WIN_GROUP = WIN_Q_HEADS // WIN_KV_HEADS
WIN_HEAD_DIM = 64
WINDOW = 128
BLOCK = 128
WIN_SCALE = WIN_HEAD_DIM ** -0.5
N_BRANCH = 3
BRANCH_WIDTH = 512
N_EXPERTS = 16
CAPACITY_FACTOR = 2
ROPE_BASE = 10000.0
LN_EPS = 1e-6
NEG_INF = -1e30
LOG2E = math.log2(math.e)
TOKEN_TILE = 256
IN_TILE = 384
MOE_TILE = 256
MOE_WINDOW = 64
MOE_BOUNDS = 16
FFN_SAMPLES = 2
VMEM_LIMIT = 56 * 1024 * 1024


def _cparams(n_axes):
    return pltpu.CompilerParams(dimension_semantics=("arbitrary",) * n_axes, vmem_limit_bytes=VMEM_LIMIT)


def _bdot(a, b):
    return jnp.dot(a, b, preferred_element_type=F32)


def _dot_nt(a, b):
    return lax.dot_general(a, b, (((1,), (1,)), ((), ())), preferred_element_type=F32)


def _layer_norm(x):
    mu = jnp.mean(x, axis=-1, keepdims=True)
    xc = x - mu
    var = jnp.mean(xc * xc, axis=-1, keepdims=True)
    return xc * lax.rsqrt(var + LN_EPS)


def _ada_kernel(cond_ref, w_ref, b_ref, o_ref):
    s = cond_ref[...]
    s = s * jax.nn.sigmoid(s)
    o_ref[0] = jnp.dot(s, w_ref[0], precision=HIGHEST, preferred_element_type=F32) + b_ref[0]


def _ada_call(cond, w_ada, b_ada):
    depth, d, d6 = w_ada.shape
    tn = 1536
    rows = cond.shape[0]
    return pl.pallas_call(
        _ada_kernel,
        grid=(depth, d6 // tn),
        in_specs=[
            pl.BlockSpec((rows, d), lambda i, j: (0, 0)),
            pl.BlockSpec((1, d, tn), lambda i, j: (i, 0, j)),
            pl.BlockSpec((1, 1, tn), lambda i, j: (i, 0, j)),
        ],
        out_specs=pl.BlockSpec((1, rows, tn), lambda i, j: (i, 0, j)),
        out_shape=jax.ShapeDtypeStruct((depth, rows, d6), F32),
        compiler_params=_cparams(2),
    )(cond, w_ada, b_ada.reshape(depth, 1, d6))


IN_WIDTHS = (S5_WIDTH, MLA_Q_RANK, MLA_KV_RANK, MLA_PAD, WIN_Q_HEADS * WIN_HEAD_DIM,
             WIN_KV_HEADS * WIN_HEAD_DIM, WIN_KV_HEADS * WIN_HEAD_DIM)
IN_OFFSETS = tuple(int(v) for v in np.cumsum((0,) + IN_WIDTHS))


def _mod_rows(mod_ref, r, tile, n_ctx):
    row = pl.program_id(1) * tile + lax.broadcasted_iota(jnp.int32, (tile, 1), 0)
    return jnp.where(row < n_ctx, mod_ref[0, 0, r:r + 1, :], mod_ref[0, 1, r:r + 1, :])


def _rope_lanes(x, cos, sa, sb, shift):
    return x * cos + pltpu.roll(x, 128 - shift, 1) * sa + pltpu.roll(x, shift, 1) * sb


def _mla_project(qa, kva, kr, qg, kvg, wq_ref, wk_ref, wv_ref, cos, sa, sb, qp_ref, qr_ref, k_ref, v_ref):
    def rms(x, gain):
        return (x * lax.rsqrt(jnp.mean(x * x, axis=-1, keepdims=True) + LN_EPS) * gain).astype(BF16)

    qn = rms(qa, qg)
    kvn = rms(kva, kvg)
    q = _bdot(qn, wq_ref[...]) * (MLA_SCALE * LOG2E)
    k = _bdot(kvn, wk_ref[...])
    v_ref[0] = _bdot(kvn, wv_ref[...]).astype(BF16)
    kr_rot = _rope_lanes(kr, cos, sa, sb, MLA_ROPE // 2)
    qp_ref[0] = q.astype(BF16)
    for h in range(MLA_HEADS):
        sl = slice(h * MLA_PAD, (h + 1) * MLA_PAD)
        qr_ref[0, :, sl] = _rope_lanes(q[:, sl], cos, sa, sb, MLA_ROPE // 2).astype(BF16)
        k_ref[0, :, sl] = (k[:, sl] + kr_rot).astype(BF16)


def _in_kernel(x_ref, mod_ref, w_ref, qg_ref, kvg_ref, wuq_ref, wuk_ref, wuv_ref, cos_ref, sa_ref, sb_ref,
               u_ref, wq_ref, wk_ref, wv_ref, gate_ref, qp_ref, qr_ref, k_ref, v_ref, *, n_ctx):
    tile = x_ref.shape[1]
    xn = _layer_norm(x_ref[0])
    h = (xn * (1.0 + _mod_rows(mod_ref, 1, tile, n_ctx)) + _mod_rows(mod_ref, 0, tile, n_ctx)).astype(BF16)
    widths = IN_WIDTHS + (gate_ref.shape[-1],)
    proj = lambda i: _bdot(h, w_ref[:, IN_OFFSETS[i]:IN_OFFSETS[i] + widths[i]])
    _mla_project(proj(1), proj(2), proj(3), qg_ref[...], kvg_ref[...], wuq_ref, wuk_ref, wuv_ref,
                 cos_ref[...], sa_ref[...], sb_ref[...], qp_ref, qr_ref, k_ref, v_ref)
    u_ref[0] = proj(0)
    wq_ref[0], wk_ref[0], wv_ref[0] = proj(4), proj(5), proj(6)
    gate_ref[0] = proj(7).astype(gate_ref.dtype)


_sigmoid = jax.nn.sigmoid


def _in_call(xall, mod, w_cat, qg, kvg, wuq, wuk, wuv, tabs, n_ctx):
    b, n, d = xall.shape
    tm = IN_TILE
    gate_w = w_cat.shape[1] - IN_OFFSETS[-1]
    hw = MLA_HEADS * MLA_PAD
    tok = lambda w: pl.BlockSpec((1, tm, w), lambda i, t: (i, t, 0))
    full = lambda a: pl.BlockSpec(a.shape, lambda i, t: (0,) * a.ndim)
    tab = pl.BlockSpec((tm, 128), lambda i, t: (t, 0))
    out_widths = (IN_WIDTHS[0],) + IN_WIDTHS[4:7] + (gate_w, hw, hw, hw, MLA_HEADS * MLA_V)
    out_dtypes = (F32,) * 4 + (BF16,) * 5
    return pl.pallas_call(
        functools.partial(_in_kernel, n_ctx=n_ctx),
        grid=(b, n // tm),
        in_specs=[tok(d), pl.BlockSpec((1, 2, 6, d), lambda i, t: (i, 0, 0, 0)),
                  pl.BlockSpec(w_cat.shape, lambda i, t: (0, 0), pipeline_mode=pl.Buffered(1)),
                  full(qg), full(kvg), full(wuq), full(wuk), full(wuv), tab, tab, tab],
        out_specs=[tok(w) for w in out_widths],
        out_shape=[jax.ShapeDtypeStruct((b, n, w), dt) for w, dt in zip(out_widths, out_dtypes)],
        compiler_params=_cparams(2),
    )(xall, mod, w_cat, qg, kvg, wuq, wuk, wuv, *tabs)


S5_LANE_GROUPS = 128 // S5_GROUP
S5_SCAN_GROUPS = 4
S5_PITCH_PAD = 8


def _s5_param_kernel(*refs):
    for g in range(refs[0].shape[2]):
        _s5_param_group(g, *refs)


def _s5_param_group(g, cre_ref, cim_ref, bre_ref, bim_ref, pr_ref, pi_ref, tz_ref, bc_ref, cc_ref, coef_ref):
    t = S5_CHUNK
    w = t * S5_GROUP
    nt = (((1,), (1,)), ((), ()))
    tz = None
    bcs, ccs, coefs = [], [], []
    for d in range(2):
        cre, cim = cre_ref[0, d, g], cim_ref[0, d, g]
        bre, bim = bre_ref[0, d, g], bim_ref[0, d, g]
        power = lambda k: (pr_ref[0, d, g, k:k + 1, :], pi_ref[0, d, g, k:k + 1, :])
        rt = []
        for k in range(t + 1):
            prk, pik = power(k)
            rt.append(jnp.concatenate([cre * prk - cim * pik, -(cre * pik + cim * prk)], axis=1))
        bt = jnp.concatenate([bre, bim], axis=1)
        zeros = jnp.zeros((S5_GROUP, w), F32)
        if d == 0:
            kt = lax.dot_general(bt, jnp.concatenate(rt[:t], axis=0), nt, precision=HIGHEST,
                                 preferred_element_type=F32)
            pad = jnp.concatenate([zeros, kt], axis=1)
            rows = [kt] + [pltpu.roll(pad, S5_GROUP * s, 1)[:, w:] for s in range(1, t)]
        else:
            kt = lax.dot_general(bt, jnp.concatenate(rt[t - 1::-1], axis=0), nt, precision=HIGHEST,
                                 preferred_element_type=F32)
            pad = jnp.concatenate([kt, zeros], axis=1)
            rows = [pltpu.roll(pad, 2 * w - S5_GROUP * (t - 1 - s), 1)[:, :w] for s in range(t - 1)] + [kt]
        tz_d = jnp.concatenate(rows, axis=0)
        tz = tz_d if tz is None else tz + tz_d
        bc_rows = []
        for s in range(t):
            prk, pik = power(t - 1 - s if d == 0 else s)
            br = bre * prk - bim * pik
            bi = bim * prk + bre * pik
            bc_rows.append(jnp.concatenate([br, bi, bi, br], axis=1))
        bcs.append(jnp.concatenate(bc_rows, axis=0))
        ccs.append(jnp.concatenate(rt[1:] if d == 0 else rt[t:0:-1], axis=0))
        er, ei = power(t)
        coefs += [jnp.concatenate([er, er], axis=1), jnp.concatenate([-ei, ei], axis=1),
                  jnp.concatenate([ei, -ei], axis=1)]
    tz_ref[0, g] = tz.astype(BF16)
    bc_ref[0, g] = jnp.concatenate(bcs, axis=1).astype(BF16)
    cc_ref[0, g] = jnp.concatenate(ccs, axis=1).astype(BF16)
    coef_ref[0, g] = jnp.concatenate(coefs + [jnp.zeros((2, 2 * S5_STATE), F32)], axis=0)


def _s5_param_call(p):
    t = S5_CHUNK
    f = lambda name: p[name].astype(F32)
    lam_re, lam_im = f('s5_lam_re'), f('s5_lam_im')
    depth = lam_re.shape[0]
    dt = jnp.exp(f('s5_log_dt'))[..., None]
    k = jnp.arange(t + 1, dtype=F32)[:, None]
    mag = jnp.exp((lam_re * dt)[..., None, :] * k)
    ang = (lam_im * dt)[..., None, :] * k
    pr, pi = mag * jnp.cos(ang), mag * jnp.sin(ang)
    ar, ai = pr[..., 1, :], pi[..., 1, :]
    den = lam_re * lam_re + lam_im * lam_im
    qr = (((ar - 1) * lam_re + ai * lam_im) / den)[..., None, :]
    qi = ((ai * lam_re - (ar - 1) * lam_im) / den)[..., None, :]
    b_re = jnp.swapaxes(f('s5_b_re'), -1, -2)
    b_im = jnp.swapaxes(f('s5_b_im'), -1, -2)
    bbr = qr * b_re - qi * b_im
    bbi = qr * b_im + qi * b_re
    g, hg, ps = S5_GROUPS, S5_GROUP, S5_STATE
    w = t * hg
    gs = S5_LANE_GROUPS
    small = lambda rows: pl.BlockSpec((1, 2, gs, rows, ps), lambda i, j: (i, 0, j, 0, 0))
    out = lambda cols: pl.BlockSpec((1, gs, w, cols), lambda i, j: (i, j, 0, 0))
    return pl.pallas_call(
        _s5_param_kernel,
        grid=(depth, g // gs),
        in_specs=[small(hg)] * 4 + [small(t + 1)] * 2,
        out_specs=[out(w), out(2 * w), out(w), pl.BlockSpec((1, gs, 8, 2 * ps), lambda i, j: (i, j, 0, 0))],
        out_shape=[jax.ShapeDtypeStruct((depth, g, w, w), BF16), jax.ShapeDtypeStruct((depth, g, w, 2 * w), BF16),
                   jax.ShapeDtypeStruct((depth, g, w, w), BF16), jax.ShapeDtypeStruct((depth, g, 8, 2 * ps), F32)],
        compiler_params=_cparams(2),
    )(f('s5_c_re'), f('s5_c_im'), bbr, bbi, pr, pi)


def _s5_kernel(u_ref, tz_ref, bc_ref, cc_ref, coef_ref, d_ref, y_ref, uy_ref, loc_ref, sp_ref, slab_ref,
               *, n_ctx, n_batch):
    ph, b = pl.program_id(1), pl.program_id(2)
    t, hg, ng = S5_CHUNK, S5_GROUP, S5_LANE_GROUPS
    ncc = n_ctx // t
    ncl = (u_ref.shape[1] - n_ctx) // t
    nc = ncc + ncl
    pitch = nc + S5_PITCH_PAD

    def to_chunk_rows(slabs):
        tr = [s.T for s in slabs]
        return [jnp.concatenate([x[g * hg:(g + 1) * hg] for x in tr], axis=0).T for g in range(ng)]

    def to_token_slabs(rows):
        tr = [r.T for r in rows]
        return [jnp.concatenate([x[tau * hg:(tau + 1) * hg] for x in tr], axis=0).T for tau in range(t)]

    base = pl.multiple_of(b * pitch, 8)
    cbase = pl.multiple_of(b * ncc, 8)

    @pl.when(ph == 0)
    def _():
        rows = to_chunk_rows([u_ref[0, pl.ds(n_ctx + tau, ncl, stride=t), :] for tau in range(t)])
        for g in range(ng):
            uy_ref[g, pl.ds(base + ncc, ncl), :] = rows[g]
            uy_ref[g, pl.ds(base + nc, S5_PITCH_PAD), :] = jnp.zeros((S5_PITCH_PAD, t * hg), F32)
        for tau in range(t):
            slab_ref[tau, pl.ds(cbase, ncc), :] = u_ref[0, pl.ds(tau, ncc, stride=t), :]

    @pl.when((ph == 1) & (b == 0))
    def _():
        rows = to_chunk_rows([slab_ref[tau] for tau in range(t)])
        for g in range(ng):
            for s in range(n_batch):
                uy_ref[g, s * pitch:s * pitch + ncc, :] = rows[g][s * ncc:(s + 1) * ncc]
        for part in range(ng // S5_SCAN_GROUPS):
            gs = [part * S5_SCAN_GROUPS + gl for gl in range(S5_SCAN_GROUPS)]
            for gl, g in enumerate(gs):
                ub = uy_ref[g].astype(BF16)
                loc = _bdot(ub, bc_ref[0, g])
                for q in range(4):
                    loc_ref[gl, q] = loc[:, q * 128:(q + 1) * 128]
                uy_ref[g] = _bdot(ub, tz_ref[0, g])
                for s in range(n_batch):
                    for d in range(2):
                        sp_ref[gl, d, s * pitch + nc:(s + 1) * pitch, :] = jnp.zeros((S5_PITCH_PAD, 128), F32)

            def coef(g, r):
                return jnp.broadcast_to(coef_ref[0, g, r:r + 1, :], (n_batch, 128))

            def step(i, carry):
                cb = jnp.where(i < ncc, ncc - 1 - i, nc + ncc - 1 - i)
                fwd = pl.ds(i, n_batch, stride=pitch)
                bwd = pl.ds(cb, n_batch, stride=pitch)
                out = []
                for gl, g in enumerate(gs):
                    v0f, v1f, v0b, v1b = carry[gl]
                    sp_ref[gl, 0, fwd, :] = v0f
                    sp_ref[gl, 1, bwd, :] = v0b
                    n0f = coef(g, 0) * v0f + coef(g, 1) * v1f + loc_ref[gl, 0, fwd, :]
                    n1f = coef(g, 0) * v1f + coef(g, 2) * v0f + loc_ref[gl, 1, fwd, :]
                    n0b = coef(g, 3) * v0b + coef(g, 4) * v1b + loc_ref[gl, 2, bwd, :]
                    n1b = coef(g, 3) * v1b + coef(g, 5) * v0b + loc_ref[gl, 3, bwd, :]
                    out.append((n0f, n1f, n0b, n1b))
                return tuple(out)

            z = jnp.zeros((n_batch, 128), F32)
            lax.fori_loop(0, nc, step, tuple((z, z, z, z) for _ in gs))
            for gl, g in enumerate(gs):
                sp = jnp.concatenate([sp_ref[gl, 0], sp_ref[gl, 1]], axis=1).astype(BF16)
                uy_ref[g] = uy_ref[g] + _dot_nt(sp, cc_ref[0, g])
        rows = [jnp.concatenate([uy_ref[g, s * pitch:s * pitch + ncc, :] for s in range(n_batch)], axis=0)
                for g in range(ng)]
        for tau, slab in enumerate(to_token_slabs(rows)):
            slab_ref[tau] = slab

    @pl.when(ph == 1)
    def _():
        slabs = to_token_slabs([uy_ref[g, pl.ds(base + ncc, ncl), :] for g in range(ng)])
        for tau in range(t):
            y_ref[0, pl.ds(n_ctx + tau, ncl, stride=t), :] = slabs[tau]
            y_ref[0, pl.ds(tau, ncc, stride=t), :] = slab_ref[tau, pl.ds(cbase, ncc), :]
        y_ref[0] = y_ref[0] + d_ref[0] * u_ref[0]


def _s5_call(layer, u, tz, bc, cc, coef, s5_d, n_ctx):
    b, n, width = u.shape
    t, ng = S5_CHUNK, S5_LANE_GROUPS
    rows = b * (n // t + S5_PITCH_PAD)
    assert (n - n_ctx) // t == 128 and b * (n_ctx // t) == 128
    wspec = lambda a: pl.BlockSpec((1, ng) + a.shape[2:], lambda g, ph, i: (layer, g, 0, 0))
    return pl.pallas_call(
        functools.partial(_s5_kernel, n_ctx=n_ctx, n_batch=b),
        grid=(width // 128, 2, b),
        in_specs=[pl.BlockSpec((1, n, 128), lambda g, ph, i: (i, 0, g)),
                  wspec(tz), wspec(bc), wspec(cc), wspec(coef),
                  pl.BlockSpec((1, 1, 128), lambda g, ph, i: (layer, 0, g))],
        out_specs=pl.BlockSpec((1, n, 128), lambda g, ph, i: (i * ph, 0, g)),
        out_shape=jax.ShapeDtypeStruct((b, n, width), F32),
        scratch_shapes=[pltpu.VMEM((ng, rows, t * S5_GROUP), F32),
                        pltpu.VMEM((S5_SCAN_GROUPS, 4, rows, 128), F32),
                        pltpu.VMEM((S5_SCAN_GROUPS, 2, rows, 128), F32),
                        pltpu.VMEM((t, 128, 128), F32)],
        compiler_params=_cparams(3),
    )(u, tz, bc, cc, coef, s5_d.astype(F32).reshape(s5_d.shape[0], 1, width))


def _lane_chunks(xs):
    return [x[:, i * 128:(i + 1) * 128] for x in xs for i in range(x.shape[1] // 128)]


def _row_max(scores, floor=None):
    mm = functools.reduce(jnp.maximum, _lane_chunks(scores))
    if floor is not None:
        mm = jnp.maximum(mm, floor)
    return jnp.max(mm, axis=-1, keepdims=True)


def _softmax_av(scores, values, sink=None):
    m = _row_max(scores, sink)
    ps = [jnp.exp2(s - m) for s in scores]
    ll = functools.reduce(jnp.add, _lane_chunks(ps))
    if sink is not None:
        lane = lax.broadcasted_iota(jnp.int32, sink.shape, 1)
        ll = ll + jnp.where(lane == 0, jnp.exp2(sink - m), 0.0)
    l = jnp.sum(ll, axis=-1, keepdims=True)
    o = functools.reduce(jnp.add, [_bdot(p.astype(BF16), v) for p, v in zip(ps, values)])
    return o / l


def _mla_attn_kernel(qp_ref, qr_ref, k_ref, v_ref, o_ref, *, n_ctx, n_ctx_tiles):
    heads = qp_ref.shape[-1] // MLA_PAD
    t = pl.program_id(2)
    lane = lax.broadcasted_iota(jnp.int32, (1, heads * MLA_V), 1)

    def run(latent):
        all_scores = []
        for h in range(heads):
            sl = slice(h * MLA_PAD, (h + 1) * MLA_PAD)
            scores = [_dot_nt(qp_ref[0, :, sl], k_ref[0, :n_ctx, sl])]
            if latent:
                scores.append(_dot_nt(qr_ref[0, :, sl], k_ref[0, n_ctx:, sl]))
            all_scores.append(scores)
        probs = []
        for scores in all_scores:
            m = _row_max(scores)
            ps = [jnp.exp2(s - m) for s in scores]
            l = jnp.sum(functools.reduce(jnp.add, _lane_chunks(ps)), axis=-1, keepdims=True)
            probs.append(([p.astype(BF16) for p in ps], l))
        acc = None
        for h, (ps, l) in enumerate(probs):
            own = (lane >= h * MLA_V) & (lane < (h + 1) * MLA_V)
            zero = jnp.zeros((), BF16)
            values = [jnp.where(own, v_ref[0, :n_ctx, :], zero)]
            if latent:
                values.append(jnp.where(own, v_ref[0, n_ctx:, :], zero))
            o = functools.reduce(jnp.add, [_bdot(p, v) for p, v in zip(ps, values)]) / l
            acc = o if acc is None else acc + o
        o_ref[0] = acc.astype(o_ref.dtype)

    pl.when(t < n_ctx_tiles)(lambda: run(False))
    pl.when(t >= n_ctx_tiles)(lambda: run(True))


def _mla_attn_call(qp, qr, k, v, n_ctx):
    b, n, _ = qp.shape
    tq = TOKEN_TILE
    hp = MLA_HEADS_PER_STEP
    qspec = pl.BlockSpec((1, tq, hp * MLA_PAD), lambda i, h, t: (i, t, h))
    return pl.pallas_call(
        functools.partial(_mla_attn_kernel, n_ctx=n_ctx, n_ctx_tiles=n_ctx // tq),
        grid=(b, MLA_HEADS // hp, n // tq),
        in_specs=[qspec, qspec,
                  pl.BlockSpec((1, n, hp * MLA_PAD), lambda i, h, t: (i, 0, h)),
                  pl.BlockSpec((1, n, hp * MLA_V), lambda i, h, t: (i, 0, h))],
        out_specs=pl.BlockSpec((1, tq, hp * MLA_V), lambda i, h, t: (i, t, h)),
        out_shape=jax.ShapeDtypeStruct((b, n, MLA_HEADS * MLA_V), BF16),
        compiler_params=_cparams(3),
    )(qp, qr, k, v)


def _win_kernel(sink_ref, q_ref, k_ref, v_ref, cos_ref, sa_ref, sb_ref, o_ref, *, n_ctx_blocks, n_blocks):
    j = pl.program_id(1)
    hd = WIN_HEAD_DIM
    half = hd // 2
    lane = lax.broadcasted_iota(jnp.int32, (1, 128), 1)
    lo = jnp.where(lane < hd, 1.0, 0.0)
    hi = 1.0 - lo
    upper_rows = lax.broadcasted_iota(jnp.int32, (2 * BLOCK, 128), 0) < BLOCK

    def lane_halves(x):
        xr = pltpu.roll(x, hd, 1)
        return {(0, 0): (x * lo).astype(BF16), (0, 1): (xr * hi).astype(BF16),
                (1, 0): (xr * lo).astype(BF16), (1, 1): (x * hi).astype(BF16)}

    def attend(queries, keys, values, masks):
        all_scores = {}
        for kh in range(WIN_KV_HEADS):
            stacked = [jnp.concatenate([qs[:, (2 * kh) * 128:(2 * kh + 1) * 128],
                                        qs[:, (2 * kh + 1) * 128:(2 * kh + 2) * 128]], axis=0).astype(BF16)
                       for qs in queries]
            for par in range(2):
                scores = []
                for qst, ks, msk in zip(stacked, keys, masks):
                    s = _dot_nt(qst, ks[(kh, par)])
                    scores.append(s if msk is None else jnp.where(msk, s, NEG_INF))
                all_scores[(kh, par)] = scores
        for kh in range(WIN_KV_HEADS):
            acc = None
            for par in range(2):
                sink = jnp.where(upper_rows, sink_ref[4 * kh + par], sink_ref[4 * kh + 2 + par]) * LOG2E
                o = _softmax_av(all_scores[(kh, par)], [vs[(kh, par)] for vs in values], sink)
                acc = o if acc is None else acc + o
            o_ref[0, :, (2 * kh) * 128:(2 * kh + 1) * 128] = acc[:BLOCK].astype(o_ref.dtype)
            o_ref[0, :, (2 * kh + 1) * 128:(2 * kh + 2) * 128] = acc[BLOCK:].astype(o_ref.dtype)

    q = q_ref[0] * (WIN_SCALE * LOG2E)
    n_ctx = n_ctx_blocks * BLOCK
    kctx = lane_halves(k_ref[0, :n_ctx, :])
    vctx = lane_halves(v_ref[0, :n_ctx, :])

    def ctx_path():
        attend([q], [kctx], [vctx], [None])

    def lat_path():
        blk = j - n_ctx_blocks
        band = [pl.ds(pl.multiple_of((n_ctx_blocks + jnp.clip(blk + d, 0, n_blocks - 1)) * BLOCK, BLOCK), BLOCK)
                for d in (-1, 0, 1)]
        rope = lambda x, rows: _rope_lanes(x, cos_ref[rows, :], sa_ref[rows, :], sb_ref[rows, :], half)
        q_rot = jnp.concatenate([rope(q[:, c * 128:(c + 1) * 128], band[1]) for c in range(q.shape[1] // 128)],
                                axis=-1)
        kband = jnp.concatenate([rope(k_ref[0, rows, :], rows) for rows in band], axis=0)
        vband = jnp.concatenate([v_ref[0, rows, :] for rows in band], axis=0)
        r = lax.broadcasted_iota(jnp.int32, (2 * BLOCK, 3 * BLOCK), 0) % BLOCK
        c = lax.broadcasted_iota(jnp.int32, (2 * BLOCK, 3 * BLOCK), 1)
        first = jnp.where(blk > 0, 0, BLOCK)
        last = jnp.where(blk < n_blocks - 1, 3 * BLOCK, 2 * BLOCK)
        valid = (jnp.abs(c - BLOCK - r) <= WINDOW) & (c >= first) & (c < last)
        attend([q_rot, q], [lane_halves(kband), kctx], [lane_halves(vband), vctx], [valid, None])

    pl.when(j < n_ctx_blocks)(ctx_path)
    pl.when(j >= n_ctx_blocks)(lat_path)


def _win_call(sink, wq, wk, wv, tabs, n_ctx):
    b, n, _ = wq.shape
    ncb = n_ctx // BLOCK
    nb = n // BLOCK
    kvw = WIN_KV_HEADS * WIN_HEAD_DIM
    cur = lambda i, j: (i, j, 0)
    kv = pl.BlockSpec((1, n, kvw), lambda i, j: (i, 0, 0))
    tab = pl.BlockSpec((n, 128), lambda i, j: (0, 0))
    return pl.pallas_call(
        functools.partial(_win_kernel, n_ctx_blocks=ncb, n_blocks=nb - ncb),
        grid=(b, nb),
        in_specs=[pl.BlockSpec(memory_space=pltpu.SMEM),
                  pl.BlockSpec((1, BLOCK, WIN_Q_HEADS * WIN_HEAD_DIM), cur), kv, kv, tab, tab, tab],
        out_specs=pl.BlockSpec((1, BLOCK, WIN_Q_HEADS * WIN_HEAD_DIM), cur),
        out_shape=jax.ShapeDtypeStruct((b, n, WIN_Q_HEADS * WIN_HEAD_DIM), BF16),
        compiler_params=_cparams(2),
    )(sink, wq, wk, wv, *tabs)


def _merge_kernel(x_ref, s5_ref, mla_ref, win_ref, gate_ref, mod_ref, wglu_ref, bglu_ref, wbr_ref, wout_ref,
                  g1_ref, b1_ref, wr_ref, x1_ref, h2_ref, lg_ref, *, alpha):
    d = x_ref.shape[-1]
    proj = {1: _bdot(mla_ref[0], wbr_ref[1]), 2: _bdot(win_ref[0], wbr_ref[2])}
    g = jax.nn.gelu(s5_ref[0])
    s5o = g * _sigmoid(_bdot(g.astype(BF16), wglu_ref[...]) + bglu_ref[...])
    proj[0] = _bdot(s5o.astype(BF16), wbr_ref[0])
    mix = None
    for kk in (1, 2, 0):
        term = _sigmoid(gate_ref[0, :, kk * d:(kk + 1) * d].astype(F32)) * proj[kk]
        mix = term if mix is None else mix + term
    y = _bdot(mix.astype(BF16), wout_ref[...])
    mod = lambda r: mod_ref[0, 0, r:r + 1, :]
    x1 = _layer_norm(alpha * x_ref[0] + mod(2) * y) * g1_ref[...] + b1_ref[...]
    x1_ref[0] = x1
    h2 = (_layer_norm(x1) * (1.0 + mod(4)) + mod(3)).astype(BF16)
    h2_ref[0] = h2
    lg_ref[0] = _dot_nt(wr_ref[...], h2)


def _merge_call(xall, s5y, mla_o, win_o, gates, mod, wglu, bglu, wbr, wout, g1, b1, wr_t, n_ctx_tiles, alpha):
    b, n, d = xall.shape
    tm = TOKEN_TILE
    tok = lambda w: pl.BlockSpec((1, tm, w), lambda i, t: (i, t, 0))
    full = lambda a: pl.BlockSpec(a.shape, lambda i, t: (0,) * a.ndim)
    return pl.pallas_call(
        functools.partial(_merge_kernel, alpha=alpha),
        grid=(b, n // tm),
        in_specs=[tok(d), tok(BRANCH_WIDTH), tok(BRANCH_WIDTH), tok(BRANCH_WIDTH), tok(N_BRANCH * d),
                  pl.BlockSpec((1, 1, 6, d), lambda i, t: (i, jnp.where(t < n_ctx_tiles, 0, 1), 0, 0)),
                  full(wglu), full(bglu), full(wbr), full(wout), full(g1), full(b1), full(wr_t)],
        out_specs=[tok(d), tok(d), pl.BlockSpec((1, N_EXPERTS, tm), lambda i, t: (i, 0, t))],
        out_shape=[jax.ShapeDtypeStruct((b, n, d), F32), jax.ShapeDtypeStruct((b, n, d), BF16),
                   jax.ShapeDtypeStruct((b, N_EXPERTS, n), F32)],
        compiler_params=_cparams(2),
    )(xall, s5y, mla_o, win_o, gates, mod, wglu, bglu, wbr, wout, g1, b1, wr_t)


def _excl_cumsum_lanes(m):
    rows, n = m.shape
    r = lax.broadcasted_iota(jnp.int32, (128, 128), 0)
    c = lax.broadcasted_iota(jnp.int32, (128, 128), 1)
    tri = jnp.where(r < c, 1.0, 0.0).astype(BF16)
    off = jnp.zeros((rows, 1), F32)
    outs, offs = [], []
    for jb in range(n // 128):
        blk = m[:, jb * 128:(jb + 1) * 128]
        offs.append(off)
        outs.append(_bdot(blk.astype(BF16), tri) + off)
        off = off + jnp.sum(blk, axis=1, keepdims=True)
    return jnp.concatenate(outs, axis=1), offs + [off]


def _topk_slots(affs, caps):
    bits = [pltpu.bitcast(aff, jnp.int32) for aff in affs]

    def body(i, thrs):
        out = []
        for b, cap, thr in zip(bits, caps, thrs):
            cand = thr | (jnp.int32(1) << (30 - i))
            cnt = jnp.sum(jnp.where(b >= cand, 1.0, 0.0), axis=1, keepdims=True)
            out.append(jnp.where(cnt >= cap, cand, thr))
        return tuple(out)

    zero = jnp.zeros((affs[0].shape[0], 1), jnp.int32)
    thrs = lax.fori_loop(0, 31, body, tuple(zero for _ in affs))
    results = []
    for b, cap, thr in zip(bits, caps, thrs):
        gt = jnp.where(b > thr, 1.0, 0.0)
        eq = jnp.where(b == thr, 1.0, 0.0)
        need = cap - jnp.sum(gt, axis=1, keepdims=True)
        sel = gt + eq * jnp.where(_excl_cumsum_lanes(eq)[0] < need, 1.0, 0.0)
        rank, offs = _excl_cumsum_lanes(sel)
        results.append((jnp.where(sel > 0.5, rank, -1.0).astype(jnp.int32), offs[::MOE_TILE // 128]))
    return results


def _route_kernel(lg_ref, slot_ref, aff_ref, bnd_ref, *, n_ctx, cap_ctx, cap_lat):
    lg = lg_ref[0]
    m = jnp.max(lg, axis=0, keepdims=True)
    ex = jnp.exp(lg - m)
    aff = ex / jnp.sum(ex, axis=0, keepdims=True)
    aff_ref[0] = aff
    (slots_ctx, _), (slots, counts) = _topk_slots([aff[:, :n_ctx], aff[:, n_ctx:]], [cap_ctx, cap_lat])
    slot_ref[0, :, :n_ctx] = slots_ctx
    slot_ref[0, :, n_ctx:] = slots
    lane = lax.broadcasted_iota(jnp.int32, bnd_ref.shape[1:], 1)
    bnd = jnp.zeros(bnd_ref.shape[1:], F32)
    for k, cnt in enumerate(counts):
        bnd = jnp.where(lane == k, cnt, bnd)
    bnd_ref[0] = bnd.astype(jnp.int32)


def _route_call(logits_t, n_ctx, cap_ctx, cap_lat):
    b, e, n = logits_t.shape
    assert (n - n_ctx) % MOE_TILE == 0 and (n - n_ctx) // MOE_TILE < MOE_BOUNDS
    spec = pl.BlockSpec((1, e, n), lambda i: (i, 0, 0))
    return pl.pallas_call(
        functools.partial(_route_kernel, n_ctx=n_ctx, cap_ctx=cap_ctx, cap_lat=cap_lat),
        grid=(b,),
        in_specs=[spec],
        out_specs=[spec, spec, pl.BlockSpec((1, e, MOE_BOUNDS), lambda i: (i, 0, 0))],
        out_shape=[jax.ShapeDtypeStruct((b, e, n), jnp.int32), jax.ShapeDtypeStruct((b, e, n), F32),
                   jax.ShapeDtypeStruct((b, e, MOE_BOUNDS), jnp.int32)],
        compiler_params=_cparams(1),
    )(logits_t)


def _gather_kernel(bnd_ref, slot_ref, aff_ref, h_ref, xs_ref, gate_ref, xl_ref, gl_ref, *, n_ctx, cap_ctx):
    cap_lat = xs_ref.shape[2] - cap_ctx
    slot = slot_ref[0, 0]
    aff = aff_ref[0, 0]
    n = h_ref.shape[1]
    ib, ie = pl.program_id(0), pl.program_id(1)
    xl_ref[...] = jnp.zeros(xl_ref.shape, F32)
    gl_ref[...] = jnp.zeros(gl_ref.shape, F32)
    tiles = range((n - n_ctx) // MOE_TILE)
    firsts = [(bnd_ref[ib, ie, kt] // 16) * 16 for kt in tiles]

    def window(kt, start):
        tok = slice(n_ctx + kt * MOE_TILE, n_ctx + (kt + 1) * MOE_TILE)
        rows = start + lax.broadcasted_iota(jnp.int32, (MOE_WINDOW, MOE_TILE), 0)
        hit = slot[:, tok] == rows
        picked = jnp.sum(jnp.where(hit, aff[:, tok], 0.0), axis=1, keepdims=True)
        return _bdot(jnp.where(hit, 1.0, 0.0).astype(BF16), h_ref[0, tok, :]), picked

    def add_window(kt, start, parts=None):
        start = pl.multiple_of(start, 16)
        rows, picked = parts if parts is not None else window(kt, start)
        xl_ref[pl.ds(start, MOE_WINDOW), :] += rows
        gl_ref[pl.ds(start, MOE_WINDOW), :] += jnp.broadcast_to(picked, (MOE_WINDOW, 128))

    first_windows = [window(kt, firsts[kt]) for kt in tiles]
    for kt in tiles:
        add_window(kt, firsts[kt], first_windows[kt])
    for kt in tiles:
        def more(w, carry, kt=kt):
            add_window(kt, firsts[kt] + (w + 1) * MOE_WINDOW)
            return carry

        n_win = (bnd_ref[ib, ie, kt + 1] - firsts[kt] + MOE_WINDOW - 1) // MOE_WINDOW
        lax.fori_loop(0, jnp.maximum(n_win - 1, 0), more, 0)
    iota = lax.broadcasted_iota(jnp.int32, (cap_ctx, n_ctx), 0)
    hit = slot[:, :n_ctx] == iota
    xc = _bdot(jnp.where(hit, 1.0, 0.0).astype(BF16), h_ref[0, :n_ctx, :])
    gc = jnp.sum(jnp.where(hit, aff[:, :n_ctx], 0.0), axis=1, keepdims=True)
    xs_ref[0, 0, :cap_lat] = xl_ref[:cap_lat].astype(BF16)
    xs_ref[0, 0, cap_lat:] = xc.astype(BF16)
    gate_ref[0, 0, :cap_lat] = gl_ref[:cap_lat]
    gate_ref[0, 0, cap_lat:] = jnp.broadcast_to(gc, (cap_ctx, 128))


def _gather_call(bounds, slot, aff, h2, n_ctx, cap_ctx, cap_lat):
    b, e, n = slot.shape
    d = h2.shape[-1]
    cap = cap_lat + cap_ctx
    row = pl.BlockSpec((1, 1, 1, n), lambda ib, ie: (ib, ie, 0, 0))
    return pl.pallas_call(
        functools.partial(_gather_kernel, n_ctx=n_ctx, cap_ctx=cap_ctx),
        grid=(b, e),
        in_specs=[pl.BlockSpec(memory_space=pltpu.SMEM), row, row, pl.BlockSpec((1, n, d), lambda ib, ie: (ib, 0, 0))],
        out_specs=[pl.BlockSpec((1, 1, cap, d), lambda ib, ie: (ib, ie, 0, 0)),
                   pl.BlockSpec((1, 1, cap, 128), lambda ib, ie: (ib, ie, 0, 0))],
        out_shape=[jax.ShapeDtypeStruct((b, e, cap, d), BF16), jax.ShapeDtypeStruct((b, e, cap, 128), F32)],
        scratch_shapes=[pltpu.VMEM((cap_lat + MOE_WINDOW, d), F32), pltpu.VMEM((cap_lat + MOE_WINDOW, 128), F32)],
        compiler_params=_cparams(2),
    )(bounds, slot.reshape(b, e, 1, n), aff.reshape(b, e, 1, n), h2)


def _ffn_kernel(xs_ref, gate_ref, wg_ref, wu_ref, wd_ref, yl_ref, yc_ref, wg_s, wu_s, wd_s):
    ns, _, cap, d = xs_ref.shape
    cap_ctx = yc_ref.shape[2]
    cap_lat = cap - cap_ctx

    @pl.when(pl.program_id(1) == 0)
    def _():
        wg_s[...] = wg_ref[0, 0].astype(BF16)
        wu_s[...] = wu_ref[0, 0].astype(BF16)
        wd_s[...] = wd_ref[0, 0].astype(BF16)

    au = [(_bdot(xs_ref[s, 0], wg_s[...]), _bdot(xs_ref[s, 0], wu_s[...])) for s in range(ns)]
    for s, (a, u) in enumerate(au):
        hm = (a * _sigmoid(a) * u).astype(BF16)
        y = _bdot(hm, wd_s[...]) * gate_ref[s, 0, :, 0:1]
        yl_ref[s, 0, :cap_lat] = y[:cap_lat].astype(yl_ref.dtype)
        yl_ref[s, 0, cap_lat:] = jnp.zeros((MOE_WINDOW, d), yl_ref.dtype)
        yc_ref[s, 0] = y[cap_lat:].astype(yc_ref.dtype)


def _ffn_call(layer, xs, gate, w_gate, w_up, w_down, cap_ctx):
    b, e, cap, d = xs.shape
    f = w_gate.shape[-1]
    ns = FFN_SAMPLES
    rows_lat = cap - cap_ctx + MOE_WINDOW
    tok = lambda rows, w: pl.BlockSpec((ns, 1, rows, w), lambda ie, j: (j, ie, 0, 0))
    return pl.pallas_call(
        _ffn_kernel,
        grid=(e, b // ns),
        in_specs=[tok(cap, d), tok(cap, 128),
                  pl.BlockSpec((1, 1, d, f), lambda ie, j: (layer, ie, 0, 0)),
                  pl.BlockSpec((1, 1, d, f), lambda ie, j: (layer, ie, 0, 0)),
                  pl.BlockSpec((1, 1, f, d), lambda ie, j: (layer, ie, 0, 0))],
        out_specs=[tok(rows_lat, d), tok(cap_ctx, d)],
        out_shape=[jax.ShapeDtypeStruct((b, e, rows_lat, d), BF16), jax.ShapeDtypeStruct((b, e, cap_ctx, d), BF16)],
        scratch_shapes=[pltpu.VMEM((d, f), BF16), pltpu.VMEM((d, f), BF16), pltpu.VMEM((f, d), BF16)],
        compiler_params=_cparams(2),
    )(xs, gate, w_gate, w_up, w_down)


def _combine_kernel(bnd_ref, slot_ref, yl_ref, yc_ref, x1_ref, mod_ref, g2_ref, b2_ref, o_ref, fl_ref,
                    *, n_ctx_tiles, alpha):
    ib, t = pl.program_id(0), pl.program_id(1)
    tm = x1_ref.shape[1]
    slot = slot_ref[0]
    win = MOE_WINDOW

    def finish(fl):
        x1 = x1_ref[0]
        o_ref[0] = _layer_norm(alpha * x1 + mod_ref[0, 0, 5:6, :] * fl) * g2_ref[...] + b2_ref[...]

    def onehot(e, first, width):
        iota = lax.broadcasted_iota(jnp.int32, (tm, width), 1)
        return jnp.where(slot[:, e:e + 1] - first == iota, 1.0, 0.0).astype(BF16)

    def ctx_path():
        cap = yc_ref.shape[2]
        fl = None
        for e in range(N_EXPERTS):
            term = _bdot(onehot(e, 0, cap), yc_ref[0, e])
            fl = term if fl is None else fl + term
        finish(fl)

    def lat_path():
        kt = t - n_ctx_tiles
        firsts = [pl.multiple_of((bnd_ref[ib, e, kt] // 16) * 16, 16) for e in range(N_EXPERTS)]
        lane = lax.broadcasted_iota(jnp.int32, (tm, 2 * win), 1)
        pieces, ywins = [], []
        for e in range(0, N_EXPERTS, 2):
            rel = jnp.where(lane < win, slot[:, e:e + 1] - firsts[e], slot[:, e + 1:e + 2] - firsts[e + 1] + win)
            pieces.append(jnp.where(rel == lane, 1.0, 0.0).astype(BF16))
            ywins += [yl_ref[0, e, pl.ds(firsts[e], win), :], yl_ref[0, e + 1, pl.ds(firsts[e + 1], win), :]]
        fl_ref[...] = _bdot(jnp.concatenate(pieces, axis=1), jnp.concatenate(ywins, axis=0))
        for e in range(N_EXPERTS):
            def window(w, carry, e=e):
                first = pl.multiple_of(firsts[e] + (w + 1) * win, 16)
                fl_ref[...] += _bdot(onehot(e, first, win), yl_ref[0, e, pl.ds(first, win), :])
                return carry

            n_win = (bnd_ref[ib, e, kt + 1] - firsts[e] + win - 1) // win
            lax.fori_loop(0, jnp.maximum(n_win - 1, 0), window, 0)
        finish(fl_ref[...])

    pl.when(t < n_ctx_tiles)(ctx_path)
    pl.when(t >= n_ctx_tiles)(lat_path)


def _combine_call(bounds, slot_t, yl, yc, x1, mod, g2, b2, n_ctx_tiles, alpha, latent_only):
    b, n, d = x1.shape
    tm = MOE_TILE
    e = N_EXPERTS
    full = lambda a: pl.BlockSpec(a.shape, lambda i, t: (0,) * a.ndim)
    skip = n_ctx_tiles if latent_only else 0
    return pl.pallas_call(
        functools.partial(_combine_kernel, n_ctx_tiles=n_ctx_tiles, alpha=alpha),
        grid=(b, n // tm),
        scratch_shapes=[pltpu.VMEM((tm, d), F32)],
        in_specs=[pl.BlockSpec(memory_space=pltpu.SMEM),
                  pl.BlockSpec((1, tm, e), lambda i, t: (i, t, 0)),
                  pl.BlockSpec((1,) + yl.shape[1:], lambda i, t: (i, 0, 0, 0)),
                  pl.BlockSpec((1,) + yc.shape[1:], lambda i, t: (i, 0, 0, 0)),
                  pl.BlockSpec((1, tm, d), lambda i, t: (i, t, 0)),
                  pl.BlockSpec((1, 1, 6, d), lambda i, t: (i, jnp.where(t < n_ctx_tiles, 0, 1), 0, 0)),
                  full(g2), full(b2)],
        out_specs=pl.BlockSpec((1, tm, d), lambda i, t: (i, jnp.maximum(t - skip, 0), 0)),
        out_shape=jax.ShapeDtypeStruct((b, n - skip * tm, d), F32),
        compiler_params=_cparams(2),
    )(bounds, slot_t, yl, yc, x1, mod, g2, b2)


def _rope_tables(n_ctx, seq, head_dim, lane_offset):
    half = head_dim // 2
    nf = head_dim // 4
    t = jnp.arange(seq, dtype=F32)
    row = jnp.floor(t / GRID_W)
    col = t - row * GRID_W
    freqs = ROPE_BASE ** (-jnp.arange(nf, dtype=F32) / nf)
    ang = jnp.concatenate([row[:, None] * freqs, col[:, None] * freqs], axis=-1)
    cos, sin = jnp.cos(ang), jnp.sin(ang)
    zeros = jnp.zeros_like(sin)
    n_heads = (128 - lane_offset) // head_dim if lane_offset == 0 else 1
    c = jnp.concatenate([jnp.ones((seq, lane_offset), F32)] + [cos, cos] * n_heads, axis=-1)
    sa = jnp.concatenate([jnp.zeros((seq, lane_offset), F32)] + [-sin, zeros] * n_heads, axis=-1)
    sb = jnp.concatenate([jnp.zeros((seq, lane_offset), F32)] + [zeros, sin] * n_heads, axis=-1)
    pad = 128 - c.shape[1]
    c = jnp.pad(c, ((n_ctx, 0), (0, pad)), constant_values=1.0)
    sa = jnp.pad(sa, ((n_ctx, 0), (0, pad)))
    sb = jnp.pad(sb, ((n_ctx, 0), (0, pad)))
    return c, sa, sb


def _layer_weights(i, p):
    d = p['w_in'].shape[1]
    pts = np.cumsum((S5_WIDTH, MLA_Q_RANK, MLA_KV_RANK, MLA_ROPE, WIN_Q_HEADS * WIN_HEAD_DIM,
                     WIN_KV_HEADS * WIN_HEAD_DIM, WIN_KV_HEADS * WIN_HEAD_DIM))
    cols = jnp.split(p['w_in'][i], [int(v) for v in pts], axis=1)
    kr = jnp.pad(cols[3], ((0, 0), (MLA_NOPE, MLA_PAD - MLA_NOPE - MLA_ROPE)))
    w_cat = jnp.concatenate([cols[0], cols[1], cols[2], kr, cols[4], cols[5], cols[6], cols[7]], axis=1)
    dq = MLA_NOPE + MLA_ROPE
    wq = p['mla_w_uq'][i].reshape(MLA_Q_RANK, MLA_HEADS, dq)
    wq = jnp.pad(wq, ((0, 0), (0, 0), (0, MLA_PAD - dq))).reshape(MLA_Q_RANK, MLA_HEADS * MLA_PAD)
    wkv = p['mla_w_ukv'][i].reshape(MLA_KV_RANK, MLA_HEADS, MLA_NOPE + MLA_V)
    wk = jnp.pad(wkv[:, :, :MLA_NOPE], ((0, 0), (0, 0), (0, MLA_PAD - MLA_NOPE)))
    wk = wk.reshape(MLA_KV_RANK, MLA_HEADS * MLA_PAD)
    wv = wkv[:, :, MLA_NOPE:].reshape(MLA_KV_RANK, MLA_HEADS * MLA_V)
    row = lambda a: a[i].astype(F32).reshape(1, -1)
    return dict(
        w_cat=w_cat.astype(BF16), wq=wq.astype(BF16), wk=wk.astype(BF16), wv=wv.astype(BF16),
        qg=row(p['mla_q_norm']), kvg=row(p['mla_kv_norm']),
        wglu=p['s5_w_glu'][i].astype(BF16), bglu=row(p['s5_b_glu']),
        sink=p['win_sink'][i].astype(F32),
        wbr=p['w_branch'][i].astype(BF16), wout=p['w_out'][i].astype(BF16),
        g1=row(p['ln1_g']), b1=row(p['ln1_b']), g2=row(p['ln2_g']), b2=row(p['ln2_b']),
        wr_t=p['w_router'][i].T.astype(BF16),
    )


def _forward(p):
    x, c, ctx, c_ctx = p['x'], p['c'], p['ctx'], p['c_ctx']
    b, seq, d = x.shape
    n_ctx = ctx.shape[1]
    depth = p['w_ada'].shape[0]
    assert b == 8 and seq % TOKEN_TILE == 0 and n_ctx % TOKEN_TILE == 0 and seq % GRID_W == 0
    alpha = float((2 * depth) ** 0.25)
    n_ctx_tiles = n_ctx // TOKEN_TILE
    cap_lat = CAPACITY_FACTOR * seq // N_EXPERTS
    cap_ctx = CAPACITY_FACTOR * n_ctx // N_EXPERTS

    cond = jnp.concatenate([c, c_ctx[None], jnp.zeros((16 - b - 1, d), F32)], axis=0)
    mods = _ada_call(cond, p['w_ada'], p['b_ada'])
    mods = mods.reshape(depth, 16, 6, d)
    tabs_mla = _rope_tables(n_ctx, seq, MLA_ROPE, MLA_NOPE)
    tabs_win = _rope_tables(n_ctx, seq, WIN_HEAD_DIM, 0)
    s5w = _s5_param_call(p)

    xall = jnp.concatenate([ctx, x], axis=1)
    for i in range(depth):
        w = _layer_weights(i, p)
        mod = jnp.stack([jnp.broadcast_to(mods[i, b], (b, 6, d)), mods[i, :b]], axis=1)
        u, wq, wk, wv, gates, qp, qr, kk, vv = _in_call(xall, mod, w['w_cat'], w['qg'], w['kvg'], w['wq'], w['wk'],
                                                        w['wv'], tabs_mla, n_ctx)
        s5y = _s5_call(i, u, *s5w, p['s5_d'], n_ctx)
        mla_o = _mla_attn_call(qp, qr, kk, vv, n_ctx)
        win_o = _win_call(w['sink'], wq, wk, wv, tabs_win, n_ctx)
        x1, h2, logits_t = _merge_call(xall, s5y, mla_o, win_o, gates, mod, w['wglu'], w['bglu'], w['wbr'],
                                       w['wout'], w['g1'], w['b1'], w['wr_t'], n_ctx_tiles, alpha)
        slot, aff, bounds = _route_call(logits_t, n_ctx, cap_ctx, cap_lat)
        xs, gate = _gather_call(bounds, slot, aff, h2, n_ctx, cap_ctx, cap_lat)
        yl, yc = _ffn_call(i, xs, gate, p['w_gate'], p['w_up'], p['w_down'], cap_ctx)
        slot_t = jnp.swapaxes(slot, 1, 2)
        xall = _combine_call(bounds, slot_t, yl, yc, x1, mod, w['g2'], w['b2'], n_ctx_tiles, alpha,
                             latent_only=(i == depth - 1))
    return xall


def kernel(x, c, ctx, c_ctx, w_ada, b_ada, w_in, s5_lam_re, s5_lam_im, s5_log_dt, s5_b_re, s5_b_im, s5_c_re, s5_c_im, s5_d, s5_w_glu, s5_b_glu, mla_q_norm, mla_w_uq, mla_kv_norm, mla_w_ukv, win_sink, w_branch, w_out, ln1_g, ln1_b, ln2_g, ln2_b, w_router, w_gate, w_up, w_down):
    return _forward(dict(
        x=x, c=c, ctx=ctx, c_ctx=c_ctx, w_ada=w_ada, b_ada=b_ada, w_in=w_in, s5_lam_re=s5_lam_re,
        s5_lam_im=s5_lam_im, s5_log_dt=s5_log_dt, s5_b_re=s5_b_re, s5_b_im=s5_b_im, s5_c_re=s5_c_re,
        s5_c_im=s5_c_im, s5_d=s5_d, s5_w_glu=s5_w_glu, s5_b_glu=s5_b_glu, mla_q_norm=mla_q_norm,
        mla_w_uq=mla_w_uq, mla_kv_norm=mla_kv_norm, mla_w_ukv=mla_w_ukv, win_sink=win_sink, w_branch=w_branch,
        w_out=w_out, ln1_g=ln1_g, ln1_b=ln1_b, ln2_g=ln2_g, ln2_b=ln2_b, w_router=w_router, w_gate=w_gate,
        w_up=w_up, w_down=w_down))
```

```python
import functools
import math

import jax
import jax.numpy as jnp
import numpy as np
from jax import lax
from jax.experimental import pallas as pl
from jax.experimental.pallas import tpu as pltpu

F32 = jnp.float32
BF16 = jnp.bfloat16
HIGHEST = lax.Precision.HIGHEST

GRID_W = 64
S5_WIDTH = 512
S5_GROUP = 16
S5_GROUPS = S5_WIDTH // S5_GROUP
S5_STATE = 64
S5_CHUNK = 16
S5_GROUPS_PER_STEP = 4
MLA_HEADS = 8
MLA_NOPE = 64
MLA_ROPE = 32
MLA_V = 64
MLA_Q_RANK = 384
MLA_KV_RANK = 256
MLA_PAD = 128
MLA_HEADS_PER_STEP = 8
MLA_SCALE = (MLA_NOPE + MLA_ROPE) ** -0.5
WIN_Q_HEADS = 8
WIN_KV_HEADS = 2
WIN_GROUP = WIN_Q_HEADS // WIN_KV_HEADS
WIN_HEAD_DIM = 64
WINDOW = 128
BLOCK = 128
WIN_SCALE = WIN_HEAD_DIM ** -0.5
N_BRANCH = 3
BRANCH_WIDTH = 512
N_EXPERTS = 16
CAPACITY_FACTOR = 2
ROPE_BASE = 10000.0
LN_EPS = 1e-6
NEG_INF = -1e30
LOG2E = math.log2(math.e)
TOKEN_TILE = 256
IN_TILE = 384
MOE_TILE = 256
MOE_WINDOW = 64
MOE_BOUNDS = 16
FFN_SAMPLES = 2
VMEM_LIMIT = 56 * 1024 * 1024


def _cparams(n_axes):
    return pltpu.CompilerParams(dimension_semantics=("arbitrary",) * n_axes, vmem_limit_bytes=VMEM_LIMIT)


def _bdot(a, b):
    return jnp.dot(a, b, preferred_element_type=F32)


def _dot_nt(a, b):
    return lax.dot_general(a, b, (((1,), (1,)), ((), ())), preferred_element_type=F32)


def _layer_norm(x):
    mu = jnp.mean(x, axis=-1, keepdims=True)
    xc = x - mu
    var = jnp.mean(xc * xc, axis=-1, keepdims=True)
    return xc * lax.rsqrt(var + LN_EPS)


def _ada_kernel(cond_ref, w_ref, b_ref, o_ref):
    s = cond_ref[...]
    s = s * jax.nn.sigmoid(s)
    o_ref[0] = jnp.dot(s, w_ref[0], precision=HIGHEST, preferred_element_type=F32) + b_ref[0]


def _ada_call(cond, w_ada, b_ada):
    depth, d, d6 = w_ada.shape
    tn = 1536
    rows = cond.shape[0]
    return pl.pallas_call(
        _ada_kernel,
        grid=(depth, d6 // tn),
        in_specs=[
            pl.BlockSpec((rows, d), lambda i, j: (0, 0)),
            pl.BlockSpec((1, d, tn), lambda i, j: (i, 0, j)),
            pl.BlockSpec((1, 1, tn), lambda i, j: (i, 0, j)),
        ],
        out_specs=pl.BlockSpec((1, rows, tn), lambda i, j: (i, 0, j)),
        out_shape=jax.ShapeDtypeStruct((depth, rows, d6), F32),
        compiler_params=_cparams(2),
    )(cond, w_ada, b_ada.reshape(depth, 1, d6))


IN_WIDTHS = (S5_WIDTH, MLA_Q_RANK, MLA_KV_RANK, MLA_PAD, WIN_Q_HEADS * WIN_HEAD_DIM,
             WIN_KV_HEADS * WIN_HEAD_DIM, WIN_KV_HEADS * WIN_HEAD_DIM)
IN_OFFSETS = tuple(int(v) for v in np.cumsum((0,) + IN_WIDTHS))


def _mod_rows(mod_ref, r, tile, n_ctx):
    row = pl.program_id(1) * tile + lax.broadcasted_iota(jnp.int32, (tile, 1), 0)
    return jnp.where(row < n_ctx, mod_ref[0, 0, r:r + 1, :], mod_ref[0, 1, r:r + 1, :])


def _rope_lanes(x, cos, sa, sb, shift):
    return x * cos + pltpu.roll(x, 128 - shift, 1) * sa + pltpu.roll(x, shift, 1) * sb


def _mla_project(qa, kva, kr, qg, kvg, wq_ref, wk_ref, wv_ref, cos, sa, sb, qp_ref, qr_ref, k_ref, v_ref):
    def rms(x, gain):
        return (x * lax.rsqrt(jnp.mean(x * x, axis=-1, keepdims=True) + LN_EPS) * gain).astype(BF16)

    qn = rms(qa, qg)
    kvn = rms(kva, kvg)
    q = _bdot(qn, wq_ref[...]) * (MLA_SCALE * LOG2E)
    k = _bdot(kvn, wk_ref[...])
    v_ref[0] = _bdot(kvn, wv_ref[...]).astype(BF16)
    kr_rot = _rope_lanes(kr, cos, sa, sb, MLA_ROPE // 2)
    qp_ref[0] = q.astype(BF16)
    for h in range(MLA_HEADS):
        sl = slice(h * MLA_PAD, (h + 1) * MLA_PAD)
        qr_ref[0, :, sl] = _rope_lanes(q[:, sl], cos, sa, sb, MLA_ROPE // 2).astype(BF16)
        k_ref[0, :, sl] = (k[:, sl] + kr_rot).astype(BF16)


def _in_kernel(x_ref, mod_ref, w_ref, qg_ref, kvg_ref, wuq_ref, wuk_ref, wuv_ref, cos_ref, sa_ref, sb_ref,
               u_ref, wq_ref, wk_ref, wv_ref, gate_ref, qp_ref, qr_ref, k_ref, v_ref, *, n_ctx):
    tile = x_ref.shape[1]
    xn = _layer_norm(x_ref[0])
    h = (xn * (1.0 + _mod_rows(mod_ref, 1, tile, n_ctx)) + _mod_rows(mod_ref, 0, tile, n_ctx)).astype(BF16)
    widths = IN_WIDTHS + (gate_ref.shape[-1],)
    proj = lambda i: _bdot(h, w_ref[:, IN_OFFSETS[i]:IN_OFFSETS[i] + widths[i]])
    _mla_project(proj(1), proj(2), proj(3), qg_ref[...], kvg_ref[...], wuq_ref, wuk_ref, wuv_ref,
                 cos_ref[...], sa_ref[...], sb_ref[...], qp_ref, qr_ref, k_ref, v_ref)
    u_ref[0] = proj(0)
    wq_ref[0], wk_ref[0], wv_ref[0] = proj(4), proj(5), proj(6)
    gate_ref[0] = proj(7).astype(gate_ref.dtype)


_sigmoid = jax.nn.sigmoid


def _in_call(xall, mod, w_cat, qg, kvg, wuq, wuk, wuv, tabs, n_ctx):
    b, n, d = xall.shape
    tm = IN_TILE
    gate_w = w_cat.shape[1] - IN_OFFSETS[-1]
    hw = MLA_HEADS * MLA_PAD
    tok = lambda w: pl.BlockSpec((1, tm, w), lambda i, t: (i, t, 0))
    full = lambda a: pl.BlockSpec(a.shape, lambda i, t: (0,) * a.ndim)
    tab = pl.BlockSpec((tm, 128), lambda i, t: (t, 0))
    out_widths = (IN_WIDTHS[0],) + IN_WIDTHS[4:7] + (gate_w, hw, hw, hw, MLA_HEADS * MLA_V)
    out_dtypes = (F32,) * 4 + (BF16,) * 5
    return pl.pallas_call(
        functools.partial(_in_kernel, n_ctx=n_ctx),
        grid=(b, n // tm),
        in_specs=[tok(d), pl.BlockSpec((1, 2, 6, d), lambda i, t: (i, 0, 0, 0)),
                  pl.BlockSpec(w_cat.shape, lambda i, t: (0, 0), pipeline_mode=pl.Buffered(1)),
                  full(qg), full(kvg), full(wuq), full(wuk), full(wuv), tab, tab, tab],
        out_specs=[tok(w) for w in out_widths],
        out_shape=[jax.ShapeDtypeStruct((b, n, w), dt) for w, dt in zip(out_widths, out_dtypes)],
        compiler_params=_cparams(2),
    )(xall, mod, w_cat, qg, kvg, wuq, wuk, wuv, *tabs)


S5_LANE_GROUPS = 128 // S5_GROUP
S5_SCAN_GROUPS = 4
S5_PITCH_PAD = 8


def _s5_param_kernel(*refs):
    for g in range(refs[0].shape[2]):
        _s5_param_group(g, *refs)


def _s5_param_group(g, cre_ref, cim_ref, bre_ref, bim_ref, pr_ref, pi_ref, tz_ref, bc_ref, cc_ref, coef_ref):
    t = S5_CHUNK
    w = t * S5_GROUP
    nt = (((1,), (1,)), ((), ()))
    tz = None
    bcs, ccs, coefs = [], [], []
    for d in range(2):
        cre, cim = cre_ref[0, d, g], cim_ref[0, d, g]
        bre, bim = bre_ref[0, d, g], bim_ref[0, d, g]
        power = lambda k: (pr_ref[0, d, g, k:k + 1, :], pi_ref[0, d, g, k:k + 1, :])
        rt = []
        for k in range(t + 1):
            prk, pik = power(k)
            rt.append(jnp.concatenate([cre * prk - cim * pik, -(cre * pik + cim * prk)], axis=1))
        bt = jnp.concatenate([bre, bim], axis=1)
        zeros = jnp.zeros((S5_GROUP, w), F32)
        if d == 0:
            kt = lax.dot_general(bt, jnp.concatenate(rt[:t], axis=0), nt, precision=HIGHEST,
                                 preferred_element_type=F32)
            pad = jnp.concatenate([zeros, kt], axis=1)
            rows = [kt] + [pltpu.roll(pad, S5_GROUP * s, 1)[:, w:] for s in range(1, t)]
        else:
            kt = lax.dot_general(bt, jnp.concatenate(rt[t - 1::-1], axis=0), nt, precision=HIGHEST,
                                 preferred_element_type=F32)
            pad = jnp.concatenate([kt, zeros], axis=1)
            rows = [pltpu.roll(pad, 2 * w - S5_GROUP * (t - 1 - s), 1)[:, :w] for s in range(t - 1)] + [kt]
        tz_d = jnp.concatenate(rows, axis=0)
        tz = tz_d if tz is None else tz + tz_d
        bc_rows = []
        for s in range(t):
            prk, pik = power(t - 1 - s if d == 0 else s)
            br = bre * prk - bim * pik
            bi = bim * prk + bre * pik
            bc_rows.append(jnp.concatenate([br, bi, bi, br], axis=1))
        bcs.append(jnp.concatenate(bc_rows, axis=0))
        ccs.append(jnp.concatenate(rt[1:] if d == 0 else rt[t:0:-1], axis=0))
        er, ei = power(t)
        coefs += [jnp.concatenate([er, er], axis=1), jnp.concatenate([-ei, ei], axis=1),
                  jnp.concatenate([ei, -ei], axis=1)]
    tz_ref[0, g] = tz.astype(BF16)
    bc_ref[0, g] = jnp.concatenate(bcs, axis=1).astype(BF16)
    cc_ref[0, g] = jnp.concatenate(ccs, axis=1).astype(BF16)
    coef_ref[0, g] = jnp.concatenate(coefs + [jnp.zeros((2, 2 * S5_STATE), F32)], axis=0)


def _s5_param_call(p):
    t = S5_CHUNK
    f = lambda name: p[name].astype(F32)
    lam_re, lam_im = f('s5_lam_re'), f('s5_lam_im')
    depth = lam_re.shape[0]
    dt = jnp.exp(f('s5_log_dt'))[..., None]
    k = jnp.arange(t + 1, dtype=F32)[:, None]
    mag = jnp.exp((lam_re * dt)[..., None, :] * k)
    ang = (lam_im * dt)[..., None, :] * k
    pr, pi = mag * jnp.cos(ang), mag * jnp.sin(ang)
    ar, ai = pr[..., 1, :], pi[..., 1, :]
    den = lam_re * lam_re + lam_im * lam_im
    qr = (((ar - 1) * lam_re + ai * lam_im) / den)[..., None, :]
    qi = ((ai * lam_re - (ar - 1) * lam_im) / den)[..., None, :]
    b_re = jnp.swapaxes(f('s5_b_re'), -1, -2)
    b_im = jnp.swapaxes(f('s5_b_im'), -1, -2)
    bbr = qr * b_re - qi * b_im
    bbi = qr * b_im + qi * b_re
    g, hg, ps = S5_GROUPS, S5_GROUP, S5_STATE
    w = t * hg
    gs = S5_LANE_GROUPS
    small = lambda rows: pl.BlockSpec((1, 2, gs, rows, ps), lambda i, j: (i, 0, j, 0, 0))
    out = lambda cols: pl.BlockSpec((1, gs, w, cols), lambda i, j: (i, j, 0, 0))
    return pl.pallas_call(
        _s5_param_kernel,
        grid=(depth, g // gs),
        in_specs=[small(hg)] * 4 + [small(t + 1)] * 2,
        out_specs=[out(w), out(2 * w), out(w), pl.BlockSpec((1, gs, 8, 2 * ps), lambda i, j: (i, j, 0, 0))],
        out_shape=[jax.ShapeDtypeStruct((depth, g, w, w), BF16), jax.ShapeDtypeStruct((depth, g, w, 2 * w), BF16),
                   jax.ShapeDtypeStruct((depth, g, w, w), BF16), jax.ShapeDtypeStruct((depth, g, 8, 2 * ps), F32)],
        compiler_params=_cparams(2),
    )(f('s5_c_re'), f('s5_c_im'), bbr, bbi, pr, pi)


def _s5_kernel(u_ref, tz_ref, bc_ref, cc_ref, coef_ref, d_ref, y_ref, uy_ref, loc_ref, sp_ref, slab_ref,
               *, n_ctx, n_batch):
    ph, b = pl.program_id(1), pl.program_id(2)
    t, hg, ng = S5_CHUNK, S5_GROUP, S5_LANE_GROUPS
    ncc = n_ctx // t
    ncl = (u_ref.shape[1] - n_ctx) // t
    nc = ncc + ncl
    pitch = nc + S5_PITCH_PAD

    def to_chunk_rows(slabs):
        tr = [s.T for s in slabs]
        return [jnp.concatenate([x[g * hg:(g + 1) * hg] for x in tr], axis=0).T for g in range(ng)]

    def to_token_slabs(rows):
        tr = [r.T for r in rows]
        return [jnp.concatenate([x[tau * hg:(tau + 1) * hg] for x in tr], axis=0).T for tau in range(t)]

    base = pl.multiple_of(b * pitch, 8)
    cbase = pl.multiple_of(b * ncc, 8)

    @pl.when(ph == 0)
    def _():
        rows = to_chunk_rows([u_ref[0, pl.ds(n_ctx + tau, ncl, stride=t), :] for tau in range(t)])
        for g in range(ng):
            uy_ref[g, pl.ds(base + ncc, ncl), :] = rows[g]
            uy_ref[g, pl.ds(base + nc, S5_PITCH_PAD), :] = jnp.zeros((S5_PITCH_PAD, t * hg), F32)
        for tau in range(t):
            slab_ref[tau, pl.ds(cbase, ncc), :] = u_ref[0, pl.ds(tau, ncc, stride=t), :]

    @pl.when((ph == 1) & (b == 0))
    def _():
        rows = to_chunk_rows([slab_ref[tau] for tau in range(t)])
        for g in range(ng):
            for s in range(n_batch):
                uy_ref[g, s * pitch:s * pitch + ncc, :] = rows[g][s * ncc:(s + 1) * ncc]
        for part in range(ng // S5_SCAN_GROUPS):
            gs = [part * S5_SCAN_GROUPS + gl for gl in range(S5_SCAN_GROUPS)]
            for gl, g in enumerate(gs):
                ub = uy_ref[g].astype(BF16)
                loc = _bdot(ub, bc_ref[0, g])
                for q in range(4):
                    loc_ref[gl, q] = loc[:, q * 128:(q + 1) * 128]
                uy_ref[g] = _bdot(ub, tz_ref[0, g])
                for s in range(n_batch):
                    for d in range(2):
                        sp_ref[gl, d, s * pitch + nc:(s + 1) * pitch, :] = jnp.zeros((S5_PITCH_PAD, 128), F32)

            def coef(g, r):
                return jnp.broadcast_to(coef_ref[0, g, r:r + 1, :], (n_batch, 128))

            def step(i, carry):
                cb = jnp.where(i < ncc, ncc - 1 - i, nc + ncc - 1 - i)
                fwd = pl.ds(i, n_batch, stride=pitch)
                bwd = pl.ds(cb, n_batch, stride=pitch)
                out = []
                for gl, g in enumerate(gs):
                    v0f, v1f, v0b, v1b = carry[gl]
                    sp_ref[gl, 0, fwd, :] = v0f
                    sp_ref[gl, 1, bwd, :] = v0b
                    n0f = coef(g, 0) * v0f + coef(g, 1) * v1f + loc_ref[gl, 0, fwd, :]
                    n1f = coef(g, 0) * v1f + coef(g, 2) * v0f + loc_ref[gl, 1, fwd, :]
                    n0b = coef(g, 3) * v0b + coef(g, 4) * v1b + loc_ref[gl, 2, bwd, :]
                    n1b = coef(g, 3) * v1b + coef(g, 5) * v0b + loc_ref[gl, 3, bwd, :]
                    out.append((n0f, n1f, n0b, n1b))
                return tuple(out)

            z = jnp.zeros((n_batch, 128), F32)
            lax.fori_loop(0, nc, step, tuple((z, z, z, z) for _ in gs))
            for gl, g in enumerate(gs):
                sp = jnp.concatenate([sp_ref[gl, 0], sp_ref[gl, 1]], axis=1).astype(BF16)
                uy_ref[g] = uy_ref[g] + _dot_nt(sp, cc_ref[0, g])
        rows = [jnp.concatenate([uy_ref[g, s * pitch:s * pitch + ncc, :] for s in range(n_batch)], axis=0)
                for g in range(ng)]
        for tau, slab in enumerate(to_token_slabs(rows)):
            slab_ref[tau] = slab

    @pl.when(ph == 1)
    def _():
        slabs = to_token_slabs([uy_ref[g, pl.ds(base + ncc, ncl), :] for g in range(ng)])
        for tau in range(t):
            y_ref[0, pl.ds(n_ctx + tau, ncl, stride=t), :] = slabs[tau]
            y_ref[0, pl.ds(tau, ncc, stride=t), :] = slab_ref[tau, pl.ds(cbase, ncc), :]
        y_ref[0] = y_ref[0] + d_ref[0] * u_ref[0]


def _s5_call(layer, u, tz, bc, cc, coef, s5_d, n_ctx):
    b, n, width = u.shape
    t, ng = S5_CHUNK, S5_LANE_GROUPS
    rows = b * (n // t + S5_PITCH_PAD)
    assert (n - n_ctx) // t == 128 and b * (n_ctx // t) == 128
    wspec = lambda a: pl.BlockSpec((1, ng) + a.shape[2:], lambda g, ph, i: (layer, g, 0, 0))
    return pl.pallas_call(
        functools.partial(_s5_kernel, n_ctx=n_ctx, n_batch=b),
        grid=(width // 128, 2, b),
        in_specs=[pl.BlockSpec((1, n, 128), lambda g, ph, i: (i, 0, g)),
                  wspec(tz), wspec(bc), wspec(cc), wspec(coef),
                  pl.BlockSpec((1, 1, 128), lambda g, ph, i: (layer, 0, g))],
        out_specs=pl.BlockSpec((1, n, 128), lambda g, ph, i: (i * ph, 0, g)),
        out_shape=jax.ShapeDtypeStruct((b, n, width), F32),
        scratch_shapes=[pltpu.VMEM((ng, rows, t * S5_GROUP), F32),
                        pltpu.VMEM((S5_SCAN_GROUPS, 4, rows, 128), F32),
                        pltpu.VMEM((S5_SCAN_GROUPS, 2, rows, 128), F32),
                        pltpu.VMEM((t, 128, 128), F32)],
        compiler_params=_cparams(3),
    )(u, tz, bc, cc, coef, s5_d.astype(F32).reshape(s5_d.shape[0], 1, width))


def _lane_chunks(xs):
    return [x[:, i * 128:(i + 1) * 128] for x in xs for i in range(x.shape[1] // 128)]


def _row_max(scores, floor=None):
    mm = functools.reduce(jnp.maximum, _lane_chunks(scores))
    if floor is not None:
        mm = jnp.maximum(mm, floor)
    return jnp.max(mm, axis=-1, keepdims=True)


def _softmax_av(scores, values, sink=None):
    m = _row_max(scores, sink)
    ps = [jnp.exp2(s - m) for s in scores]
    ll = functools.reduce(jnp.add, _lane_chunks(ps))
    if sink is not None:
        lane = lax.broadcasted_iota(jnp.int32, sink.shape, 1)
        ll = ll + jnp.where(lane == 0, jnp.exp2(sink - m), 0.0)
    l = jnp.sum(ll, axis=-1, keepdims=True)
    o = functools.reduce(jnp.add, [_bdot(p.astype(BF16), v) for p, v in zip(ps, values)])
    return o / l


def _mla_attn_kernel(qp_ref, qr_ref, k_ref, v_ref, o_ref, *, n_ctx, n_ctx_tiles):
    heads = qp_ref.shape[-1] // MLA_PAD
    t = pl.program_id(2)
    group = 256 // MLA_V
    lane = lax.broadcasted_iota(jnp.int32, (1, 256), 1)

    def run(latent):
        all_scores = []
        for h in range(heads):
            sl = slice(h * MLA_PAD, (h + 1) * MLA_PAD)
            scores = [_dot_nt(qp_ref[0, :, sl], k_ref[0, :n_ctx, sl])]
            if latent:
                scores.append(_dot_nt(qr_ref[0, :, sl], k_ref[0, n_ctx:, sl]))
            all_scores.append(scores)
        probs = []
        for scores in all_scores:
            m = _row_max(scores)
            ps = [jnp.exp2(s - m) for s in scores]
            l = jnp.sum(functools.reduce(jnp.add, _lane_chunks(ps)), axis=-1, keepdims=True)
            probs.append(([p.astype(BF16) for p in ps], l))
        acc = [None] * (heads // group)
        for h, (ps, l) in enumerate(probs):
            blk, hh = divmod(h, group)
            cols = slice(blk * 256, (blk + 1) * 256)
            own = (lane >= hh * MLA_V) & (lane < (hh + 1) * MLA_V)
            zero = jnp.zeros((), BF16)
            values = [jnp.where(own, v_ref[0, :n_ctx, cols], zero)]
            if latent:
                values.append(jnp.where(own, v_ref[0, n_ctx:, cols], zero))
            o = functools.reduce(jnp.add, [_bdot(p, v) for p, v in zip(ps, values)]) / l
            acc[blk] = o if acc[blk] is None else acc[blk] + o
        for blk, o in enumerate(acc):
            o_ref[0, :, blk * 256:(blk + 1) * 256] = o.astype(o_ref.dtype)

    pl.when(t < n_ctx_tiles)(lambda: run(False))
    pl.when(t >= n_ctx_tiles)(lambda: run(True))


def _mla_attn_call(qp, qr, k, v, n_ctx):
    b, n, _ = qp.shape
    tq = TOKEN_TILE
    hp = MLA_HEADS_PER_STEP
    qspec = pl.BlockSpec((1, tq, hp * MLA_PAD), lambda i, h, t: (i, t, h))
    return pl.pallas_call(
        functools.partial(_mla_attn_kernel, n_ctx=n_ctx, n_ctx_tiles=n_ctx // tq),
        grid=(b, MLA_HEADS // hp, n // tq),
        in_specs=[qspec, qspec,
                  pl.BlockSpec((1, n, hp * MLA_PAD), lambda i, h, t: (i, 0, h)),
                  pl.BlockSpec((1, n, hp * MLA_V), lambda i, h, t: (i, 0, h))],
        out_specs=pl.BlockSpec((1, tq, hp * MLA_V), lambda i, h, t: (i, t, h)),
        out_shape=jax.ShapeDtypeStruct((b, n, MLA_HEADS * MLA_V), BF16),
        compiler_params=_cparams(3),
    )(qp, qr, k, v)


def _win_kernel(sink_ref, q_ref, k_ref, v_ref, cos_ref, sa_ref, sb_ref, o_ref, *, n_ctx_blocks, n_blocks):
    j = pl.program_id(1)
    hd = WIN_HEAD_DIM
    half = hd // 2
    lane = lax.broadcasted_iota(jnp.int32, (1, 128), 1)
    lo = jnp.where(lane < hd, 1.0, 0.0)
    hi = 1.0 - lo
    upper_rows = lax.broadcasted_iota(jnp.int32, (2 * BLOCK, 128), 0) < BLOCK

    def lane_halves(x):
        xr = pltpu.roll(x, hd, 1)
        return {(0, 0): (x * lo).astype(BF16), (0, 1): (xr * hi).astype(BF16),
                (1, 0): (xr * lo).astype(BF16), (1, 1): (x * hi).astype(BF16)}

    def attend(queries, keys, values, masks):
        all_scores = {}
        for kh in range(WIN_KV_HEADS):
            stacked = [jnp.concatenate([qs[:, (2 * kh) * 128:(2 * kh + 1) * 128],
                                        qs[:, (2 * kh + 1) * 128:(2 * kh + 2) * 128]], axis=0).astype(BF16)
                       for qs in queries]
            for par in range(2):
                scores = []
                for qst, ks, msk in zip(stacked, keys, masks):
                    s = _dot_nt(qst, ks[(kh, par)])
                    scores.append(s if msk is None else jnp.where(msk, s, NEG_INF))
                all_scores[(kh, par)] = scores
        for kh in range(WIN_KV_HEADS):
            acc = None
            for par in range(2):
                sink = jnp.where(upper_rows, sink_ref[4 * kh + par], sink_ref[4 * kh + 2 + par]) * LOG2E
                o = _softmax_av(all_scores[(kh, par)], [vs[(kh, par)] for vs in values], sink)
                acc = o if acc is None else acc + o
            o_ref[0, :, (2 * kh) * 128:(2 * kh + 1) * 128] = acc[:BLOCK].astype(o_ref.dtype)
            o_ref[0, :, (2 * kh + 1) * 128:(2 * kh + 2) * 128] = acc[BLOCK:].astype(o_ref.dtype)

    q = q_ref[0] * (WIN_SCALE * LOG2E)
    n_ctx = n_ctx_blocks * BLOCK
    kctx = lane_halves(k_ref[0, :n_ctx, :])
    vctx = lane_halves(v_ref[0, :n_ctx, :])

    def ctx_path():
        attend([q], [kctx], [vctx], [None])

    def lat_path():
        blk = j - n_ctx_blocks
        band = [pl.ds(pl.multiple_of((n_ctx_blocks + jnp.clip(blk + d, 0, n_blocks - 1)) * BLOCK, BLOCK), BLOCK)
                for d in (-1, 0, 1)]
        rope = lambda x, rows: _rope_lanes(x, cos_ref[rows, :], sa_ref[rows, :], sb_ref[rows, :], half)
        q_rot = jnp.concatenate([rope(q[:, c * 128:(c + 1) * 128], band[1]) for c in range(q.shape[1] // 128)],
                                axis=-1)
        kband = jnp.concatenate([rope(k_ref[0, rows, :], rows) for rows in band], axis=0)
        vband = jnp.concatenate([v_ref[0, rows, :] for rows in band], axis=0)
        r = lax.broadcasted_iota(jnp.int32, (2 * BLOCK, 3 * BLOCK), 0) % BLOCK
        c = lax.broadcasted_iota(jnp.int32, (2 * BLOCK, 3 * BLOCK), 1)
        first = jnp.where(blk > 0, 0, BLOCK)
        last = jnp.where(blk < n_blocks - 1, 3 * BLOCK, 2 * BLOCK)
        valid = (jnp.abs(c - BLOCK - r) <= WINDOW) & (c >= first) & (c < last)
        attend([q_rot, q], [lane_halves(kband), kctx], [lane_halves(vband), vctx], [valid, None])

    pl.when(j < n_ctx_blocks)(ctx_path)
    pl.when(j >= n_ctx_blocks)(lat_path)


def _win_call(sink, wq, wk, wv, tabs, n_ctx):
    b, n, _ = wq.shape
    ncb = n_ctx // BLOCK
    nb = n // BLOCK
    kvw = WIN_KV_HEADS * WIN_HEAD_DIM
    cur = lambda i, j: (i, j, 0)
    kv = pl.BlockSpec((1, n, kvw), lambda i, j: (i, 0, 0))
    tab = pl.BlockSpec((n, 128), lambda i, j: (0, 0))
    return pl.pallas_call(
        functools.partial(_win_kernel, n_ctx_blocks=ncb, n_blocks=nb - ncb),
        grid=(b, nb),
        in_specs=[pl.BlockSpec(memory_space=pltpu.SMEM),
                  pl.BlockSpec((1, BLOCK, WIN_Q_HEADS * WIN_HEAD_DIM), cur), kv, kv, tab, tab, tab],
        out_specs=pl.BlockSpec((1, BLOCK, WIN_Q_HEADS * WIN_HEAD_DIM), cur),
        out_shape=jax.ShapeDtypeStruct((b, n, WIN_Q_HEADS * WIN_HEAD_DIM), BF16),
        compiler_params=_cparams(2),
    )(sink, wq, wk, wv, *tabs)


def _merge_kernel(x_ref, s5_ref, mla_ref, win_ref, gate_ref, mod_ref, wglu_ref, bglu_ref, wbr_ref, wout_ref,
                  g1_ref, b1_ref, wr_ref, x1_ref, h2_ref, lg_ref, *, alpha):
    d = x_ref.shape[-1]
    proj = {1: _bdot(mla_ref[0], wbr_ref[1]), 2: _bdot(win_ref[0], wbr_ref[2])}
    g = jax.nn.gelu(s5_ref[0])
    s5o = g * _sigmoid(_bdot(g.astype(BF16), wglu_ref[...]) + bglu_ref[...])
    proj[0] = _bdot(s5o.astype(BF16), wbr_ref[0])
    mix = None
    for kk in (1, 2, 0):
        term = _sigmoid(gate_ref[0, :, kk * d:(kk + 1) * d].astype(F32)) * proj[kk]
        mix = term if mix is None else mix + term
    y = _bdot(mix.astype(BF16), wout_ref[...])
    mod = lambda r: mod_ref[0, 0, r:r + 1, :]
    x1 = _layer_norm(alpha * x_ref[0] + mod(2) * y) * g1_ref[...] + b1_ref[...]
    x1_ref[0] = x1
    h2 = (_layer_norm(x1) * (1.0 + mod(4)) + mod(3)).astype(BF16)
    h2_ref[0] = h2
    lg_ref[0] = _dot_nt(wr_ref[...], h2)


def _merge_call(xall, s5y, mla_o, win_o, gates, mod, wglu, bglu, wbr, wout, g1, b1, wr_t, n_ctx_tiles, alpha):
    b, n, d = xall.shape
    tm = TOKEN_TILE
    tok = lambda w: pl.BlockSpec((1, tm, w), lambda i, t: (i, t, 0))
    full = lambda a: pl.BlockSpec(a.shape, lambda i, t: (0,) * a.ndim)
    return pl.pallas_call(
        functools.partial(_merge_kernel, alpha=alpha),
        grid=(b, n // tm),
        in_specs=[tok(d), tok(BRANCH_WIDTH), tok(BRANCH_WIDTH), tok(BRANCH_WIDTH), tok(N_BRANCH * d),
                  pl.BlockSpec((1, 1, 6, d), lambda i, t: (i, jnp.where(t < n_ctx_tiles, 0, 1), 0, 0)),
                  full(wglu), full(bglu), full(wbr), full(wout), full(g1), full(b1), full(wr_t)],
        out_specs=[tok(d), tok(d), pl.BlockSpec((1, N_EXPERTS, tm), lambda i, t: (i, 0, t))],
        out_shape=[jax.ShapeDtypeStruct((b, n, d), F32), jax.ShapeDtypeStruct((b, n, d), BF16),
                   jax.ShapeDtypeStruct((b, N_EXPERTS, n), F32)],
        compiler_params=_cparams(2),
    )(xall, s5y, mla_o, win_o, gates, mod, wglu, bglu, wbr, wout, g1, b1, wr_t)


def _excl_cumsum_lanes(m):
    rows, n = m.shape
    r = lax.broadcasted_iota(jnp.int32, (128, 128), 0)
    c = lax.broadcasted_iota(jnp.int32, (128, 128), 1)
    tri = jnp.where(r < c, 1.0, 0.0).astype(BF16)
    off = jnp.zeros((rows, 1), F32)
    outs, offs = [], []
    for jb in range(n // 128):
        blk = m[:, jb * 128:(jb + 1) * 128]
        offs.append(off)
        outs.append(_bdot(blk.astype(BF16), tri) + off)
        off = off + jnp.sum(blk, axis=1, keepdims=True)
    return jnp.concatenate(outs, axis=1), offs + [off]


def _topk_slots(affs, caps):
    bits = [pltpu.bitcast(aff, jnp.int32) for aff in affs]

    def body(i, thrs):
        out = []
        for b, cap, thr in zip(bits, caps, thrs):
            cand = thr | (jnp.int32(1) << (30 - i))
            cnt = jnp.sum(jnp.where(b >= cand, 1.0, 0.0), axis=1, keepdims=True)
            out.append(jnp.where(cnt >= cap, cand, thr))
        return tuple(out)

    zero = jnp.zeros((affs[0].shape[0], 1), jnp.int32)
    thrs = lax.fori_loop(0, 31, body, tuple(zero for _ in affs))
    results = []
    for b, cap, thr in zip(bits, caps, thrs):
        gt = jnp.where(b > thr, 1.0, 0.0)
        eq = jnp.where(b == thr, 1.0, 0.0)
        need = cap - jnp.sum(gt, axis=1, keepdims=True)
        sel = gt + eq * jnp.where(_excl_cumsum_lanes(eq)[0] < need, 1.0, 0.0)
        rank, offs = _excl_cumsum_lanes(sel)
        results.append((jnp.where(sel > 0.5, rank, -1.0).astype(jnp.int32), offs[::MOE_TILE // 128]))
    return results


def _route_kernel(lg_ref, slot_ref, aff_ref, bnd_ref, *, n_ctx, cap_ctx, cap_lat):
    lg = lg_ref[0]
    m = jnp.max(lg, axis=0, keepdims=True)
    ex = jnp.exp(lg - m)
    aff = ex / jnp.sum(ex, axis=0, keepdims=True)
    aff_ref[0] = aff
    (slots_ctx, _), (slots, counts) = _topk_slots([aff[:, :n_ctx], aff[:, n_ctx:]], [cap_ctx, cap_lat])
    slot_ref[0, :, :n_ctx] = slots_ctx
    slot_ref[0, :, n_ctx:] = slots
    lane = lax.broadcasted_iota(jnp.int32, bnd_ref.shape[1:], 1)
    bnd = jnp.zeros(bnd_ref.shape[1:], F32)
    for k, cnt in enumerate(counts):
        bnd = jnp.where(lane == k, cnt, bnd)
    bnd_ref[0] = bnd.astype(jnp.int32)


def _route_call(logits_t, n_ctx, cap_ctx, cap_lat):
    b, e, n = logits_t.shape
    assert (n - n_ctx) % MOE_TILE == 0 and (n - n_ctx) // MOE_TILE < MOE_BOUNDS
    spec = pl.BlockSpec((1, e, n), lambda i: (i, 0, 0))
    return pl.pallas_call(
        functools.partial(_route_kernel, n_ctx=n_ctx, cap_ctx=cap_ctx, cap_lat=cap_lat),
        grid=(b,),
        in_specs=[spec],
        out_specs=[spec, spec, pl.BlockSpec((1, e, MOE_BOUNDS), lambda i: (i, 0, 0))],
        out_shape=[jax.ShapeDtypeStruct((b, e, n), jnp.int32), jax.ShapeDtypeStruct((b, e, n), F32),
                   jax.ShapeDtypeStruct((b, e, MOE_BOUNDS), jnp.int32)],
        compiler_params=_cparams(1),
    )(logits_t)


def _gather_kernel(bnd_ref, slot_ref, aff_ref, h_ref, xs_ref, gate_ref, xl_ref, gl_ref, *, n_ctx, cap_ctx):
    cap_lat = xs_ref.shape[2] - cap_ctx
    slot = slot_ref[0, 0]
    aff = aff_ref[0, 0]
    n = h_ref.shape[1]
    ib, ie = pl.program_id(0), pl.program_id(1)
    xl_ref[...] = jnp.zeros(xl_ref.shape, F32)
    gl_ref[...] = jnp.zeros(gl_ref.shape, F32)
    tiles = range((n - n_ctx) // MOE_TILE)
    firsts = [(bnd_ref[ib, ie, kt] // 16) * 16 for kt in tiles]

    def window(kt, start):
        tok = slice(n_ctx + kt * MOE_TILE, n_ctx + (kt + 1) * MOE_TILE)
        rows = start + lax.broadcasted_iota(jnp.int32, (MOE_WINDOW, MOE_TILE), 0)
        hit = slot[:, tok] == rows
        picked = jnp.sum(jnp.where(hit, aff[:, tok], 0.0), axis=1, keepdims=True)
        return _bdot(jnp.where(hit, 1.0, 0.0).astype(BF16), h_ref[0, tok, :]), picked

    def add_window(kt, start, parts=None):
        start = pl.multiple_of(start, 16)
        rows, picked = parts if parts is not None else window(kt, start)
        xl_ref[pl.ds(start, MOE_WINDOW), :] += rows
        gl_ref[pl.ds(start, MOE_WINDOW), :] += jnp.broadcast_to(picked, (MOE_WINDOW, 128))

    first_windows = [window(kt, firsts[kt]) for kt in tiles]
    for kt in tiles:
        add_window(kt, firsts[kt], first_windows[kt])
    for kt in tiles:
        def more(w, carry, kt=kt):
            add_window(kt, firsts[kt] + (w + 1) * MOE_WINDOW)
            return carry

        n_win = (bnd_ref[ib, ie, kt + 1] - firsts[kt] + MOE_WINDOW - 1) // MOE_WINDOW
        lax.fori_loop(0, jnp.maximum(n_win - 1, 0), more, 0)
    iota = lax.broadcasted_iota(jnp.int32, (cap_ctx, n_ctx), 0)
    hit = slot[:, :n_ctx] == iota
    xc = _bdot(jnp.where(hit, 1.0, 0.0).astype(BF16), h_ref[0, :n_ctx, :])
    gc = jnp.sum(jnp.where(hit, aff[:, :n_ctx], 0.0), axis=1, keepdims=True)
    xs_ref[0, 0, :cap_lat] = xl_ref[:cap_lat].astype(BF16)
    xs_ref[0, 0, cap_lat:] = xc.astype(BF16)
    gate_ref[0, 0, :cap_lat] = gl_ref[:cap_lat]
    gate_ref[0, 0, cap_lat:] = jnp.broadcast_to(gc, (cap_ctx, 128))


def _gather_call(bounds, slot, aff, h2, n_ctx, cap_ctx, cap_lat):
    b, e, n = slot.shape
    d = h2.shape[-1]
    cap = cap_lat + cap_ctx
    row = pl.BlockSpec((1, 1, 1, n), lambda ib, ie: (ib, ie, 0, 0))
    return pl.pallas_call(
        functools.partial(_gather_kernel, n_ctx=n_ctx, cap_ctx=cap_ctx),
        grid=(b, e),
        in_specs=[pl.BlockSpec(memory_space=pltpu.SMEM), row, row, pl.BlockSpec((1, n, d), lambda ib, ie: (ib, 0, 0))],
        out_specs=[pl.BlockSpec((1, 1, cap, d), lambda ib, ie: (ib, ie, 0, 0)),
                   pl.BlockSpec((1, 1, cap, 128), lambda ib, ie: (ib, ie, 0, 0))],
        out_shape=[jax.ShapeDtypeStruct((b, e, cap, d), BF16), jax.ShapeDtypeStruct((b, e, cap, 128), F32)],
        scratch_shapes=[pltpu.VMEM((cap_lat + MOE_WINDOW, d), F32), pltpu.VMEM((cap_lat + MOE_WINDOW, 128), F32)],
        compiler_params=_cparams(2),
    )(bounds, slot.reshape(b, e, 1, n), aff.reshape(b, e, 1, n), h2)


def _ffn_kernel(xs_ref, gate_ref, wg_ref, wu_ref, wd_ref, yl_ref, yc_ref, wg_s, wu_s, wd_s):
    ns, _, cap, d = xs_ref.shape
    cap_ctx = yc_ref.shape[2]
    cap_lat = cap - cap_ctx

    @pl.when(pl.program_id(1) == 0)
    def _():
        wg_s[...] = wg_ref[0, 0].astype(BF16)
        wu_s[...] = wu_ref[0, 0].astype(BF16)
        wd_s[...] = wd_ref[0, 0].astype(BF16)

    au = [(_bdot(xs_ref[s, 0], wg_s[...]), _bdot(xs_ref[s, 0], wu_s[...])) for s in range(ns)]
    for s, (a, u) in enumerate(au):
        hm = (a * _sigmoid(a) * u).astype(BF16)
        y = _bdot(hm, wd_s[...]) * gate_ref[s, 0, :, 0:1]
        yl_ref[s, 0, :cap_lat] = y[:cap_lat].astype(yl_ref.dtype)
        yl_ref[s, 0, cap_lat:] = jnp.zeros((MOE_WINDOW, d), yl_ref.dtype)
        yc_ref[s, 0] = y[cap_lat:].astype(yc_ref.dtype)


def _ffn_call(layer, xs, gate, w_gate, w_up, w_down, cap_ctx):
    b, e, cap, d = xs.shape
    f = w_gate.shape[-1]
    ns = FFN_SAMPLES
    rows_lat = cap - cap_ctx + MOE_WINDOW
    tok = lambda rows, w: pl.BlockSpec((ns, 1, rows, w), lambda ie, j: (j, ie, 0, 0))
    return pl.pallas_call(
        _ffn_kernel,
        grid=(e, b // ns),
        in_specs=[tok(cap, d), tok(cap, 128),
                  pl.BlockSpec((1, 1, d, f), lambda ie, j: (layer, ie, 0, 0)),
                  pl.BlockSpec((1, 1, d, f), lambda ie, j: (layer, ie, 0, 0)),
                  pl.BlockSpec((1, 1, f, d), lambda ie, j: (layer, ie, 0, 0))],
        out_specs=[tok(rows_lat, d), tok(cap_ctx, d)],
        out_shape=[jax.ShapeDtypeStruct((b, e, rows_lat, d), BF16), jax.ShapeDtypeStruct((b, e, cap_ctx, d), BF16)],
        scratch_shapes=[pltpu.VMEM((d, f), BF16), pltpu.VMEM((d, f), BF16), pltpu.VMEM((f, d), BF16)],
        compiler_params=_cparams(2),
    )(xs, gate, w_gate, w_up, w_down)


def _combine_kernel(bnd_ref, slot_ref, yl_ref, yc_ref, x1_ref, mod_ref, g2_ref, b2_ref, o_ref, fl_ref,
                    *, n_ctx_tiles, alpha):
    ib, t = pl.program_id(0), pl.program_id(1)
    tm = x1_ref.shape[1]
    slot = slot_ref[0]
    win = MOE_WINDOW

    def finish(fl):
        x1 = x1_ref[0]
        o_ref[0] = _layer_norm(alpha * x1 + mod_ref[0, 0, 5:6, :] * fl) * g2_ref[...] + b2_ref[...]

    def onehot(e, first, width):
        iota = lax.broadcasted_iota(jnp.int32, (tm, width), 1)
        return jnp.where(slot[:, e:e + 1] - first == iota, 1.0, 0.0).astype(BF16)

    def ctx_path():
        cap = yc_ref.shape[2]
        fl = None
        for e in range(N_EXPERTS):
            term = _bdot(onehot(e, 0, cap), yc_ref[0, e])
            fl = term if fl is None else fl + term
        finish(fl)

    def lat_path():
        kt = t - n_ctx_tiles
        firsts = [pl.multiple_of((bnd_ref[ib, e, kt] // 16) * 16, 16) for e in range(N_EXPERTS)]
        lane = lax.broadcasted_iota(jnp.int32, (tm, 2 * win), 1)
        pieces, ywins = [], []
        for e in range(0, N_EXPERTS, 2):
            rel = jnp.where(lane < win, slot[:, e:e + 1] - firsts[e], slot[:, e + 1:e + 2] - firsts[e + 1] + win)
            pieces.append(jnp.where(rel == lane, 1.0, 0.0).astype(BF16))
            ywins += [yl_ref[0, e, pl.ds(firsts[e], win), :], yl_ref[0, e + 1, pl.ds(firsts[e + 1], win), :]]
        fl_ref[...] = _bdot(jnp.concatenate(pieces, axis=1), jnp.concatenate(ywins, axis=0))
        for e in range(N_EXPERTS):
            def window(w, carry, e=e):
                first = pl.multiple_of(firsts[e] + (w + 1) * win, 16)
                fl_ref[...] += _bdot(onehot(e, first, win), yl_ref[0, e, pl.ds(first, win), :])
                return carry

            n_win = (bnd_ref[ib, e, kt + 1] - firsts[e] + win - 1) // win
            lax.fori_loop(0, jnp.maximum(n_win - 1, 0), window, 0)
        finish(fl_ref[...])

    pl.when(t < n_ctx_tiles)(ctx_path)
    pl.when(t >= n_ctx_tiles)(lat_path)


def _combine_call(bounds, slot_t, yl, yc, x1, mod, g2, b2, n_ctx_tiles, alpha, latent_only):
    b, n, d = x1.shape
    tm = MOE_TILE
    e = N_EXPERTS
    full = lambda a: pl.BlockSpec(a.shape, lambda i, t: (0,) * a.ndim)
    skip = n_ctx_tiles if latent_only else 0
    return pl.pallas_call(
        functools.partial(_combine_kernel, n_ctx_tiles=n_ctx_tiles, alpha=alpha),
        grid=(b, n // tm),
        scratch_shapes=[pltpu.VMEM((tm, d), F32)],
        in_specs=[pl.BlockSpec(memory_space=pltpu.SMEM),
                  pl.BlockSpec((1, tm, e), lambda i, t: (i, t, 0)),
                  pl.BlockSpec((1,) + yl.shape[1:], lambda i, t: (i, 0, 0, 0)),
                  pl.BlockSpec((1,) + yc.shape[1:], lambda i, t: (i, 0, 0, 0)),
                  pl.BlockSpec((1, tm, d), lambda i, t: (i, t, 0)),
                  pl.BlockSpec((1, 1, 6, d), lambda i, t: (i, jnp.where(t < n_ctx_tiles, 0, 1), 0, 0)),
                  full(g2), full(b2)],
        out_specs=pl.BlockSpec((1, tm, d), lambda i, t: (i, jnp.maximum(t - skip, 0), 0)),
        out_shape=jax.ShapeDtypeStruct((b, n - skip * tm, d), F32),
        compiler_params=_cparams(2),
    )(bounds, slot_t, yl, yc, x1, mod, g2, b2)


def _rope_tables(n_ctx, seq, head_dim, lane_offset):
    half = head_dim // 2
    nf = head_dim // 4
    t = jnp.arange(seq, dtype=F32)
    row = jnp.floor(t / GRID_W)
    col = t - row * GRID_W
    freqs = ROPE_BASE ** (-jnp.arange(nf, dtype=F32) / nf)
    ang = jnp.concatenate([row[:, None] * freqs, col[:, None] * freqs], axis=-1)
    cos, sin = jnp.cos(ang), jnp.sin(ang)
    zeros = jnp.zeros_like(sin)
    n_heads = (128 - lane_offset) // head_dim if lane_offset == 0 else 1
    c = jnp.concatenate([jnp.ones((seq, lane_offset), F32)] + [cos, cos] * n_heads, axis=-1)
    sa = jnp.concatenate([jnp.zeros((seq, lane_offset), F32)] + [-sin, zeros] * n_heads, axis=-1)
    sb = jnp.concatenate([jnp.zeros((seq, lane_offset), F32)] + [zeros, sin] * n_heads, axis=-1)
    pad = 128 - c.shape[1]
    c = jnp.pad(c, ((n_ctx, 0), (0, pad)), constant_values=1.0)
    sa = jnp.pad(sa, ((n_ctx, 0), (0, pad)))
    sb = jnp.pad(sb, ((n_ctx, 0), (0, pad)))
    return c, sa, sb


def _layer_weights(i, p):
    d = p['w_in'].shape[1]
    pts = np.cumsum((S5_WIDTH, MLA_Q_RANK, MLA_KV_RANK, MLA_ROPE, WIN_Q_HEADS * WIN_HEAD_DIM,
                     WIN_KV_HEADS * WIN_HEAD_DIM, WIN_KV_HEADS * WIN_HEAD_DIM))
    cols = jnp.split(p['w_in'][i], [int(v) for v in pts], axis=1)
    kr = jnp.pad(cols[3], ((0, 0), (MLA_NOPE, MLA_PAD - MLA_NOPE - MLA_ROPE)))
    w_cat = jnp.concatenate([cols[0], cols[1], cols[2], kr, cols[4], cols[5], cols[6], cols[7]], axis=1)
    dq = MLA_NOPE + MLA_ROPE
    wq = p['mla_w_uq'][i].reshape(MLA_Q_RANK, MLA_HEADS, dq)
    wq = jnp.pad(wq, ((0, 0), (0, 0), (0, MLA_PAD - dq))).reshape(MLA_Q_RANK, MLA_HEADS * MLA_PAD)
    wkv = p['mla_w_ukv'][i].reshape(MLA_KV_RANK, MLA_HEADS, MLA_NOPE + MLA_V)
    wk = jnp.pad(wkv[:, :, :MLA_NOPE], ((0, 0), (0, 0), (0, MLA_PAD - MLA_NOPE)))
    wk = wk.reshape(MLA_KV_RANK, MLA_HEADS * MLA_PAD)
    wv = wkv[:, :, MLA_NOPE:].reshape(MLA_KV_RANK, MLA_HEADS * MLA_V)
    row = lambda a: a[i].astype(F32).reshape(1, -1)
    return dict(
        w_cat=w_cat.astype(BF16), wq=wq.astype(BF16), wk=wk.astype(BF16), wv=wv.astype(BF16),
        qg=row(p['mla_q_norm']), kvg=row(p['mla_kv_norm']),
        wglu=p['s5_w_glu'][i].astype(BF16), bglu=row(p['s5_b_glu']),
        sink=p['win_sink'][i].astype(F32),
        wbr=p['w_branch'][i].astype(BF16), wout=p['w_out'][i].astype(BF16),
        g1=row(p['ln1_g']), b1=row(p['ln1_b']), g2=row(p['ln2_g']), b2=row(p['ln2_b']),
        wr_t=p['w_router'][i].T.astype(BF16),
    )


def _forward(p):
    x, c, ctx, c_ctx = p['x'], p['c'], p['ctx'], p['c_ctx']
    b, seq, d = x.shape
    n_ctx = ctx.shape[1]
    depth = p['w_ada'].shape[0]
    assert b == 8 and seq % TOKEN_TILE == 0 and n_ctx % TOKEN_TILE == 0 and seq % GRID_W == 0
    alpha = float((2 * depth) ** 0.25)
    n_ctx_tiles = n_ctx // TOKEN_TILE
    cap_lat = CAPACITY_FACTOR * seq // N_EXPERTS
    cap_ctx = CAPACITY_FACTOR * n_ctx // N_EXPERTS

    cond = jnp.concatenate([c, c_ctx[None], jnp.zeros((16 - b - 1, d), F32)], axis=0)
    mods = _ada_call(cond, p['w_ada'], p['b_ada'])
    mods = mods.reshape(depth, 16, 6, d)
    tabs_mla = _rope_tables(n_ctx, seq, MLA_ROPE, MLA_NOPE)
    tabs_win = _rope_tables(n_ctx, seq, WIN_HEAD_DIM, 0)
    s5w = _s5_param_call(p)

    xall = jnp.concatenate([ctx, x], axis=1)
    for i in range(depth):
        w = _layer_weights(i, p)
        mod = jnp.stack([jnp.broadcast_to(mods[i, b], (b, 6, d)), mods[i, :b]], axis=1)
        u, wq, wk, wv, gates, qp, qr, kk, vv = _in_call(xall, mod, w['w_cat'], w['qg'], w['kvg'], w['wq'], w['wk'],
                                                        w['wv'], tabs_mla, n_ctx)
        s5y = _s5_call(i, u, *s5w, p['s5_d'], n_ctx)
        mla_o = _mla_attn_call(qp, qr, kk, vv, n_ctx)
        win_o = _win_call(w['sink'], wq, wk, wv, tabs_win, n_ctx)
        x1, h2, logits_t = _merge_call(xall, s5y, mla_o, win_o, gates, mod, w['wglu'], w['bglu'], w['wbr'],
                                       w['wout'], w['g1'], w['b1'], w['wr_t'], n_ctx_tiles, alpha)
        slot, aff, bounds = _route_call(logits_t, n_ctx, cap_ctx, cap_lat)
        xs, gate = _gather_call(bounds, slot, aff, h2, n_ctx, cap_ctx, cap_lat)
        yl, yc = _ffn_call(i, xs, gate, p['w_gate'], p['w_up'], p['w_down'], cap_ctx)
        slot_t = jnp.swapaxes(slot, 1, 2)
        xall = _combine_call(bounds, slot_t, yl, yc, x1, mod, w['g2'], w['b2'], n_ctx_tiles, alpha,
                             latent_only=(i == depth - 1))
    return xall


def kernel(x, c, ctx, c_ctx, w_ada, b_ada, w_in, s5_lam_re, s5_lam_im, s5_log_dt, s5_b_re, s5_b_im, s5_c_re, s5_c_im, s5_d, s5_w_glu, s5_b_glu, mla_q_norm, mla_w_uq, mla_kv_norm, mla_w_ukv, win_sink, w_branch, w_out, ln1_g, ln1_b, ln2_g, ln2_b, w_router, w_gate, w_up, w_down):
    return _forward(dict(
        x=x, c=c, ctx=ctx, c_ctx=c_ctx, w_ada=w_ada, b_ada=b_ada, w_in=w_in, s5_lam_re=s5_lam_re,
        s5_lam_im=s5_lam_im, s5_log_dt=s5_log_dt, s5_b_re=s5_b_re, s5_b_im=s5_b_im, s5_c_re=s5_c_re,
        s5_c_im=s5_c_im, s5_d=s5_d, s5_w_glu=s5_w_glu, s5_b_glu=s5_b_glu, mla_q_norm=mla_q_norm,
        mla_w_uq=mla_w_uq, mla_kv_norm=mla_kv_norm, mla_w_ukv=mla_w_ukv, win_sink=win_sink, w_branch=w_branch,
        w_out=w_out, ln1_g=ln1_g, ln1_b=ln1_b, ln2_g=ln2_g, ln2_b=ln2_b, w_router=w_router, w_gate=w_gate,
        w_up=w_up, w_down=w_down))
```

```python
import functools
import math

import jax
import jax.numpy as jnp
import numpy as np
from jax import lax
from jax.experimental import pallas as pl
from jax.experimental.pallas import tpu as pltpu

F32 = jnp.float32
BF16 = jnp.bfloat16
HIGHEST = lax.Precision.HIGHEST

GRID_W = 64
S5_WIDTH = 512
S5_GROUP = 16
S5_GROUPS = S5_WIDTH // S5_GROUP
S5_STATE = 64
S5_CHUNK = 16
MLA_HEADS = 8
MLA_NOPE = 64
MLA_ROPE = 32
MLA_V = 64
MLA_Q_RANK = 384
MLA_KV_RANK = 256
MLA_PAD = 128
MLA_HEADS_PER_STEP = 8
MLA_SCALE = (MLA_NOPE + MLA_ROPE) ** -0.5
WIN_Q_HEADS = 8
WIN_KV_HEADS = 2
WIN_HEAD_DIM = 64
WINDOW = 128
BLOCK = 128
WIN_SCALE = WIN_HEAD_DIM ** -0.5
N_BRANCH = 3
BRANCH_WIDTH = 512
N_EXPERTS = 16
CAPACITY_FACTOR = 2
ROPE_BASE = 10000.0
LN_EPS = 1e-6
NEG_INF = -1e30
LOG2E = math.log2(math.e)
TOKEN_TILE = 256
IN_TILE = 384
MOE_TILE = 256
MOE_WINDOW = 64
MOE_BOUNDS = 16
FFN_SAMPLES = 2
GATHER_EXPERTS = 4
VMEM_LIMIT = 56 * 1024 * 1024


def _cparams(n_axes):
    return pltpu.CompilerParams(dimension_semantics=("arbitrary",) * n_axes, vmem_limit_bytes=VMEM_LIMIT)


def _bdot(a, b):
    return jnp.dot(a, b, preferred_element_type=F32)


def _dot_nt(a, b):
    return lax.dot_general(a, b, (((1,), (1,)), ((), ())), preferred_element_type=F32)


def _layer_norm(x):
    mu = jnp.mean(x, axis=-1, keepdims=True)
    xc = x - mu
    var = jnp.mean(xc * xc, axis=-1, keepdims=True)
    return xc * lax.rsqrt(var + LN_EPS)


def _ada_kernel(cond_ref, w_ref, b_ref, o_ref):
    s = cond_ref[...]
    s = s * jax.nn.sigmoid(s)
    o_ref[0] = jnp.dot(s, w_ref[0], precision=HIGHEST, preferred_element_type=F32) + b_ref[0]


def _ada_call(cond, w_ada, b_ada):
    depth, d, d6 = w_ada.shape
    tn = 1536
    rows = cond.shape[0]
    return pl.pallas_call(
        _ada_kernel,
        grid=(depth, d6 // tn),
        in_specs=[
            pl.BlockSpec((rows, d), lambda i, j: (0, 0)),
            pl.BlockSpec((1, d, tn), lambda i, j: (i, 0, j)),
            pl.BlockSpec((1, 1, tn), lambda i, j: (i, 0, j)),
        ],
        out_specs=pl.BlockSpec((1, rows, tn), lambda i, j: (i, 0, j)),
        out_shape=jax.ShapeDtypeStruct((depth, rows, d6), F32),
        compiler_params=_cparams(2),
    )(cond, w_ada, b_ada.reshape(depth, 1, d6))


IN_WIDTHS = (S5_WIDTH, MLA_Q_RANK, MLA_KV_RANK, MLA_PAD, WIN_Q_HEADS * WIN_HEAD_DIM,
             WIN_KV_HEADS * WIN_HEAD_DIM, WIN_KV_HEADS * WIN_HEAD_DIM)
IN_OFFSETS = tuple(int(v) for v in np.cumsum((0,) + IN_WIDTHS))


def _mod_rows(mod_ref, r, tile, n_ctx):
    row = pl.program_id(1) * tile + lax.broadcasted_iota(jnp.int32, (tile, 1), 0)
    return jnp.where(row < n_ctx, mod_ref[0, 0, r:r + 1, :], mod_ref[0, 1, r:r + 1, :])


def _rope_lanes(x, cos, sa, sb, shift):
    return x * cos + pltpu.roll(x, 128 - shift, 1) * sa + pltpu.roll(x, shift, 1) * sb


def _mla_project(qa, kva, kr, qg, kvg, wq_ref, wk_ref, wv_ref, cos, sa, sb, qp_ref, qr_ref, k_ref, v_ref):
    def rms(x, gain):
        return (x * lax.rsqrt(jnp.mean(x * x, axis=-1, keepdims=True) + LN_EPS) * gain).astype(BF16)

    qn = rms(qa, qg)
    kvn = rms(kva, kvg)
    q = _bdot(qn, wq_ref[...]) * (MLA_SCALE * LOG2E)
    k = _bdot(kvn, wk_ref[...])
    v_ref[0] = _bdot(kvn, wv_ref[...]).astype(BF16)
    kr_rot = _rope_lanes(kr, cos, sa, sb, MLA_ROPE // 2)
    qp_ref[0] = q.astype(BF16)
    for h in range(MLA_HEADS):
        sl = slice(h * MLA_PAD, (h + 1) * MLA_PAD)
        qr_ref[0, :, sl] = _rope_lanes(q[:, sl], cos, sa, sb, MLA_ROPE // 2).astype(BF16)
        k_ref[0, :, sl] = (k[:, sl] + kr_rot).astype(BF16)


def _in_kernel(x_ref, mod_ref, w_ref, qg_ref, kvg_ref, wuq_ref, wuk_ref, wuv_ref, cos_ref, sa_ref, sb_ref,
               u_ref, wq_ref, wk_ref, wv_ref, gate_ref, qp_ref, qr_ref, k_ref, v_ref, *, n_ctx):
    tile = x_ref.shape[1]
    xn = _layer_norm(x_ref[0])
    h = (xn * (1.0 + _mod_rows(mod_ref, 1, tile, n_ctx)) + _mod_rows(mod_ref, 0, tile, n_ctx)).astype(BF16)
    widths = IN_WIDTHS + (gate_ref.shape[-1],)
    proj = lambda i: _bdot(h, w_ref[:, IN_OFFSETS[i]:IN_OFFSETS[i] + widths[i]])
    _mla_project(proj(1), proj(2), proj(3), qg_ref[...], kvg_ref[...], wuq_ref, wuk_ref, wuv_ref,
                 cos_ref[...], sa_ref[...], sb_ref[...], qp_ref, qr_ref, k_ref, v_ref)
    u_ref[0] = proj(0)
    wq_ref[0], wk_ref[0], wv_ref[0] = proj(4), proj(5), proj(6)
    gate_ref[0] = proj(7).astype(gate_ref.dtype)


_sigmoid = jax.nn.sigmoid


def _in_call(xall, mod, w_cat, qg, kvg, wuq, wuk, wuv, tabs, n_ctx):
    b, n, d = xall.shape
    tm = IN_TILE
    gate_w = w_cat.shape[1] - IN_OFFSETS[-1]
    hw = MLA_HEADS * MLA_PAD
    tok = lambda w: pl.BlockSpec((1, tm, w), lambda i, t: (i, t, 0))
    full = lambda a: pl.BlockSpec(a.shape, lambda i, t: (0,) * a.ndim)
    tab = pl.BlockSpec((tm, 128), lambda i, t: (t, 0))
    out_widths = (IN_WIDTHS[0],) + IN_WIDTHS[4:7] + (gate_w, hw, hw, hw, MLA_HEADS * MLA_V)
    out_dtypes = (F32,) * 4 + (BF16,) * 5
    return pl.pallas_call(
        functools.partial(_in_kernel, n_ctx=n_ctx),
        grid=(b, n // tm),
        in_specs=[tok(d), pl.BlockSpec((1, 2, 6, d), lambda i, t: (i, 0, 0, 0)),
                  pl.BlockSpec(w_cat.shape, lambda i, t: (0, 0), pipeline_mode=pl.Buffered(1)),
                  full(qg), full(kvg), full(wuq), full(wuk), full(wuv), tab, tab, tab],
        out_specs=[tok(w) for w in out_widths],
        out_shape=[jax.ShapeDtypeStruct((b, n, w), dt) for w, dt in zip(out_widths, out_dtypes)],
        compiler_params=_cparams(2),
    )(xall, mod, w_cat, qg, kvg, wuq, wuk, wuv, *tabs)


S5_LANE_GROUPS = 128 // S5_GROUP
S5_SCAN_GROUPS = 4
S5_PITCH_PAD = 8


def _s5_param_kernel(*refs):
    for g in range(refs[0].shape[2]):
        _s5_param_group(g, *refs)


def _s5_param_group(g, cre_ref, cim_ref, bre_ref, bim_ref, pr_ref, pi_ref, tz_ref, bc_ref, cc_ref, coef_ref):
    t = S5_CHUNK
    w = t * S5_GROUP
    nt = (((1,), (1,)), ((), ()))
    tz = None
    bcs, ccs, coefs = [], [], []
    for d in range(2):
        cre, cim = cre_ref[0, d, g], cim_ref[0, d, g]
        bre, bim = bre_ref[0, d, g], bim_ref[0, d, g]
        power = lambda k: (pr_ref[0, d, g, k:k + 1, :], pi_ref[0, d, g, k:k + 1, :])
        rt = []
        for k in range(t + 1):
            prk, pik = power(k)
            rt.append(jnp.concatenate([cre * prk - cim * pik, -(cre * pik + cim * prk)], axis=1))
        bt = jnp.concatenate([bre, bim], axis=1)
        zeros = jnp.zeros((S5_GROUP, w), F32)
        if d == 0:
            kt = lax.dot_general(bt, jnp.concatenate(rt[:t], axis=0), nt, precision=HIGHEST,
                                 preferred_element_type=F32)
            pad = jnp.concatenate([zeros, kt], axis=1)
            rows = [kt] + [pltpu.roll(pad, S5_GROUP * s, 1)[:, w:] for s in range(1, t)]
        else:
            kt = lax.dot_general(bt, jnp.concatenate(rt[t - 1::-1], axis=0), nt, precision=HIGHEST,
                                 preferred_element_type=F32)
            pad = jnp.concatenate([kt, zeros], axis=1)
            rows = [pltpu.roll(pad, 2 * w - S5_GROUP * (t - 1 - s), 1)[:, :w] for s in range(t - 1)] + [kt]
        tz_d = jnp.concatenate(rows, axis=0)
        tz = tz_d if tz is None else tz + tz_d
        bc_rows = []
        for s in range(t):
            prk, pik = power(t - 1 - s if d == 0 else s)
            br = bre * prk - bim * pik
            bi = bim * prk + bre * pik
            bc_rows.append(jnp.concatenate([br, bi, bi, br], axis=1))
        bcs.append(jnp.concatenate(bc_rows, axis=0))
        ccs.append(jnp.concatenate(rt[1:] if d == 0 else rt[t:0:-1], axis=0))
        er, ei = power(t)
        coefs += [jnp.concatenate([er, er], axis=1), jnp.concatenate([-ei, ei], axis=1),
                  jnp.concatenate([ei, -ei], axis=1)]
    tz_ref[0, g] = tz.astype(BF16)
    bc_ref[0, g] = jnp.concatenate(bcs, axis=1).astype(BF16)
    cc_ref[0, g] = jnp.concatenate(ccs, axis=1).astype(BF16)
    coef_ref[0, g] = jnp.concatenate(coefs + [jnp.zeros((2, 2 * S5_STATE), F32)], axis=0)


def _s5_param_call(p):
    t = S5_CHUNK
    f = lambda name: p[name].astype(F32)
    lam_re, lam_im = f('s5_lam_re'), f('s5_lam_im')
    depth = lam_re.shape[0]
    dt = jnp.exp(f('s5_log_dt'))[..., None]
    k = jnp.arange(t + 1, dtype=F32)[:, None]
    mag = jnp.exp((lam_re * dt)[..., None, :] * k)
    ang = (lam_im * dt)[..., None, :] * k
    pr, pi = mag * jnp.cos(ang), mag * jnp.sin(ang)
    ar, ai = pr[..., 1, :], pi[..., 1, :]
    den = lam_re * lam_re + lam_im * lam_im
    qr = (((ar - 1) * lam_re + ai * lam_im) / den)[..., None, :]
    qi = ((ai * lam_re - (ar - 1) * lam_im) / den)[..., None, :]
    b_re = jnp.swapaxes(f('s5_b_re'), -1, -2)
    b_im = jnp.swapaxes(f('s5_b_im'), -1, -2)
    bbr = qr * b_re - qi * b_im
    bbi = qr * b_im + qi * b_re
    g, hg, ps = S5_GROUPS, S5_GROUP, S5_STATE
    w = t * hg
    gs = S5_LANE_GROUPS
    small = lambda rows: pl.BlockSpec((1, 2, gs, rows, ps), lambda i, j: (i, 0, j, 0, 0))
    out = lambda cols: pl.BlockSpec((1, gs, w, cols), lambda i, j: (i, j, 0, 0))
    return pl.pallas_call(
        _s5_param_kernel,
        grid=(depth, g // gs),
        in_specs=[small(hg)] * 4 + [small(t + 1)] * 2,
        out_specs=[out(w), out(2 * w), out(w), pl.BlockSpec((1, gs, 8, 2 * ps), lambda i, j: (i, j, 0, 0))],
        out_shape=[jax.ShapeDtypeStruct((depth, g, w, w), BF16), jax.ShapeDtypeStruct((depth, g, w, 2 * w), BF16),
                   jax.ShapeDtypeStruct((depth, g, w, w), BF16), jax.ShapeDtypeStruct((depth, g, 8, 2 * ps), F32)],
        compiler_params=_cparams(2),
    )(f('s5_c_re'), f('s5_c_im'), bbr, bbi, pr, pi)


def _s5_kernel(u_ref, tz_ref, bc_ref, cc_ref, coef_ref, d_ref, y_ref, uy_ref, loc_ref, sp_ref, slab_ref,
               *, n_ctx, n_batch):
    ph, b = pl.program_id(1), pl.program_id(2)
    t, hg, ng = S5_CHUNK, S5_GROUP, S5_LANE_GROUPS
    ncc = n_ctx // t
    ncl = (u_ref.shape[1] - n_ctx) // t
    nc = ncc + ncl
    pitch = nc + S5_PITCH_PAD

    def to_chunk_rows(slabs):
        tr = [s.T for s in slabs]
        return [jnp.concatenate([x[g * hg:(g + 1) * hg] for x in tr], axis=0).T for g in range(ng)]

    def to_token_slabs(rows):
        tr = [r.T for r in rows]
        return [jnp.concatenate([x[tau * hg:(tau + 1) * hg] for x in tr], axis=0).T for tau in range(t)]

    base = pl.multiple_of(b * pitch, 8)
    cbase = pl.multiple_of(b * ncc, 8)

    @pl.when(ph == 0)
    def _():
        rows = to_chunk_rows([u_ref[0, pl.ds(n_ctx + tau, ncl, stride=t), :] for tau in range(t)])
        for g in range(ng):
            uy_ref[g, pl.ds(base + ncc, ncl), :] = rows[g]
            uy_ref[g, pl.ds(base + nc, S5_PITCH_PAD), :] = jnp.zeros((S5_PITCH_PAD, t * hg), F32)
        for tau in range(t):
            slab_ref[tau, pl.ds(cbase, ncc), :] = u_ref[0, pl.ds(tau, ncc, stride=t), :]

    @pl.when((ph == 1) & (b == 0))
    def _():
        rows = to_chunk_rows([slab_ref[tau] for tau in range(t)])
        for g in range(ng):
            for s in range(n_batch):
                uy_ref[g, s * pitch:s * pitch + ncc, :] = rows[g][s * ncc:(s + 1) * ncc]
        for part in range(ng // S5_SCAN_GROUPS):
            gs = [part * S5_SCAN_GROUPS + gl for gl in range(S5_SCAN_GROUPS)]
            for gl, g in enumerate(gs):
                ub = uy_ref[g].astype(BF16)
                loc = _bdot(ub, bc_ref[0, g])
                for q in range(4):
                    loc_ref[gl, q] = loc[:, q * 128:(q + 1) * 128]
                uy_ref[g] = _bdot(ub, tz_ref[0, g])
                for s in range(n_batch):
                    for d in range(2):
                        sp_ref[gl, d, s * pitch + nc:(s + 1) * pitch, :] = jnp.zeros((S5_PITCH_PAD, 128), F32)

            def coef(g, r):
                return jnp.broadcast_to(coef_ref[0, g, r:r + 1, :], (n_batch, 128))

            def step(i, carry):
                cb = jnp.where(i < ncc, ncc - 1 - i, nc + ncc - 1 - i)
                fwd = pl.ds(i, n_batch, stride=pitch)
                bwd = pl.ds(cb, n_batch, stride=pitch)
                out = []
                for gl, g in enumerate(gs):
                    v0f, v1f, v0b, v1b = carry[gl]
                    sp_ref[gl, 0, fwd, :] = v0f
                    sp_ref[gl, 1, bwd, :] = v0b
                    n0f = coef(g, 0) * v0f + coef(g, 1) * v1f + loc_ref[gl, 0, fwd, :]
                    n1f = coef(g, 0) * v1f + coef(g, 2) * v0f + loc_ref[gl, 1, fwd, :]
                    n0b = coef(g, 3) * v0b + coef(g, 4) * v1b + loc_ref[gl, 2, bwd, :]
                    n1b = coef(g, 3) * v1b + coef(g, 5) * v0b + loc_ref[gl, 3, bwd, :]
                    out.append((n0f, n1f, n0b, n1b))
                return tuple(out)

            z = jnp.zeros((n_batch, 128), F32)
            lax.fori_loop(0, nc, step, tuple((z, z, z, z) for _ in gs))
            for gl, g in enumerate(gs):
                sp = jnp.concatenate([sp_ref[gl, 0], sp_ref[gl, 1]], axis=1).astype(BF16)
                uy_ref[g] = uy_ref[g] + _dot_nt(sp, cc_ref[0, g])
        rows = [jnp.concatenate([uy_ref[g, s * pitch:s * pitch + ncc, :] for s in range(n_batch)], axis=0)
                for g in range(ng)]
        for tau, slab in enumerate(to_token_slabs(rows)):
            slab_ref[tau] = slab

    @pl.when(ph == 1)
    def _():
        slabs = to_token_slabs([uy_ref[g, pl.ds(base + ncc, ncl), :] for g in range(ng)])
        for tau in range(t):
            y_ref[0, pl.ds(n_ctx + tau, ncl, stride=t), :] = slabs[tau]
            y_ref[0, pl.ds(tau, ncc, stride=t), :] = slab_ref[tau, pl.ds(cbase, ncc), :]
        y_ref[0] = y_ref[0] + d_ref[0] * u_ref[0]


def _s5_call(layer, u, tz, bc, cc, coef, s5_d, n_ctx):
    b, n, width = u.shape
    t, ng = S5_CHUNK, S5_LANE_GROUPS
    rows = b * (n // t + S5_PITCH_PAD)
    assert (n - n_ctx) // t == 128 and b * (n_ctx // t) == 128
    wspec = lambda a: pl.BlockSpec((1, ng) + a.shape[2:], lambda g, ph, i: (layer, g, 0, 0))
    return pl.pallas_call(
        functools.partial(_s5_kernel, n_ctx=n_ctx, n_batch=b),
        grid=(width // 128, 2, b),
        in_specs=[pl.BlockSpec((1, n, 128), lambda g, ph, i: (i, 0, g)),
                  wspec(tz), wspec(bc), wspec(cc), wspec(coef),
                  pl.BlockSpec((1, 1, 128), lambda g, ph, i: (layer, 0, g))],
        out_specs=pl.BlockSpec((1, n, 128), lambda g, ph, i: (i * ph, 0, g)),
        out_shape=jax.ShapeDtypeStruct((b, n, width), F32),
        scratch_shapes=[pltpu.VMEM((ng, rows, t * S5_GROUP), F32),
                        pltpu.VMEM((S5_SCAN_GROUPS, 4, rows, 128), F32),
                        pltpu.VMEM((S5_SCAN_GROUPS, 2, rows, 128), F32),
                        pltpu.VMEM((t, 128, 128), F32)],
        compiler_params=_cparams(3),
    )(u, tz, bc, cc, coef, s5_d.astype(F32).reshape(s5_d.shape[0], 1, width))


def _lane_chunks(xs):
    return [x[:, i * 128:(i + 1) * 128] for x in xs for i in range(x.shape[1] // 128)]


def _row_max(scores, floor=None):
    mm = functools.reduce(jnp.maximum, _lane_chunks(scores))
    if floor is not None:
        mm = jnp.maximum(mm, floor)
    return jnp.max(mm, axis=-1, keepdims=True)


def _softmax_av(scores, values, sink=None):
    m = _row_max(scores, sink)
    ps = [jnp.exp2(s - m) for s in scores]
    ll = functools.reduce(jnp.add, _lane_chunks(ps))
    if sink is not None:
        lane = lax.broadcasted_iota(jnp.int32, sink.shape, 1)
        ll = ll + jnp.where(lane == 0, jnp.exp2(sink - m), 0.0)
    l = jnp.sum(ll, axis=-1, keepdims=True)
    o = functools.reduce(jnp.add, [_bdot(p.astype(BF16), v) for p, v in zip(ps, values)])
    return o / l


def _mla_attn_kernel(qp_ref, qr_ref, k_ref, v_ref, o_ref, *, n_ctx, n_ctx_tiles):
    heads = qp_ref.shape[-1] // MLA_PAD
    t = pl.program_id(2)
    group = 256 // MLA_V
    lane = lax.broadcasted_iota(jnp.int32, (1, 256), 1)

    def run(latent):
        all_scores = []
        for h in range(heads):
            sl = slice(h * MLA_PAD, (h + 1) * MLA_PAD)
            scores = [_dot_nt(qp_ref[0, :, sl], k_ref[0, :n_ctx, sl])]
            if latent:
                scores.append(_dot_nt(qr_ref[0, :, sl], k_ref[0, n_ctx:, sl]))
            all_scores.append(scores)
        probs = []
        for scores in all_scores:
            m = _row_max(scores)
            ps = [jnp.exp2(s - m) for s in scores]
            l = jnp.sum(functools.reduce(jnp.add, _lane_chunks(ps)), axis=-1, keepdims=True)
            probs.append(([p.astype(BF16) for p in ps], l))
        acc = [None] * (heads // group)
        for h, (ps, l) in enumerate(probs):
            blk, hh = divmod(h, group)
            cols = slice(blk * 256, (blk + 1) * 256)
            own = (lane >= hh * MLA_V) & (lane < (hh + 1) * MLA_V)
            zero = jnp.zeros((), BF16)
            values = [jnp.where(own, v_ref[0, :n_ctx, cols], zero)]
            if latent:
                values.append(jnp.where(own, v_ref[0, n_ctx:, cols], zero))
            o = functools.reduce(jnp.add, [_bdot(p, v) for p, v in zip(ps, values)]) / l
            acc[blk] = o if acc[blk] is None else acc[blk] + o
        for blk, o in enumerate(acc):
            o_ref[0, :, blk * 256:(blk + 1) * 256] = o.astype(o_ref.dtype)

    pl.when(t < n_ctx_tiles)(lambda: run(False))
    pl.when(t >= n_ctx_tiles)(lambda: run(True))


def _mla_attn_call(qp, qr, k, v, n_ctx):
    b, n, _ = qp.shape
    tq = TOKEN_TILE
    hp = MLA_HEADS_PER_STEP
    qspec = pl.BlockSpec((1, tq, hp * MLA_PAD), lambda i, h, t: (i, t, h))
    return pl.pallas_call(
        functools.partial(_mla_attn_kernel, n_ctx=n_ctx, n_ctx_tiles=n_ctx // tq),
        grid=(b, MLA_HEADS // hp, n // tq),
        in_specs=[qspec, qspec,
                  pl.BlockSpec((1, n, hp * MLA_PAD), lambda i, h, t: (i, 0, h)),
                  pl.BlockSpec((1, n, hp * MLA_V), lambda i, h, t: (i, 0, h))],
        out_specs=pl.BlockSpec((1, tq, hp * MLA_V), lambda i, h, t: (i, t, h)),
        out_shape=jax.ShapeDtypeStruct((b, n, MLA_HEADS * MLA_V), BF16),
        compiler_params=_cparams(3),
    )(qp, qr, k, v)


def _win_kernel(sink_ref, q_ref, k_ref, v_ref, cos_ref, sa_ref, sb_ref, o_ref, *, n_ctx_blocks, n_blocks):
    j = pl.program_id(1)
    hd = WIN_HEAD_DIM
    half = hd // 2
    lane = lax.broadcasted_iota(jnp.int32, (1, 128), 1)
    lo = jnp.where(lane < hd, 1.0, 0.0)
    hi = 1.0 - lo
    upper_rows = lax.broadcasted_iota(jnp.int32, (2 * BLOCK, 128), 0) < BLOCK

    def lane_halves(x):
        xr = pltpu.roll(x, hd, 1)
        return {(0, 0): (x * lo).astype(BF16), (0, 1): (xr * hi).astype(BF16),
                (1, 0): (xr * lo).astype(BF16), (1, 1): (x * hi).astype(BF16)}

    def attend(queries, keys, values, masks):
        all_scores = {}
        for kh in range(WIN_KV_HEADS):
            stacked = [jnp.concatenate([qs[:, (2 * kh) * 128:(2 * kh + 1) * 128],
                                        qs[:, (2 * kh + 1) * 128:(2 * kh + 2) * 128]], axis=0).astype(BF16)
                       for qs in queries]
            for par in range(2):
                scores = []
                for qst, ks, msk in zip(stacked, keys, masks):
                    s = _dot_nt(qst, ks[(kh, par)])
                    scores.append(s if msk is None else jnp.where(msk, s, NEG_INF))
                all_scores[(kh, par)] = scores
        for kh in range(WIN_KV_HEADS):
            acc = None
            for par in range(2):
                sink = jnp.where(upper_rows, sink_ref[4 * kh + par], sink_ref[4 * kh + 2 + par]) * LOG2E
                o = _softmax_av(all_scores[(kh, par)], [vs[(kh, par)] for vs in values], sink)
                acc = o if acc is None else acc + o
            o_ref[0, :, (2 * kh) * 128:(2 * kh + 1) * 128] = acc[:BLOCK].astype(o_ref.dtype)
            o_ref[0, :, (2 * kh + 1) * 128:(2 * kh + 2) * 128] = acc[BLOCK:].astype(o_ref.dtype)

    q = q_ref[0] * (WIN_SCALE * LOG2E)
    n_ctx = n_ctx_blocks * BLOCK
    kctx = lane_halves(k_ref[0, :n_ctx, :])
    vctx = lane_halves(v_ref[0, :n_ctx, :])

    def ctx_path():
        attend([q], [kctx], [vctx], [None])

    def lat_path():
        blk = j - n_ctx_blocks
        band = [pl.ds(pl.multiple_of((n_ctx_blocks + jnp.clip(blk + d, 0, n_blocks - 1)) * BLOCK, BLOCK), BLOCK)
                for d in (-1, 0, 1)]
        rope = lambda x, rows: _rope_lanes(x, cos_ref[rows, :], sa_ref[rows, :], sb_ref[rows, :], half)
        q_rot = jnp.concatenate([rope(q[:, c * 128:(c + 1) * 128], band[1]) for c in range(q.shape[1] // 128)],
                                axis=-1)
        kband = jnp.concatenate([rope(k_ref[0, rows, :], rows) for rows in band], axis=0)
        vband = jnp.concatenate([v_ref[0, rows, :] for rows in band], axis=0)
        r = lax.broadcasted_iota(jnp.int32, (2 * BLOCK, 3 * BLOCK), 0) % BLOCK
        c = lax.broadcasted_iota(jnp.int32, (2 * BLOCK, 3 * BLOCK), 1)
        first = jnp.where(blk > 0, 0, BLOCK)
        last = jnp.where(blk < n_blocks - 1, 3 * BLOCK, 2 * BLOCK)
        valid = (jnp.abs(c - BLOCK - r) <= WINDOW) & (c >= first) & (c < last)
        attend([q_rot, q], [lane_halves(kband), kctx], [lane_halves(vband), vctx], [valid, None])

    pl.when(j < n_ctx_blocks)(ctx_path)
    pl.when(j >= n_ctx_blocks)(lat_path)


def _win_call(sink, wq, wk, wv, tabs, n_ctx):
    b, n, _ = wq.shape
    ncb = n_ctx // BLOCK
    nb = n // BLOCK
    kvw = WIN_KV_HEADS * WIN_HEAD_DIM
    cur = lambda i, j: (i, j, 0)
    kv = pl.BlockSpec((1, n, kvw), lambda i, j: (i, 0, 0))
    tab = pl.BlockSpec((n, 128), lambda i, j: (0, 0))
    return pl.pallas_call(
        functools.partial(_win_kernel, n_ctx_blocks=ncb, n_blocks=nb - ncb),
        grid=(b, nb),
        in_specs=[pl.BlockSpec(memory_space=pltpu.SMEM),
                  pl.BlockSpec((1, BLOCK, WIN_Q_HEADS * WIN_HEAD_DIM), cur), kv, kv, tab, tab, tab],
        out_specs=pl.BlockSpec((1, BLOCK, WIN_Q_HEADS * WIN_HEAD_DIM), cur),
        out_shape=jax.ShapeDtypeStruct((b, n, WIN_Q_HEADS * WIN_HEAD_DIM), BF16),
        compiler_params=_cparams(2),
    )(sink, wq, wk, wv, *tabs)


def _merge_kernel(x_ref, s5_ref, mla_ref, win_ref, gate_ref, mod_ref, wglu_ref, bglu_ref, wbr_ref, wout_ref,
                  g1_ref, b1_ref, wr_ref, x1_ref, h2_ref, lg_ref, *, alpha):
    d = x_ref.shape[-1]
    proj = {1: _bdot(mla_ref[0], wbr_ref[1]), 2: _bdot(win_ref[0], wbr_ref[2])}
    g = jax.nn.gelu(s5_ref[0])
    s5o = g * _sigmoid(_bdot(g.astype(BF16), wglu_ref[...]) + bglu_ref[...])
    proj[0] = _bdot(s5o.astype(BF16), wbr_ref[0])
    mix = None
    for kk in (1, 2, 0):
        term = _sigmoid(gate_ref[0, :, kk * d:(kk + 1) * d].astype(F32)) * proj[kk]
        mix = term if mix is None else mix + term
    y = _bdot(mix.astype(BF16), wout_ref[...])
    mod = lambda r: mod_ref[0, 0, r:r + 1, :]
    x1 = _layer_norm(alpha * x_ref[0] + mod(2) * y) * g1_ref[...] + b1_ref[...]
    x1_ref[0] = x1
    h2 = (_layer_norm(x1) * (1.0 + mod(4)) + mod(3)).astype(BF16)
    h2_ref[0] = h2
    lg_ref[0] = _dot_nt(wr_ref[...], h2)


def _merge_call(xall, s5y, mla_o, win_o, gates, mod, wglu, bglu, wbr, wout, g1, b1, wr_t, n_ctx_tiles, alpha):
    b, n, d = xall.shape
    tm = TOKEN_TILE
    tok = lambda w: pl.BlockSpec((1, tm, w), lambda i, t: (i, t, 0))
    full = lambda a: pl.BlockSpec(a.shape, lambda i, t: (0,) * a.ndim)
    return pl.pallas_call(
        functools.partial(_merge_kernel, alpha=alpha),
        grid=(b, n // tm),
        in_specs=[tok(d), tok(BRANCH_WIDTH), tok(BRANCH_WIDTH), tok(BRANCH_WIDTH), tok(N_BRANCH * d),
                  pl.BlockSpec((1, 1, 6, d), lambda i, t: (i, jnp.where(t < n_ctx_tiles, 0, 1), 0, 0)),
                  full(wglu), full(bglu), full(wbr), full(wout), full(g1), full(b1), full(wr_t)],
        out_specs=[tok(d), tok(d), pl.BlockSpec((1, N_EXPERTS, tm), lambda i, t: (i, 0, t))],
        out_shape=[jax.ShapeDtypeStruct((b, n, d), F32), jax.ShapeDtypeStruct((b, n, d), BF16),
                   jax.ShapeDtypeStruct((b, N_EXPERTS, n), F32)],
        compiler_params=_cparams(2),
    )(xall, s5y, mla_o, win_o, gates, mod, wglu, bglu, wbr, wout, g1, b1, wr_t)


def _excl_cumsum_lanes(m):
    rows, n = m.shape
    r = lax.broadcasted_iota(jnp.int32, (128, 128), 0)
    c = lax.broadcasted_iota(jnp.int32, (128, 128), 1)
    tri = jnp.where(r < c, 1.0, 0.0).astype(BF16)
    off = jnp.zeros((rows, 1), F32)
    outs, offs = [], []
    for jb in range(n // 128):
        blk = m[:, jb * 128:(jb + 1) * 128]
        offs.append(off)
        outs.append(_bdot(blk.astype(BF16), tri) + off)
        off = off + jnp.sum(blk, axis=1, keepdims=True)
    return jnp.concatenate(outs, axis=1), offs + [off]


def _topk_slots(affs, caps):
    bits = [pltpu.bitcast(aff, jnp.int32) for aff in affs]

    def body(i, thrs):
        out = []
        for b, cap, thr in zip(bits, caps, thrs):
            cand = thr | (jnp.int32(1) << (30 - i))
            cnt = jnp.sum(jnp.where(b >= cand, 1.0, 0.0), axis=1, keepdims=True)
            out.append(jnp.where(cnt >= cap, cand, thr))
        return tuple(out)

    zero = jnp.zeros((affs[0].shape[0], 1), jnp.int32)
    thrs = lax.fori_loop(0, 31, body, tuple(zero for _ in affs))
    results = []
    for b, cap, thr in zip(bits, caps, thrs):
        gt = jnp.where(b > thr, 1.0, 0.0)
        eq = jnp.where(b == thr, 1.0, 0.0)
        need = cap - jnp.sum(gt, axis=1, keepdims=True)
        sel = gt + eq * jnp.where(_excl_cumsum_lanes(eq)[0] < need, 1.0, 0.0)
        rank, offs = _excl_cumsum_lanes(sel)
        results.append((jnp.where(sel > 0.5, rank, -1.0).astype(jnp.int32), offs[::MOE_TILE // 128]))
    return results


def _route_kernel(lg_ref, slot_ref, aff_ref, bnd_ref, *, n_ctx, cap_ctx, cap_lat):
    lg = lg_ref[0]
    m = jnp.max(lg, axis=0, keepdims=True)
    ex = jnp.exp(lg - m)
    aff = ex / jnp.sum(ex, axis=0, keepdims=True)
    aff_ref[0] = aff
    (slots_ctx, _), (slots, counts) = _topk_slots([aff[:, :n_ctx], aff[:, n_ctx:]], [cap_ctx, cap_lat])
    slot_ref[0, :, :n_ctx] = slots_ctx
    slot_ref[0, :, n_ctx:] = slots
    lane = lax.broadcasted_iota(jnp.int32, bnd_ref.shape[1:], 1)
    bnd = jnp.zeros(bnd_ref.shape[1:], F32)
    for k, cnt in enumerate(counts):
        bnd = jnp.where(lane == k, cnt, bnd)
    bnd_ref[0] = bnd.astype(jnp.int32)


def _route_call(logits_t, n_ctx, cap_ctx, cap_lat):
    b, e, n = logits_t.shape
    assert (n - n_ctx) % MOE_TILE == 0 and (n - n_ctx) // MOE_TILE < MOE_BOUNDS
    spec = pl.BlockSpec((1, e, n), lambda i: (i, 0, 0))
    return pl.pallas_call(
        functools.partial(_route_kernel, n_ctx=n_ctx, cap_ctx=cap_ctx, cap_lat=cap_lat),
        grid=(b,),
        in_specs=[spec],
        out_specs=[spec, spec, pl.BlockSpec((1, e, MOE_BOUNDS), lambda i: (i, 0, 0))],
        out_shape=[jax.ShapeDtypeStruct((b, e, n), jnp.int32), jax.ShapeDtypeStruct((b, e, n), F32),
                   jax.ShapeDtypeStruct((b, e, MOE_BOUNDS), jnp.int32)],
        compiler_params=_cparams(1),
    )(logits_t)


def _gather_kernel(bnd_ref, slot_ref, aff_ref, h_ref, xs_ref, gate_ref, xl_ref, gl_ref, *, n_ctx, cap_ctx):
    for j in range(xs_ref.shape[1]):
        _gather_expert(j, bnd_ref, slot_ref, aff_ref, h_ref, xs_ref, gate_ref, xl_ref.at[j], gl_ref.at[j],
                       n_ctx=n_ctx, cap_ctx=cap_ctx)


def _gather_expert(j, bnd_ref, slot_ref, aff_ref, h_ref, xs_ref, gate_ref, xl_ref, gl_ref, *, n_ctx, cap_ctx):
    cap_lat = xs_ref.shape[2] - cap_ctx
    slot = slot_ref[0, j]
    aff = aff_ref[0, j]
    n = h_ref.shape[1]
    ib, ie = pl.program_id(0), pl.program_id(1) * xs_ref.shape[1] + j
    xl_ref[...] = jnp.zeros(xl_ref.shape, F32)
    gl_ref[...] = jnp.zeros(gl_ref.shape, F32)
    tiles = range((n - n_ctx) // MOE_TILE)
    firsts = [(bnd_ref[ib, ie, kt] // 16) * 16 for kt in tiles]

    def window(kt, start):
        tok = slice(n_ctx + kt * MOE_TILE, n_ctx + (kt + 1) * MOE_TILE)
        rows = start + lax.broadcasted_iota(jnp.int32, (MOE_WINDOW, MOE_TILE), 0)
        hit = slot[:, tok] == rows
        picked = jnp.sum(jnp.where(hit, aff[:, tok], 0.0), axis=1, keepdims=True)
        return _bdot(jnp.where(hit, 1.0, 0.0).astype(BF16), h_ref[0, tok, :]), picked

    def add_window(kt, start, parts=None):
        start = pl.multiple_of(start, 16)
        rows, picked = parts if parts is not None else window(kt, start)
        xl_ref[pl.ds(start, MOE_WINDOW), :] += rows
        gl_ref[pl.ds(start, MOE_WINDOW), :] += jnp.broadcast_to(picked, (MOE_WINDOW, 128))

    first_windows = [window(kt, firsts[kt]) for kt in tiles]
    for kt in tiles:
        add_window(kt, firsts[kt], first_windows[kt])
    for kt in tiles:
        def more(w, carry, kt=kt):
            add_window(kt, firsts[kt] + (w + 1) * MOE_WINDOW)
            return carry

        n_win = (bnd_ref[ib, ie, kt + 1] - firsts[kt] + MOE_WINDOW - 1) // MOE_WINDOW
        lax.fori_loop(0, jnp.maximum(n_win - 1, 0), more, 0)
    iota = lax.broadcasted_iota(jnp.int32, (cap_ctx, n_ctx), 0)
    hit = slot[:, :n_ctx] == iota
    xc = _bdot(jnp.where(hit, 1.0, 0.0).astype(BF16), h_ref[0, :n_ctx, :])
    gc = jnp.sum(jnp.where(hit, aff[:, :n_ctx], 0.0), axis=1, keepdims=True)
    xs_ref[0, j, :cap_lat] = xl_ref[:cap_lat].astype(BF16)
    xs_ref[0, j, cap_lat:] = xc.astype(BF16)
    gate_ref[0, j, :cap_lat] = gl_ref[:cap_lat]
    gate_ref[0, j, cap_lat:] = jnp.broadcast_to(gc, (cap_ctx, 128))


def _gather_call(bounds, slot, aff, h2, n_ctx, cap_ctx, cap_lat):
    b, e, n = slot.shape
    d = h2.shape[-1]
    cap = cap_lat + cap_ctx
    ne = GATHER_EXPERTS
    row = pl.BlockSpec((1, ne, 1, n), lambda ib, ie: (ib, ie, 0, 0))
    return pl.pallas_call(
        functools.partial(_gather_kernel, n_ctx=n_ctx, cap_ctx=cap_ctx),
        grid=(b, e // ne),
        in_specs=[pl.BlockSpec(memory_space=pltpu.SMEM), row, row, pl.BlockSpec((1, n, d), lambda ib, ie: (ib, 0, 0))],
        out_specs=[pl.BlockSpec((1, ne, cap, d), lambda ib, ie: (ib, ie, 0, 0)),
                   pl.BlockSpec((1, ne, cap, 128), lambda ib, ie: (ib, ie, 0, 0))],
        out_shape=[jax.ShapeDtypeStruct((b, e, cap, d), BF16), jax.ShapeDtypeStruct((b, e, cap, 128), F32)],
        scratch_shapes=[pltpu.VMEM((ne, cap_lat + MOE_WINDOW, d), F32),
                        pltpu.VMEM((ne, cap_lat + MOE_WINDOW, 128), F32)],
        compiler_params=_cparams(2),
    )(bounds, slot.reshape(b, e, 1, n), aff.reshape(b, e, 1, n), h2)


def _ffn_kernel(xs_ref, gate_ref, wg_ref, wu_ref, wd_ref, yl_ref, yc_ref, wg_s, wu_s, wd_s):
    ns, _, cap, d = xs_ref.shape
    cap_ctx = yc_ref.shape[2]
    cap_lat = cap - cap_ctx

    @pl.when(pl.program_id(1) == 0)
    def _():
        wg_s[...] = wg_ref[0, 0].astype(BF16)
        wu_s[...] = wu_ref[0, 0].astype(BF16)
        wd_s[...] = wd_ref[0, 0].astype(BF16)

    au = [(_bdot(xs_ref[s, 0], wg_s[...]), _bdot(xs_ref[s, 0], wu_s[...])) for s in range(ns)]
    for s, (a, u) in enumerate(au):
        hm = (a * _sigmoid(a) * u).astype(BF16)
        y = _bdot(hm, wd_s[...]) * gate_ref[s, 0, :, 0:1]
        yl_ref[s, 0, :cap_lat] = y[:cap_lat].astype(yl_ref.dtype)
        yl_ref[s, 0, cap_lat:] = jnp.zeros((MOE_WINDOW, d), yl_ref.dtype)
        yc_ref[s, 0] = y[cap_lat:].astype(yc_ref.dtype)


def _ffn_call(layer, xs, gate, w_gate, w_up, w_down, cap_ctx):
    b, e, cap, d = xs.shape
    f = w_gate.shape[-1]
    ns = FFN_SAMPLES
    rows_lat = cap - cap_ctx + MOE_WINDOW
    tok = lambda rows, w: pl.BlockSpec((ns, 1, rows, w), lambda ie, j: (j, ie, 0, 0))
    return pl.pallas_call(
        _ffn_kernel,
        grid=(e, b // ns),
        in_specs=[tok(cap, d), tok(cap, 128),
                  pl.BlockSpec((1, 1, d, f), lambda ie, j: (layer, ie, 0, 0)),
                  pl.BlockSpec((1, 1, d, f), lambda ie, j: (layer, ie, 0, 0)),
                  pl.BlockSpec((1, 1, f, d), lambda ie, j: (layer, ie, 0, 0))],
        out_specs=[tok(rows_lat, d), tok(cap_ctx, d)],
        out_shape=[jax.ShapeDtypeStruct((b, e, rows_lat, d), BF16), jax.ShapeDtypeStruct((b, e, cap_ctx, d), BF16)],
        scratch_shapes=[pltpu.VMEM((d, f), BF16), pltpu.VMEM((d, f), BF16), pltpu.VMEM((f, d), BF16)],
        compiler_params=_cparams(2),
    )(xs, gate, w_gate, w_up, w_down)


def _combine_kernel(bnd_ref, slot_ref, yl_ref, yc_ref, x1_ref, mod_ref, g2_ref, b2_ref, o_ref, fl_ref,
                    *, n_ctx_tiles, alpha):
    ib, t = pl.program_id(0), pl.program_id(1)
    tm = x1_ref.shape[1]
    slot = slot_ref[0]
    win = MOE_WINDOW

    def finish(fl):
        x1 = x1_ref[0]
        o_ref[0] = _layer_norm(alpha * x1 + mod_ref[0, 0, 5:6, :] * fl) * g2_ref[...] + b2_ref[...]

    def onehot(e, first, width):
        iota = lax.broadcasted_iota(jnp.int32, (tm, width), 1)
        return jnp.where(slot[:, e:e + 1] - first == iota, 1.0, 0.0).astype(BF16)

    def ctx_path():
        cap = yc_ref.shape[2]
        fl = None
        for e in range(N_EXPERTS):
            term = _bdot(onehot(e, 0, cap), yc_ref[0, e])
            fl = term if fl is None else fl + term
        finish(fl)

    def lat_path():
        kt = t - n_ctx_tiles
        firsts = [pl.multiple_of((bnd_ref[ib, e, kt] // 16) * 16, 16) for e in range(N_EXPERTS)]
        lane = lax.broadcasted_iota(jnp.int32, (tm, 2 * win), 1)
        pieces, ywins = [], []
        for e in range(0, N_EXPERTS, 2):
            rel = jnp.where(lane < win, slot[:, e:e + 1] - firsts[e], slot[:, e + 1:e + 2] - firsts[e + 1] + win)
            pieces.append(jnp.where(rel == lane, 1.0, 0.0).astype(BF16))
            ywins += [yl_ref[0, e, pl.ds(firsts[e], win), :], yl_ref[0, e + 1, pl.ds(firsts[e + 1], win), :]]
        fl_ref[...] = _bdot(jnp.concatenate(pieces, axis=1), jnp.concatenate(ywins, axis=0))
        for e in range(N_EXPERTS):
            def window(w, carry, e=e):
                first = pl.multiple_of(firsts[e] + (w + 1) * win, 16)
                fl_ref[...] += _bdot(onehot(e, first, win), yl_ref[0, e, pl.ds(first, win), :])
                return carry

            n_win = (bnd_ref[ib, e, kt + 1] - firsts[e] + win - 1) // win
            lax.fori_loop(0, jnp.maximum(n_win - 1, 0), window, 0)
        finish(fl_ref[...])

    pl.when(t < n_ctx_tiles)(ctx_path)
    pl.when(t >= n_ctx_tiles)(lat_path)


def _combine_call(bounds, slot_t, yl, yc, x1, mod, g2, b2, n_ctx_tiles, alpha, latent_only):
    b, n, d = x1.shape
    tm = MOE_TILE
    e = N_EXPERTS
    full = lambda a: pl.BlockSpec(a.shape, lambda i, t: (0,) * a.ndim)
    skip = n_ctx_tiles if latent_only else 0
    return pl.pallas_call(
        functools.partial(_combine_kernel, n_ctx_tiles=n_ctx_tiles, alpha=alpha),
        grid=(b, n // tm),
        scratch_shapes=[pltpu.VMEM((tm, d), F32)],
        in_specs=[pl.BlockSpec(memory_space=pltpu.SMEM),
                  pl.BlockSpec((1, tm, e), lambda i, t: (i, t, 0)),
                  pl.BlockSpec((1,) + yl.shape[1:], lambda i, t: (i, 0, 0, 0)),
                  pl.BlockSpec((1,) + yc.shape[1:], lambda i, t: (i, 0, 0, 0)),
                  pl.BlockSpec((1, tm, d), lambda i, t: (i, t, 0)),
                  pl.BlockSpec((1, 1, 6, d), lambda i, t: (i, jnp.where(t < n_ctx_tiles, 0, 1), 0, 0)),
                  full(g2), full(b2)],
        out_specs=pl.BlockSpec((1, tm, d), lambda i, t: (i, jnp.maximum(t - skip, 0), 0)),
        out_shape=jax.ShapeDtypeStruct((b, n - skip * tm, d), F32),
        compiler_params=_cparams(2),
    )(bounds, slot_t, yl, yc, x1, mod, g2, b2)


def _rope_tables(n_ctx, seq, head_dim, lane_offset):
    half = head_dim // 2
    nf = head_dim // 4
    t = jnp.arange(seq, dtype=F32)
    row = jnp.floor(t / GRID_W)
    col = t - row * GRID_W
    freqs = ROPE_BASE ** (-jnp.arange(nf, dtype=F32) / nf)
    ang = jnp.concatenate([row[:, None] * freqs, col[:, None] * freqs], axis=-1)
    cos, sin = jnp.cos(ang), jnp.sin(ang)
    zeros = jnp.zeros_like(sin)
    n_heads = (128 - lane_offset) // head_dim if lane_offset == 0 else 1
    c = jnp.concatenate([jnp.ones((seq, lane_offset), F32)] + [cos, cos] * n_heads, axis=-1)
    sa = jnp.concatenate([jnp.zeros((seq, lane_offset), F32)] + [-sin, zeros] * n_heads, axis=-1)
    sb = jnp.concatenate([jnp.zeros((seq, lane_offset), F32)] + [zeros, sin] * n_heads, axis=-1)
    pad = 128 - c.shape[1]
    c = jnp.pad(c, ((n_ctx, 0), (0, pad)), constant_values=1.0)
    sa = jnp.pad(sa, ((n_ctx, 0), (0, pad)))
    sb = jnp.pad(sb, ((n_ctx, 0), (0, pad)))
    return c, sa, sb


def _layer_weights(i, p):
    d = p['w_in'].shape[1]
    pts = np.cumsum((S5_WIDTH, MLA_Q_RANK, MLA_KV_RANK, MLA_ROPE, WIN_Q_HEADS * WIN_HEAD_DIM,
                     WIN_KV_HEADS * WIN_HEAD_DIM, WIN_KV_HEADS * WIN_HEAD_DIM))
    cols = jnp.split(p['w_in'][i], [int(v) for v in pts], axis=1)
    kr = jnp.pad(cols[3], ((0, 0), (MLA_NOPE, MLA_PAD - MLA_NOPE - MLA_ROPE)))
    w_cat = jnp.concatenate([cols[0], cols[1], cols[2], kr, cols[4], cols[5], cols[6], cols[7]], axis=1)
    dq = MLA_NOPE + MLA_ROPE
    wq = p['mla_w_uq'][i].reshape(MLA_Q_RANK, MLA_HEADS, dq)
    wq = jnp.pad(wq, ((0, 0), (0, 0), (0, MLA_PAD - dq))).reshape(MLA_Q_RANK, MLA_HEADS * MLA_PAD)
    wkv = p['mla_w_ukv'][i].reshape(MLA_KV_RANK, MLA_HEADS, MLA_NOPE + MLA_V)
    wk = jnp.pad(wkv[:, :, :MLA_NOPE], ((0, 0), (0, 0), (0, MLA_PAD - MLA_NOPE)))
    wk = wk.reshape(MLA_KV_RANK, MLA_HEADS * MLA_PAD)
    wv = wkv[:, :, MLA_NOPE:].reshape(MLA_KV_RANK, MLA_HEADS * MLA_V)
    row = lambda a: a[i].astype(F32).reshape(1, -1)
    return dict(
        w_cat=w_cat.astype(BF16), wq=wq.astype(BF16), wk=wk.astype(BF16), wv=wv.astype(BF16),
        qg=row(p['mla_q_norm']), kvg=row(p['mla_kv_norm']),
        wglu=p['s5_w_glu'][i].astype(BF16), bglu=row(p['s5_b_glu']),
        sink=p['win_sink'][i].astype(F32),
        wbr=p['w_branch'][i].astype(BF16), wout=p['w_out'][i].astype(BF16),
        g1=row(p['ln1_g']), b1=row(p['ln1_b']), g2=row(p['ln2_g']), b2=row(p['ln2_b']),
        wr_t=p['w_router'][i].T.astype(BF16),
    )


def _forward(p):
    x, c, ctx, c_ctx = p['x'], p['c'], p['ctx'], p['c_ctx']
    b, seq, d = x.shape
    n_ctx = ctx.shape[1]
    depth = p['w_ada'].shape[0]
    assert b == 8 and seq % TOKEN_TILE == 0 and n_ctx % TOKEN_TILE == 0 and seq % GRID_W == 0
    alpha = float((2 * depth) ** 0.25)
    n_ctx_tiles = n_ctx // TOKEN_TILE
    cap_lat = CAPACITY_FACTOR * seq // N_EXPERTS
    cap_ctx = CAPACITY_FACTOR * n_ctx // N_EXPERTS

    cond = jnp.concatenate([c, c_ctx[None], jnp.zeros((16 - b - 1, d), F32)], axis=0)
    mods = _ada_call(cond, p['w_ada'], p['b_ada'])
    mods = mods.reshape(depth, 16, 6, d)
    tabs_mla = _rope_tables(n_ctx, seq, MLA_ROPE, MLA_NOPE)
    tabs_win = _rope_tables(n_ctx, seq, WIN_HEAD_DIM, 0)
    s5w = _s5_param_call(p)

    xall = jnp.concatenate([ctx, x], axis=1)
    for i in range(depth):
        w = _layer_weights(i, p)
        mod = jnp.stack([jnp.broadcast_to(mods[i, b], (b, 6, d)), mods[i, :b]], axis=1)
        u, wq, wk, wv, gates, qp, qr, kk, vv = _in_call(xall, mod, w['w_cat'], w['qg'], w['kvg'], w['wq'], w['wk'],
                                                        w['wv'], tabs_mla, n_ctx)
        s5y = _s5_call(i, u, *s5w, p['s5_d'], n_ctx)
        mla_o = _mla_attn_call(qp, qr, kk, vv, n_ctx)
        win_o = _win_call(w['sink'], wq, wk, wv, tabs_win, n_ctx)
        x1, h2, logits_t = _merge_call(xall, s5y, mla_o, win_o, gates, mod, w['wglu'], w['bglu'], w['wbr'],
                                       w['wout'], w['g1'], w['b1'], w['wr_t'], n_ctx_tiles, alpha)
        slot, aff, bounds = _route_call(logits_t, n_ctx, cap_ctx, cap_lat)
        xs, gate = _gather_call(bounds, slot, aff, h2, n_ctx, cap_ctx, cap_lat)
        yl, yc = _ffn_call(i, xs, gate, p['w_gate'], p['w_up'], p['w_down'], cap_ctx)
        slot_t = jnp.swapaxes(slot, 1, 2)
        xall = _combine_call(bounds, slot_t, yl, yc, x1, mod, w['g2'], w['b2'], n_ctx_tiles, alpha,
                             latent_only=(i == depth - 1))
    return xall


def kernel(x, c, ctx, c_ctx, w_ada, b_ada, w_in, s5_lam_re, s5_lam_im, s5_log_dt, s5_b_re, s5_b_im, s5_c_re, s5_c_im, s5_d, s5_w_glu, s5_b_glu, mla_q_norm, mla_w_uq, mla_kv_norm, mla_w_ukv, win_sink, w_branch, w_out, ln1_g, ln1_b, ln2_g, ln2_b, w_router, w_gate, w_up, w_down):
    return _forward(dict(
        x=x, c=c, ctx=ctx, c_ctx=c_ctx, w_ada=w_ada, b_ada=b_ada, w_in=w_in, s5_lam_re=s5_lam_re,
        s5_lam_im=s5_lam_im, s5_log_dt=s5_log_dt, s5_b_re=s5_b_re, s5_b_im=s5_b_im, s5_c_re=s5_c_re,
        s5_c_im=s5_c_im, s5_d=s5_d, s5_w_glu=s5_w_glu, s5_b_glu=s5_b_glu, mla_q_norm=mla_q_norm,
        mla_w_uq=mla_w_uq, mla_kv_norm=mla_kv_norm, mla_w_ukv=mla_w_ukv, win_sink=win_sink, w_branch=w_branch,
        w_out=w_out, ln1_g=ln1_g, ln1_b=ln1_b, ln2_g=ln2_g, ln2_b=ln2_b, w_router=w_router, w_gate=w_gate,
        w_up=w_up, w_down=w_down))
```

```python
import functools
import math

import jax
import jax.numpy as jnp
import numpy as np
from jax import lax
from jax.experimental import pallas as pl
from jax.experimental.pallas import tpu as pltpu

F32 = jnp.float32
BF16 = jnp.bfloat16
HIGHEST = lax.Precision.HIGHEST

GRID_W = 64
S5_WIDTH = 512
S5_GROUP = 16
S5_GROUPS = S5_WIDTH // S5_GROUP
S5_STATE = 64
S5_CHUNK = 16
MLA_HEADS = 8
MLA_NOPE = 64
MLA_ROPE = 32
MLA_V = 64
MLA_Q_RANK = 384
MLA_KV_RANK = 256
MLA_PAD = 128
MLA_HEADS_PER_STEP = 8
MLA_SCALE = (MLA_NOPE + MLA_ROPE) ** -0.5
WIN_Q_HEADS = 8
WIN_KV_HEADS = 2
WIN_HEAD_DIM = 64
WINDOW = 128
BLOCK = 128
WIN_SCALE = WIN_HEAD_DIM ** -0.5
N_BRANCH = 3
BRANCH_WIDTH = 512
N_EXPERTS = 16
CAPACITY_FACTOR = 2
ROPE_BASE = 10000.0
LN_EPS = 1e-6
NEG_INF = -1e30
LOG2E = math.log2(math.e)
TOKEN_TILE = 256
IN_TILE = 384
MOE_TILE = 256
MOE_WINDOW = 64
MOE_BOUNDS = 16
FFN_SAMPLES = 4
GATHER_EXPERTS = 4
WIN_BLOCKS_PER_STEP = 2
VMEM_LIMIT = 56 * 1024 * 1024


def _cparams(n_axes):
    return pltpu.CompilerParams(dimension_semantics=("arbitrary",) * n_axes, vmem_limit_bytes=VMEM_LIMIT)


def _bdot(a, b):
    return jnp.dot(a, b, preferred_element_type=F32)


def _dot_nt(a, b):
    return lax.dot_general(a, b, (((1,), (1,)), ((), ())), preferred_element_type=F32)


def _layer_norm(x):
    mu = jnp.mean(x, axis=-1, keepdims=True)
    xc = x - mu
    var = jnp.mean(xc * xc, axis=-1, keepdims=True)
    return xc * lax.rsqrt(var + LN_EPS)


def _ada_kernel(cond_ref, w_ref, b_ref, o_ref):
    s = cond_ref[...]
    s = s * jax.nn.sigmoid(s)
    o_ref[0] = jnp.dot(s, w_ref[0], precision=HIGHEST, preferred_element_type=F32) + b_ref[0]


def _ada_call(cond, w_ada, b_ada):
    depth, d, d6 = w_ada.shape
    tn = 1536
    rows = cond.shape[0]
    return pl.pallas_call(
        _ada_kernel,
        grid=(depth, d6 // tn),
        in_specs=[
            pl.BlockSpec((rows, d), lambda i, j: (0, 0)),
            pl.BlockSpec((1, d, tn), lambda i, j: (i, 0, j)),
            pl.BlockSpec((1, 1, tn), lambda i, j: (i, 0, j)),
        ],
        out_specs=pl.BlockSpec((1, rows, tn), lambda i, j: (i, 0, j)),
        out_shape=jax.ShapeDtypeStruct((depth, rows, d6), F32),
        compiler_params=_cparams(2),
    )(cond, w_ada, b_ada.reshape(depth, 1, d6))


IN_WIDTHS = (S5_WIDTH, MLA_Q_RANK, MLA_KV_RANK, MLA_PAD, WIN_Q_HEADS * WIN_HEAD_DIM,
             WIN_KV_HEADS * WIN_HEAD_DIM, WIN_KV_HEADS * WIN_HEAD_DIM)
IN_OFFSETS = tuple(int(v) for v in np.cumsum((0,) + IN_WIDTHS))


def _mod_rows(mod_ref, r, tile, n_ctx):
    row = pl.program_id(1) * tile + lax.broadcasted_iota(jnp.int32, (tile, 1), 0)
    return jnp.where(row < n_ctx, mod_ref[0, 0, r:r + 1, :], mod_ref[0, 1, r:r + 1, :])


def _rope_lanes(x, cos, sa, sb, shift):
    return x * cos + pltpu.roll(x, 128 - shift, 1) * sa + pltpu.roll(x, shift, 1) * sb


def _mla_project(qa, kva, kr, qg, kvg, wq_ref, wk_ref, wv_ref, cos, sa, sb, qp_ref, qr_ref, k_ref, v_ref):
    def rms(x, gain):
        return (x * lax.rsqrt(jnp.mean(x * x, axis=-1, keepdims=True) + LN_EPS) * gain).astype(BF16)

    qn = rms(qa, qg)
    kvn = rms(kva, kvg)
    q = _bdot(qn, wq_ref[...]) * (MLA_SCALE * LOG2E)
    k = _bdot(kvn, wk_ref[...])
    v_ref[0] = _bdot(kvn, wv_ref[...]).astype(BF16)
    kr_rot = _rope_lanes(kr, cos, sa, sb, MLA_ROPE // 2)
    qp_ref[0] = q.astype(BF16)
    for h in range(MLA_HEADS):
        sl = slice(h * MLA_PAD, (h + 1) * MLA_PAD)
        qr_ref[0, :, sl] = _rope_lanes(q[:, sl], cos, sa, sb, MLA_ROPE // 2).astype(BF16)
        k_ref[0, :, sl] = (k[:, sl] + kr_rot).astype(BF16)


def _in_kernel(x_ref, mod_ref, w_ref, qg_ref, kvg_ref, wuq_ref, wuk_ref, wuv_ref, cos_ref, sa_ref, sb_ref,
               u_ref, wq_ref, wk_ref, wv_ref, gate_ref, qp_ref, qr_ref, k_ref, v_ref, *, n_ctx):
    tile = x_ref.shape[1]
    xn = _layer_norm(x_ref[0])
    h = (xn * (1.0 + _mod_rows(mod_ref, 1, tile, n_ctx)) + _mod_rows(mod_ref, 0, tile, n_ctx)).astype(BF16)
    widths = IN_WIDTHS + (gate_ref.shape[-1],)
    proj = lambda i: _bdot(h, w_ref[:, IN_OFFSETS[i]:IN_OFFSETS[i] + widths[i]])
    _mla_project(proj(1), proj(2), proj(3), qg_ref[...], kvg_ref[...], wuq_ref, wuk_ref, wuv_ref,
                 cos_ref[...], sa_ref[...], sb_ref[...], qp_ref, qr_ref, k_ref, v_ref)
    u_ref[0] = proj(0)
    wq_ref[0], wk_ref[0], wv_ref[0] = proj(4), proj(5), proj(6)
    gate_ref[0] = proj(7).astype(gate_ref.dtype)


_sigmoid = jax.nn.sigmoid


def _in_call(xall, mod, w_cat, qg, kvg, wuq, wuk, wuv, tabs, n_ctx):
    b, n, d = xall.shape
    tm = IN_TILE
    gate_w = w_cat.shape[1] - IN_OFFSETS[-1]
    hw = MLA_HEADS * MLA_PAD
    tok = lambda w: pl.BlockSpec((1, tm, w), lambda i, t: (i, t, 0))
    full = lambda a: pl.BlockSpec(a.shape, lambda i, t: (0,) * a.ndim)
    tab = pl.BlockSpec((tm, 128), lambda i, t: (t, 0))
    out_widths = (IN_WIDTHS[0],) + IN_WIDTHS[4:7] + (gate_w, hw, hw, hw, MLA_HEADS * MLA_V)
    out_dtypes = (F32,) * 4 + (BF16,) * 5
    return pl.pallas_call(
        functools.partial(_in_kernel, n_ctx=n_ctx),
        grid=(b, n // tm),
        in_specs=[tok(d), pl.BlockSpec((1, 2, 6, d), lambda i, t: (i, 0, 0, 0)),
                  pl.BlockSpec(w_cat.shape, lambda i, t: (0, 0), pipeline_mode=pl.Buffered(1)),
                  full(qg), full(kvg), full(wuq), full(wuk), full(wuv), tab, tab, tab],
        out_specs=[tok(w) for w in out_widths],
        out_shape=[jax.ShapeDtypeStruct((b, n, w), dt) for w, dt in zip(out_widths, out_dtypes)],
        compiler_params=_cparams(2),
    )(xall, mod, w_cat, qg, kvg, wuq, wuk, wuv, *tabs)


S5_LANE_GROUPS = 128 // S5_GROUP
S5_SCAN_GROUPS = 4
S5_PITCH_PAD = 8


def _s5_param_kernel(*refs):
    for g in range(refs[0].shape[2]):
        _s5_param_group(g, *refs)


def _s5_param_group(g, cre_ref, cim_ref, bre_ref, bim_ref, pr_ref, pi_ref, tz_ref, bc_ref, cc_ref, coef_ref):
    t = S5_CHUNK
    w = t * S5_GROUP
    nt = (((1,), (1,)), ((), ()))
    tz = None
    bcs, ccs, coefs = [], [], []
    for d in range(2):
        cre, cim = cre_ref[0, d, g], cim_ref[0, d, g]
        bre, bim = bre_ref[0, d, g], bim_ref[0, d, g]
        power = lambda k: (pr_ref[0, d, g, k:k + 1, :], pi_ref[0, d, g, k:k + 1, :])
        rt = []
        for k in range(t + 1):
            prk, pik = power(k)
            rt.append(jnp.concatenate([cre * prk - cim * pik, -(cre * pik + cim * prk)], axis=1))
        bt = jnp.concatenate([bre, bim], axis=1)
        zeros = jnp.zeros((S5_GROUP, w), F32)
        if d == 0:
            kt = lax.dot_general(bt, jnp.concatenate(rt[:t], axis=0), nt, precision=HIGHEST,
                                 preferred_element_type=F32)
            pad = jnp.concatenate([zeros, kt], axis=1)
            rows = [kt] + [pltpu.roll(pad, S5_GROUP * s, 1)[:, w:] for s in range(1, t)]
        else:
            kt = lax.dot_general(bt, jnp.concatenate(rt[t - 1::-1], axis=0), nt, precision=HIGHEST,
                                 preferred_element_type=F32)
            pad = jnp.concatenate([kt, zeros], axis=1)
            rows = [pltpu.roll(pad, 2 * w - S5_GROUP * (t - 1 - s), 1)[:, :w] for s in range(t - 1)] + [kt]
        tz_d = jnp.concatenate(rows, axis=0)
        tz = tz_d if tz is None else tz + tz_d
        bc_rows = []
        for s in range(t):
            prk, pik = power(t - 1 - s if d == 0 else s)
            br = bre * prk - bim * pik
            bi = bim * prk + bre * pik
            bc_rows.append(jnp.concatenate([br, bi, bi, br], axis=1))
        bcs.append(jnp.concatenate(bc_rows, axis=0))
        ccs.append(jnp.concatenate(rt[1:] if d == 0 else rt[t:0:-1], axis=0))
        er, ei = power(t)
        coefs += [jnp.concatenate([er, er], axis=1), jnp.concatenate([-ei, ei], axis=1),
                  jnp.concatenate([ei, -ei], axis=1)]
    tz_ref[0, g] = tz.astype(BF16)
    bc_ref[0, g] = jnp.concatenate(bcs, axis=1).astype(BF16)
    cc_ref[0, g] = jnp.concatenate(ccs, axis=1).astype(BF16)
    coef_ref[0, g] = jnp.concatenate(coefs + [jnp.zeros((2, 2 * S5_STATE), F32)], axis=0)


def _s5_param_call(p):
    t = S5_CHUNK
    f = lambda name: p[name].astype(F32)
    lam_re, lam_im = f('s5_lam_re'), f('s5_lam_im')
    depth = lam_re.shape[0]
    dt = jnp.exp(f('s5_log_dt'))[..., None]
    k = jnp.arange(t + 1, dtype=F32)[:, None]
    mag = jnp.exp((lam_re * dt)[..., None, :] * k)
    ang = (lam_im * dt)[..., None, :] * k
    pr, pi = mag * jnp.cos(ang), mag * jnp.sin(ang)
    ar, ai = pr[..., 1, :], pi[..., 1, :]
    den = lam_re * lam_re + lam_im * lam_im
    qr = (((ar - 1) * lam_re + ai * lam_im) / den)[..., None, :]
    qi = ((ai * lam_re - (ar - 1) * lam_im) / den)[..., None, :]
    b_re = jnp.swapaxes(f('s5_b_re'), -1, -2)
    b_im = jnp.swapaxes(f('s5_b_im'), -1, -2)
    bbr = qr * b_re - qi * b_im
    bbi = qr * b_im + qi * b_re
    g, hg, ps = S5_GROUPS, S5_GROUP, S5_STATE
    w = t * hg
    gs = S5_LANE_GROUPS
    small = lambda rows: pl.BlockSpec((1, 2, gs, rows, ps), lambda i, j: (i, 0, j, 0, 0))
    out = lambda cols: pl.BlockSpec((1, gs, w, cols), lambda i, j: (i, j, 0, 0))
    return pl.pallas_call(
        _s5_param_kernel,
        grid=(depth, g // gs),
        in_specs=[small(hg)] * 4 + [small(t + 1)] * 2,
        out_specs=[out(w), out(2 * w), out(w), pl.BlockSpec((1, gs, 8, 2 * ps), lambda i, j: (i, j, 0, 0))],
        out_shape=[jax.ShapeDtypeStruct((depth, g, w, w), BF16), jax.ShapeDtypeStruct((depth, g, w, 2 * w), BF16),
                   jax.ShapeDtypeStruct((depth, g, w, w), BF16), jax.ShapeDtypeStruct((depth, g, 8, 2 * ps), F32)],
        compiler_params=_cparams(2),
    )(f('s5_c_re'), f('s5_c_im'), bbr, bbi, pr, pi)


def _s5_kernel(u_ref, tz_ref, bc_ref, cc_ref, coef_ref, d_ref, y_ref, uy_ref, loc_ref, sp_ref, slab_ref,
               *, n_ctx, n_batch):
    ph, b = pl.program_id(1), pl.program_id(2)
    t, hg, ng = S5_CHUNK, S5_GROUP, S5_LANE_GROUPS
    ncc = n_ctx // t
    ncl = (u_ref.shape[1] - n_ctx) // t
    nc = ncc + ncl
    pitch = nc + S5_PITCH_PAD

    def to_chunk_rows(slabs):
        tr = [s.T for s in slabs]
        return [jnp.concatenate([x[g * hg:(g + 1) * hg] for x in tr], axis=0).T for g in range(ng)]

    def to_token_slabs(rows):
        tr = [r.T for r in rows]
        return [jnp.concatenate([x[tau * hg:(tau + 1) * hg] for x in tr], axis=0).T for tau in range(t)]

    base = pl.multiple_of(b * pitch, 8)
    cbase = pl.multiple_of(b * ncc, 8)

    @pl.when(ph == 0)
    def _():
        rows = to_chunk_rows([u_ref[0, pl.ds(n_ctx + tau, ncl, stride=t), :] for tau in range(t)])
        for g in range(ng):
            uy_ref[g, pl.ds(base + ncc, ncl), :] = rows[g]
            uy_ref[g, pl.ds(base + nc, S5_PITCH_PAD), :] = jnp.zeros((S5_PITCH_PAD, t * hg), F32)
        for tau in range(t):
            slab_ref[tau, pl.ds(cbase, ncc), :] = u_ref[0, pl.ds(tau, ncc, stride=t), :]

    @pl.when((ph == 1) & (b == 0))
    def _():
        rows = to_chunk_rows([slab_ref[tau] for tau in range(t)])
        for g in range(ng):
            for s in range(n_batch):
                uy_ref[g, s * pitch:s * pitch + ncc, :] = rows[g][s * ncc:(s + 1) * ncc]
        for part in range(ng // S5_SCAN_GROUPS):
            gs = [part * S5_SCAN_GROUPS + gl for gl in range(S5_SCAN_GROUPS)]
            for gl, g in enumerate(gs):
                ub = uy_ref[g].astype(BF16)
                loc = _bdot(ub, bc_ref[0, g])
                for q in range(4):
                    loc_ref[gl, q] = loc[:, q * 128:(q + 1) * 128]
                uy_ref[g] = _bdot(ub, tz_ref[0, g])
                for s in range(n_batch):
                    for d in range(2):
                        sp_ref[gl, d, s * pitch + nc:(s + 1) * pitch, :] = jnp.zeros((S5_PITCH_PAD, 128), F32)

            def coef(g, r):
                return jnp.broadcast_to(coef_ref[0, g, r:r + 1, :], (n_batch, 128))

            def step(i, carry):
                cb = jnp.where(i < ncc, ncc - 1 - i, nc + ncc - 1 - i)
                fwd = pl.ds(i, n_batch, stride=pitch)
                bwd = pl.ds(cb, n_batch, stride=pitch)
                out = []
                for gl, g in enumerate(gs):
                    v0f, v1f, v0b, v1b = carry[gl]
                    sp_ref[gl, 0, fwd, :] = v0f
                    sp_ref[gl, 1, bwd, :] = v0b
                    n0f = coef(g, 0) * v0f + coef(g, 1) * v1f + loc_ref[gl, 0, fwd, :]
                    n1f = coef(g, 0) * v1f + coef(g, 2) * v0f + loc_ref[gl, 1, fwd, :]
                    n0b = coef(g, 3) * v0b + coef(g, 4) * v1b + loc_ref[gl, 2, bwd, :]
                    n1b = coef(g, 3) * v1b + coef(g, 5) * v0b + loc_ref[gl, 3, bwd, :]
                    out.append((n0f, n1f, n0b, n1b))
                return tuple(out)

            z = jnp.zeros((n_batch, 128), F32)
            lax.fori_loop(0, nc, step, tuple((z, z, z, z) for _ in gs))
            for gl, g in enumerate(gs):
                sp = jnp.concatenate([sp_ref[gl, 0], sp_ref[gl, 1]], axis=1).astype(BF16)
                uy_ref[g] = uy_ref[g] + _dot_nt(sp, cc_ref[0, g])
        rows = [jnp.concatenate([uy_ref[g, s * pitch:s * pitch + ncc, :] for s in range(n_batch)], axis=0)
                for g in range(ng)]
        for tau, slab in enumerate(to_token_slabs(rows)):
            slab_ref[tau] = slab

    @pl.when(ph == 1)
    def _():
        slabs = to_token_slabs([uy_ref[g, pl.ds(base + ncc, ncl), :] for g in range(ng)])
        for tau in range(t):
            y_ref[0, pl.ds(n_ctx + tau, ncl, stride=t), :] = slabs[tau]
            y_ref[0, pl.ds(tau, ncc, stride=t), :] = slab_ref[tau, pl.ds(cbase, ncc), :]
        y_ref[0] = y_ref[0] + d_ref[0] * u_ref[0]


def _s5_call(layer, u, tz, bc, cc, coef, s5_d, n_ctx):
    b, n, width = u.shape
    t, ng = S5_CHUNK, S5_LANE_GROUPS
    rows = b * (n // t + S5_PITCH_PAD)
    assert (n - n_ctx) // t == 128 and b * (n_ctx // t) == 128
    wspec = lambda a: pl.BlockSpec((1, ng) + a.shape[2:], lambda g, ph, i: (layer, g, 0, 0))
    return pl.pallas_call(
        functools.partial(_s5_kernel, n_ctx=n_ctx, n_batch=b),
        grid=(width // 128, 2, b),
        in_specs=[pl.BlockSpec((1, n, 128), lambda g, ph, i: (i, 0, g)),
                  wspec(tz), wspec(bc), wspec(cc), wspec(coef),
                  pl.BlockSpec((1, 1, 128), lambda g, ph, i: (layer, 0, g))],
        out_specs=pl.BlockSpec((1, n, 128), lambda g, ph, i: (i * ph, 0, g)),
        out_shape=jax.ShapeDtypeStruct((b, n, width), F32),
        scratch_shapes=[pltpu.VMEM((ng, rows, t * S5_GROUP), F32),
                        pltpu.VMEM((S5_SCAN_GROUPS, 4, rows, 128), F32),
                        pltpu.VMEM((S5_SCAN_GROUPS, 2, rows, 128), F32),
                        pltpu.VMEM((t, 128, 128), F32)],
        compiler_params=_cparams(3),
    )(u, tz, bc, cc, coef, s5_d.astype(F32).reshape(s5_d.shape[0], 1, width))


def _lane_chunks(xs):
    return [x[:, i * 128:(i + 1) * 128] for x in xs for i in range(x.shape[1] // 128)]


def _row_max(scores, floor=None):
    mm = functools.reduce(jnp.maximum, _lane_chunks(scores))
    if floor is not None:
        mm = jnp.maximum(mm, floor)
    return jnp.max(mm, axis=-1, keepdims=True)


def _softmax_av(scores, values, sink=None):
    m = _row_max(scores, sink)
    ps = [jnp.exp2(s - m) for s in scores]
    ll = functools.reduce(jnp.add, _lane_chunks(ps))
    if sink is not None:
        lane = lax.broadcasted_iota(jnp.int32, sink.shape, 1)
        ll = ll + jnp.where(lane == 0, jnp.exp2(sink - m), 0.0)
    l = jnp.sum(ll, axis=-1, keepdims=True)
    o = functools.reduce(jnp.add, [_bdot(p.astype(BF16), v) for p, v in zip(ps, values)])
    return o / l


def _mla_attn_kernel(qp_ref, qr_ref, k_ref, v_ref, o_ref, *, n_ctx, n_ctx_tiles):
    heads = qp_ref.shape[-1] // MLA_PAD
    t = pl.program_id(2)
    group = 256 // MLA_V
    lane = lax.broadcasted_iota(jnp.int32, (1, 256), 1)

    def run(latent):
        all_scores = []
        for h in range(heads):
            sl = slice(h * MLA_PAD, (h + 1) * MLA_PAD)
            scores = [_dot_nt(qp_ref[0, :, sl], k_ref[0, :n_ctx, sl])]
            if latent:
                scores.append(_dot_nt(qr_ref[0, :, sl], k_ref[0, n_ctx:, sl]))
            all_scores.append(scores)
        probs = []
        for scores in all_scores:
            m = _row_max(scores)
            ps = [jnp.exp2(s - m) for s in scores]
            l = jnp.sum(functools.reduce(jnp.add, _lane_chunks(ps)), axis=-1, keepdims=True)
            probs.append(([p.astype(BF16) for p in ps], l))
        acc = [None] * (heads // group)
        for h, (ps, l) in enumerate(probs):
            blk, hh = divmod(h, group)
            cols = slice(blk * 256, (blk + 1) * 256)
            own = (lane >= hh * MLA_V) & (lane < (hh + 1) * MLA_V)
            zero = jnp.zeros((), BF16)
            values = [jnp.where(own, v_ref[0, :n_ctx, cols], zero)]
            if latent:
                values.append(jnp.where(own, v_ref[0, n_ctx:, cols], zero))
            o = functools.reduce(jnp.add, [_bdot(p, v) for p, v in zip(ps, values)]) / l
            acc[blk] = o if acc[blk] is None else acc[blk] + o
        for blk, o in enumerate(acc):
            o_ref[0, :, blk * 256:(blk + 1) * 256] = o.astype(o_ref.dtype)

    pl.when(t < n_ctx_tiles)(lambda: run(False))
    pl.when(t >= n_ctx_tiles)(lambda: run(True))


def _mla_attn_call(qp, qr, k, v, n_ctx):
    b, n, _ = qp.shape
    tq = TOKEN_TILE
    hp = MLA_HEADS_PER_STEP
    qspec = pl.BlockSpec((1, tq, hp * MLA_PAD), lambda i, h, t: (i, t, h))
    return pl.pallas_call(
        functools.partial(_mla_attn_kernel, n_ctx=n_ctx, n_ctx_tiles=n_ctx // tq),
        grid=(b, MLA_HEADS // hp, n // tq),
        in_specs=[qspec, qspec,
                  pl.BlockSpec((1, n, hp * MLA_PAD), lambda i, h, t: (i, 0, h)),
                  pl.BlockSpec((1, n, hp * MLA_V), lambda i, h, t: (i, 0, h))],
        out_specs=pl.BlockSpec((1, tq, hp * MLA_V), lambda i, h, t: (i, t, h)),
        out_shape=jax.ShapeDtypeStruct((b, n, MLA_HEADS * MLA_V), BF16),
        compiler_params=_cparams(3),
    )(qp, qr, k, v)


def _win_kernel(sink_ref, q_ref, k_ref, v_ref, cos_ref, sa_ref, sb_ref, o_ref, *, n_ctx_blocks, n_blocks):
    j = pl.program_id(1)
    hd = WIN_HEAD_DIM
    half = hd // 2
    lane = lax.broadcasted_iota(jnp.int32, (1, 128), 1)
    lo = jnp.where(lane < hd, 1.0, 0.0)
    hi = 1.0 - lo
    upper_rows = lax.broadcasted_iota(jnp.int32, (2 * BLOCK, 128), 0) < BLOCK

    def lane_halves(x):
        xr = pltpu.roll(x, hd, 1)
        return {(0, 0): (x * lo).astype(BF16), (0, 1): (xr * hi).astype(BF16),
                (1, 0): (xr * lo).astype(BF16), (1, 1): (x * hi).astype(BF16)}

    def attend(out_rows, queries, keys, values, masks):
        all_scores = {}
        for kh in range(WIN_KV_HEADS):
            stacked = [jnp.concatenate([qs[:, (2 * kh) * 128:(2 * kh + 1) * 128],
                                        qs[:, (2 * kh + 1) * 128:(2 * kh + 2) * 128]], axis=0).astype(BF16)
                       for qs in queries]
            for par in range(2):
                scores = []
                for qst, ks, msk in zip(stacked, keys, masks):
                    s = _dot_nt(qst, ks[(kh, par)])
                    scores.append(s if msk is None else jnp.where(msk, s, NEG_INF))
                all_scores[(kh, par)] = scores
        for kh in range(WIN_KV_HEADS):
            acc = None
            for par in range(2):
                sink = jnp.where(upper_rows, sink_ref[4 * kh + par], sink_ref[4 * kh + 2 + par]) * LOG2E
                o = _softmax_av(all_scores[(kh, par)], [vs[(kh, par)] for vs in values], sink)
                acc = o if acc is None else acc + o
            o_ref[0, out_rows, (2 * kh) * 128:(2 * kh + 1) * 128] = acc[:BLOCK].astype(o_ref.dtype)
            o_ref[0, out_rows, (2 * kh + 1) * 128:(2 * kh + 2) * 128] = acc[BLOCK:].astype(o_ref.dtype)

    n_ctx = n_ctx_blocks * BLOCK
    kctx = lane_halves(k_ref[0, :n_ctx, :])
    vctx = lane_halves(v_ref[0, :n_ctx, :])
    subs = [slice(s * BLOCK, (s + 1) * BLOCK) for s in range(q_ref.shape[1] // BLOCK)]

    def ctx_path():
        for rows in subs:
            attend(rows, [q_ref[0, rows, :] * (WIN_SCALE * LOG2E)], [kctx], [vctx], [None])

    def lat_path():
        for s, rows in enumerate(subs):
            lat_block(j * len(subs) + s - n_ctx_blocks, rows)

    def lat_block(blk, out_rows):
        q = q_ref[0, out_rows, :] * (WIN_SCALE * LOG2E)
        band = [pl.ds(pl.multiple_of((n_ctx_blocks + jnp.clip(blk + d, 0, n_blocks - 1)) * BLOCK, BLOCK), BLOCK)
                for d in (-1, 0, 1)]
        rope = lambda x, rows: _rope_lanes(x, cos_ref[rows, :], sa_ref[rows, :], sb_ref[rows, :], half)
        q_rot = jnp.concatenate([rope(q[:, c * 128:(c + 1) * 128], band[1]) for c in range(q.shape[1] // 128)],
                                axis=-1)
        kband = jnp.concatenate([rope(k_ref[0, rows, :], rows) for rows in band], axis=0)
        vband = jnp.concatenate([v_ref[0, rows, :] for rows in band], axis=0)
        r = lax.broadcasted_iota(jnp.int32, (2 * BLOCK, 3 * BLOCK), 0) % BLOCK
        c = lax.broadcasted_iota(jnp.int32, (2 * BLOCK, 3 * BLOCK), 1)
        first = jnp.where(blk > 0, 0, BLOCK)
        last = jnp.where(blk < n_blocks - 1, 3 * BLOCK, 2 * BLOCK)
        valid = (jnp.abs(c - BLOCK - r) <= WINDOW) & (c >= first) & (c < last)
        attend(out_rows, [q_rot, q], [lane_halves(kband), kctx], [lane_halves(vband), vctx], [valid, None])

    pl.when(j * len(subs) < n_ctx_blocks)(ctx_path)
    pl.when(j * len(subs) >= n_ctx_blocks)(lat_path)


def _win_call(sink, wq, wk, wv, tabs, n_ctx):
    b, n, _ = wq.shape
    ncb = n_ctx // BLOCK
    nb = n // BLOCK
    kvw = WIN_KV_HEADS * WIN_HEAD_DIM
    per = WIN_BLOCKS_PER_STEP
    assert ncb % per == 0 and nb % per == 0
    cur = lambda i, j: (i, j, 0)
    kv = pl.BlockSpec((1, n, kvw), lambda i, j: (i, 0, 0))
    tab = pl.BlockSpec((n, 128), lambda i, j: (0, 0))
    return pl.pallas_call(
        functools.partial(_win_kernel, n_ctx_blocks=ncb, n_blocks=nb - ncb),
        grid=(b, nb // per),
        in_specs=[pl.BlockSpec(memory_space=pltpu.SMEM),
                  pl.BlockSpec((1, per * BLOCK, WIN_Q_HEADS * WIN_HEAD_DIM), cur), kv, kv, tab, tab, tab],
        out_specs=pl.BlockSpec((1, per * BLOCK, WIN_Q_HEADS * WIN_HEAD_DIM), cur),
        out_shape=jax.ShapeDtypeStruct((b, n, WIN_Q_HEADS * WIN_HEAD_DIM), BF16),
        compiler_params=_cparams(2),
    )(sink, wq, wk, wv, *tabs)


def _merge_kernel(x_ref, s5_ref, mla_ref, win_ref, gate_ref, mod_ref, wglu_ref, bglu_ref, wbr_ref, wout_ref,
                  g1_ref, b1_ref, wr_ref, x1_ref, h2_ref, lg_ref, *, alpha):
    d = x_ref.shape[-1]
    proj = {1: _bdot(mla_ref[0], wbr_ref[1]), 2: _bdot(win_ref[0], wbr_ref[2])}
    g = jax.nn.gelu(s5_ref[0])
    s5o = g * _sigmoid(_bdot(g.astype(BF16), wglu_ref[...]) + bglu_ref[...])
    proj[0] = _bdot(s5o.astype(BF16), wbr_ref[0])
    mix = None
    for kk in (1, 2, 0):
        term = _sigmoid(gate_ref[0, :, kk * d:(kk + 1) * d].astype(F32)) * proj[kk]
        mix = term if mix is None else mix + term
    y = _bdot(mix.astype(BF16), wout_ref[...])
    mod = lambda r: mod_ref[0, 0, r:r + 1, :]
    x1 = _layer_norm(alpha * x_ref[0] + mod(2) * y) * g1_ref[...] + b1_ref[...]
    x1_ref[0] = x1
    h2 = (_layer_norm(x1) * (1.0 + mod(4)) + mod(3)).astype(BF16)
    h2_ref[0] = h2
    lg_ref[0] = _dot_nt(wr_ref[...], h2)


def _merge_call(xall, s5y, mla_o, win_o, gates, mod, wglu, bglu, wbr, wout, g1, b1, wr_t, n_ctx_tiles, alpha):
    b, n, d = xall.shape
    tm = TOKEN_TILE
    tok = lambda w: pl.BlockSpec((1, tm, w), lambda i, t: (i, t, 0))
    full = lambda a: pl.BlockSpec(a.shape, lambda i, t: (0,) * a.ndim)
    return pl.pallas_call(
        functools.partial(_merge_kernel, alpha=alpha),
        grid=(b, n // tm),
        in_specs=[tok(d), tok(BRANCH_WIDTH), tok(BRANCH_WIDTH), tok(BRANCH_WIDTH), tok(N_BRANCH * d),
                  pl.BlockSpec((1, 1, 6, d), lambda i, t: (i, jnp.where(t < n_ctx_tiles, 0, 1), 0, 0)),
                  full(wglu), full(bglu), full(wbr), full(wout), full(g1), full(b1), full(wr_t)],
        out_specs=[tok(d), tok(d), pl.BlockSpec((1, N_EXPERTS, tm), lambda i, t: (i, 0, t))],
        out_shape=[jax.ShapeDtypeStruct((b, n, d), F32), jax.ShapeDtypeStruct((b, n, d), BF16),
                   jax.ShapeDtypeStruct((b, N_EXPERTS, n), F32)],
        compiler_params=_cparams(2),
    )(xall, s5y, mla_o, win_o, gates, mod, wglu, bglu, wbr, wout, g1, b1, wr_t)


def _excl_cumsum_lanes(m):
    rows, n = m.shape
    r = lax.broadcasted_iota(jnp.int32, (128, 128), 0)
    c = lax.broadcasted_iota(jnp.int32, (128, 128), 1)
    tri = jnp.where(r < c, 1.0, 0.0).astype(BF16)
    off = jnp.zeros((rows, 1), F32)
    outs, offs = [], []
    for jb in range(n // 128):
        blk = m[:, jb * 128:(jb + 1) * 128]
        offs.append(off)
        outs.append(_bdot(blk.astype(BF16), tri) + off)
        off = off + jnp.sum(blk, axis=1, keepdims=True)
    return jnp.concatenate(outs, axis=1), offs + [off]


def _topk_slots(affs, caps):
    bits = [pltpu.bitcast(aff, jnp.int32) for aff in affs]

    def body(i, thrs):
        out = []
        for b, cap, thr in zip(bits, caps, thrs):
            cand = thr | (jnp.int32(1) << (30 - i))
            cnt = jnp.sum(jnp.where(b >= cand, 1.0, 0.0), axis=1, keepdims=True)
            out.append(jnp.where(cnt >= cap, cand, thr))
        return tuple(out)

    zero = jnp.zeros((affs[0].shape[0], 1), jnp.int32)
    thrs = lax.fori_loop(0, 31, body, tuple(zero for _ in affs))
    results = []
    for b, cap, thr in zip(bits, caps, thrs):
        gt = jnp.where(b > thr, 1.0, 0.0)
        eq = jnp.where(b == thr, 1.0, 0.0)
        need = cap - jnp.sum(gt, axis=1, keepdims=True)
        sel = gt + eq * jnp.where(_excl_cumsum_lanes(eq)[0] < need, 1.0, 0.0)
        rank, offs = _excl_cumsum_lanes(sel)
        results.append((jnp.where(sel > 0.5, rank, -1.0).astype(jnp.int32), offs[::MOE_TILE // 128]))
    return results


def _route_kernel(lg_ref, slot_ref, aff_ref, bnd_ref, *, n_ctx, cap_ctx, cap_lat):
    lg = lg_ref[0]
    m = jnp.max(lg, axis=0, keepdims=True)
    ex = jnp.exp(lg - m)
    aff = ex / jnp.sum(ex, axis=0, keepdims=True)
    aff_ref[0] = aff
    (slots_ctx, _), (slots, counts) = _topk_slots([aff[:, :n_ctx], aff[:, n_ctx:]], [cap_ctx, cap_lat])
    slot_ref[0, :, :n_ctx] = slots_ctx
    slot_ref[0, :, n_ctx:] = slots
    lane = lax.broadcasted_iota(jnp.int32, bnd_ref.shape[1:], 1)
    bnd = jnp.zeros(bnd_ref.shape[1:], F32)
    for k, cnt in enumerate(counts):
        bnd = jnp.where(lane == k, cnt, bnd)
    bnd_ref[0] = bnd.astype(jnp.int32)


def _route_call(logits_t, n_ctx, cap_ctx, cap_lat):
    b, e, n = logits_t.shape
    assert (n - n_ctx) % MOE_TILE == 0 and (n - n_ctx) // MOE_TILE < MOE_BOUNDS
    spec = pl.BlockSpec((1, e, n), lambda i: (i, 0, 0))
    return pl.pallas_call(
        functools.partial(_route_kernel, n_ctx=n_ctx, cap_ctx=cap_ctx, cap_lat=cap_lat),
        grid=(b,),
        in_specs=[spec],
        out_specs=[spec, spec, pl.BlockSpec((1, e, MOE_BOUNDS), lambda i: (i, 0, 0))],
        out_shape=[jax.ShapeDtypeStruct((b, e, n), jnp.int32), jax.ShapeDtypeStruct((b, e, n), F32),
                   jax.ShapeDtypeStruct((b, e, MOE_BOUNDS), jnp.int32)],
        compiler_params=_cparams(1),
    )(logits_t)


def _gather_kernel(bnd_ref, slot_ref, aff_ref, h_ref, xs_ref, gate_ref, xl_ref, gl_ref, *, n_ctx, cap_ctx):
    for j in range(xs_ref.shape[1]):
        _gather_expert(j, bnd_ref, slot_ref, aff_ref, h_ref, xs_ref, gate_ref, xl_ref.at[j], gl_ref.at[j],
                       n_ctx=n_ctx, cap_ctx=cap_ctx)


def _gather_expert(j, bnd_ref, slot_ref, aff_ref, h_ref, xs_ref, gate_ref, xl_ref, gl_ref, *, n_ctx, cap_ctx):
    cap_lat = xs_ref.shape[2] - cap_ctx
    slot = slot_ref[0, j]
    aff = aff_ref[0, j]
    n = h_ref.shape[1]
    ib, ie = pl.program_id(0), pl.program_id(1) * xs_ref.shape[1] + j
    xl_ref[...] = jnp.zeros(xl_ref.shape, F32)
    gl_ref[...] = jnp.zeros(gl_ref.shape, F32)
    tiles = range((n - n_ctx) // MOE_TILE)
    firsts = [(bnd_ref[ib, ie, kt] // 16) * 16 for kt in tiles]

    def window(kt, start):
        tok = slice(n_ctx + kt * MOE_TILE, n_ctx + (kt + 1) * MOE_TILE)
        rows = start + lax.broadcasted_iota(jnp.int32, (MOE_WINDOW, MOE_TILE), 0)
        hit = slot[:, tok] == rows
        picked = jnp.sum(jnp.where(hit, aff[:, tok], 0.0), axis=1, keepdims=True)
        return _bdot(jnp.where(hit, 1.0, 0.0).astype(BF16), h_ref[0, tok, :]), picked

    def add_window(kt, start, parts=None):
        start = pl.multiple_of(start, 16)
        rows, picked = parts if parts is not None else window(kt, start)
        xl_ref[pl.ds(start, MOE_WINDOW), :] += rows
        gl_ref[pl.ds(start, MOE_WINDOW), :] += jnp.broadcast_to(picked, (MOE_WINDOW, 128))

    first_windows = [window(kt, firsts[kt]) for kt in tiles]
    for kt in tiles:
        add_window(kt, firsts[kt], first_windows[kt])
    for kt in tiles:
        def more(w, carry, kt=kt):
            add_window(kt, firsts[kt] + (w + 1) * MOE_WINDOW)
            return carry

        n_win = (bnd_ref[ib, ie, kt + 1] - firsts[kt] + MOE_WINDOW - 1) // MOE_WINDOW
        lax.fori_loop(0, jnp.maximum(n_win - 1, 0), more, 0)
    iota = lax.broadcasted_iota(jnp.int32, (cap_ctx, n_ctx), 0)
    hit = slot[:, :n_ctx] == iota
    xc = _bdot(jnp.where(hit, 1.0, 0.0).astype(BF16), h_ref[0, :n_ctx, :])
    gc = jnp.sum(jnp.where(hit, aff[:, :n_ctx], 0.0), axis=1, keepdims=True)
    xs_ref[0, j, :cap_lat] = xl_ref[:cap_lat].astype(BF16)
    xs_ref[0, j, cap_lat:] = xc.astype(BF16)
    gate_ref[0, j, :cap_lat] = gl_ref[:cap_lat]
    gate_ref[0, j, cap_lat:] = jnp.broadcast_to(gc, (cap_ctx, 128))


def _gather_call(bounds, slot, aff, h2, n_ctx, cap_ctx, cap_lat):
    b, e, n = slot.shape
    d = h2.shape[-1]
    cap = cap_lat + cap_ctx
    ne = GATHER_EXPERTS
    row = pl.BlockSpec((1, ne, 1, n), lambda ib, ie: (ib, ie, 0, 0))
    return pl.pallas_call(
        functools.partial(_gather_kernel, n_ctx=n_ctx, cap_ctx=cap_ctx),
        grid=(b, e // ne),
        in_specs=[pl.BlockSpec(memory_space=pltpu.SMEM), row, row, pl.BlockSpec((1, n, d), lambda ib, ie: (ib, 0, 0))],
        out_specs=[pl.BlockSpec((1, ne, cap, d), lambda ib, ie: (ib, ie, 0, 0)),
                   pl.BlockSpec((1, ne, cap, 128), lambda ib, ie: (ib, ie, 0, 0))],
        out_shape=[jax.ShapeDtypeStruct((b, e, cap, d), BF16), jax.ShapeDtypeStruct((b, e, cap, 128), F32)],
        scratch_shapes=[pltpu.VMEM((ne, cap_lat + MOE_WINDOW, d), F32),
                        pltpu.VMEM((ne, cap_lat + MOE_WINDOW, 128), F32)],
        compiler_params=_cparams(2),
    )(bounds, slot.reshape(b, e, 1, n), aff.reshape(b, e, 1, n), h2)


def _ffn_kernel(xs_ref, gate_ref, wg_ref, wu_ref, wd_ref, yl_ref, yc_ref, wg_s, wu_s, wd_s):
    ns, _, cap, d = xs_ref.shape
    cap_ctx = yc_ref.shape[2]
    cap_lat = cap - cap_ctx

    @pl.when(pl.program_id(1) == 0)
    def _():
        wg_s[...] = wg_ref[0, 0].astype(BF16)
        wu_s[...] = wu_ref[0, 0].astype(BF16)
        wd_s[...] = wd_ref[0, 0].astype(BF16)

    au = [(_bdot(xs_ref[s, 0], wg_s[...]), _bdot(xs_ref[s, 0], wu_s[...])) for s in range(ns)]
    for s, (a, u) in enumerate(au):
        hm = (a * _sigmoid(a) * u).astype(BF16)
        y = _bdot(hm, wd_s[...]) * gate_ref[s, 0, :, 0:1]
        yl_ref[s, 0, :cap_lat] = y[:cap_lat].astype(yl_ref.dtype)
        yl_ref[s, 0, cap_lat:] = jnp.zeros((MOE_WINDOW, d), yl_ref.dtype)
        yc_ref[s, 0] = y[cap_lat:].astype(yc_ref.dtype)


def _ffn_call(layer, xs, gate, w_gate, w_up, w_down, cap_ctx):
    b, e, cap, d = xs.shape
    f = w_gate.shape[-1]
    ns = FFN_SAMPLES
    rows_lat = cap - cap_ctx + MOE_WINDOW
    tok = lambda rows, w: pl.BlockSpec((ns, 1, rows, w), lambda ie, j: (j, ie, 0, 0))
    return pl.pallas_call(
        _ffn_kernel,
        grid=(e, b // ns),
        in_specs=[tok(cap, d), tok(cap, 128),
                  pl.BlockSpec((1, 1, d, f), lambda ie, j: (layer, ie, 0, 0)),
                  pl.BlockSpec((1, 1, d, f), lambda ie, j: (layer, ie, 0, 0)),
                  pl.BlockSpec((1, 1, f, d), lambda ie, j: (layer, ie, 0, 0))],
        out_specs=[tok(rows_lat, d), tok(cap_ctx, d)],
        out_shape=[jax.ShapeDtypeStruct((b, e, rows_lat, d), BF16), jax.ShapeDtypeStruct((b, e, cap_ctx, d), BF16)],
        scratch_shapes=[pltpu.VMEM((d, f), BF16), pltpu.VMEM((d, f), BF16), pltpu.VMEM((f, d), BF16)],
        compiler_params=_cparams(2),
    )(xs, gate, w_gate, w_up, w_down)


def _combine_kernel(bnd_ref, slot_ref, yl_ref, yc_ref, x1_ref, mod_ref, g2_ref, b2_ref, o_ref, fl_ref,
                    *, n_ctx_tiles, alpha):
    ib, t = pl.program_id(0), pl.program_id(1)
    tm = x1_ref.shape[1]
    slot = slot_ref[0]
    win = MOE_WINDOW

    def finish(fl):
        x1 = x1_ref[0]
        o_ref[0] = _layer_norm(alpha * x1 + mod_ref[0, 0, 5:6, :] * fl) * g2_ref[...] + b2_ref[...]

    def onehot(e, first, width):
        iota = lax.broadcasted_iota(jnp.int32, (tm, width), 1)
        return jnp.where(slot[:, e:e + 1] - first == iota, 1.0, 0.0).astype(BF16)

    def ctx_path():
        cap = yc_ref.shape[2]
        fl = None
        for e in range(N_EXPERTS):
            term = _bdot(onehot(e, 0, cap), yc_ref[0, e])
            fl = term if fl is None else fl + term
        finish(fl)

    def lat_path():
        kt = t - n_ctx_tiles
        firsts = [pl.multiple_of((bnd_ref[ib, e, kt] // 16) * 16, 16) for e in range(N_EXPERTS)]
        lane = lax.broadcasted_iota(jnp.int32, (tm, 2 * win), 1)
        pieces, ywins = [], []
        for e in range(0, N_EXPERTS, 2):
            rel = jnp.where(lane < win, slot[:, e:e + 1] - firsts[e], slot[:, e + 1:e + 2] - firsts[e + 1] + win)
            pieces.append(jnp.where(rel == lane, 1.0, 0.0).astype(BF16))
            ywins += [yl_ref[0, e, pl.ds(firsts[e], win), :], yl_ref[0, e + 1, pl.ds(firsts[e + 1], win), :]]
        fl_ref[...] = _bdot(jnp.concatenate(pieces, axis=1), jnp.concatenate(ywins, axis=0))
        for e in range(N_EXPERTS):
            def window(w, carry, e=e):
                first = pl.multiple_of(firsts[e] + (w + 1) * win, 16)
                fl_ref[...] += _bdot(onehot(e, first, win), yl_ref[0, e, pl.ds(first, win), :])
                return carry

            n_win = (bnd_ref[ib, e, kt + 1] - firsts[e] + win - 1) // win
            lax.fori_loop(0, jnp.maximum(n_win - 1, 0), window, 0)
        finish(fl_ref[...])

    pl.when(t < n_ctx_tiles)(ctx_path)
    pl.when(t >= n_ctx_tiles)(lat_path)


def _combine_call(bounds, slot_t, yl, yc, x1, mod, g2, b2, n_ctx_tiles, alpha, latent_only):
    b, n, d = x1.shape
    tm = MOE_TILE
    e = N_EXPERTS
    full = lambda a: pl.BlockSpec(a.shape, lambda i, t: (0,) * a.ndim)
    skip = n_ctx_tiles if latent_only else 0
    return pl.pallas_call(
        functools.partial(_combine_kernel, n_ctx_tiles=n_ctx_tiles, alpha=alpha),
        grid=(b, n // tm),
        scratch_shapes=[pltpu.VMEM((tm, d), F32)],
        in_specs=[pl.BlockSpec(memory_space=pltpu.SMEM),
                  pl.BlockSpec((1, tm, e), lambda i, t: (i, t, 0)),
                  pl.BlockSpec((1,) + yl.shape[1:], lambda i, t: (i, 0, 0, 0)),
                  pl.BlockSpec((1,) + yc.shape[1:], lambda i, t: (i, 0, 0, 0)),
                  pl.BlockSpec((1, tm, d), lambda i, t: (i, t, 0)),
                  pl.BlockSpec((1, 1, 6, d), lambda i, t: (i, jnp.where(t < n_ctx_tiles, 0, 1), 0, 0)),
                  full(g2), full(b2)],
        out_specs=pl.BlockSpec((1, tm, d), lambda i, t: (i, jnp.maximum(t - skip, 0), 0)),
        out_shape=jax.ShapeDtypeStruct((b, n - skip * tm, d), F32),
        compiler_params=_cparams(2),
    )(bounds, slot_t, yl, yc, x1, mod, g2, b2)


def _rope_tables(n_ctx, seq, head_dim, lane_offset):
    half = head_dim // 2
    nf = head_dim // 4
    t = jnp.arange(seq, dtype=F32)
    row = jnp.floor(t / GRID_W)
    col = t - row * GRID_W
    freqs = ROPE_BASE ** (-jnp.arange(nf, dtype=F32) / nf)
    ang = jnp.concatenate([row[:, None] * freqs, col[:, None] * freqs], axis=-1)
    cos, sin = jnp.cos(ang), jnp.sin(ang)
    zeros = jnp.zeros_like(sin)
    n_heads = (128 - lane_offset) // head_dim if lane_offset == 0 else 1
    c = jnp.concatenate([jnp.ones((seq, lane_offset), F32)] + [cos, cos] * n_heads, axis=-1)
    sa = jnp.concatenate([jnp.zeros((seq, lane_offset), F32)] + [-sin, zeros] * n_heads, axis=-1)
    sb = jnp.concatenate([jnp.zeros((seq, lane_offset), F32)] + [zeros, sin] * n_heads, axis=-1)
    pad = 128 - c.shape[1]
    c = jnp.pad(c, ((n_ctx, 0), (0, pad)), constant_values=1.0)
    sa = jnp.pad(sa, ((n_ctx, 0), (0, pad)))
    sb = jnp.pad(sb, ((n_ctx, 0), (0, pad)))
    return c, sa, sb


def _layer_weights(i, p):
    d = p['w_in'].shape[1]
    pts = np.cumsum((S5_WIDTH, MLA_Q_RANK, MLA_KV_RANK, MLA_ROPE, WIN_Q_HEADS * WIN_HEAD_DIM,
                     WIN_KV_HEADS * WIN_HEAD_DIM, WIN_KV_HEADS * WIN_HEAD_DIM))
    cols = jnp.split(p['w_in'][i], [int(v) for v in pts], axis=1)
    kr = jnp.pad(cols[3], ((0, 0), (MLA_NOPE, MLA_PAD - MLA_NOPE - MLA_ROPE)))
    w_cat = jnp.concatenate([cols[0], cols[1], cols[2], kr, cols[4], cols[5], cols[6], cols[7]], axis=1)
    dq = MLA_NOPE + MLA_ROPE
    wq = p['mla_w_uq'][i].reshape(MLA_Q_RANK, MLA_HEADS, dq)
    wq = jnp.pad(wq, ((0, 0), (0, 0), (0, MLA_PAD - dq))).reshape(MLA_Q_RANK, MLA_HEADS * MLA_PAD)
    wkv = p['mla_w_ukv'][i].reshape(MLA_KV_RANK, MLA_HEADS, MLA_NOPE + MLA_V)
    wk = jnp.pad(wkv[:, :, :MLA_NOPE], ((0, 0), (0, 0), (0, MLA_PAD - MLA_NOPE)))
    wk = wk.reshape(MLA_KV_RANK, MLA_HEADS * MLA_PAD)
    wv = wkv[:, :, MLA_NOPE:].reshape(MLA_KV_RANK, MLA_HEADS * MLA_V)
    row = lambda a: a[i].astype(F32).reshape(1, -1)
    return dict(
        w_cat=w_cat.astype(BF16), wq=wq.astype(BF16), wk=wk.astype(BF16), wv=wv.astype(BF16),
        qg=row(p['mla_q_norm']), kvg=row(p['mla_kv_norm']),
        wglu=p['s5_w_glu'][i].astype(BF16), bglu=row(p['s5_b_glu']),
        sink=p['win_sink'][i].astype(F32),
        wbr=p['w_branch'][i].astype(BF16), wout=p['w_out'][i].astype(BF16),
        g1=row(p['ln1_g']), b1=row(p['ln1_b']), g2=row(p['ln2_g']), b2=row(p['ln2_b']),
        wr_t=p['w_router'][i].T.astype(BF16),
    )


def _forward(p):
    x, c, ctx, c_ctx = p['x'], p['c'], p['ctx'], p['c_ctx']
    b, seq, d = x.shape
    n_ctx = ctx.shape[1]
    depth = p['w_ada'].shape[0]
    assert b == 8 and seq % TOKEN_TILE == 0 and n_ctx % TOKEN_TILE == 0 and seq % GRID_W == 0
    alpha = float((2 * depth) ** 0.25)
    n_ctx_tiles = n_ctx // TOKEN_TILE
    cap_lat = CAPACITY_FACTOR * seq // N_EXPERTS
    cap_ctx = CAPACITY_FACTOR * n_ctx // N_EXPERTS

    cond = jnp.concatenate([c, c_ctx[None], jnp.zeros((16 - b - 1, d), F32)], axis=0)
    mods = _ada_call(cond, p['w_ada'], p['b_ada'])
    mods = mods.reshape(depth, 16, 6, d)
    tabs_mla = _rope_tables(n_ctx, seq, MLA_ROPE, MLA_NOPE)
    tabs_win = _rope_tables(n_ctx, seq, WIN_HEAD_DIM, 0)
    s5w = _s5_param_call(p)

    xall = jnp.concatenate([ctx, x], axis=1)
    for i in range(depth):
        w = _layer_weights(i, p)
        mod = jnp.stack([jnp.broadcast_to(mods[i, b], (b, 6, d)), mods[i, :b]], axis=1)
        u, wq, wk, wv, gates, qp, qr, kk, vv = _in_call(xall, mod, w['w_cat'], w['qg'], w['kvg'], w['wq'], w['wk'],
                                                        w['wv'], tabs_mla, n_ctx)
        s5y = _s5_call(i, u, *s5w, p['s5_d'], n_ctx)
        mla_o = _mla_attn_call(qp, qr, kk, vv, n_ctx)
        win_o = _win_call(w['sink'], wq, wk, wv, tabs_win, n_ctx)
        x1, h2, logits_t = _merge_call(xall, s5y, mla_o, win_o, gates, mod, w['wglu'], w['bglu'], w['wbr'],
                                       w['wout'], w['g1'], w['b1'], w['wr_t'], n_ctx_tiles, alpha)
        slot, aff, bounds = _route_call(logits_t, n_ctx, cap_ctx, cap_lat)
        xs, gate = _gather_call(bounds, slot, aff, h2, n_ctx, cap_ctx, cap_lat)
        yl, yc = _ffn_call(i, xs, gate, p['w_gate'], p['w_up'], p['w_down'], cap_ctx)
        slot_t = jnp.swapaxes(slot, 1, 2)
        xall = _combine_call(bounds, slot_t, yl, yc, x1, mod, w['g2'], w['b2'], n_ctx_tiles, alpha,
                             latent_only=(i == depth - 1))
    return xall


def kernel(x, c, ctx, c_ctx, w_ada, b_ada, w_in, s5_lam_re, s5_lam_im, s5_log_dt, s5_b_re, s5_b_im, s5_c_re, s5_c_im, s5_d, s5_w_glu, s5_b_glu, mla_q_norm, mla_w_uq, mla_kv_norm, mla_w_ukv, win_sink, w_branch, w_out, ln1_g, ln1_b, ln2_g, ln2_b, w_router, w_gate, w_up, w_down):
    return _forward(dict(
        x=x, c=c, ctx=ctx, c_ctx=c_ctx, w_ada=w_ada, b_ada=b_ada, w_in=w_in, s5_lam_re=s5_lam_re,
        s5_lam_im=s5_lam_im, s5_log_dt=s5_log_dt, s5_b_re=s5_b_re, s5_b_im=s5_b_im, s5_c_re=s5_c_re,
        s5_c_im=s5_c_im, s5_d=s5_d, s5_w_glu=s5_w_glu, s5_b_glu=s5_b_glu, mla_q_norm=mla_q_norm,
        mla_w_uq=mla_w_uq, mla_kv_norm=mla_kv_norm, mla_w_ukv=mla_w_ukv, win_sink=win_sink, w_branch=w_branch,
        w_out=w_out, ln1_g=ln1_g, ln1_b=ln1_b, ln2_g=ln2_g, ln2_b=ln2_b, w_router=w_router, w_gate=w_gate,
        w_up=w_up, w_down=w_down))
```

```python
import functools
import math

import jax
import jax.numpy as jnp
import numpy as np
from jax import lax
from jax.experimental import pallas as pl
from jax.experimental.pallas import tpu as pltpu

F32 = jnp.float32
BF16 = jnp.bfloat16
HIGHEST = lax.Precision.HIGHEST

GRID_W = 64
S5_WIDTH = 512
S5_GROUP = 16
S5_GROUPS = S5_WIDTH // S5_GROUP
S5_STATE = 64
S5_CHUNK = 16
MLA_HEADS = 8
MLA_NOPE = 64
MLA_ROPE = 32
MLA_V = 64
MLA_Q_RANK = 384
MLA_KV_RANK = 256
MLA_PAD = 128
MLA_HEADS_PER_STEP = 8
MLA_SCALE = (MLA_NOPE + MLA_ROPE) ** -0.5
WIN_Q_HEADS = 8
WIN_KV_HEADS = 2
WIN_HEAD_DIM = 64
WINDOW = 128
BLOCK = 128
WIN_SCALE = WIN_HEAD_DIM ** -0.5
N_BRANCH = 3
BRANCH_WIDTH = 512
N_EXPERTS = 16
CAPACITY_FACTOR = 2
ROPE_BASE = 10000.0
LN_EPS = 1e-6
NEG_INF = -1e30
LOG2E = math.log2(math.e)
TOKEN_TILE = 256
IN_TILE = 384
MERGE_TILE = 384
MOE_TILE = 256
MOE_WINDOW = 64
MOE_BOUNDS = 16
FFN_SAMPLES = 4
GATHER_EXPERTS = 8
WIN_BLOCKS_PER_STEP = 2
VMEM_LIMIT = 56 * 1024 * 1024


def _cparams(n_axes):
    return pltpu.CompilerParams(dimension_semantics=("arbitrary",) * n_axes, vmem_limit_bytes=VMEM_LIMIT)


def _bdot(a, b):
    return jnp.dot(a, b, preferred_element_type=F32)


def _dot_nt(a, b):
    return lax.dot_general(a, b, (((1,), (1,)), ((), ())), preferred_element_type=F32)


def _layer_norm(x):
    mu = jnp.mean(x, axis=-1, keepdims=True)
    xc = x - mu
    var = jnp.mean(xc * xc, axis=-1, keepdims=True)
    return xc * lax.rsqrt(var + LN_EPS)


def _ada_kernel(cond_ref, w_ref, b_ref, o_ref):
    s = cond_ref[...]
    s = s * jax.nn.sigmoid(s)
    o_ref[0] = jnp.dot(s, w_ref[0], precision=HIGHEST, preferred_element_type=F32) + b_ref[0]


def _ada_call(cond, w_ada, b_ada):
    depth, d, d6 = w_ada.shape
    tn = 1536
    rows = cond.shape[0]
    return pl.pallas_call(
        _ada_kernel,
        grid=(depth, d6 // tn),
        in_specs=[
            pl.BlockSpec((rows, d), lambda i, j: (0, 0)),
            pl.BlockSpec((1, d, tn), lambda i, j: (i, 0, j)),
            pl.BlockSpec((1, 1, tn), lambda i, j: (i, 0, j)),
        ],
        out_specs=pl.BlockSpec((1, rows, tn), lambda i, j: (i, 0, j)),
        out_shape=jax.ShapeDtypeStruct((depth, rows, d6), F32),
        compiler_params=_cparams(2),
    )(cond, w_ada, b_ada.reshape(depth, 1, d6))


IN_WIDTHS = (S5_WIDTH, MLA_Q_RANK, MLA_KV_RANK, MLA_PAD, WIN_Q_HEADS * WIN_HEAD_DIM,
             WIN_KV_HEADS * WIN_HEAD_DIM, WIN_KV_HEADS * WIN_HEAD_DIM)
IN_OFFSETS = tuple(int(v) for v in np.cumsum((0,) + IN_WIDTHS))


def _mod_rows(mod_ref, r, tile, n_ctx):
    row = pl.program_id(1) * tile + lax.broadcasted_iota(jnp.int32, (tile, 1), 0)
    return jnp.where(row < n_ctx, mod_ref[0, 0, r:r + 1, :], mod_ref[0, 1, r:r + 1, :])


def _rope_lanes(x, cos, sa, sb, shift):
    return x * cos + pltpu.roll(x, 128 - shift, 1) * sa + pltpu.roll(x, shift, 1) * sb


def _mla_project(qa, kva, kr, qg, kvg, wq_ref, wk_ref, wv_ref, cos, sa, sb, qp_ref, qr_ref, k_ref, v_ref):
    def rms(x, gain):
        return (x * lax.rsqrt(jnp.mean(x * x, axis=-1, keepdims=True) + LN_EPS) * gain).astype(BF16)

    qn = rms(qa, qg)
    kvn = rms(kva, kvg)
    q = _bdot(qn, wq_ref[...]) * (MLA_SCALE * LOG2E)
    k = _bdot(kvn, wk_ref[...])
    v_ref[0] = _bdot(kvn, wv_ref[...]).astype(BF16)
    kr_rot = _rope_lanes(kr, cos, sa, sb, MLA_ROPE // 2)
    qp_ref[0] = q.astype(BF16)
    for h in range(MLA_HEADS):
        sl = slice(h * MLA_PAD, (h + 1) * MLA_PAD)
        qr_ref[0, :, sl] = _rope_lanes(q[:, sl], cos, sa, sb, MLA_ROPE // 2).astype(BF16)
        k_ref[0, :, sl] = (k[:, sl] + kr_rot).astype(BF16)


def _in_kernel(x_ref, mod_ref, w_ref, qg_ref, kvg_ref, wuq_ref, wuk_ref, wuv_ref, cos_ref, sa_ref, sb_ref,
               u_ref, wq_ref, wk_ref, wv_ref, gate_ref, qp_ref, qr_ref, k_ref, v_ref, *, n_ctx):
    tile = x_ref.shape[1]
    xn = _layer_norm(x_ref[0])
    h = (xn * (1.0 + _mod_rows(mod_ref, 1, tile, n_ctx)) + _mod_rows(mod_ref, 0, tile, n_ctx)).astype(BF16)
    widths = IN_WIDTHS + (gate_ref.shape[-1],)
    proj = lambda i: _bdot(h, w_ref[:, IN_OFFSETS[i]:IN_OFFSETS[i] + widths[i]])
    _mla_project(proj(1), proj(2), proj(3), qg_ref[...], kvg_ref[...], wuq_ref, wuk_ref, wuv_ref,
                 cos_ref[...], sa_ref[...], sb_ref[...], qp_ref, qr_ref, k_ref, v_ref)
    u_ref[0] = proj(0)
    wq_ref[0], wk_ref[0], wv_ref[0] = proj(4), proj(5), proj(6)
    gate_ref[0] = proj(7).astype(gate_ref.dtype)


_sigmoid = jax.nn.sigmoid


def _in_call(xall, mod, w_cat, qg, kvg, wuq, wuk, wuv, tabs, n_ctx):
    b, n, d = xall.shape
    tm = IN_TILE
    gate_w = w_cat.shape[1] - IN_OFFSETS[-1]
    hw = MLA_HEADS * MLA_PAD
    tok = lambda w: pl.BlockSpec((1, tm, w), lambda i, t: (i, t, 0))
    full = lambda a: pl.BlockSpec(a.shape, lambda i, t: (0,) * a.ndim)
    tab = pl.BlockSpec((tm, 128), lambda i, t: (t, 0))
    out_widths = (IN_WIDTHS[0],) + IN_WIDTHS[4:7] + (gate_w, hw, hw, hw, MLA_HEADS * MLA_V)
    out_dtypes = (F32,) * 4 + (BF16,) * 5
    return pl.pallas_call(
        functools.partial(_in_kernel, n_ctx=n_ctx),
        grid=(b, n // tm),
        in_specs=[tok(d), pl.BlockSpec((1, 2, 6, d), lambda i, t: (i, 0, 0, 0)),
                  pl.BlockSpec(w_cat.shape, lambda i, t: (0, 0), pipeline_mode=pl.Buffered(1)),
                  full(qg), full(kvg), full(wuq), full(wuk), full(wuv), tab, tab, tab],
        out_specs=[tok(w) for w in out_widths],
        out_shape=[jax.ShapeDtypeStruct((b, n, w), dt) for w, dt in zip(out_widths, out_dtypes)],
        compiler_params=_cparams(2),
    )(xall, mod, w_cat, qg, kvg, wuq, wuk, wuv, *tabs)


S5_LANE_GROUPS = 128 // S5_GROUP
S5_SCAN_GROUPS = 4
S5_PITCH_PAD = 8


def _s5_param_kernel(*refs):
    for g in range(refs[0].shape[2]):
        _s5_param_group(g, *refs)


def _s5_param_group(g, cre_ref, cim_ref, bre_ref, bim_ref, pr_ref, pi_ref, tz_ref, bc_ref, cc_ref, coef_ref):
    t = S5_CHUNK
    w = t * S5_GROUP
    nt = (((1,), (1,)), ((), ()))
    tz = None
    bcs, ccs, coefs = [], [], []
    for d in range(2):
        cre, cim = cre_ref[0, d, g], cim_ref[0, d, g]
        bre, bim = bre_ref[0, d, g], bim_ref[0, d, g]
        power = lambda k: (pr_ref[0, d, g, k:k + 1, :], pi_ref[0, d, g, k:k + 1, :])
        rt = []
        for k in range(t + 1):
            prk, pik = power(k)
            rt.append(jnp.concatenate([cre * prk - cim * pik, -(cre * pik + cim * prk)], axis=1))
        bt = jnp.concatenate([bre, bim], axis=1)
        zeros = jnp.zeros((S5_GROUP, w), F32)
        if d == 0:
            kt = lax.dot_general(bt, jnp.concatenate(rt[:t], axis=0), nt, precision=HIGHEST,
                                 preferred_element_type=F32)
            pad = jnp.concatenate([zeros, kt], axis=1)
            rows = [kt] + [pltpu.roll(pad, S5_GROUP * s, 1)[:, w:] for s in range(1, t)]
        else:
            kt = lax.dot_general(bt, jnp.concatenate(rt[t - 1::-1], axis=0), nt, precision=HIGHEST,
                                 preferred_element_type=F32)
            pad = jnp.concatenate([kt, zeros], axis=1)
            rows = [pltpu.roll(pad, 2 * w - S5_GROUP * (t - 1 - s), 1)[:, :w] for s in range(t - 1)] + [kt]
        tz_d = jnp.concatenate(rows, axis=0)
        tz = tz_d if tz is None else tz + tz_d
        bc_rows = []
        for s in range(t):
            prk, pik = power(t - 1 - s if d == 0 else s)
            br = bre * prk - bim * pik
            bi = bim * prk + bre * pik
            bc_rows.append(jnp.concatenate([br, bi, bi, br], axis=1))
        bcs.append(jnp.concatenate(bc_rows, axis=0))
        ccs.append(jnp.concatenate(rt[1:] if d == 0 else rt[t:0:-1], axis=0))
        er, ei = power(t)
        coefs += [jnp.concatenate([er, er], axis=1), jnp.concatenate([-ei, ei], axis=1),
                  jnp.concatenate([ei, -ei], axis=1)]
    tz_ref[0, g] = tz.astype(BF16)
    bc_ref[0, g] = jnp.concatenate(bcs, axis=1).astype(BF16)
    cc_ref[0, g] = jnp.concatenate(ccs, axis=1).astype(BF16)
    coef_ref[0, g] = jnp.concatenate(coefs + [jnp.zeros((2, 2 * S5_STATE), F32)], axis=0)


def _s5_param_call(p):
    t = S5_CHUNK
    f = lambda name: p[name].astype(F32)
    lam_re, lam_im = f('s5_lam_re'), f('s5_lam_im')
    depth = lam_re.shape[0]
    dt = jnp.exp(f('s5_log_dt'))[..., None]
    k = jnp.arange(t + 1, dtype=F32)[:, None]
    mag = jnp.exp((lam_re * dt)[..., None, :] * k)
    ang = (lam_im * dt)[..., None, :] * k
    pr, pi = mag * jnp.cos(ang), mag * jnp.sin(ang)
    ar, ai = pr[..., 1, :], pi[..., 1, :]
    den = lam_re * lam_re + lam_im * lam_im
    qr = (((ar - 1) * lam_re + ai * lam_im) / den)[..., None, :]
    qi = ((ai * lam_re - (ar - 1) * lam_im) / den)[..., None, :]
    b_re = jnp.swapaxes(f('s5_b_re'), -1, -2)
    b_im = jnp.swapaxes(f('s5_b_im'), -1, -2)
    bbr = qr * b_re - qi * b_im
    bbi = qr * b_im + qi * b_re
    g, hg, ps = S5_GROUPS, S5_GROUP, S5_STATE
    w = t * hg
    gs = S5_LANE_GROUPS
    small = lambda rows: pl.BlockSpec((1, 2, gs, rows, ps), lambda i, j: (i, 0, j, 0, 0))
    out = lambda cols: pl.BlockSpec((1, gs, w, cols), lambda i, j: (i, j, 0, 0))
    return pl.pallas_call(
        _s5_param_kernel,
        grid=(depth, g // gs),
        in_specs=[small(hg)] * 4 + [small(t + 1)] * 2,
        out_specs=[out(w), out(2 * w), out(w), pl.BlockSpec((1, gs, 8, 2 * ps), lambda i, j: (i, j, 0, 0))],
        out_shape=[jax.ShapeDtypeStruct((depth, g, w, w), BF16), jax.ShapeDtypeStruct((depth, g, w, 2 * w), BF16),
                   jax.ShapeDtypeStruct((depth, g, w, w), BF16), jax.ShapeDtypeStruct((depth, g, 8, 2 * ps), F32)],
        compiler_params=_cparams(2),
    )(f('s5_c_re'), f('s5_c_im'), bbr, bbi, pr, pi)


def _s5_kernel(u_ref, tz_ref, bc_ref, cc_ref, coef_ref, d_ref, y_ref, uy_ref, loc_ref, sp_ref, slab_ref,
               *, n_ctx, n_batch):
    ph, b = pl.program_id(1), pl.program_id(2)
    t, hg, ng = S5_CHUNK, S5_GROUP, S5_LANE_GROUPS
    ncc = n_ctx // t
    ncl = (u_ref.shape[1] - n_ctx) // t
    nc = ncc + ncl
    pitch = nc + S5_PITCH_PAD

    def to_chunk_rows(slabs):
        tr = [s.T for s in slabs]
        return [jnp.concatenate([x[g * hg:(g + 1) * hg] for x in tr], axis=0).T for g in range(ng)]

    def to_token_slabs(rows):
        tr = [r.T for r in rows]
        return [jnp.concatenate([x[tau * hg:(tau + 1) * hg] for x in tr], axis=0).T for tau in range(t)]

    base = pl.multiple_of(b * pitch, 8)
    cbase = pl.multiple_of(b * ncc, 8)

    @pl.when(ph == 0)
    def _():
        rows = to_chunk_rows([u_ref[0, pl.ds(n_ctx + tau, ncl, stride=t), :] for tau in range(t)])
        for g in range(ng):
            uy_ref[g, pl.ds(base + ncc, ncl), :] = rows[g]
            uy_ref[g, pl.ds(base + nc, S5_PITCH_PAD), :] = jnp.zeros((S5_PITCH_PAD, t * hg), F32)
        for tau in range(t):
            slab_ref[tau, pl.ds(cbase, ncc), :] = u_ref[0, pl.ds(tau, ncc, stride=t), :]

    @pl.when((ph == 1) & (b == 0))
    def _():
        rows = to_chunk_rows([slab_ref[tau] for tau in range(t)])
        for g in range(ng):
            for s in range(n_batch):
                uy_ref[g, s * pitch:s * pitch + ncc, :] = rows[g][s * ncc:(s + 1) * ncc]
        for part in range(ng // S5_SCAN_GROUPS):
            gs = [part * S5_SCAN_GROUPS + gl for gl in range(S5_SCAN_GROUPS)]
            for gl, g in enumerate(gs):
                ub = uy_ref[g].astype(BF16)
                loc = _bdot(ub, bc_ref[0, g])
                for q in range(4):
                    loc_ref[gl, q] = loc[:, q * 128:(q + 1) * 128]
                uy_ref[g] = _bdot(ub, tz_ref[0, g])
                for s in range(n_batch):
                    for d in range(2):
                        sp_ref[gl, d, s * pitch + nc:(s + 1) * pitch, :] = jnp.zeros((S5_PITCH_PAD, 128), F32)

            def coef(g, r):
                return jnp.broadcast_to(coef_ref[0, g, r:r + 1, :], (n_batch, 128))

            def step(i, carry):
                cb = jnp.where(i < ncc, ncc - 1 - i, nc + ncc - 1 - i)
                fwd = pl.ds(i, n_batch, stride=pitch)
                bwd = pl.ds(cb, n_batch, stride=pitch)
                out = []
                for gl, g in enumerate(gs):
                    v0f, v1f, v0b, v1b = carry[gl]
                    sp_ref[gl, 0, fwd, :] = v0f
                    sp_ref[gl, 1, bwd, :] = v0b
                    n0f = coef(g, 0) * v0f + coef(g, 1) * v1f + loc_ref[gl, 0, fwd, :]
                    n1f = coef(g, 0) * v1f + coef(g, 2) * v0f + loc_ref[gl, 1, fwd, :]
                    n0b = coef(g, 3) * v0b + coef(g, 4) * v1b + loc_ref[gl, 2, bwd, :]
                    n1b = coef(g, 3) * v1b + coef(g, 5) * v0b + loc_ref[gl, 3, bwd, :]
                    out.append((n0f, n1f, n0b, n1b))
                return tuple(out)

            z = jnp.zeros((n_batch, 128), F32)
            lax.fori_loop(0, nc, step, tuple((z, z, z, z) for _ in gs))
            for gl, g in enumerate(gs):
                sp = jnp.concatenate([sp_ref[gl, 0], sp_ref[gl, 1]], axis=1).astype(BF16)
                uy_ref[g] = uy_ref[g] + _dot_nt(sp, cc_ref[0, g])
        rows = [jnp.concatenate([uy_ref[g, s * pitch:s * pitch + ncc, :] for s in range(n_batch)], axis=0)
                for g in range(ng)]
        for tau, slab in enumerate(to_token_slabs(rows)):
            slab_ref[tau] = slab

    @pl.when(ph == 1)
    def _():
        slabs = to_token_slabs([uy_ref[g, pl.ds(base + ncc, ncl), :] for g in range(ng)])
        for tau in range(t):
            y_ref[0, pl.ds(n_ctx + tau, ncl, stride=t), :] = slabs[tau]
            y_ref[0, pl.ds(tau, ncc, stride=t), :] = slab_ref[tau, pl.ds(cbase, ncc), :]
        y_ref[0] = y_ref[0] + d_ref[0] * u_ref[0]


def _s5_call(layer, u, tz, bc, cc, coef, s5_d, n_ctx):
    b, n, width = u.shape
    t, ng = S5_CHUNK, S5_LANE_GROUPS
    rows = b * (n // t + S5_PITCH_PAD)
    assert (n - n_ctx) // t == 128 and b * (n_ctx // t) == 128
    wspec = lambda a: pl.BlockSpec((1, ng) + a.shape[2:], lambda g, ph, i: (layer, g, 0, 0))
    return pl.pallas_call(
        functools.partial(_s5_kernel, n_ctx=n_ctx, n_batch=b),
        grid=(width // 128, 2, b),
        in_specs=[pl.BlockSpec((1, n, 128), lambda g, ph, i: (i, 0, g)),
                  wspec(tz), wspec(bc), wspec(cc), wspec(coef),
                  pl.BlockSpec((1, 1, 128), lambda g, ph, i: (layer, 0, g))],
        out_specs=pl.BlockSpec((1, n, 128), lambda g, ph, i: (i * ph, 0, g)),
        out_shape=jax.ShapeDtypeStruct((b, n, width), F32),
        scratch_shapes=[pltpu.VMEM((ng, rows, t * S5_GROUP), F32),
                        pltpu.VMEM((S5_SCAN_GROUPS, 4, rows, 128), F32),
                        pltpu.VMEM((S5_SCAN_GROUPS, 2, rows, 128), F32),
                        pltpu.VMEM((t, 128, 128), F32)],
        compiler_params=_cparams(3),
    )(u, tz, bc, cc, coef, s5_d.astype(F32).reshape(s5_d.shape[0], 1, width))


def _lane_chunks(xs):
    return [x[:, i * 128:(i + 1) * 128] for x in xs for i in range(x.shape[1] // 128)]


def _row_max(scores, floor=None):
    mm = functools.reduce(jnp.maximum, _lane_chunks(scores))
    if floor is not None:
        mm = jnp.maximum(mm, floor)
    return jnp.max(mm, axis=-1, keepdims=True)


def _softmax_av(scores, values, sink=None):
    m = _row_max(scores, sink)
    ps = [jnp.exp2(s - m) for s in scores]
    ll = functools.reduce(jnp.add, _lane_chunks(ps))
    if sink is not None:
        lane = lax.broadcasted_iota(jnp.int32, sink.shape, 1)
        ll = ll + jnp.where(lane == 0, jnp.exp2(sink - m), 0.0)
    l = jnp.sum(ll, axis=-1, keepdims=True)
    o = functools.reduce(jnp.add, [_bdot(p.astype(BF16), v) for p, v in zip(ps, values)])
    return o / l


def _mla_attn_kernel(qp_ref, qr_ref, k_ref, v_ref, o_ref, *, n_ctx, n_ctx_tiles):
    heads = qp_ref.shape[-1] // MLA_PAD
    t = pl.program_id(2)
    group = 256 // MLA_V
    lane = lax.broadcasted_iota(jnp.int32, (1, 256), 1)

    def run(latent):
        all_scores = []
        for h in range(heads):
            sl = slice(h * MLA_PAD, (h + 1) * MLA_PAD)
            scores = [_dot_nt(qp_ref[0, :, sl], k_ref[0, :n_ctx, sl])]
            if latent:
                scores.append(_dot_nt(qr_ref[0, :, sl], k_ref[0, n_ctx:, sl]))
            all_scores.append(scores)
        probs = []
        for scores in all_scores:
            m = _row_max(scores)
            ps = [jnp.exp2(s - m) for s in scores]
            l = jnp.sum(functools.reduce(jnp.add, _lane_chunks(ps)), axis=-1, keepdims=True)
            probs.append(([p.astype(BF16) for p in ps], l))
        acc = [None] * (heads // group)
        for h, (ps, l) in enumerate(probs):
            blk, hh = divmod(h, group)
            cols = slice(blk * 256, (blk + 1) * 256)
            own = (lane >= hh * MLA_V) & (lane < (hh + 1) * MLA_V)
            zero = jnp.zeros((), BF16)
            values = [jnp.where(own, v_ref[0, :n_ctx, cols], zero)]
            if latent:
                values.append(jnp.where(own, v_ref[0, n_ctx:, cols], zero))
            o = functools.reduce(jnp.add, [_bdot(p, v) for p, v in zip(ps, values)]) / l
            acc[blk] = o if acc[blk] is None else acc[blk] + o
        for blk, o in enumerate(acc):
            o_ref[0, :, blk * 256:(blk + 1) * 256] = o.astype(o_ref.dtype)

    pl.when(t < n_ctx_tiles)(lambda: run(False))
    pl.when(t >= n_ctx_tiles)(lambda: run(True))


def _mla_attn_call(qp, qr, k, v, n_ctx):
    b, n, _ = qp.shape
    tq = TOKEN_TILE
    hp = MLA_HEADS_PER_STEP
    qspec = pl.BlockSpec((1, tq, hp * MLA_PAD), lambda i, h, t: (i, t, h))
    return pl.pallas_call(
        functools.partial(_mla_attn_kernel, n_ctx=n_ctx, n_ctx_tiles=n_ctx // tq),
        grid=(b, MLA_HEADS // hp, n // tq),
        in_specs=[qspec, qspec,
                  pl.BlockSpec((1, n, hp * MLA_PAD), lambda i, h, t: (i, 0, h)),
                  pl.BlockSpec((1, n, hp * MLA_V), lambda i, h, t: (i, 0, h))],
        out_specs=pl.BlockSpec((1, tq, hp * MLA_V), lambda i, h, t: (i, t, h)),
        out_shape=jax.ShapeDtypeStruct((b, n, MLA_HEADS * MLA_V), BF16),
        compiler_params=_cparams(3),
    )(qp, qr, k, v)


def _win_kernel(sink_ref, q_ref, k_ref, v_ref, cos_ref, sa_ref, sb_ref, o_ref, *, n_ctx_blocks, n_blocks):
    j = pl.program_id(1)
    hd = WIN_HEAD_DIM
    half = hd // 2
    lane = lax.broadcasted_iota(jnp.int32, (1, 128), 1)
    lo = jnp.where(lane < hd, 1.0, 0.0)
    hi = 1.0 - lo
    upper_rows = lax.broadcasted_iota(jnp.int32, (2 * BLOCK, 128), 0) < BLOCK

    def lane_halves(x):
        xr = pltpu.roll(x, hd, 1)
        return {(0, 0): (x * lo).astype(BF16), (0, 1): (xr * hi).astype(BF16),
                (1, 0): (xr * lo).astype(BF16), (1, 1): (x * hi).astype(BF16)}

    def attend(out_rows, queries, keys, values, masks):
        all_scores = {}
        for kh in range(WIN_KV_HEADS):
            stacked = [jnp.concatenate([qs[:, (2 * kh) * 128:(2 * kh + 1) * 128],
                                        qs[:, (2 * kh + 1) * 128:(2 * kh + 2) * 128]], axis=0).astype(BF16)
                       for qs in queries]
            for par in range(2):
                scores = []
                for qst, ks, msk in zip(stacked, keys, masks):
                    s = _dot_nt(qst, ks[(kh, par)])
                    scores.append(s if msk is None else jnp.where(msk, s, NEG_INF))
                all_scores[(kh, par)] = scores
        for kh in range(WIN_KV_HEADS):
            acc = None
            for par in range(2):
                sink = jnp.where(upper_rows, sink_ref[4 * kh + par], sink_ref[4 * kh + 2 + par]) * LOG2E
                o = _softmax_av(all_scores[(kh, par)], [vs[(kh, par)] for vs in values], sink)
                acc = o if acc is None else acc + o
            o_ref[0, out_rows, (2 * kh) * 128:(2 * kh + 1) * 128] = acc[:BLOCK].astype(o_ref.dtype)
            o_ref[0, out_rows, (2 * kh + 1) * 128:(2 * kh + 2) * 128] = acc[BLOCK:].astype(o_ref.dtype)

    n_ctx = n_ctx_blocks * BLOCK
    kctx = lane_halves(k_ref[0, :n_ctx, :])
    vctx = lane_halves(v_ref[0, :n_ctx, :])
    subs = [slice(s * BLOCK, (s + 1) * BLOCK) for s in range(q_ref.shape[1] // BLOCK)]

    def ctx_path():
        for rows in subs:
            attend(rows, [q_ref[0, rows, :] * (WIN_SCALE * LOG2E)], [kctx], [vctx], [None])

    def lat_path():
        for s, rows in enumerate(subs):
            lat_block(j * len(subs) + s - n_ctx_blocks, rows)

    def lat_block(blk, out_rows):
        q = q_ref[0, out_rows, :] * (WIN_SCALE * LOG2E)
        band = [pl.ds(pl.multiple_of((n_ctx_blocks + jnp.clip(blk + d, 0, n_blocks - 1)) * BLOCK, BLOCK), BLOCK)
                for d in (-1, 0, 1)]
        rope = lambda x, rows: _rope_lanes(x, cos_ref[rows, :], sa_ref[rows, :], sb_ref[rows, :], half)
        q_rot = jnp.concatenate([rope(q[:, c * 128:(c + 1) * 128], band[1]) for c in range(q.shape[1] // 128)],
                                axis=-1)
        kband = jnp.concatenate([rope(k_ref[0, rows, :], rows) for rows in band], axis=0)
        vband = jnp.concatenate([v_ref[0, rows, :] for rows in band], axis=0)
        r = lax.broadcasted_iota(jnp.int32, (2 * BLOCK, 3 * BLOCK), 0) % BLOCK
        c = lax.broadcasted_iota(jnp.int32, (2 * BLOCK, 3 * BLOCK), 1)
        first = jnp.where(blk > 0, 0, BLOCK)
        last = jnp.where(blk < n_blocks - 1, 3 * BLOCK, 2 * BLOCK)
        valid = (jnp.abs(c - BLOCK - r) <= WINDOW) & (c >= first) & (c < last)
        attend(out_rows, [q_rot, q], [lane_halves(kband), kctx], [lane_halves(vband), vctx], [valid, None])

    pl.when(j * len(subs) < n_ctx_blocks)(ctx_path)
    pl.when(j * len(subs) >= n_ctx_blocks)(lat_path)


def _win_call(sink, wq, wk, wv, tabs, n_ctx):
    b, n, _ = wq.shape
    ncb = n_ctx // BLOCK
    nb = n // BLOCK
    kvw = WIN_KV_HEADS * WIN_HEAD_DIM
    per = WIN_BLOCKS_PER_STEP
    assert ncb % per == 0 and nb % per == 0
    cur = lambda i, j: (i, j, 0)
    kv = pl.BlockSpec((1, n, kvw), lambda i, j: (i, 0, 0))
    tab = pl.BlockSpec((n, 128), lambda i, j: (0, 0))
    return pl.pallas_call(
        functools.partial(_win_kernel, n_ctx_blocks=ncb, n_blocks=nb - ncb),
        grid=(b, nb // per),
        in_specs=[pl.BlockSpec(memory_space=pltpu.SMEM),
                  pl.BlockSpec((1, per * BLOCK, WIN_Q_HEADS * WIN_HEAD_DIM), cur), kv, kv, tab, tab, tab],
        out_specs=pl.BlockSpec((1, per * BLOCK, WIN_Q_HEADS * WIN_HEAD_DIM), cur),
        out_shape=jax.ShapeDtypeStruct((b, n, WIN_Q_HEADS * WIN_HEAD_DIM), BF16),
        compiler_params=_cparams(2),
    )(sink, wq, wk, wv, *tabs)


def _merge_kernel(x_ref, s5_ref, mla_ref, win_ref, gate_ref, mod_ref, wglu_ref, bglu_ref, wbr_ref, wout_ref,
                  g1_ref, b1_ref, wr_ref, x1_ref, h2_ref, lg_ref, *, alpha, n_ctx):
    d = x_ref.shape[-1]
    proj = {1: _bdot(mla_ref[0], wbr_ref[1]), 2: _bdot(win_ref[0], wbr_ref[2])}
    g = jax.nn.gelu(s5_ref[0])
    s5o = g * _sigmoid(_bdot(g.astype(BF16), wglu_ref[...]) + bglu_ref[...])
    proj[0] = _bdot(s5o.astype(BF16), wbr_ref[0])
    mix = None
    for kk in (1, 2, 0):
        term = _sigmoid(gate_ref[0, :, kk * d:(kk + 1) * d].astype(F32)) * proj[kk]
        mix = term if mix is None else mix + term
    y = _bdot(mix.astype(BF16), wout_ref[...])
    mod = lambda r: _mod_rows(mod_ref, r, x_ref.shape[1], n_ctx)
    x1 = _layer_norm(alpha * x_ref[0] + mod(2) * y) * g1_ref[...] + b1_ref[...]
    x1_ref[0] = x1
    h2 = (_layer_norm(x1) * (1.0 + mod(4)) + mod(3)).astype(BF16)
    h2_ref[0] = h2
    lg_ref[0] = _dot_nt(wr_ref[...], h2)


def _merge_call(xall, s5y, mla_o, win_o, gates, mod, wglu, bglu, wbr, wout, g1, b1, wr_t, n_ctx, alpha):
    b, n, d = xall.shape
    tm = MERGE_TILE
    tok = lambda w: pl.BlockSpec((1, tm, w), lambda i, t: (i, t, 0))
    full = lambda a: pl.BlockSpec(a.shape, lambda i, t: (0,) * a.ndim)
    return pl.pallas_call(
        functools.partial(_merge_kernel, alpha=alpha, n_ctx=n_ctx),
        grid=(b, n // tm),
        in_specs=[tok(d), tok(BRANCH_WIDTH), tok(BRANCH_WIDTH), tok(BRANCH_WIDTH), tok(N_BRANCH * d),
                  pl.BlockSpec((1, 2, 6, d), lambda i, t: (i, 0, 0, 0)),
                  full(wglu), full(bglu), full(wbr), full(wout), full(g1), full(b1), full(wr_t)],
        out_specs=[tok(d), tok(d), pl.BlockSpec((1, N_EXPERTS, tm), lambda i, t: (i, 0, t))],
        out_shape=[jax.ShapeDtypeStruct((b, n, d), F32), jax.ShapeDtypeStruct((b, n, d), BF16),
                   jax.ShapeDtypeStruct((b, N_EXPERTS, n), F32)],
        compiler_params=_cparams(2),
    )(xall, s5y, mla_o, win_o, gates, mod, wglu, bglu, wbr, wout, g1, b1, wr_t)


def _excl_cumsum_lanes(m):
    rows, n = m.shape
    r = lax.broadcasted_iota(jnp.int32, (128, 128), 0)
    c = lax.broadcasted_iota(jnp.int32, (128, 128), 1)
    tri = jnp.where(r < c, 1.0, 0.0).astype(BF16)
    off = jnp.zeros((rows, 1), F32)
    outs, offs = [], []
    for jb in range(n // 128):
        blk = m[:, jb * 128:(jb + 1) * 128]
        offs.append(off)
        outs.append(_bdot(blk.astype(BF16), tri) + off)
        off = off + jnp.sum(blk, axis=1, keepdims=True)
    return jnp.concatenate(outs, axis=1), offs + [off]


def _topk_slots(affs, caps):
    bits = [pltpu.bitcast(aff, jnp.int32) for aff in affs]

    def body(i, thrs):
        out = []
        for b, cap, thr in zip(bits, caps, thrs):
            cand = thr | (jnp.int32(1) << (30 - i))
            cnt = jnp.sum(jnp.where(b >= cand, 1.0, 0.0), axis=1, keepdims=True)
            out.append(jnp.where(cnt >= cap, cand, thr))
        return tuple(out)

    zero = jnp.zeros((affs[0].shape[0], 1), jnp.int32)
    thrs = lax.fori_loop(0, 31, body, tuple(zero for _ in affs))
    results = []
    for b, cap, thr in zip(bits, caps, thrs):
        gt = jnp.where(b > thr, 1.0, 0.0)
        eq = jnp.where(b == thr, 1.0, 0.0)
        need = cap - jnp.sum(gt, axis=1, keepdims=True)
        sel = gt + eq * jnp.where(_excl_cumsum_lanes(eq)[0] < need, 1.0, 0.0)
        rank, offs = _excl_cumsum_lanes(sel)
        results.append((jnp.where(sel > 0.5, rank, -1.0).astype(jnp.int32), offs[::MOE_TILE // 128]))
    return results


def _route_kernel(lg_ref, slot_ref, aff_ref, bnd_ref, *, n_ctx, cap_ctx, cap_lat):
    lg = lg_ref[0]
    m = jnp.max(lg, axis=0, keepdims=True)
    ex = jnp.exp(lg - m)
    aff = ex / jnp.sum(ex, axis=0, keepdims=True)
    aff_ref[0] = aff
    (slots_ctx, _), (slots, counts) = _topk_slots([aff[:, :n_ctx], aff[:, n_ctx:]], [cap_ctx, cap_lat])
    slot_ref[0, :, :n_ctx] = slots_ctx
    slot_ref[0, :, n_ctx:] = slots
    lane = lax.broadcasted_iota(jnp.int32, bnd_ref.shape[1:], 1)
    bnd = jnp.zeros(bnd_ref.shape[1:], F32)
    for k, cnt in enumerate(counts):
        bnd = jnp.where(lane == k, cnt, bnd)
    bnd_ref[0] = bnd.astype(jnp.int32)


def _route_call(logits_t, n_ctx, cap_ctx, cap_lat):
    b, e, n = logits_t.shape
    assert (n - n_ctx) % MOE_TILE == 0 and (n - n_ctx) // MOE_TILE < MOE_BOUNDS
    spec = pl.BlockSpec((1, e, n), lambda i: (i, 0, 0))
    return pl.pallas_call(
        functools.partial(_route_kernel, n_ctx=n_ctx, cap_ctx=cap_ctx, cap_lat=cap_lat),
        grid=(b,),
        in_specs=[spec],
        out_specs=[spec, spec, pl.BlockSpec((1, e, MOE_BOUNDS), lambda i: (i, 0, 0))],
        out_shape=[jax.ShapeDtypeStruct((b, e, n), jnp.int32), jax.ShapeDtypeStruct((b, e, n), F32),
                   jax.ShapeDtypeStruct((b, e, MOE_BOUNDS), jnp.int32)],
        compiler_params=_cparams(1),
    )(logits_t)


def _gather_kernel(bnd_ref, slot_ref, aff_ref, h_ref, xs_ref, gate_ref, xl_ref, gl_ref, *, n_ctx, cap_ctx):
    for j in range(xs_ref.shape[1]):
        _gather_expert(j, bnd_ref, slot_ref, aff_ref, h_ref, xs_ref, gate_ref, xl_ref.at[j], gl_ref.at[j],
                       n_ctx=n_ctx, cap_ctx=cap_ctx)


def _gather_expert(j, bnd_ref, slot_ref, aff_ref, h_ref, xs_ref, gate_ref, xl_ref, gl_ref, *, n_ctx, cap_ctx):
    cap_lat = xs_ref.shape[2] - cap_ctx
    slot = slot_ref[0, j]
    aff = aff_ref[0, j]
    n = h_ref.shape[1]
    ib, ie = pl.program_id(0), pl.program_id(1) * xs_ref.shape[1] + j
    xl_ref[...] = jnp.zeros(xl_ref.shape, F32)
    gl_ref[...] = jnp.zeros(gl_ref.shape, F32)
    tiles = range((n - n_ctx) // MOE_TILE)
    firsts = [(bnd_ref[ib, ie, kt] // 16) * 16 for kt in tiles]

    def window(kt, start):
        tok = slice(n_ctx + kt * MOE_TILE, n_ctx + (kt + 1) * MOE_TILE)
        rows = start + lax.broadcasted_iota(jnp.int32, (MOE_WINDOW, MOE_TILE), 0)
        hit = slot[:, tok] == rows
        picked = jnp.sum(jnp.where(hit, aff[:, tok], 0.0), axis=1, keepdims=True)
        return _bdot(jnp.where(hit, 1.0, 0.0).astype(BF16), h_ref[0, tok, :]), picked

    def add_window(kt, start, parts=None):
        start = pl.multiple_of(start, 16)
        rows, picked = parts if parts is not None else window(kt, start)
        xl_ref[pl.ds(start, MOE_WINDOW), :] += rows
        gl_ref[pl.ds(start, MOE_WINDOW), :] += jnp.broadcast_to(picked, (MOE_WINDOW, 128))

    first_windows = [window(kt, firsts[kt]) for kt in tiles]
    for kt in tiles:
        add_window(kt, firsts[kt], first_windows[kt])
    for kt in tiles:
        def more(w, carry, kt=kt):
            add_window(kt, firsts[kt] + (w + 1) * MOE_WINDOW)
            return carry

        n_win = (bnd_ref[ib, ie, kt + 1] - firsts[kt] + MOE_WINDOW - 1) // MOE_WINDOW
        lax.fori_loop(0, jnp.maximum(n_win - 1, 0), more, 0)
    iota = lax.broadcasted_iota(jnp.int32, (cap_ctx, n_ctx), 0)
    hit = slot[:, :n_ctx] == iota
    xc = _bdot(jnp.where(hit, 1.0, 0.0).astype(BF16), h_ref[0, :n_ctx, :])
    gc = jnp.sum(jnp.where(hit, aff[:, :n_ctx], 0.0), axis=1, keepdims=True)
    xs_ref[0, j, :cap_lat] = xl_ref[:cap_lat].astype(BF16)
    xs_ref[0, j, cap_lat:] = xc.astype(BF16)
    gate_ref[0, j, :cap_lat] = gl_ref[:cap_lat]
    gate_ref[0, j, cap_lat:] = jnp.broadcast_to(gc, (cap_ctx, 128))


def _gather_call(bounds, slot, aff, h2, n_ctx, cap_ctx, cap_lat):
    b, e, n = slot.shape
    d = h2.shape[-1]
    cap = cap_lat + cap_ctx
    ne = GATHER_EXPERTS
    assert e % ne == 0
    row =pl.BlockSpec((1, ne, 1, n), lambda ib, ie: (ib, ie, 0, 0))
    return pl.pallas_call(
        functools.partial(_gather_kernel, n_ctx=n_ctx, cap_ctx=cap_ctx),
        grid=(b, e // ne),
        in_specs=[pl.BlockSpec(memory_space=pltpu.SMEM), row, row, pl.BlockSpec((1, n, d), lambda ib, ie: (ib, 0, 0))],
        out_specs=[pl.BlockSpec((1, ne, cap, d), lambda ib, ie: (ib, ie, 0, 0)),
                   pl.BlockSpec((1, ne, cap, 128), lambda ib, ie: (ib, ie, 0, 0))],
        out_shape=[jax.ShapeDtypeStruct((b, e, cap, d), BF16), jax.ShapeDtypeStruct((b, e, cap, 128), F32)],
        scratch_shapes=[pltpu.VMEM((ne, cap_lat + MOE_WINDOW, d), F32),
                        pltpu.VMEM((ne, cap_lat + MOE_WINDOW, 128), F32)],
        compiler_params=_cparams(2),
    )(bounds, slot.reshape(b, e, 1, n), aff.reshape(b, e, 1, n), h2)


def _ffn_kernel(xs_ref, gate_ref, wg_ref, wu_ref, wd_ref, yl_ref, yc_ref, wg_s, wu_s, wd_s):
    ns, _, cap, d = xs_ref.shape
    cap_ctx = yc_ref.shape[2]
    cap_lat = cap - cap_ctx

    @pl.when(pl.program_id(1) == 0)
    def _():
        wg_s[...] = wg_ref[0, 0].astype(BF16)
        wu_s[...] = wu_ref[0, 0].astype(BF16)
        wd_s[...] = wd_ref[0, 0].astype(BF16)

    au = [(_bdot(xs_ref[s, 0], wg_s[...]), _bdot(xs_ref[s, 0], wu_s[...])) for s in range(ns)]
    for s, (a, u) in enumerate(au):
        hm = (a * _sigmoid(a) * u).astype(BF16)
        y = _bdot(hm, wd_s[...]) * gate_ref[s, 0, :, 0:1]
        yl_ref[s, 0, :cap_lat] = y[:cap_lat].astype(yl_ref.dtype)
        yl_ref[s, 0, cap_lat:] = jnp.zeros((MOE_WINDOW, d), yl_ref.dtype)
        yc_ref[s, 0] = y[cap_lat:].astype(yc_ref.dtype)


def _ffn_call(layer, xs, gate, w_gate, w_up, w_down, cap_ctx):
    b, e, cap, d = xs.shape
    f = w_gate.shape[-1]
    ns = FFN_SAMPLES
    assert b % ns == 0
    rows_lat = cap - cap_ctx + MOE_WINDOW
    tok = lambda rows, w: pl.BlockSpec((ns, 1, rows, w), lambda ie, j: (j, ie, 0, 0))
    return pl.pallas_call(
        _ffn_kernel,
        grid=(e, b // ns),
        in_specs=[tok(cap, d), tok(cap, 128),
                  pl.BlockSpec((1, 1, d, f), lambda ie, j: (layer, ie, 0, 0)),
                  pl.BlockSpec((1, 1, d, f), lambda ie, j: (layer, ie, 0, 0)),
                  pl.BlockSpec((1, 1, f, d), lambda ie, j: (layer, ie, 0, 0))],
        out_specs=[tok(rows_lat, d), tok(cap_ctx, d)],
        out_shape=[jax.ShapeDtypeStruct((b, e, rows_lat, d), BF16), jax.ShapeDtypeStruct((b, e, cap_ctx, d), BF16)],
        scratch_shapes=[pltpu.VMEM((d, f), BF16), pltpu.VMEM((d, f), BF16), pltpu.VMEM((f, d), BF16)],
        compiler_params=_cparams(2),
    )(xs, gate, w_gate, w_up, w_down)


def _combine_kernel(bnd_ref, slot_ref, yl_ref, yc_ref, x1_ref, mod_ref, g2_ref, b2_ref, o_ref, fl_ref,
                    *, n_ctx_tiles, alpha):
    ib, t = pl.program_id(0), pl.program_id(1)
    tm = x1_ref.shape[1]
    slot = slot_ref[0]
    win = MOE_WINDOW

    def finish(fl):
        x1 = x1_ref[0]
        o_ref[0] = _layer_norm(alpha * x1 + mod_ref[0, 0, 5:6, :] * fl) * g2_ref[...] + b2_ref[...]

    def onehot(e, first, width):
        iota = lax.broadcasted_iota(jnp.int32, (tm, width), 1)
        return jnp.where(slot[:, e:e + 1] - first == iota, 1.0, 0.0).astype(BF16)

    def ctx_path():
        cap = yc_ref.shape[2]
        fl = None
        for e in range(N_EXPERTS):
            term = _bdot(onehot(e, 0, cap), yc_ref[0, e])
            fl = term if fl is None else fl + term
        finish(fl)

    def lat_path():
        kt = t - n_ctx_tiles
        firsts = [pl.multiple_of((bnd_ref[ib, e, kt] // 16) * 16, 16) for e in range(N_EXPERTS)]
        lane = lax.broadcasted_iota(jnp.int32, (tm, 2 * win), 1)
        pieces, ywins = [], []
        for e in range(0, N_EXPERTS, 2):
            rel = jnp.where(lane < win, slot[:, e:e + 1] - firsts[e], slot[:, e + 1:e + 2] - firsts[e + 1] + win)
            pieces.append(jnp.where(rel == lane, 1.0, 0.0).astype(BF16))
            ywins += [yl_ref[0, e, pl.ds(firsts[e], win), :], yl_ref[0, e + 1, pl.ds(firsts[e + 1], win), :]]
        fl_ref[...] = _bdot(jnp.concatenate(pieces, axis=1), jnp.concatenate(ywins, axis=0))
        for e in range(N_EXPERTS):
            def window(w, carry, e=e):
                first = pl.multiple_of(firsts[e] + (w + 1) * win, 16)
                fl_ref[...] += _bdot(onehot(e, first, win), yl_ref[0, e, pl.ds(first, win), :])
                return carry

            n_win = (bnd_ref[ib, e, kt + 1] - firsts[e] + win - 1) // win
            lax.fori_loop(0, jnp.maximum(n_win - 1, 0), window, 0)
        finish(fl_ref[...])

    pl.when(t < n_ctx_tiles)(ctx_path)
    pl.when(t >= n_ctx_tiles)(lat_path)


def _combine_call(bounds, slot_t, yl, yc, x1, mod, g2, b2, n_ctx_tiles, alpha, latent_only):
    b, n, d = x1.shape
    tm = MOE_TILE
    e = N_EXPERTS
    full = lambda a: pl.BlockSpec(a.shape, lambda i, t: (0,) * a.ndim)
    skip = n_ctx_tiles if latent_only else 0
    return pl.pallas_call(
        functools.partial(_combine_kernel, n_ctx_tiles=n_ctx_tiles, alpha=alpha),
        grid=(b, n // tm),
        scratch_shapes=[pltpu.VMEM((tm, d), F32)],
        in_specs=[pl.BlockSpec(memory_space=pltpu.SMEM),
                  pl.BlockSpec((1, tm, e), lambda i, t: (i, t, 0)),
                  pl.BlockSpec((1,) + yl.shape[1:], lambda i, t: (i, 0, 0, 0)),
                  pl.BlockSpec((1,) + yc.shape[1:], lambda i, t: (i, 0, 0, 0)),
                  pl.BlockSpec((1, tm, d), lambda i, t: (i, t, 0)),
                  pl.BlockSpec((1, 1, 6, d), lambda i, t: (i, jnp.where(t < n_ctx_tiles, 0, 1), 0, 0)),
                  full(g2), full(b2)],
        out_specs=pl.BlockSpec((1, tm, d), lambda i, t: (i, jnp.maximum(t - skip, 0), 0)),
        out_shape=jax.ShapeDtypeStruct((b, n - skip * tm, d), F32),
        compiler_params=_cparams(2),
    )(bounds, slot_t, yl, yc, x1, mod, g2, b2)


def _rope_tables(n_ctx, seq, head_dim, lane_offset):
    half = head_dim // 2
    nf = head_dim // 4
    t = jnp.arange(seq, dtype=F32)
    row = jnp.floor(t / GRID_W)
    col = t - row * GRID_W
    freqs = ROPE_BASE ** (-jnp.arange(nf, dtype=F32) / nf)
    ang = jnp.concatenate([row[:, None] * freqs, col[:, None] * freqs], axis=-1)
    cos, sin = jnp.cos(ang), jnp.sin(ang)
    zeros = jnp.zeros_like(sin)
    n_heads = (128 - lane_offset) // head_dim if lane_offset == 0 else 1
    c = jnp.concatenate([jnp.ones((seq, lane_offset), F32)] + [cos, cos] * n_heads, axis=-1)
    sa = jnp.concatenate([jnp.zeros((seq, lane_offset), F32)] + [-sin, zeros] * n_heads, axis=-1)
    sb = jnp.concatenate([jnp.zeros((seq, lane_offset), F32)] + [zeros, sin] * n_heads, axis=-1)
    pad = 128 - c.shape[1]
    c = jnp.pad(c, ((n_ctx, 0), (0, pad)), constant_values=1.0)
    sa = jnp.pad(sa, ((n_ctx, 0), (0, pad)))
    sb = jnp.pad(sb, ((n_ctx, 0), (0, pad)))
    return c, sa, sb


def _layer_weights(i, p):
    d = p['w_in'].shape[1]
    pts = np.cumsum((S5_WIDTH, MLA_Q_RANK, MLA_KV_RANK, MLA_ROPE, WIN_Q_HEADS * WIN_HEAD_DIM,
                     WIN_KV_HEADS * WIN_HEAD_DIM, WIN_KV_HEADS * WIN_HEAD_DIM))
    cols = jnp.split(p['w_in'][i], [int(v) for v in pts], axis=1)
    kr = jnp.pad(cols[3], ((0, 0), (MLA_NOPE, MLA_PAD - MLA_NOPE - MLA_ROPE)))
    w_cat = jnp.concatenate([cols[0], cols[1], cols[2], kr, cols[4], cols[5], cols[6], cols[7]], axis=1)
    dq = MLA_NOPE + MLA_ROPE
    wq = p['mla_w_uq'][i].reshape(MLA_Q_RANK, MLA_HEADS, dq)
    wq = jnp.pad(wq, ((0, 0), (0, 0), (0, MLA_PAD - dq))).reshape(MLA_Q_RANK, MLA_HEADS * MLA_PAD)
    wkv = p['mla_w_ukv'][i].reshape(MLA_KV_RANK, MLA_HEADS, MLA_NOPE + MLA_V)
    wk = jnp.pad(wkv[:, :, :MLA_NOPE], ((0, 0), (0, 0), (0, MLA_PAD - MLA_NOPE)))
    wk = wk.reshape(MLA_KV_RANK, MLA_HEADS * MLA_PAD)
    wv = wkv[:, :, MLA_NOPE:].reshape(MLA_KV_RANK, MLA_HEADS * MLA_V)
    row = lambda a: a[i].astype(F32).reshape(1, -1)
    return dict(
        w_cat=w_cat.astype(BF16), wq=wq.astype(BF16), wk=wk.astype(BF16), wv=wv.astype(BF16),
        qg=row(p['mla_q_norm']), kvg=row(p['mla_kv_norm']),
        wglu=p['s5_w_glu'][i].astype(BF16), bglu=row(p['s5_b_glu']),
        sink=p['win_sink'][i].astype(F32),
        wbr=p['w_branch'][i].astype(BF16), wout=p['w_out'][i].astype(BF16),
        g1=row(p['ln1_g']), b1=row(p['ln1_b']), g2=row(p['ln2_g']), b2=row(p['ln2_b']),
        wr_t=p['w_router'][i].T.astype(BF16),
    )


def _forward(p):
    x, c, ctx, c_ctx = p['x'], p['c'], p['ctx'], p['c_ctx']
    b, seq, d = x.shape
    n_ctx = ctx.shape[1]
    depth = p['w_ada'].shape[0]
    assert b == 8 and seq % TOKEN_TILE == 0 and n_ctx % TOKEN_TILE == 0 and seq % GRID_W == 0
    alpha = float((2 * depth) ** 0.25)
    n_ctx_tiles = n_ctx // TOKEN_TILE
    cap_lat = CAPACITY_FACTOR * seq // N_EXPERTS
    cap_ctx = CAPACITY_FACTOR * n_ctx // N_EXPERTS

    cond = jnp.concatenate([c, c_ctx[None], jnp.zeros((16 - b - 1, d), F32)], axis=0)
    mods = _ada_call(cond, p['w_ada'], p['b_ada'])
    mods = mods.reshape(depth, 16, 6, d)
    tabs_mla = _rope_tables(n_ctx, seq, MLA_ROPE, MLA_NOPE)
    tabs_win = _rope_tables(n_ctx, seq, WIN_HEAD_DIM, 0)
    s5w = _s5_param_call(p)

    xall = jnp.concatenate([ctx, x], axis=1)
    for i in range(depth):
        w = _layer_weights(i, p)
        mod = jnp.stack([jnp.broadcast_to(mods[i, b], (b, 6, d)), mods[i, :b]], axis=1)
        u, wq, wk, wv, gates, qp, qr, kk, vv = _in_call(xall, mod, w['w_cat'], w['qg'], w['kvg'], w['wq'], w['wk'],
                                                        w['wv'], tabs_mla, n_ctx)
        s5y = _s5_call(i, u, *s5w, p['s5_d'], n_ctx)
        mla_o = _mla_attn_call(qp, qr, kk, vv, n_ctx)
        win_o = _win_call(w['sink'], wq, wk, wv, tabs_win, n_ctx)
        x1, h2, logits_t = _merge_call(xall, s5y, mla_o, win_o, gates, mod, w['wglu'], w['bglu'], w['wbr'],
                                       w['wout'], w['g1'], w['b1'], w['wr_t'], n_ctx, alpha)
        slot, aff, bounds = _route_call(logits_t, n_ctx, cap_ctx, cap_lat)
        xs, gate = _gather_call(bounds, slot, aff, h2, n_ctx, cap_ctx, cap_lat)
        yl, yc = _ffn_call(i, xs, gate, p['w_gate'], p['w_up'], p['w_down'], cap_ctx)
        slot_t = jnp.swapaxes(slot, 1, 2)
        xall = _combine_call(bounds, slot_t, yl, yc, x1, mod, w['g2'], w['b2'], n_ctx_tiles, alpha,
                             latent_only=(i == depth - 1))
    return xall


def kernel(x, c, ctx, c_ctx, w_ada, b_ada, w_in, s5_lam_re, s5_lam_im, s5_log_dt, s5_b_re, s5_b_im, s5_c_re, s5_c_im, s5_d, s5_w_glu, s5_b_glu, mla_q_norm, mla_w_uq, mla_kv_norm, mla_w_ukv, win_sink, w_branch, w_out, ln1_g, ln1_b, ln2_g, ln2_b, w_router, w_gate, w_up, w_down):
    return _forward(dict(
        x=x, c=c, ctx=ctx, c_ctx=c_ctx, w_ada=w_ada, b_ada=b_ada, w_in=w_in, s5_lam_re=s5_lam_re,
        s5_lam_im=s5_lam_im, s5_log_dt=s5_log_dt, s5_b_re=s5_b_re, s5_b_im=s5_b_im, s5_c_re=s5_c_re,
        s5_c_im=s5_c_im, s5_d=s5_d, s5_w_glu=s5_w_glu, s5_b_glu=s5_b_glu, mla_q_norm=mla_q_norm,
        mla_w_uq=mla_w_uq, mla_kv_norm=mla_kv_norm, mla_w_ukv=mla_w_ukv, win_sink=win_sink, w_branch=w_branch,
        w_out=w_out, ln1_g=ln1_g, ln1_b=ln1_b, ln2_g=ln2_g, ln2_b=ln2_b, w_router=w_router, w_gate=w_gate,
        w_up=w_up, w_down=w_down))
```

```python
import functools
import math

import jax
import jax.numpy as jnp
import numpy as np
from jax import lax
from jax.experimental import pallas as pl
from jax.experimental.pallas import tpu as pltpu

F32 = jnp.float32
BF16 = jnp.bfloat16
HIGHEST = lax.Precision.HIGHEST

GRID_W = 64
S5_WIDTH = 512
S5_GROUP = 16
S5_GROUPS = S5_WIDTH // S5_GROUP
S5_STATE = 64
S5_CHUNK = 16
MLA_HEADS = 8
MLA_NOPE = 64
MLA_ROPE = 32
MLA_V = 64
MLA_Q_RANK = 384
MLA_KV_RANK = 256
MLA_PAD = 128
MLA_HEADS_PER_STEP = 8
MLA_SCALE = (MLA_NOPE + MLA_ROPE) ** -0.5
WIN_Q_HEADS = 8
WIN_KV_HEADS = 2
WIN_HEAD_DIM = 64
WINDOW = 128
BLOCK = 128
WIN_SCALE = WIN_HEAD_DIM ** -0.5
N_BRANCH = 3
BRANCH_WIDTH = 512
N_EXPERTS = 16
CAPACITY_FACTOR = 2
ROPE_BASE = 10000.0
LN_EPS = 1e-6
NEG_INF = -1e30
LOG2E = math.log2(math.e)
TOKEN_TILE = 256
IN_TILE = 384
MOE_TILE = 256
MOE_WINDOW = 64
MOE_BOUNDS = 16
FFN_SAMPLES = 4
GATHER_EXPERTS = 4
WIN_BLOCKS_PER_STEP = 2
VMEM_LIMIT = 56 * 1024 * 1024


def _cparams(n_axes):
    return pltpu.CompilerParams(dimension_semantics=("arbitrary",) * n_axes, vmem_limit_bytes=VMEM_LIMIT)


def _bdot(a, b):
    return jnp.dot(a, b, preferred_element_type=F32)


def _dot_nt(a, b):
    return lax.dot_general(a, b, (((1,), (1,)), ((), ())), preferred_element_type=F32)


def _layer_norm(x):
    mu = jnp.mean(x, axis=-1, keepdims=True)
    xc = x - mu
    var = jnp.mean(xc * xc, axis=-1, keepdims=True)
    return xc * lax.rsqrt(var + LN_EPS)


def _ada_kernel(cond_ref, w_ref, b_ref, o_ref):
    s = cond_ref[...]
    s = s * jax.nn.sigmoid(s)
    o_ref[0] = jnp.dot(s, w_ref[0], precision=HIGHEST, preferred_element_type=F32) + b_ref[0]


def _ada_call(cond, w_ada, b_ada):
    depth, d, d6 = w_ada.shape
    tn = 1536
    rows = cond.shape[0]
    return pl.pallas_call(
        _ada_kernel,
        grid=(depth, d6 // tn),
        in_specs=[
            pl.BlockSpec((rows, d), lambda i, j: (0, 0)),
            pl.BlockSpec((1, d, tn), lambda i, j: (i, 0, j)),
            pl.BlockSpec((1, 1, tn), lambda i, j: (i, 0, j)),
        ],
        out_specs=pl.BlockSpec((1, rows, tn), lambda i, j: (i, 0, j)),
        out_shape=jax.ShapeDtypeStruct((depth, rows, d6), F32),
        compiler_params=_cparams(2),
    )(cond, w_ada, b_ada.reshape(depth, 1, d6))


IN_WIDTHS = (S5_WIDTH, MLA_Q_RANK, MLA_KV_RANK, MLA_PAD, WIN_Q_HEADS * WIN_HEAD_DIM,
             WIN_KV_HEADS * WIN_HEAD_DIM, WIN_KV_HEADS * WIN_HEAD_DIM)
IN_OFFSETS = tuple(int(v) for v in np.cumsum((0,) + IN_WIDTHS))


def _mod_rows(mod_ref, r, tile, n_ctx):
    row = pl.program_id(1) * tile + lax.broadcasted_iota(jnp.int32, (tile, 1), 0)
    return jnp.where(row < n_ctx, mod_ref[0, 0, r:r + 1, :], mod_ref[0, 1, r:r + 1, :])


def _rope_lanes(x, cos, sa, sb, shift):
    return x * cos + pltpu.roll(x, 128 - shift, 1) * sa + pltpu.roll(x, shift, 1) * sb


def _mla_project(qa, kva, kr, qg, kvg, wq_ref, wk_ref, wv_ref, cos, sa, sb, qp_ref, qr_ref, k_ref, v_ref):
    def rms(x, gain):
        return (x * lax.rsqrt(jnp.mean(x * x, axis=-1, keepdims=True) + LN_EPS) * gain).astype(BF16)

    qn = rms(qa, qg)
    kvn = rms(kva, kvg)
    q = _bdot(qn, wq_ref[...]) * (MLA_SCALE * LOG2E)
    k = _bdot(kvn, wk_ref[...])
    v_ref[0] = _bdot(kvn, wv_ref[...]).astype(BF16)
    kr_rot = _rope_lanes(kr, cos, sa, sb, MLA_ROPE // 2)
    qp_ref[0] = q.astype(BF16)
    for h in range(MLA_HEADS):
        sl = slice(h * MLA_PAD, (h + 1) * MLA_PAD)
        qr_ref[0, :, sl] = _rope_lanes(q[:, sl], cos, sa, sb, MLA_ROPE // 2).astype(BF16)
        k_ref[0, :, sl] = (k[:, sl] + kr_rot).astype(BF16)


def _in_kernel(x_ref, mod_ref, w_ref, qg_ref, kvg_ref, wuq_ref, wuk_ref, wuv_ref, cos_ref, sa_ref, sb_ref,
               u_ref, wq_ref, wk_ref, wv_ref, gate_ref, qp_ref, qr_ref, k_ref, v_ref, *, n_ctx):
    tile = x_ref.shape[1]
    xn = _layer_norm(x_ref[0])
    h = (xn * (1.0 + _mod_rows(mod_ref, 1, tile, n_ctx)) + _mod_rows(mod_ref, 0, tile, n_ctx)).astype(BF16)
    widths = IN_WIDTHS + (gate_ref.shape[-1],)
    proj = lambda i: _bdot(h, w_ref[0, :, IN_OFFSETS[i]:IN_OFFSETS[i] + widths[i]])
    _mla_project(proj(1), proj(2), proj(3), qg_ref[...], kvg_ref[...], wuq_ref, wuk_ref, wuv_ref,
                 cos_ref[...], sa_ref[...], sb_ref[...], qp_ref, qr_ref, k_ref, v_ref)
    u_ref[0] = proj(0)
    wq_ref[0], wk_ref[0], wv_ref[0] = proj(4), proj(5), proj(6)
    gate_ref[0] = proj(7).astype(gate_ref.dtype)


_sigmoid = jax.nn.sigmoid


def _in_call(layer, xall, mod, w_cat, qg, kvg, wuq, wuk, wuv, tabs, n_ctx):
    b, n, d = xall.shape
    tm = IN_TILE
    gate_w = w_cat.shape[-1] - IN_OFFSETS[-1]
    hw = MLA_HEADS * MLA_PAD
    tok = lambda w: pl.BlockSpec((1, tm, w), lambda i, t: (i, t, 0))
    full = lambda a: pl.BlockSpec(a.shape, lambda i, t: (0,) * a.ndim)
    tab = pl.BlockSpec((tm, 128), lambda i, t: (t, 0))
    out_widths = (IN_WIDTHS[0],) + IN_WIDTHS[4:7] + (gate_w, hw, hw, hw, MLA_HEADS * MLA_V)
    out_dtypes = (F32,) * 4 + (BF16,) * 5
    return pl.pallas_call(
        functools.partial(_in_kernel, n_ctx=n_ctx),
        grid=(b, n // tm),
        in_specs=[tok(d), pl.BlockSpec((1, 2, 6, d), lambda i, t: (i, 0, 0, 0)),
                  pl.BlockSpec((1,) + w_cat.shape[1:], lambda i, t: (layer, 0, 0), pipeline_mode=pl.Buffered(1)),
                  full(qg), full(kvg), full(wuq), full(wuk), full(wuv), tab, tab, tab],
        out_specs=[tok(w) for w in out_widths],
        out_shape=[jax.ShapeDtypeStruct((b, n, w), dt) for w, dt in zip(out_widths, out_dtypes)],
        compiler_params=_cparams(2),
    )(xall, mod, w_cat, qg, kvg, wuq, wuk, wuv, *tabs)


S5_LANE_GROUPS = 128 // S5_GROUP
S5_SCAN_GROUPS = 4
S5_PITCH_PAD = 8


def _s5_param_kernel(*refs):
    for g in range(refs[0].shape[2]):
        _s5_param_group(g, *refs)


def _s5_param_group(g, cre_ref, cim_ref, bre_ref, bim_ref, pr_ref, pi_ref, tz_ref, bc_ref, cc_ref, coef_ref):
    t = S5_CHUNK
    w = t * S5_GROUP
    nt = (((1,), (1,)), ((), ()))
    tz = None
    bcs, ccs, coefs = [], [], []
    for d in range(2):
        cre, cim = cre_ref[0, d, g], cim_ref[0, d, g]
        bre, bim = bre_ref[0, d, g], bim_ref[0, d, g]
        power = lambda k: (pr_ref[0, d, g, k:k + 1, :], pi_ref[0, d, g, k:k + 1, :])
        rt = []
        for k in range(t + 1):
            prk, pik = power(k)
            rt.append(jnp.concatenate([cre * prk - cim * pik, -(cre * pik + cim * prk)], axis=1))
        bt = jnp.concatenate([bre, bim], axis=1)
        zeros = jnp.zeros((S5_GROUP, w), F32)
        if d == 0:
            kt = lax.dot_general(bt, jnp.concatenate(rt[:t], axis=0), nt, precision=HIGHEST,
                                 preferred_element_type=F32)
            pad = jnp.concatenate([zeros, kt], axis=1)
            rows = [kt] + [pltpu.roll(pad, S5_GROUP * s, 1)[:, w:] for s in range(1, t)]
        else:
            kt = lax.dot_general(bt, jnp.concatenate(rt[t - 1::-1], axis=0), nt, precision=HIGHEST,
                                 preferred_element_type=F32)
            pad = jnp.concatenate([kt, zeros], axis=1)
            rows = [pltpu.roll(pad, 2 * w - S5_GROUP * (t - 1 - s), 1)[:, :w] for s in range(t - 1)] + [kt]
        tz_d = jnp.concatenate(rows, axis=0)
        tz = tz_d if tz is None else tz + tz_d
        bc_rows = []
        for s in range(t):
            prk, pik = power(t - 1 - s if d == 0 else s)
            br = bre * prk - bim * pik
            bi = bim * prk + bre * pik
            bc_rows.append(jnp.concatenate([br, bi, bi, br], axis=1))
        bcs.append(jnp.concatenate(bc_rows, axis=0))
        ccs.append(jnp.concatenate(rt[1:] if d == 0 else rt[t:0:-1], axis=0))
        er, ei = power(t)
        coefs += [jnp.concatenate([er, er], axis=1), jnp.concatenate([-ei, ei], axis=1),
                  jnp.concatenate([ei, -ei], axis=1)]
    tz_ref[0, g] = tz.astype(BF16)
    bc_ref[0, g] = jnp.concatenate(bcs, axis=1).astype(BF16)
    cc_ref[0, g] = jnp.concatenate(ccs, axis=1).astype(BF16)
    coef_ref[0, g] = jnp.concatenate(coefs + [jnp.zeros((2, 2 * S5_STATE), F32)], axis=0)


def _s5_param_call(p):
    t = S5_CHUNK
    f = lambda name: p[name].astype(F32)
    lam_re, lam_im = f('s5_lam_re'), f('s5_lam_im')
    depth = lam_re.shape[0]
    dt = jnp.exp(f('s5_log_dt'))[..., None]
    k = jnp.arange(t + 1, dtype=F32)[:, None]
    mag = jnp.exp((lam_re * dt)[..., None, :] * k)
    ang = (lam_im * dt)[..., None, :] * k
    pr, pi = mag * jnp.cos(ang), mag * jnp.sin(ang)
    ar, ai = pr[..., 1, :], pi[..., 1, :]
    den = lam_re * lam_re + lam_im * lam_im
    qr = (((ar - 1) * lam_re + ai * lam_im) / den)[..., None, :]
    qi = ((ai * lam_re - (ar - 1) * lam_im) / den)[..., None, :]
    b_re = jnp.swapaxes(f('s5_b_re'), -1, -2)
    b_im = jnp.swapaxes(f('s5_b_im'), -1, -2)
    bbr = qr * b_re - qi * b_im
    bbi = qr * b_im + qi * b_re
    g, hg, ps = S5_GROUPS, S5_GROUP, S5_STATE
    w = t * hg
    gs = S5_LANE_GROUPS
    small = lambda rows: pl.BlockSpec((1, 2, gs, rows, ps), lambda i, j: (i, 0, j, 0, 0))
    out = lambda cols: pl.BlockSpec((1, gs, w, cols), lambda i, j: (i, j, 0, 0))
    return pl.pallas_call(
        _s5_param_kernel,
        grid=(depth, g // gs),
        in_specs=[small(hg)] * 4 + [small(t + 1)] * 2,
        out_specs=[out(w), out(2 * w), out(w), pl.BlockSpec((1, gs, 8, 2 * ps), lambda i, j: (i, j, 0, 0))],
        out_shape=[jax.ShapeDtypeStruct((depth, g, w, w), BF16), jax.ShapeDtypeStruct((depth, g, w, 2 * w), BF16),
                   jax.ShapeDtypeStruct((depth, g, w, w), BF16), jax.ShapeDtypeStruct((depth, g, 8, 2 * ps), F32)],
        compiler_params=_cparams(2),
    )(f('s5_c_re'), f('s5_c_im'), bbr, bbi, pr, pi)


def _s5_kernel(u_ref, tz_ref, bc_ref, cc_ref, coef_ref, d_ref, y_ref, uy_ref, loc_ref, sp_ref, slab_ref,
               *, n_ctx, n_batch):
    ph, b = pl.program_id(1), pl.program_id(2)
    t, hg, ng = S5_CHUNK, S5_GROUP, S5_LANE_GROUPS
    ncc = n_ctx // t
    ncl = (u_ref.shape[1] - n_ctx) // t
    nc = ncc + ncl
    pitch = nc + S5_PITCH_PAD

    def to_chunk_rows(slabs):
        tr = [s.T for s in slabs]
        return [jnp.concatenate([x[g * hg:(g + 1) * hg] for x in tr], axis=0).T for g in range(ng)]

    def to_token_slabs(rows):
        tr = [r.T for r in rows]
        return [jnp.concatenate([x[tau * hg:(tau + 1) * hg] for x in tr], axis=0).T for tau in range(t)]

    base = pl.multiple_of(b * pitch, 8)
    cbase = pl.multiple_of(b * ncc, 8)

    @pl.when(ph == 0)
    def _():
        rows = to_chunk_rows([u_ref[0, pl.ds(n_ctx + tau, ncl, stride=t), :] for tau in range(t)])
        for g in range(ng):
            uy_ref[g, pl.ds(base + ncc, ncl), :] = rows[g]
            uy_ref[g, pl.ds(base + nc, S5_PITCH_PAD), :] = jnp.zeros((S5_PITCH_PAD, t * hg), F32)
        for tau in range(t):
            slab_ref[tau, pl.ds(cbase, ncc), :] = u_ref[0, pl.ds(tau, ncc, stride=t), :]

    @pl.when((ph == 1) & (b == 0))
    def _():
        rows = to_chunk_rows([slab_ref[tau] for tau in range(t)])
        for g in range(ng):
            for s in range(n_batch):
                uy_ref[g, s * pitch:s * pitch + ncc, :] = rows[g][s * ncc:(s + 1) * ncc]
        for part in range(ng // S5_SCAN_GROUPS):
            gs = [part * S5_SCAN_GROUPS + gl for gl in range(S5_SCAN_GROUPS)]
            for gl, g in enumerate(gs):
                ub = uy_ref[g].astype(BF16)
                loc = _bdot(ub, bc_ref[0, g])
                for q in range(4):
                    loc_ref[gl, q] = loc[:, q * 128:(q + 1) * 128]
                uy_ref[g] = _bdot(ub, tz_ref[0, g])
                for s in range(n_batch):
                    for d in range(2):
                        sp_ref[gl, d, s * pitch + nc:(s + 1) * pitch, :] = jnp.zeros((S5_PITCH_PAD, 128), F32)

            def coef(g, r):
                return jnp.broadcast_to(coef_ref[0, g, r:r + 1, :], (n_batch, 128))

            def step(i, carry):
                cb = jnp.where(i < ncc, ncc - 1 - i, nc + ncc - 1 - i)
                fwd = pl.ds(i, n_batch, stride=pitch)
                bwd = pl.ds(cb, n_batch, stride=pitch)
                out = []
                for gl, g in enumerate(gs):
                    v0f, v1f, v0b, v1b = carry[gl]
                    sp_ref[gl, 0, fwd, :] = v0f
                    sp_ref[gl, 1, bwd, :] = v0b
                    n0f = coef(g, 0) * v0f + coef(g, 1) * v1f + loc_ref[gl, 0, fwd, :]
                    n1f = coef(g, 0) * v1f + coef(g, 2) * v0f + loc_ref[gl, 1, fwd, :]
                    n0b = coef(g, 3) * v0b + coef(g, 4) * v1b + loc_ref[gl, 2, bwd, :]
                    n1b = coef(g, 3) * v1b + coef(g, 5) * v0b + loc_ref[gl, 3, bwd, :]
                    out.append((n0f, n1f, n0b, n1b))
                return tuple(out)

            z = jnp.zeros((n_batch, 128), F32)
            lax.fori_loop(0, nc, step, tuple((z, z, z, z) for _ in gs))
            for gl, g in enumerate(gs):
                sp = jnp.concatenate([sp_ref[gl, 0], sp_ref[gl, 1]], axis=1).astype(BF16)
                uy_ref[g] = uy_ref[g] + _dot_nt(sp, cc_ref[0, g])
        rows = [jnp.concatenate([uy_ref[g, s * pitch:s * pitch + ncc, :] for s in range(n_batch)], axis=0)
                for g in range(ng)]
        for tau, slab in enumerate(to_token_slabs(rows)):
            slab_ref[tau] = slab

    @pl.when(ph == 1)
    def _():
        slabs = to_token_slabs([uy_ref[g, pl.ds(base + ncc, ncl), :] for g in range(ng)])
        for tau in range(t):
            y_ref[0, pl.ds(n_ctx + tau, ncl, stride=t), :] = slabs[tau]
            y_ref[0, pl.ds(tau, ncc, stride=t), :] = slab_ref[tau, pl.ds(cbase, ncc), :]
        y_ref[0] = y_ref[0] + d_ref[0] * u_ref[0]


def _s5_call(layer, u, tz, bc, cc, coef, s5_d, n_ctx):
    b, n, width = u.shape
    t, ng = S5_CHUNK, S5_LANE_GROUPS
    rows = b * (n // t + S5_PITCH_PAD)
    assert (n - n_ctx) // t == 128 and b * (n_ctx // t) == 128
    wspec = lambda a: pl.BlockSpec((1, ng) + a.shape[2:], lambda g, ph, i: (layer, g, 0, 0))
    return pl.pallas_call(
        functools.partial(_s5_kernel, n_ctx=n_ctx, n_batch=b),
        grid=(width // 128, 2, b),
        in_specs=[pl.BlockSpec((1, n, 128), lambda g, ph, i: (i, 0, g)),
                  wspec(tz), wspec(bc), wspec(cc), wspec(coef),
                  pl.BlockSpec((1, 1, 128), lambda g, ph, i: (layer, 0, g))],
        out_specs=pl.BlockSpec((1, n, 128), lambda g, ph, i: (i * ph, 0, g)),
        out_shape=jax.ShapeDtypeStruct((b, n, width), F32),
        scratch_shapes=[pltpu.VMEM((ng, rows, t * S5_GROUP), F32),
                        pltpu.VMEM((S5_SCAN_GROUPS, 4, rows, 128), F32),
                        pltpu.VMEM((S5_SCAN_GROUPS, 2, rows, 128), F32),
                        pltpu.VMEM((t, 128, 128), F32)],
        compiler_params=_cparams(3),
    )(u, tz, bc, cc, coef, s5_d.astype(F32).reshape(s5_d.shape[0], 1, width))


def _lane_chunks(xs):
    return [x[:, i * 128:(i + 1) * 128] for x in xs for i in range(x.shape[1] // 128)]


def _row_max(scores, floor=None):
    mm = functools.reduce(jnp.maximum, _lane_chunks(scores))
    if floor is not None:
        mm = jnp.maximum(mm, floor)
    return jnp.max(mm, axis=-1, keepdims=True)


def _softmax_av(scores, values, sink=None):
    m = _row_max(scores, sink)
    ps = [jnp.exp2(s - m) for s in scores]
    ll = functools.reduce(jnp.add, _lane_chunks(ps))
    if sink is not None:
        lane = lax.broadcasted_iota(jnp.int32, sink.shape, 1)
        ll = ll + jnp.where(lane == 0, jnp.exp2(sink - m), 0.0)
    l = jnp.sum(ll, axis=-1, keepdims=True)
    o = functools.reduce(jnp.add, [_bdot(p.astype(BF16), v) for p, v in zip(ps, values)])
    return o / l


def _mla_attn_kernel(qp_ref, qr_ref, k_ref, v_ref, o_ref, *, n_ctx, n_ctx_tiles):
    heads = qp_ref.shape[-1] // MLA_PAD
    t = pl.program_id(2)
    group = 256 // MLA_V
    lane = lax.broadcasted_iota(jnp.int32, (1, 256), 1)

    def run(latent):
        all_scores = []
        for h in range(heads):
            sl = slice(h * MLA_PAD, (h + 1) * MLA_PAD)
            scores = [_dot_nt(qp_ref[0, :, sl], k_ref[0, :n_ctx, sl])]
            if latent:
                scores.append(_dot_nt(qr_ref[0, :, sl], k_ref[0, n_ctx:, sl]))
            all_scores.append(scores)
        probs = []
        for scores in all_scores:
            m = _row_max(scores)
            ps = [jnp.exp2(s - m) for s in scores]
            l = jnp.sum(functools.reduce(jnp.add, _lane_chunks(ps)), axis=-1, keepdims=True)
            probs.append(([p.astype(BF16) for p in ps], l))
        acc = [None] * (heads // group)
        for h, (ps, l) in enumerate(probs):
            blk, hh = divmod(h, group)
            cols = slice(blk * 256, (blk + 1) * 256)
            own = (lane >= hh * MLA_V) & (lane < (hh + 1) * MLA_V)
            zero = jnp.zeros((), BF16)
            values = [jnp.where(own, v_ref[0, :n_ctx, cols], zero)]
            if latent:
                values.append(jnp.where(own, v_ref[0, n_ctx:, cols], zero))
            o = functools.reduce(jnp.add, [_bdot(p, v) for p, v in zip(ps, values)]) / l
            acc[blk] = o if acc[blk] is None else acc[blk] + o
        for blk, o in enumerate(acc):
            o_ref[0, :, blk * 256:(blk + 1) * 256] = o.astype(o_ref.dtype)

    pl.when(t < n_ctx_tiles)(lambda: run(False))
    pl.when(t >= n_ctx_tiles)(lambda: run(True))


def _mla_attn_call(qp, qr, k, v, n_ctx):
    b, n, _ = qp.shape
    tq = TOKEN_TILE
    hp = MLA_HEADS_PER_STEP
    qspec = pl.BlockSpec((1, tq, hp * MLA_PAD), lambda i, h, t: (i, t, h))
    return pl.pallas_call(
        functools.partial(_mla_attn_kernel, n_ctx=n_ctx, n_ctx_tiles=n_ctx // tq),
        grid=(b, MLA_HEADS // hp, n // tq),
        in_specs=[qspec, qspec,
                  pl.BlockSpec((1, n, hp * MLA_PAD), lambda i, h, t: (i, 0, h)),
                  pl.BlockSpec((1, n, hp * MLA_V), lambda i, h, t: (i, 0, h))],
        out_specs=pl.BlockSpec((1, tq, hp * MLA_V), lambda i, h, t: (i, t, h)),
        out_shape=jax.ShapeDtypeStruct((b, n, MLA_HEADS * MLA_V), BF16),
        compiler_params=_cparams(3),
    )(qp, qr, k, v)


def _win_kernel(sink_ref, q_ref, k_ref, v_ref, cos_ref, sa_ref, sb_ref, o_ref, *, n_ctx_blocks, n_blocks):
    j = pl.program_id(1)
    hd = WIN_HEAD_DIM
    half = hd // 2
    lane = lax.broadcasted_iota(jnp.int32, (1, 128), 1)
    lo = jnp.where(lane < hd, 1.0, 0.0)
    hi = 1.0 - lo
    upper_rows = lax.broadcasted_iota(jnp.int32, (2 * BLOCK, 128), 0) < BLOCK

    def lane_halves(x):
        xr = pltpu.roll(x, hd, 1)
        return {(0, 0): (x * lo).astype(BF16), (0, 1): (xr * hi).astype(BF16),
                (1, 0): (xr * lo).astype(BF16), (1, 1): (x * hi).astype(BF16)}

    def attend(out_rows, queries, keys, values, masks):
        all_scores = {}
        for kh in range(WIN_KV_HEADS):
            stacked = [jnp.concatenate([qs[:, (2 * kh) * 128:(2 * kh + 1) * 128],
                                        qs[:, (2 * kh + 1) * 128:(2 * kh + 2) * 128]], axis=0).astype(BF16)
                       for qs in queries]
            for par in range(2):
                scores = []
                for qst, ks, msk in zip(stacked, keys, masks):
                    s = _dot_nt(qst, ks[(kh, par)])
                    scores.append(s if msk is None else jnp.where(msk, s, NEG_INF))
                all_scores[(kh, par)] = scores
        for kh in range(WIN_KV_HEADS):
            acc = None
            for par in range(2):
                sink = jnp.where(upper_rows, sink_ref[4 * kh + par], sink_ref[4 * kh + 2 + par]) * LOG2E
                o = _softmax_av(all_scores[(kh, par)], [vs[(kh, par)] for vs in values], sink)
                acc = o if acc is None else acc + o
            o_ref[0, out_rows, (2 * kh) * 128:(2 * kh + 1) * 128] = acc[:BLOCK].astype(o_ref.dtype)
            o_ref[0, out_rows, (2 * kh + 1) * 128:(2 * kh + 2) * 128] = acc[BLOCK:].astype(o_ref.dtype)

    n_ctx = n_ctx_blocks * BLOCK
    kctx = lane_halves(k_ref[0, :n_ctx, :])
    vctx = lane_halves(v_ref[0, :n_ctx, :])
    subs = [slice(s * BLOCK, (s + 1) * BLOCK) for s in range(q_ref.shape[1] // BLOCK)]

    def ctx_path():
        for rows in subs:
            attend(rows, [q_ref[0, rows, :] * (WIN_SCALE * LOG2E)], [kctx], [vctx], [None])

    def lat_path():
        for s, rows in enumerate(subs):
            lat_block(j * len(subs) + s - n_ctx_blocks, rows)

    def lat_block(blk, out_rows):
        q = q_ref[0, out_rows, :] * (WIN_SCALE * LOG2E)
        band = [pl.ds(pl.multiple_of((n_ctx_blocks + jnp.clip(blk + d, 0, n_blocks - 1)) * BLOCK, BLOCK), BLOCK)
                for d in (-1, 0, 1)]
        rope = lambda x, rows: _rope_lanes(x, cos_ref[rows, :], sa_ref[rows, :], sb_ref[rows, :], half)
        q_rot = jnp.concatenate([rope(q[:, c * 128:(c + 1) * 128], band[1]) for c in range(q.shape[1] // 128)],
                                axis=-1)
        kband = jnp.concatenate([rope(k_ref[0, rows, :], rows) for rows in band], axis=0)
        vband = jnp.concatenate([v_ref[0, rows, :] for rows in band], axis=0)
        r = lax.broadcasted_iota(jnp.int32, (2 * BLOCK, 3 * BLOCK), 0) % BLOCK
        c = lax.broadcasted_iota(jnp.int32, (2 * BLOCK, 3 * BLOCK), 1)
        first = jnp.where(blk > 0, 0, BLOCK)
        last = jnp.where(blk < n_blocks - 1, 3 * BLOCK, 2 * BLOCK)
        valid = (jnp.abs(c - BLOCK - r) <= WINDOW) & (c >= first) & (c < last)
        attend(out_rows, [q_rot, q], [lane_halves(kband), kctx], [lane_halves(vband), vctx], [valid, None])

    pl.when(j * len(subs) < n_ctx_blocks)(ctx_path)
    pl.when(j * len(subs) >= n_ctx_blocks)(lat_path)


def _win_call(sink, wq, wk, wv, tabs, n_ctx):
    b, n, _ = wq.shape
    ncb = n_ctx // BLOCK
    nb = n // BLOCK
    kvw = WIN_KV_HEADS * WIN_HEAD_DIM
    per = WIN_BLOCKS_PER_STEP
    assert ncb % per == 0 and nb % per == 0
    cur = lambda i, j: (i, j, 0)
    kv = pl.BlockSpec((1, n, kvw), lambda i, j: (i, 0, 0))
    tab = pl.BlockSpec((n, 128), lambda i, j: (0, 0))
    return pl.pallas_call(
        functools.partial(_win_kernel, n_ctx_blocks=ncb, n_blocks=nb - ncb),
        grid=(b, nb // per),
        in_specs=[pl.BlockSpec(memory_space=pltpu.SMEM),
                  pl.BlockSpec((1, per * BLOCK, WIN_Q_HEADS * WIN_HEAD_DIM), cur), kv, kv, tab, tab, tab],
        out_specs=pl.BlockSpec((1, per * BLOCK, WIN_Q_HEADS * WIN_HEAD_DIM), cur),
        out_shape=jax.ShapeDtypeStruct((b, n, WIN_Q_HEADS * WIN_HEAD_DIM), BF16),
        compiler_params=_cparams(2),
    )(sink, wq, wk, wv, *tabs)


def _merge_kernel(x_ref, s5_ref, mla_ref, win_ref, gate_ref, mod_ref, wglu_ref, bglu_ref, wbr_ref, wout_ref,
                  g1_ref, b1_ref, wr_ref, x1_ref, h2_ref, lg_ref, *, alpha):
    d = x_ref.shape[-1]
    proj = {1: _bdot(mla_ref[0], wbr_ref[1]), 2: _bdot(win_ref[0], wbr_ref[2])}
    g = jax.nn.gelu(s5_ref[0])
    s5o = g * _sigmoid(_bdot(g.astype(BF16), wglu_ref[...]) + bglu_ref[...])
    proj[0] = _bdot(s5o.astype(BF16), wbr_ref[0])
    mix = None
    for kk in (1, 2, 0):
        term = _sigmoid(gate_ref[0, :, kk * d:(kk + 1) * d].astype(F32)) * proj[kk]
        mix = term if mix is None else mix + term
    y = _bdot(mix.astype(BF16), wout_ref[...])
    mod = lambda r: mod_ref[0, 0, r:r + 1, :]
    x1 = _layer_norm(alpha * x_ref[0] + mod(2) * y) * g1_ref[...] + b1_ref[...]
    x1_ref[0] = x1
    h2 = (_layer_norm(x1) * (1.0 + mod(4)) + mod(3)).astype(BF16)
    h2_ref[0] = h2
    lg_ref[0] = _dot_nt(wr_ref[...], h2)


def _merge_call(xall, s5y, mla_o, win_o, gates, mod, wglu, bglu, wbr, wout, g1, b1, wr_t, n_ctx_tiles, alpha):
    b, n, d = xall.shape
    tm = TOKEN_TILE
    tok = lambda w: pl.BlockSpec((1, tm, w), lambda i, t: (i, t, 0))
    full = lambda a: pl.BlockSpec(a.shape, lambda i, t: (0,) * a.ndim)
    return pl.pallas_call(
        functools.partial(_merge_kernel, alpha=alpha),
        grid=(b, n // tm),
        in_specs=[tok(d), tok(BRANCH_WIDTH), tok(BRANCH_WIDTH), tok(BRANCH_WIDTH), tok(N_BRANCH * d),
                  pl.BlockSpec((1, 1, 6, d), lambda i, t: (i, jnp.where(t < n_ctx_tiles, 0, 1), 0, 0)),
                  full(wglu), full(bglu), full(wbr), full(wout), full(g1), full(b1), full(wr_t)],
        out_specs=[tok(d), tok(d), pl.BlockSpec((1, N_EXPERTS, tm), lambda i, t: (i, 0, t))],
        out_shape=[jax.ShapeDtypeStruct((b, n, d), F32), jax.ShapeDtypeStruct((b, n, d), BF16),
                   jax.ShapeDtypeStruct((b, N_EXPERTS, n), F32)],
        compiler_params=_cparams(2),
    )(xall, s5y, mla_o, win_o, gates, mod, wglu, bglu, wbr, wout, g1, b1, wr_t)


def _excl_cumsum_lanes(m):
    rows, n = m.shape
    r = lax.broadcasted_iota(jnp.int32, (128, 128), 0)
    c = lax.broadcasted_iota(jnp.int32, (128, 128), 1)
    tri = jnp.where(r < c, 1.0, 0.0).astype(BF16)
    off = jnp.zeros((rows, 1), F32)
    outs, offs = [], []
    for jb in range(n // 128):
        blk = m[:, jb * 128:(jb + 1) * 128]
        offs.append(off)
        outs.append(_bdot(blk.astype(BF16), tri) + off)
        off = off + jnp.sum(blk, axis=1, keepdims=True)
    return jnp.concatenate(outs, axis=1), offs + [off]


def _topk_slots(affs, caps):
    bits = [pltpu.bitcast(aff, jnp.int32) for aff in affs]

    def body(i, thrs):
        out = []
        for b, cap, thr in zip(bits, caps, thrs):
            cand = thr | (jnp.int32(1) << (30 - i))
            cnt = jnp.sum(jnp.where(b >= cand, 1.0, 0.0), axis=1, keepdims=True)
            out.append(jnp.where(cnt >= cap, cand, thr))
        return tuple(out)

    zero = jnp.zeros((affs[0].shape[0], 1), jnp.int32)
    thrs = lax.fori_loop(0, 31, body, tuple(zero for _ in affs))
    results = []
    for b, cap, thr in zip(bits, caps, thrs):
        gt = jnp.where(b > thr, 1.0, 0.0)
        eq = jnp.where(b == thr, 1.0, 0.0)
        need = cap - jnp.sum(gt, axis=1, keepdims=True)
        sel = gt + eq * jnp.where(_excl_cumsum_lanes(eq)[0] < need, 1.0, 0.0)
        rank, offs = _excl_cumsum_lanes(sel)
        results.append((jnp.where(sel > 0.5, rank, -1.0).astype(jnp.int32), offs[::MOE_TILE // 128]))
    return results


def _route_kernel(lg_ref, slot_ref, aff_ref, bnd_ref, *, n_ctx, cap_ctx, cap_lat):
    lg = lg_ref[0]
    m = jnp.max(lg, axis=0, keepdims=True)
    ex = jnp.exp(lg - m)
    aff = ex / jnp.sum(ex, axis=0, keepdims=True)
    aff_ref[0] = aff
    (slots_ctx, _), (slots, counts) = _topk_slots([aff[:, :n_ctx], aff[:, n_ctx:]], [cap_ctx, cap_lat])
    slot_ref[0, :, :n_ctx] = slots_ctx
    slot_ref[0, :, n_ctx:] = slots
    lane = lax.broadcasted_iota(jnp.int32, bnd_ref.shape[1:], 1)
    bnd = jnp.zeros(bnd_ref.shape[1:], F32)
    for k, cnt in enumerate(counts):
        bnd = jnp.where(lane == k, cnt, bnd)
    bnd_ref[0] = bnd.astype(jnp.int32)


def _route_call(logits_t, n_ctx, cap_ctx, cap_lat):
    b, e, n = logits_t.shape
    assert (n - n_ctx) % MOE_TILE == 0 and (n - n_ctx) // MOE_TILE < MOE_BOUNDS
    spec = pl.BlockSpec((1, e, n), lambda i: (i, 0, 0))
    return pl.pallas_call(
        functools.partial(_route_kernel, n_ctx=n_ctx, cap_ctx=cap_ctx, cap_lat=cap_lat),
        grid=(b,),
        in_specs=[spec],
        out_specs=[spec, spec, pl.BlockSpec((1, e, MOE_BOUNDS), lambda i: (i, 0, 0))],
        out_shape=[jax.ShapeDtypeStruct((b, e, n), jnp.int32), jax.ShapeDtypeStruct((b, e, n), F32),
                   jax.ShapeDtypeStruct((b, e, MOE_BOUNDS), jnp.int32)],
        compiler_params=_cparams(1),
    )(logits_t)


def _gather_kernel(bnd_ref, slot_ref, aff_ref, h_ref, xs_ref, gate_ref, xl_ref, gl_ref, *, n_ctx, cap_ctx):
    for j in range(xs_ref.shape[1]):
        _gather_expert(j, bnd_ref, slot_ref, aff_ref, h_ref, xs_ref, gate_ref, xl_ref.at[j], gl_ref.at[j],
                       n_ctx=n_ctx, cap_ctx=cap_ctx)


def _gather_expert(j, bnd_ref, slot_ref, aff_ref, h_ref, xs_ref, gate_ref, xl_ref, gl_ref, *, n_ctx, cap_ctx):
    cap_lat = xs_ref.shape[2] - cap_ctx
    slot = slot_ref[0, j]
    aff = aff_ref[0, j]
    n = h_ref.shape[1]
    ib, ie = pl.program_id(0), pl.program_id(1) * xs_ref.shape[1] + j
    xl_ref[...] = jnp.zeros(xl_ref.shape, F32)
    gl_ref[...] = jnp.zeros(gl_ref.shape, F32)
    tiles = range((n - n_ctx) // MOE_TILE)
    firsts = [(bnd_ref[ib, ie, kt] // 16) * 16 for kt in tiles]

    def window(kt, start):
        tok = slice(n_ctx + kt * MOE_TILE, n_ctx + (kt + 1) * MOE_TILE)
        rows = start + lax.broadcasted_iota(jnp.int32, (MOE_WINDOW, MOE_TILE), 0)
        hit = slot[:, tok] == rows
        picked = jnp.sum(jnp.where(hit, aff[:, tok], 0.0), axis=1, keepdims=True)
        return _bdot(jnp.where(hit, 1.0, 0.0).astype(BF16), h_ref[0, tok, :]), picked

    def add_window(kt, start, parts=None):
        start = pl.multiple_of(start, 16)
        rows, picked = parts if parts is not None else window(kt, start)
        xl_ref[pl.ds(start, MOE_WINDOW), :] += rows
        gl_ref[pl.ds(start, MOE_WINDOW), :] += jnp.broadcast_to(picked, (MOE_WINDOW, 128))

    first_windows = [window(kt, firsts[kt]) for kt in tiles]
    for kt in tiles:
        add_window(kt, firsts[kt], first_windows[kt])
    for kt in tiles:
        def more(w, carry, kt=kt):
            add_window(kt, firsts[kt] + (w + 1) * MOE_WINDOW)
            return carry

        n_win = (bnd_ref[ib, ie, kt + 1] - firsts[kt] + MOE_WINDOW - 1) // MOE_WINDOW
        lax.fori_loop(0, jnp.maximum(n_win - 1, 0), more, 0)
    iota = lax.broadcasted_iota(jnp.int32, (cap_ctx, n_ctx), 0)
    hit = slot[:, :n_ctx] == iota
    xc = _bdot(jnp.where(hit, 1.0, 0.0).astype(BF16), h_ref[0, :n_ctx, :])
    gc = jnp.sum(jnp.where(hit, aff[:, :n_ctx], 0.0), axis=1, keepdims=True)
    xs_ref[0, j, :cap_lat] = xl_ref[:cap_lat].astype(BF16)
    xs_ref[0, j, cap_lat:] = xc.astype(BF16)
    gate_ref[0, j, :cap_lat] = gl_ref[:cap_lat]
    gate_ref[0, j, cap_lat:] = jnp.broadcast_to(gc, (cap_ctx, 128))


def _gather_call(bounds, slot, aff, h2, n_ctx, cap_ctx, cap_lat):
    b, e, n = slot.shape
    d = h2.shape[-1]
    cap = cap_lat + cap_ctx
    ne = GATHER_EXPERTS
    row = pl.BlockSpec((1, ne, 1, n), lambda ib, ie: (ib, ie, 0, 0))
    return pl.pallas_call(
        functools.partial(_gather_kernel, n_ctx=n_ctx, cap_ctx=cap_ctx),
        grid=(b, e // ne),
        in_specs=[pl.BlockSpec(memory_space=pltpu.SMEM), row, row, pl.BlockSpec((1, n, d), lambda ib, ie: (ib, 0, 0))],
        out_specs=[pl.BlockSpec((1, ne, cap, d), lambda ib, ie: (ib, ie, 0, 0)),
                   pl.BlockSpec((1, ne, cap, 128), lambda ib, ie: (ib, ie, 0, 0))],
        out_shape=[jax.ShapeDtypeStruct((b, e, cap, d), BF16), jax.ShapeDtypeStruct((b, e, cap, 128), F32)],
        scratch_shapes=[pltpu.VMEM((ne, cap_lat + MOE_WINDOW, d), F32),
                        pltpu.VMEM((ne, cap_lat + MOE_WINDOW, 128), F32)],
        compiler_params=_cparams(2),
    )(bounds, slot.reshape(b, e, 1, n), aff.reshape(b, e, 1, n), h2)


def _ffn_kernel(xs_ref, gate_ref, wg_ref, wu_ref, wd_ref, yl_ref, yc_ref, wg_s, wu_s, wd_s):
    ns, _, cap, d = xs_ref.shape
    cap_ctx = yc_ref.shape[2]
    cap_lat = cap - cap_ctx

    @pl.when(pl.program_id(1) == 0)
    def _():
        wg_s[...] = wg_ref[0, 0].astype(BF16)
        wu_s[...] = wu_ref[0, 0].astype(BF16)
        wd_s[...] = wd_ref[0, 0].astype(BF16)

    au = [(_bdot(xs_ref[s, 0], wg_s[...]), _bdot(xs_ref[s, 0], wu_s[...])) for s in range(ns)]
    for s, (a, u) in enumerate(au):
        hm = (a * _sigmoid(a) * u).astype(BF16)
        y = _bdot(hm, wd_s[...]) * gate_ref[s, 0, :, 0:1]
        yl_ref[s, 0, :cap_lat] = y[:cap_lat].astype(yl_ref.dtype)
        yl_ref[s, 0, cap_lat:] = jnp.zeros((MOE_WINDOW, d), yl_ref.dtype)
        yc_ref[s, 0] = y[cap_lat:].astype(yc_ref.dtype)


def _ffn_call(layer, xs, gate, w_gate, w_up, w_down, cap_ctx):
    b, e, cap, d = xs.shape
    f = w_gate.shape[-1]
    ns = FFN_SAMPLES
    rows_lat = cap - cap_ctx + MOE_WINDOW
    tok = lambda rows, w: pl.BlockSpec((ns, 1, rows, w), lambda ie, j: (j, ie, 0, 0))
    return pl.pallas_call(
        _ffn_kernel,
        grid=(e, b // ns),
        in_specs=[tok(cap, d), tok(cap, 128),
                  pl.BlockSpec((1, 1, d, f), lambda ie, j: (layer, ie, 0, 0)),
                  pl.BlockSpec((1, 1, d, f), lambda ie, j: (layer, ie, 0, 0)),
                  pl.BlockSpec((1, 1, f, d), lambda ie, j: (layer, ie, 0, 0))],
        out_specs=[tok(rows_lat, d), tok(cap_ctx, d)],
        out_shape=[jax.ShapeDtypeStruct((b, e, rows_lat, d), BF16), jax.ShapeDtypeStruct((b, e, cap_ctx, d), BF16)],
        scratch_shapes=[pltpu.VMEM((d, f), BF16), pltpu.VMEM((d, f), BF16), pltpu.VMEM((f, d), BF16)],
        compiler_params=_cparams(2),
    )(xs, gate, w_gate, w_up, w_down)


def _combine_kernel(bnd_ref, slot_ref, yl_ref, yc_ref, x1_ref, mod_ref, g2_ref, b2_ref, o_ref, fl_ref,
                    *, n_ctx_tiles, alpha):
    ib, t = pl.program_id(0), pl.program_id(1)
    tm = x1_ref.shape[1]
    slot = slot_ref[0]
    win = MOE_WINDOW

    def finish(fl):
        x1 = x1_ref[0]
        o_ref[0] = _layer_norm(alpha * x1 + mod_ref[0, 0, 5:6, :] * fl) * g2_ref[...] + b2_ref[...]

    def onehot(e, first, width):
        iota = lax.broadcasted_iota(jnp.int32, (tm, width), 1)
        return jnp.where(slot[:, e:e + 1] - first == iota, 1.0, 0.0).astype(BF16)

    def ctx_path():
        cap = yc_ref.shape[2]
        fl = None
        for e in range(N_EXPERTS):
            term = _bdot(onehot(e, 0, cap), yc_ref[0, e])
            fl = term if fl is None else fl + term
        finish(fl)

    def lat_path():
        kt = t - n_ctx_tiles
        firsts = [pl.multiple_of((bnd_ref[ib, e, kt] // 16) * 16, 16) for e in range(N_EXPERTS)]
        lane = lax.broadcasted_iota(jnp.int32, (tm, 2 * win), 1)
        pieces, ywins = [], []
        for e in range(0, N_EXPERTS, 2):
            rel = jnp.where(lane < win, slot[:, e:e + 1] - firsts[e], slot[:, e + 1:e + 2] - firsts[e + 1] + win)
            pieces.append(jnp.where(rel == lane, 1.0, 0.0).astype(BF16))
            ywins += [yl_ref[0, e, pl.ds(firsts[e], win), :], yl_ref[0, e + 1, pl.ds(firsts[e + 1], win), :]]
        fl_ref[...] = _bdot(jnp.concatenate(pieces, axis=1), jnp.concatenate(ywins, axis=0))
        for e in range(N_EXPERTS):
            def window(w, carry, e=e):
                first = pl.multiple_of(firsts[e] + (w + 1) * win, 16)
                fl_ref[...] += _bdot(onehot(e, first, win), yl_ref[0, e, pl.ds(first, win), :])
                return carry

            n_win = (bnd_ref[ib, e, kt + 1] - firsts[e] + win - 1) // win
            lax.fori_loop(0, jnp.maximum(n_win - 1, 0), window, 0)
        finish(fl_ref[...])

    pl.when(t < n_ctx_tiles)(ctx_path)
    pl.when(t >= n_ctx_tiles)(lat_path)


def _combine_call(bounds, slot_t, yl, yc, x1, mod, g2, b2, n_ctx_tiles, alpha, latent_only):
    b, n, d = x1.shape
    tm = MOE_TILE
    e = N_EXPERTS
    full = lambda a: pl.BlockSpec(a.shape, lambda i, t: (0,) * a.ndim)
    skip = n_ctx_tiles if latent_only else 0
    return pl.pallas_call(
        functools.partial(_combine_kernel, n_ctx_tiles=n_ctx_tiles, alpha=alpha),
        grid=(b, n // tm),
        scratch_shapes=[pltpu.VMEM((tm, d), F32)],
        in_specs=[pl.BlockSpec(memory_space=pltpu.SMEM),
                  pl.BlockSpec((1, tm, e), lambda i, t: (i, t, 0)),
                  pl.BlockSpec((1,) + yl.shape[1:], lambda i, t: (i, 0, 0, 0)),
                  pl.BlockSpec((1,) + yc.shape[1:], lambda i, t: (i, 0, 0, 0)),
                  pl.BlockSpec((1, tm, d), lambda i, t: (i, t, 0)),
                  pl.BlockSpec((1, 1, 6, d), lambda i, t: (i, jnp.where(t < n_ctx_tiles, 0, 1), 0, 0)),
                  full(g2), full(b2)],
        out_specs=pl.BlockSpec((1, tm, d), lambda i, t: (i, jnp.maximum(t - skip, 0), 0)),
        out_shape=jax.ShapeDtypeStruct((b, n - skip * tm, d), F32),
        compiler_params=_cparams(2),
    )(bounds, slot_t, yl, yc, x1, mod, g2, b2)


def _rope_tables(n_ctx, seq, head_dim, lane_offset):
    half = head_dim // 2
    nf = head_dim // 4
    t = jnp.arange(seq, dtype=F32)
    row = jnp.floor(t / GRID_W)
    col = t - row * GRID_W
    freqs = ROPE_BASE ** (-jnp.arange(nf, dtype=F32) / nf)
    ang = jnp.concatenate([row[:, None] * freqs, col[:, None] * freqs], axis=-1)
    cos, sin = jnp.cos(ang), jnp.sin(ang)
    zeros = jnp.zeros_like(sin)
    n_heads = (128 - lane_offset) // head_dim if lane_offset == 0 else 1
    c = jnp.concatenate([jnp.ones((seq, lane_offset), F32)] + [cos, cos] * n_heads, axis=-1)
    sa = jnp.concatenate([jnp.zeros((seq, lane_offset), F32)] + [-sin, zeros] * n_heads, axis=-1)
    sb = jnp.concatenate([jnp.zeros((seq, lane_offset), F32)] + [zeros, sin] * n_heads, axis=-1)
    pad = 128 - c.shape[1]
    c = jnp.pad(c, ((n_ctx, 0), (0, pad)), constant_values=1.0)
    sa = jnp.pad(sa, ((n_ctx, 0), (0, pad)))
    sb = jnp.pad(sb, ((n_ctx, 0), (0, pad)))
    return c, sa, sb


def _in_weights(w_in):
    lo = S5_WIDTH + MLA_Q_RANK + MLA_KV_RANK
    hi = lo + MLA_ROPE
    zeros = lambda n: jnp.zeros(w_in.shape[:2] + (n,), w_in.dtype)
    return jnp.concatenate([w_in[..., :lo], zeros(MLA_NOPE), w_in[..., lo:hi],
                            zeros(MLA_PAD - MLA_NOPE - MLA_ROPE), w_in[..., hi:]], axis=-1).astype(BF16)


def _layer_weights(i, p):
    dq = MLA_NOPE + MLA_ROPE
    wq = p['mla_w_uq'][i].reshape(MLA_Q_RANK, MLA_HEADS, dq)
    wq = jnp.pad(wq, ((0, 0), (0, 0), (0, MLA_PAD - dq))).reshape(MLA_Q_RANK, MLA_HEADS * MLA_PAD)
    wkv = p['mla_w_ukv'][i].reshape(MLA_KV_RANK, MLA_HEADS, MLA_NOPE + MLA_V)
    wk = jnp.pad(wkv[:, :, :MLA_NOPE], ((0, 0), (0, 0), (0, MLA_PAD - MLA_NOPE)))
    wk = wk.reshape(MLA_KV_RANK, MLA_HEADS * MLA_PAD)
    wv = wkv[:, :, MLA_NOPE:].reshape(MLA_KV_RANK, MLA_HEADS * MLA_V)
    row = lambda a: a[i].astype(F32).reshape(1, -1)
    return dict(
        wq=wq.astype(BF16), wk=wk.astype(BF16), wv=wv.astype(BF16),
        qg=row(p['mla_q_norm']), kvg=row(p['mla_kv_norm']),
        wglu=p['s5_w_glu'][i].astype(BF16), bglu=row(p['s5_b_glu']),
        sink=p['win_sink'][i].astype(F32),
        wbr=p['w_branch'][i].astype(BF16), wout=p['w_out'][i].astype(BF16),
        g1=row(p['ln1_g']), b1=row(p['ln1_b']), g2=row(p['ln2_g']), b2=row(p['ln2_b']),
        wr_t=p['w_router'][i].T.astype(BF16),
    )


def _forward(p):
    x, c, ctx, c_ctx = p['x'], p['c'], p['ctx'], p['c_ctx']
    b, seq, d = x.shape
    n_ctx = ctx.shape[1]
    depth = p['w_ada'].shape[0]
    assert b == 8 and seq % TOKEN_TILE == 0 and n_ctx % TOKEN_TILE == 0 and seq % GRID_W == 0
    alpha = float((2 * depth) ** 0.25)
    n_ctx_tiles = n_ctx // TOKEN_TILE
    cap_lat = CAPACITY_FACTOR * seq // N_EXPERTS
    cap_ctx = CAPACITY_FACTOR * n_ctx // N_EXPERTS

    cond = jnp.concatenate([c, c_ctx[None], jnp.zeros((16 - b - 1, d), F32)], axis=0)
    mods = _ada_call(cond, p['w_ada'], p['b_ada'])
    mods = mods.reshape(depth, 16, 6, d)
    tabs_mla = _rope_tables(n_ctx, seq, MLA_ROPE, MLA_NOPE)
    tabs_win = _rope_tables(n_ctx, seq, WIN_HEAD_DIM, 0)
    s5w = _s5_param_call(p)
    w_cat = _in_weights(p['w_in'])

    xall = jnp.concatenate([ctx, x], axis=1)
    for i in range(depth):
        w = _layer_weights(i, p)
        mod = jnp.stack([jnp.broadcast_to(mods[i, b], (b, 6, d)), mods[i, :b]], axis=1)
        u, wq, wk, wv, gates, qp, qr, kk, vv = _in_call(i, xall, mod, w_cat, w['qg'], w['kvg'], w['wq'], w['wk'],
                                                        w['wv'], tabs_mla, n_ctx)
        s5y = _s5_call(i, u, *s5w, p['s5_d'], n_ctx)
        mla_o = _mla_attn_call(qp, qr, kk, vv, n_ctx)
        win_o = _win_call(w['sink'], wq, wk, wv, tabs_win, n_ctx)
        x1, h2, logits_t = _merge_call(xall, s5y, mla_o, win_o, gates, mod, w['wglu'], w['bglu'], w['wbr'],
                                       w['wout'], w['g1'], w['b1'], w['wr_t'], n_ctx_tiles, alpha)
        slot, aff, bounds = _route_call(logits_t, n_ctx, cap_ctx, cap_lat)
        xs, gate = _gather_call(bounds, slot, aff, h2, n_ctx, cap_ctx, cap_lat)
        yl, yc = _ffn_call(i, xs, gate, p['w_gate'], p['w_up'], p['w_down'], cap_ctx)
        slot_t = jnp.swapaxes(slot, 1, 2)
        xall = _combine_call(bounds, slot_t, yl, yc, x1, mod, w['g2'], w['b2'], n_ctx_tiles, alpha,
                             latent_only=(i == depth - 1))
    return xall


def kernel(x, c, ctx, c_ctx, w_ada, b_ada, w_in, s5_lam_re, s5_lam_im, s5_log_dt, s5_b_re, s5_b_im, s5_c_re, s5_c_im, s5_d, s5_w_glu, s5_b_glu, mla_q_norm, mla_w_uq, mla_kv_norm, mla_w_ukv, win_sink, w_branch, w_out, ln1_g, ln1_b, ln2_g, ln2_b, w_router, w_gate, w_up, w_down):
    return _forward(dict(
        x=x, c=c, ctx=ctx, c_ctx=c_ctx, w_ada=w_ada, b_ada=b_ada, w_in=w_in, s5_lam_re=s5_lam_re,
        s5_lam_im=s5_lam_im, s5_log_dt=s5_log_dt, s5_b_re=s5_b_re, s5_b_im=s5_b_im, s5_c_re=s5_c_re,
        s5_c_im=s5_c_im, s5_d=s5_d, s5_w_glu=s5_w_glu, s5_b_glu=s5_b_glu, mla_q_norm=mla_q_norm,
        mla_w_uq=mla_w_uq, mla_kv_norm=mla_kv_norm, mla_w_ukv=mla_w_ukv, win_sink=win_sink, w_branch=w_branch,
        w_out=w_out, ln1_g=ln1_g, ln1_b=ln1_b, ln2_g=ln2_g, ln2_b=ln2_b, w_router=w_router, w_gate=w_gate,
        w_up=w_up, w_down=w_down))
```

```python
import functools
import math

import jax
import jax.numpy as jnp
import numpy as np
from jax import lax
from jax.experimental import pallas as pl
from jax.experimental.pallas import tpu as pltpu

F32 = jnp.float32
BF16 = jnp.bfloat16
HIGHEST = lax.Precision.HIGHEST

GRID_W = 64
S5_WIDTH = 512
S5_GROUP = 16
S5_GROUPS = S5_WIDTH // S5_GROUP
S5_STATE = 64
S5_CHUNK = 16
MLA_HEADS = 8
MLA_NOPE = 64
MLA_ROPE = 32
MLA_V = 64
MLA_Q_RANK = 384
MLA_KV_RANK = 256
MLA_PAD = 128
MLA_HEADS_PER_STEP = 8
MLA_SCALE = (MLA_NOPE + MLA_ROPE) ** -0.5
WIN_Q_HEADS = 8
WIN_KV_HEADS = 2
WIN_HEAD_DIM = 64
WINDOW = 128
BLOCK = 128
WIN_SCALE = WIN_HEAD_DIM ** -0.5
N_BRANCH = 3
BRANCH_WIDTH = 512
N_EXPERTS = 16
CAPACITY_FACTOR = 2
ROPE_BASE = 10000.0
LN_EPS = 1e-6
NEG_INF = -1e30
LOG2E = math.log2(math.e)
TOKEN_TILE = 256
IN_TILE = 576
MOE_TILE = 256
MOE_WINDOW = 64
MOE_BOUNDS = 16
FFN_SAMPLES = 4
GATHER_EXPERTS = 4
WIN_BLOCKS_PER_STEP = 2
VMEM_LIMIT = 56 * 1024 * 1024


def _cparams(n_axes):
    return pltpu.CompilerParams(dimension_semantics=("arbitrary",) * n_axes, vmem_limit_bytes=VMEM_LIMIT)


def _bdot(a, b):
    return jnp.dot(a, b, preferred_element_type=F32)


def _dot_nt(a, b):
    return lax.dot_general(a, b, (((1,), (1,)), ((), ())), preferred_element_type=F32)


def _layer_norm(x):
    mu = jnp.mean(x, axis=-1, keepdims=True)
    xc = x - mu
    var = jnp.mean(xc * xc, axis=-1, keepdims=True)
    return xc * lax.rsqrt(var + LN_EPS)


def _ada_kernel(cond_ref, w_ref, b_ref, o_ref):
    s = cond_ref[...]
    s = s * jax.nn.sigmoid(s)
    o_ref[0] = jnp.dot(s, w_ref[0], precision=HIGHEST, preferred_element_type=F32) + b_ref[0]


def _ada_call(cond, w_ada, b_ada):
    depth, d, d6 = w_ada.shape
    tn = 1536
    rows = cond.shape[0]
    return pl.pallas_call(
        _ada_kernel,
        grid=(depth, d6 // tn),
        in_specs=[
            pl.BlockSpec((rows, d), lambda i, j: (0, 0)),
            pl.BlockSpec((1, d, tn), lambda i, j: (i, 0, j)),
            pl.BlockSpec((1, 1, tn), lambda i, j: (i, 0, j)),
        ],
        out_specs=pl.BlockSpec((1, rows, tn), lambda i, j: (i, 0, j)),
        out_shape=jax.ShapeDtypeStruct((depth, rows, d6), F32),
        compiler_params=_cparams(2),
    )(cond, w_ada, b_ada.reshape(depth, 1, d6))


IN_WIDTHS = (S5_WIDTH, MLA_Q_RANK, MLA_KV_RANK, MLA_PAD, WIN_Q_HEADS * WIN_HEAD_DIM,
             WIN_KV_HEADS * WIN_HEAD_DIM, WIN_KV_HEADS * WIN_HEAD_DIM)
IN_OFFSETS = tuple(int(v) for v in np.cumsum((0,) + IN_WIDTHS))


def _mod_rows(mod_ref, r, tile, n_ctx):
    row = pl.program_id(1) * tile + lax.broadcasted_iota(jnp.int32, (tile, 1), 0)
    return jnp.where(row < n_ctx, mod_ref[0, 0, r:r + 1, :], mod_ref[0, 1, r:r + 1, :])


def _rope_lanes(x, cos, sa, sb, shift):
    return x * cos + pltpu.roll(x, 128 - shift, 1) * sa + pltpu.roll(x, shift, 1) * sb


def _mla_project(qa, kva, kr, qg, kvg, wq_ref, wk_ref, wv_ref, cos, sa, sb, qp_ref, qr_ref, k_ref, v_ref):
    def rms(x, gain):
        return (x * lax.rsqrt(jnp.mean(x * x, axis=-1, keepdims=True) + LN_EPS) * gain).astype(BF16)

    qn = rms(qa, qg)
    kvn = rms(kva, kvg)
    q = _bdot(qn, wq_ref[...]) * (MLA_SCALE * LOG2E)
    k = _bdot(kvn, wk_ref[...])
    v_ref[0] = _bdot(kvn, wv_ref[...]).astype(BF16)
    kr_rot = _rope_lanes(kr, cos, sa, sb, MLA_ROPE // 2)
    qp_ref[0] = q.astype(BF16)
    for h in range(MLA_HEADS):
        sl = slice(h * MLA_PAD, (h + 1) * MLA_PAD)
        qr_ref[0, :, sl] = _rope_lanes(q[:, sl], cos, sa, sb, MLA_ROPE // 2).astype(BF16)
        k_ref[0, :, sl] = (k[:, sl] + kr_rot).astype(BF16)


def _in_kernel(x_ref, mod_ref, w_ref, qg_ref, kvg_ref, wuq_ref, wuk_ref, wuv_ref, cos_ref, sa_ref, sb_ref,
               u_ref, wq_ref, wk_ref, wv_ref, gate_ref, qp_ref, qr_ref, k_ref, v_ref, *, n_ctx):
    tile = x_ref.shape[1]
    xn = _layer_norm(x_ref[0])
    h = (xn * (1.0 + _mod_rows(mod_ref, 1, tile, n_ctx)) + _mod_rows(mod_ref, 0, tile, n_ctx)).astype(BF16)
    widths = IN_WIDTHS + (gate_ref.shape[-1],)
    proj = lambda i: _bdot(h, w_ref[0, :, IN_OFFSETS[i]:IN_OFFSETS[i] + widths[i]])
    _mla_project(proj(1), proj(2), proj(3), qg_ref[...], kvg_ref[...], wuq_ref, wuk_ref, wuv_ref,
                 cos_ref[...], sa_ref[...], sb_ref[...], qp_ref, qr_ref, k_ref, v_ref)
    u_ref[0] = proj(0)
    wq_ref[0], wk_ref[0], wv_ref[0] = proj(4), proj(5), proj(6)
    gate_ref[0] = proj(7).astype(gate_ref.dtype)


_sigmoid = jax.nn.sigmoid


def _in_call(layer, xall, mod, w_cat, qg, kvg, wuq, wuk, wuv, tabs, n_ctx):
    b, n, d = xall.shape
    tm = IN_TILE
    gate_w = w_cat.shape[-1] - IN_OFFSETS[-1]
    hw = MLA_HEADS * MLA_PAD
    tok = lambda w: pl.BlockSpec((1, tm, w), lambda i, t: (i, t, 0))
    full = lambda a: pl.BlockSpec(a.shape, lambda i, t: (0,) * a.ndim)
    tab = pl.BlockSpec((tm, 128), lambda i, t: (t, 0))
    out_widths = (IN_WIDTHS[0],) + IN_WIDTHS[4:7] + (gate_w, hw, hw, hw, MLA_HEADS * MLA_V)
    out_dtypes = (F32,) * 4 + (BF16,) * 5
    return pl.pallas_call(
        functools.partial(_in_kernel, n_ctx=n_ctx),
        grid=(b, n // tm),
        in_specs=[tok(d), pl.BlockSpec((1, 2, 6, d), lambda i, t: (i, 0, 0, 0)),
                  pl.BlockSpec((1,) + w_cat.shape[1:], lambda i, t: (layer, 0, 0), pipeline_mode=pl.Buffered(1)),
                  full(qg), full(kvg), full(wuq), full(wuk), full(wuv), tab, tab, tab],
        out_specs=[tok(w) for w in out_widths],
        out_shape=[jax.ShapeDtypeStruct((b, n, w), dt) for w, dt in zip(out_widths, out_dtypes)],
        compiler_params=_cparams(2),
    )(xall, mod, w_cat, qg, kvg, wuq, wuk, wuv, *tabs)


S5_LANE_GROUPS = 128 // S5_GROUP
S5_SCAN_GROUPS = 4
S5_PITCH_PAD = 8


def _s5_param_kernel(*refs):
    for g in range(refs[0].shape[2]):
        _s5_param_group(g, *refs)


def _s5_param_group(g, cre_ref, cim_ref, bre_ref, bim_ref, pr_ref, pi_ref, tz_ref, bc_ref, cc_ref, coef_ref):
    t = S5_CHUNK
    w = t * S5_GROUP
    nt = (((1,), (1,)), ((), ()))
    tz = None
    bcs, ccs, coefs = [], [], []
    for d in range(2):
        cre, cim = cre_ref[0, d, g], cim_ref[0, d, g]
        bre, bim = bre_ref[0, d, g], bim_ref[0, d, g]
        power = lambda k: (pr_ref[0, d, g, k:k + 1, :], pi_ref[0, d, g, k:k + 1, :])
        rt = []
        for k in range(t + 1):
            prk, pik = power(k)
            rt.append(jnp.concatenate([cre * prk - cim * pik, -(cre * pik + cim * prk)], axis=1))
        bt = jnp.concatenate([bre, bim], axis=1)
        zeros = jnp.zeros((S5_GROUP, w), F32)
        if d == 0:
            kt = lax.dot_general(bt, jnp.concatenate(rt[:t], axis=0), nt, precision=HIGHEST,
                                 preferred_element_type=F32)
            pad = jnp.concatenate([zeros, kt], axis=1)
            rows = [kt] + [pltpu.roll(pad, S5_GROUP * s, 1)[:, w:] for s in range(1, t)]
        else:
            kt = lax.dot_general(bt, jnp.concatenate(rt[t - 1::-1], axis=0), nt, precision=HIGHEST,
                                 preferred_element_type=F32)
            pad = jnp.concatenate([kt, zeros], axis=1)
            rows = [pltpu.roll(pad, 2 * w - S5_GROUP * (t - 1 - s), 1)[:, :w] for s in range(t - 1)] + [kt]
        tz_d = jnp.concatenate(rows, axis=0)
        tz = tz_d if tz is None else tz + tz_d
        bc_rows = []
        for s in range(t):
            prk, pik = power(t - 1 - s if d == 0 else s)
            br = bre * prk - bim * pik
            bi = bim * prk + bre * pik
            bc_rows.append(jnp.concatenate([br, bi, bi, br], axis=1))
        bcs.append(jnp.concatenate(bc_rows, axis=0))
        ccs.append(jnp.concatenate(rt[1:] if d == 0 else rt[t:0:-1], axis=0))
        er, ei = power(t)
        coefs += [jnp.concatenate([er, er], axis=1), jnp.concatenate([-ei, ei], axis=1),
                  jnp.concatenate([ei, -ei], axis=1)]
    tz_ref[0, g] = tz.astype(BF16)
    bc_ref[0, g] = jnp.concatenate(bcs, axis=1).astype(BF16)
    cc_ref[0, g] = jnp.concatenate(ccs, axis=1).astype(BF16)
    coef_ref[0, g] = jnp.concatenate(coefs + [jnp.zeros((2, 2 * S5_STATE), F32)], axis=0)


def _s5_param_call(p):
    t = S5_CHUNK
    f = lambda name: p[name].astype(F32)
    lam_re, lam_im = f('s5_lam_re'), f('s5_lam_im')
    depth = lam_re.shape[0]
    dt = jnp.exp(f('s5_log_dt'))[..., None]
    k = jnp.arange(t + 1, dtype=F32)[:, None]
    mag = jnp.exp((lam_re * dt)[..., None, :] * k)
    ang = (lam_im * dt)[..., None, :] * k
    pr, pi = mag * jnp.cos(ang), mag * jnp.sin(ang)
    ar, ai = pr[..., 1, :], pi[..., 1, :]
    den = lam_re * lam_re + lam_im * lam_im
    qr = (((ar - 1) * lam_re + ai * lam_im) / den)[..., None, :]
    qi = ((ai * lam_re - (ar - 1) * lam_im) / den)[..., None, :]
    b_re = jnp.swapaxes(f('s5_b_re'), -1, -2)
    b_im = jnp.swapaxes(f('s5_b_im'), -1, -2)
    bbr = qr * b_re - qi * b_im
    bbi = qr * b_im + qi * b_re
    g, hg, ps = S5_GROUPS, S5_GROUP, S5_STATE
    w = t * hg
    gs = S5_LANE_GROUPS
    small = lambda rows: pl.BlockSpec((1, 2, gs, rows, ps), lambda i, j: (i, 0, j, 0, 0))
    out = lambda cols: pl.BlockSpec((1, gs, w, cols), lambda i, j: (i, j, 0, 0))
    return pl.pallas_call(
        _s5_param_kernel,
        grid=(depth, g // gs),
        in_specs=[small(hg)] * 4 + [small(t + 1)] * 2,
        out_specs=[out(w), out(2 * w), out(w), pl.BlockSpec((1, gs, 8, 2 * ps), lambda i, j: (i, j, 0, 0))],
        out_shape=[jax.ShapeDtypeStruct((depth, g, w, w), BF16), jax.ShapeDtypeStruct((depth, g, w, 2 * w), BF16),
                   jax.ShapeDtypeStruct((depth, g, w, w), BF16), jax.ShapeDtypeStruct((depth, g, 8, 2 * ps), F32)],
        compiler_params=_cparams(2),
    )(f('s5_c_re'), f('s5_c_im'), bbr, bbi, pr, pi)


def _s5_kernel(u_ref, tz_ref, bc_ref, cc_ref, coef_ref, d_ref, y_ref, uy_ref, loc_ref, sp_ref, slab_ref,
               *, n_ctx, n_batch):
    ph, b = pl.program_id(1), pl.program_id(2)
    t, hg, ng = S5_CHUNK, S5_GROUP, S5_LANE_GROUPS
    ncc = n_ctx // t
    ncl = (u_ref.shape[1] - n_ctx) // t
    nc = ncc + ncl
    pitch = nc + S5_PITCH_PAD

    def to_chunk_rows(slabs):
        tr = [s.T for s in slabs]
        return [jnp.concatenate([x[g * hg:(g + 1) * hg] for x in tr], axis=0).T for g in range(ng)]

    def to_token_slabs(rows):
        tr = [r.T for r in rows]
        return [jnp.concatenate([x[tau * hg:(tau + 1) * hg] for x in tr], axis=0).T for tau in range(t)]

    base = pl.multiple_of(b * pitch, 8)
    cbase = pl.multiple_of(b * ncc, 8)

    @pl.when(ph == 0)
    def _():
        rows = to_chunk_rows([u_ref[0, pl.ds(n_ctx + tau, ncl, stride=t), :] for tau in range(t)])
        for g in range(ng):
            uy_ref[g, pl.ds(base + ncc, ncl), :] = rows[g]
            uy_ref[g, pl.ds(base + nc, S5_PITCH_PAD), :] = jnp.zeros((S5_PITCH_PAD, t * hg), F32)
        for tau in range(t):
            slab_ref[tau, pl.ds(cbase, ncc), :] = u_ref[0, pl.ds(tau, ncc, stride=t), :]

    @pl.when((ph == 1) & (b == 0))
    def _():
        rows = to_chunk_rows([slab_ref[tau] for tau in range(t)])
        for g in range(ng):
            for s in range(n_batch):
                uy_ref[g, s * pitch:s * pitch + ncc, :] = rows[g][s * ncc:(s + 1) * ncc]
        for part in range(ng // S5_SCAN_GROUPS):
            gs = [part * S5_SCAN_GROUPS + gl for gl in range(S5_SCAN_GROUPS)]
            for gl, g in enumerate(gs):
                ub = uy_ref[g].astype(BF16)
                loc = _bdot(ub, bc_ref[0, g])
                for q in range(4):
                    loc_ref[gl, q] = loc[:, q * 128:(q + 1) * 128]
                uy_ref[g] = _bdot(ub, tz_ref[0, g])
                for s in range(n_batch):
                    for d in range(2):
                        sp_ref[gl, d, s * pitch + nc:(s + 1) * pitch, :] = jnp.zeros((S5_PITCH_PAD, 128), F32)

            def coef(g, r):
                return jnp.broadcast_to(coef_ref[0, g, r:r + 1, :], (n_batch, 128))

            def step(i, carry):
                cb = jnp.where(i < ncc, ncc - 1 - i, nc + ncc - 1 - i)
                fwd = pl.ds(i, n_batch, stride=pitch)
                bwd = pl.ds(cb, n_batch, stride=pitch)
                out = []
                for gl, g in enumerate(gs):
                    v0f, v1f, v0b, v1b = carry[gl]
                    sp_ref[gl, 0, fwd, :] = v0f
                    sp_ref[gl, 1, bwd, :] = v0b
                    n0f = coef(g, 0) * v0f + coef(g, 1) * v1f + loc_ref[gl, 0, fwd, :]
                    n1f = coef(g, 0) * v1f + coef(g, 2) * v0f + loc_ref[gl, 1, fwd, :]
                    n0b = coef(g, 3) * v0b + coef(g, 4) * v1b + loc_ref[gl, 2, bwd, :]
                    n1b = coef(g, 3) * v1b + coef(g, 5) * v0b + loc_ref[gl, 3, bwd, :]
                    out.append((n0f, n1f, n0b, n1b))
                return tuple(out)

            z = jnp.zeros((n_batch, 128), F32)
            lax.fori_loop(0, nc, step, tuple((z, z, z, z) for _ in gs))
            for gl, g in enumerate(gs):
                sp = jnp.concatenate([sp_ref[gl, 0], sp_ref[gl, 1]], axis=1).astype(BF16)
                uy_ref[g] = uy_ref[g] + _dot_nt(sp, cc_ref[0, g])
        rows = [jnp.concatenate([uy_ref[g, s * pitch:s * pitch + ncc, :] for s in range(n_batch)], axis=0)
                for g in range(ng)]
        for tau, slab in enumerate(to_token_slabs(rows)):
            slab_ref[tau] = slab

    @pl.when(ph == 1)
    def _():
        slabs = to_token_slabs([uy_ref[g, pl.ds(base + ncc, ncl), :] for g in range(ng)])
        for tau in range(t):
            y_ref[0, pl.ds(n_ctx + tau, ncl, stride=t), :] = slabs[tau]
            y_ref[0, pl.ds(tau, ncc, stride=t), :] = slab_ref[tau, pl.ds(cbase, ncc), :]
        y_ref[0] = y_ref[0] + d_ref[0] * u_ref[0]


def _s5_call(layer, u, tz, bc, cc, coef, s5_d, n_ctx):
    b, n, width = u.shape
    t, ng = S5_CHUNK, S5_LANE_GROUPS
    rows = b * (n // t + S5_PITCH_PAD)
    assert (n - n_ctx) // t == 128 and b * (n_ctx // t) == 128
    wspec = lambda a: pl.BlockSpec((1, ng) + a.shape[2:], lambda g, ph, i: (layer, g, 0, 0))
    return pl.pallas_call(
        functools.partial(_s5_kernel, n_ctx=n_ctx, n_batch=b),
        grid=(width // 128, 2, b),
        in_specs=[pl.BlockSpec((1, n, 128), lambda g, ph, i: (i, 0, g)),
                  wspec(tz), wspec(bc), wspec(cc), wspec(coef),
                  pl.BlockSpec((1, 1, 128), lambda g, ph, i: (layer, 0, g))],
        out_specs=pl.BlockSpec((1, n, 128), lambda g, ph, i: (i * ph, 0, g)),
        out_shape=jax.ShapeDtypeStruct((b, n, width), F32),
        scratch_shapes=[pltpu.VMEM((ng, rows, t * S5_GROUP), F32),
                        pltpu.VMEM((S5_SCAN_GROUPS, 4, rows, 128), F32),
                        pltpu.VMEM((S5_SCAN_GROUPS, 2, rows, 128), F32),
                        pltpu.VMEM((t, 128, 128), F32)],
        compiler_params=_cparams(3),
    )(u, tz, bc, cc, coef, s5_d.astype(F32).reshape(s5_d.shape[0], 1, width))


def _lane_chunks(xs):
    return [x[:, i * 128:(i + 1) * 128] for x in xs for i in range(x.shape[1] // 128)]


def _row_max(scores, floor=None):
    mm = functools.reduce(jnp.maximum, _lane_chunks(scores))
    if floor is not None:
        mm = jnp.maximum(mm, floor)
    return jnp.max(mm, axis=-1, keepdims=True)


def _softmax_av(scores, values, sink=None):
    m = _row_max(scores, sink)
    ps = [jnp.exp2(s - m) for s in scores]
    ll = functools.reduce(jnp.add, _lane_chunks(ps))
    if sink is not None:
        lane = lax.broadcasted_iota(jnp.int32, sink.shape, 1)
        ll = ll + jnp.where(lane == 0, jnp.exp2(sink - m), 0.0)
    l = jnp.sum(ll, axis=-1, keepdims=True)
    o = functools.reduce(jnp.add, [_bdot(p.astype(BF16), v) for p, v in zip(ps, values)])
    return o / l


def _mla_attn_kernel(qp_ref, qr_ref, k_ref, v_ref, o_ref, *, n_ctx, n_ctx_tiles):
    heads = qp_ref.shape[-1] // MLA_PAD
    t = pl.program_id(2)
    group = 256 // MLA_V
    lane = lax.broadcasted_iota(jnp.int32, (1, 256), 1)

    def run(latent):
        all_scores = []
        for h in range(heads):
            sl = slice(h * MLA_PAD, (h + 1) * MLA_PAD)
            scores = [_dot_nt(qp_ref[0, :, sl], k_ref[0, :n_ctx, sl])]
            if latent:
                scores.append(_dot_nt(qr_ref[0, :, sl], k_ref[0, n_ctx:, sl]))
            all_scores.append(scores)
        probs = []
        for scores in all_scores:
            m = _row_max(scores)
            ps = [jnp.exp2(s - m) for s in scores]
            l = jnp.sum(functools.reduce(jnp.add, _lane_chunks(ps)), axis=-1, keepdims=True)
            probs.append(([p.astype(BF16) for p in ps], l))
        acc = [None] * (heads // group)
        for h, (ps, l) in enumerate(probs):
            blk, hh = divmod(h, group)
            cols = slice(blk * 256, (blk + 1) * 256)
            own = (lane >= hh * MLA_V) & (lane < (hh + 1) * MLA_V)
            zero = jnp.zeros((), BF16)
            values = [jnp.where(own, v_ref[0, :n_ctx, cols], zero)]
            if latent:
                values.append(jnp.where(own, v_ref[0, n_ctx:, cols], zero))
            o = functools.reduce(jnp.add, [_bdot(p, v) for p, v in zip(ps, values)]) / l
            acc[blk] = o if acc[blk] is None else acc[blk] + o
        for blk, o in enumerate(acc):
            o_ref[0, :, blk * 256:(blk + 1) * 256] = o.astype(o_ref.dtype)

    pl.when(t < n_ctx_tiles)(lambda: run(False))
    pl.when(t >= n_ctx_tiles)(lambda: run(True))


def _mla_attn_call(qp, qr, k, v, n_ctx):
    b, n, _ = qp.shape
    tq = TOKEN_TILE
    hp = MLA_HEADS_PER_STEP
    qspec = pl.BlockSpec((1, tq, hp * MLA_PAD), lambda i, h, t: (i, t, h))
    return pl.pallas_call(
        functools.partial(_mla_attn_kernel, n_ctx=n_ctx, n_ctx_tiles=n_ctx // tq),
        grid=(b, MLA_HEADS // hp, n // tq),
        in_specs=[qspec, qspec,
                  pl.BlockSpec((1, n, hp * MLA_PAD), lambda i, h, t: (i, 0, h)),
                  pl.BlockSpec((1, n, hp * MLA_V), lambda i, h, t: (i, 0, h))],
        out_specs=pl.BlockSpec((1, tq, hp * MLA_V), lambda i, h, t: (i, t, h)),
        out_shape=jax.ShapeDtypeStruct((b, n, MLA_HEADS * MLA_V), BF16),
        compiler_params=_cparams(3),
    )(qp, qr, k, v)


def _win_kernel(sink_ref, q_ref, k_ref, v_ref, cos_ref, sa_ref, sb_ref, o_ref, *, n_ctx_blocks, n_blocks):
    j = pl.program_id(1)
    hd = WIN_HEAD_DIM
    half = hd // 2
    lane = lax.broadcasted_iota(jnp.int32, (1, 128), 1)
    lo = jnp.where(lane < hd, 1.0, 0.0)
    hi = 1.0 - lo
    upper_rows = lax.broadcasted_iota(jnp.int32, (2 * BLOCK, 128), 0) < BLOCK

    def lane_halves(x):
        xr = pltpu.roll(x, hd, 1)
        return {(0, 0): (x * lo).astype(BF16), (0, 1): (xr * hi).astype(BF16),
                (1, 0): (xr * lo).astype(BF16), (1, 1): (x * hi).astype(BF16)}

    def attend(out_rows, queries, keys, values, masks):
        all_scores = {}
        for kh in range(WIN_KV_HEADS):
            stacked = [jnp.concatenate([qs[:, (2 * kh) * 128:(2 * kh + 1) * 128],
                                        qs[:, (2 * kh + 1) * 128:(2 * kh + 2) * 128]], axis=0).astype(BF16)
                       for qs in queries]
            for par in range(2):
                scores = []
                for qst, ks, msk in zip(stacked, keys, masks):
                    s = _dot_nt(qst, ks[(kh, par)])
                    scores.append(s if msk is None else jnp.where(msk, s, NEG_INF))
                all_scores[(kh, par)] = scores
        for kh in range(WIN_KV_HEADS):
            acc = None
            for par in range(2):
                sink = jnp.where(upper_rows, sink_ref[4 * kh + par], sink_ref[4 * kh + 2 + par]) * LOG2E
                o = _softmax_av(all_scores[(kh, par)], [vs[(kh, par)] for vs in values], sink)
                acc = o if acc is None else acc + o
            o_ref[0, out_rows, (2 * kh) * 128:(2 * kh + 1) * 128] = acc[:BLOCK].astype(o_ref.dtype)
            o_ref[0, out_rows, (2 * kh + 1) * 128:(2 * kh + 2) * 128] = acc[BLOCK:].astype(o_ref.dtype)

    n_ctx = n_ctx_blocks * BLOCK
    kctx = lane_halves(k_ref[0, :n_ctx, :])
    vctx = lane_halves(v_ref[0, :n_ctx, :])
    subs = [slice(s * BLOCK, (s + 1) * BLOCK) for s in range(q_ref.shape[1] // BLOCK)]

    def ctx_path():
        for rows in subs:
            attend(rows, [q_ref[0, rows, :] * (WIN_SCALE * LOG2E)], [kctx], [vctx], [None])

    def lat_path():
        for s, rows in enumerate(subs):
            lat_block(j * len(subs) + s - n_ctx_blocks, rows)

    def lat_block(blk, out_rows):
        q = q_ref[0, out_rows, :] * (WIN_SCALE * LOG2E)
        band = [pl.ds(pl.multiple_of((n_ctx_blocks + jnp.clip(blk + d, 0, n_blocks - 1)) * BLOCK, BLOCK), BLOCK)
                for d in (-1, 0, 1)]
        rope = lambda x, rows: _rope_lanes(x, cos_ref[rows, :], sa_ref[rows, :], sb_ref[rows, :], half)
        q_rot = jnp.concatenate([rope(q[:, c * 128:(c + 1) * 128], band[1]) for c in range(q.shape[1] // 128)],
                                axis=-1)
        kband = jnp.concatenate([rope(k_ref[0, rows, :], rows) for rows in band], axis=0)
        vband = jnp.concatenate([v_ref[0, rows, :] for rows in band], axis=0)
        r = lax.broadcasted_iota(jnp.int32, (2 * BLOCK, 3 * BLOCK), 0) % BLOCK
        c = lax.broadcasted_iota(jnp.int32, (2 * BLOCK, 3 * BLOCK), 1)
        first = jnp.where(blk > 0, 0, BLOCK)
        last = jnp.where(blk < n_blocks - 1, 3 * BLOCK, 2 * BLOCK)
        valid = (jnp.abs(c - BLOCK - r) <= WINDOW) & (c >= first) & (c < last)
        attend(out_rows, [q_rot, q], [lane_halves(kband), kctx], [lane_halves(vband), vctx], [valid, None])

    pl.when(j * len(subs) < n_ctx_blocks)(ctx_path)
    pl.when(j * len(subs) >= n_ctx_blocks)(lat_path)


def _win_call(sink, wq, wk, wv, tabs, n_ctx):
    b, n, _ = wq.shape
    ncb = n_ctx // BLOCK
    nb = n // BLOCK
    kvw = WIN_KV_HEADS * WIN_HEAD_DIM
    per = WIN_BLOCKS_PER_STEP
    assert ncb % per == 0 and nb % per == 0
    cur = lambda i, j: (i, j, 0)
    kv = pl.BlockSpec((1, n, kvw), lambda i, j: (i, 0, 0))
    tab = pl.BlockSpec((n, 128), lambda i, j: (0, 0))
    return pl.pallas_call(
        functools.partial(_win_kernel, n_ctx_blocks=ncb, n_blocks=nb - ncb),
        grid=(b, nb // per),
        in_specs=[pl.BlockSpec(memory_space=pltpu.SMEM),
                  pl.BlockSpec((1, per * BLOCK, WIN_Q_HEADS * WIN_HEAD_DIM), cur), kv, kv, tab, tab, tab],
        out_specs=pl.BlockSpec((1, per * BLOCK, WIN_Q_HEADS * WIN_HEAD_DIM), cur),
        out_shape=jax.ShapeDtypeStruct((b, n, WIN_Q_HEADS * WIN_HEAD_DIM), BF16),
        compiler_params=_cparams(2),
    )(sink, wq, wk, wv, *tabs)


def _merge_kernel(x_ref, s5_ref, mla_ref, win_ref, gate_ref, mod_ref, wglu_ref, bglu_ref, wbr_ref, wout_ref,
                  g1_ref, b1_ref, wr_ref, x1_ref, h2_ref, lg_ref, *, alpha):
    d = x_ref.shape[-1]
    proj = {1: _bdot(mla_ref[0], wbr_ref[1]), 2: _bdot(win_ref[0], wbr_ref[2])}
    g = jax.nn.gelu(s5_ref[0])
    s5o = g * _sigmoid(_bdot(g.astype(BF16), wglu_ref[...]) + bglu_ref[...])
    proj[0] = _bdot(s5o.astype(BF16), wbr_ref[0])
    mix = None
    for kk in (1, 2, 0):
        term = _sigmoid(gate_ref[0, :, kk * d:(kk + 1) * d].astype(F32)) * proj[kk]
        mix = term if mix is None else mix + term
    y = _bdot(mix.astype(BF16), wout_ref[...])
    mod = lambda r: mod_ref[0, 0, r:r + 1, :]
    x1 = _layer_norm(alpha * x_ref[0] + mod(2) * y) * g1_ref[...] + b1_ref[...]
    x1_ref[0] = x1
    h2 = (_layer_norm(x1) * (1.0 + mod(4)) + mod(3)).astype(BF16)
    h2_ref[0] = h2
    lg_ref[0] = _dot_nt(wr_ref[...], h2)


def _merge_call(xall, s5y, mla_o, win_o, gates, mod, wglu, bglu, wbr, wout, g1, b1, wr_t, n_ctx_tiles, alpha):
    b, n, d = xall.shape
    tm = TOKEN_TILE
    tok = lambda w: pl.BlockSpec((1, tm, w), lambda i, t: (i, t, 0))
    full = lambda a: pl.BlockSpec(a.shape, lambda i, t: (0,) * a.ndim)
    return pl.pallas_call(
        functools.partial(_merge_kernel, alpha=alpha),
        grid=(b, n // tm),
        in_specs=[tok(d), tok(BRANCH_WIDTH), tok(BRANCH_WIDTH), tok(BRANCH_WIDTH), tok(N_BRANCH * d),
                  pl.BlockSpec((1, 1, 6, d), lambda i, t: (i, jnp.where(t < n_ctx_tiles, 0, 1), 0, 0)),
                  full(wglu), full(bglu), full(wbr), full(wout), full(g1), full(b1), full(wr_t)],
        out_specs=[tok(d), tok(d), pl.BlockSpec((1, N_EXPERTS, tm), lambda i, t: (i, 0, t))],
        out_shape=[jax.ShapeDtypeStruct((b, n, d), F32), jax.ShapeDtypeStruct((b, n, d), BF16),
                   jax.ShapeDtypeStruct((b, N_EXPERTS, n), F32)],
        compiler_params=_cparams(2),
    )(xall, s5y, mla_o, win_o, gates, mod, wglu, bglu, wbr, wout, g1, b1, wr_t)


def _excl_cumsum_lanes(m):
    rows, n = m.shape
    r = lax.broadcasted_iota(jnp.int32, (128, 128), 0)
    c = lax.broadcasted_iota(jnp.int32, (128, 128), 1)
    tri = jnp.where(r < c, 1.0, 0.0).astype(BF16)
    off = jnp.zeros((rows, 1), F32)
    outs, offs = [], []
    for jb in range(n // 128):
        blk = m[:, jb * 128:(jb + 1) * 128]
        offs.append(off)
        outs.append(_bdot(blk.astype(BF16), tri) + off)
        off = off + jnp.sum(blk, axis=1, keepdims=True)
    return jnp.concatenate(outs, axis=1), offs + [off]


def _topk_slots(affs, caps):
    bits = [pltpu.bitcast(aff, jnp.int32) for aff in affs]

    def body(i, thrs):
        out = []
        for b, cap, thr in zip(bits, caps, thrs):
            cand = thr | (jnp.int32(1) << (30 - i))
            cnt = jnp.sum(jnp.where(b >= cand, 1.0, 0.0), axis=1, keepdims=True)
            out.append(jnp.where(cnt >= cap, cand, thr))
        return tuple(out)

    zero = jnp.zeros((affs[0].shape[0], 1), jnp.int32)
    thrs = lax.fori_loop(0, 31, body, tuple(zero for _ in affs))
    results = []
    for b, cap, thr in zip(bits, caps, thrs):
        gt = jnp.where(b > thr, 1.0, 0.0)
        eq = jnp.where(b == thr, 1.0, 0.0)
        need = cap - jnp.sum(gt, axis=1, keepdims=True)
        sel = gt + eq * jnp.where(_excl_cumsum_lanes(eq)[0] < need, 1.0, 0.0)
        rank, offs = _excl_cumsum_lanes(sel)
        results.append((jnp.where(sel > 0.5, rank, -1.0).astype(jnp.int32), offs[::MOE_TILE // 128]))
    return results


def _route_kernel(lg_ref, slot_ref, aff_ref, bnd_ref, *, n_ctx, cap_ctx, cap_lat):
    lg = lg_ref[0]
    m = jnp.max(lg, axis=0, keepdims=True)
    ex = jnp.exp(lg - m)
    aff = ex / jnp.sum(ex, axis=0, keepdims=True)
    aff_ref[0] = aff
    (slots_ctx, _), (slots, counts) = _topk_slots([aff[:, :n_ctx], aff[:, n_ctx:]], [cap_ctx, cap_lat])
    slot_ref[0, :, :n_ctx] = slots_ctx
    slot_ref[0, :, n_ctx:] = slots
    lane = lax.broadcasted_iota(jnp.int32, bnd_ref.shape[1:], 1)
    bnd = jnp.zeros(bnd_ref.shape[1:], F32)
    for k, cnt in enumerate(counts):
        bnd = jnp.where(lane == k, cnt, bnd)
    bnd_ref[0] = bnd.astype(jnp.int32)


def _route_call(logits_t, n_ctx, cap_ctx, cap_lat):
    b, e, n = logits_t.shape
    assert (n - n_ctx) % MOE_TILE == 0 and (n - n_ctx) // MOE_TILE < MOE_BOUNDS
    spec = pl.BlockSpec((1, e, n), lambda i: (i, 0, 0))
    return pl.pallas_call(
        functools.partial(_route_kernel, n_ctx=n_ctx, cap_ctx=cap_ctx, cap_lat=cap_lat),
        grid=(b,),
        in_specs=[spec],
        out_specs=[spec, spec, pl.BlockSpec((1, e, MOE_BOUNDS), lambda i: (i, 0, 0))],
        out_shape=[jax.ShapeDtypeStruct((b, e, n), jnp.int32), jax.ShapeDtypeStruct((b, e, n), F32),
                   jax.ShapeDtypeStruct((b, e, MOE_BOUNDS), jnp.int32)],
        compiler_params=_cparams(1),
    )(logits_t)


def _gather_kernel(bnd_ref, slot_ref, aff_ref, h_ref, xs_ref, gate_ref, xl_ref, gl_ref, *, n_ctx, cap_ctx):
    for j in range(xs_ref.shape[1]):
        _gather_expert(j, bnd_ref, slot_ref, aff_ref, h_ref, xs_ref, gate_ref, xl_ref.at[j], gl_ref.at[j],
                       n_ctx=n_ctx, cap_ctx=cap_ctx)


def _gather_expert(j, bnd_ref, slot_ref, aff_ref, h_ref, xs_ref, gate_ref, xl_ref, gl_ref, *, n_ctx, cap_ctx):
    cap_lat = xs_ref.shape[2] - cap_ctx
    slot = slot_ref[0, j]
    aff = aff_ref[0, j]
    n = h_ref.shape[1]
    ib, ie = pl.program_id(0), pl.program_id(1) * xs_ref.shape[1] + j
    xl_ref[...] = jnp.zeros(xl_ref.shape, F32)
    gl_ref[...] = jnp.zeros(gl_ref.shape, F32)
    tiles = range((n - n_ctx) // MOE_TILE)
    firsts = [(bnd_ref[ib, ie, kt] // 16) * 16 for kt in tiles]

    def window(kt, start):
        tok = slice(n_ctx + kt * MOE_TILE, n_ctx + (kt + 1) * MOE_TILE)
        rows = start + lax.broadcasted_iota(jnp.int32, (MOE_WINDOW, MOE_TILE), 0)
        hit = slot[:, tok] == rows
        picked = jnp.sum(jnp.where(hit, aff[:, tok], 0.0), axis=1, keepdims=True)
        return _bdot(jnp.where(hit, 1.0, 0.0).astype(BF16), h_ref[0, tok, :]), picked

    def add_window(kt, start, parts=None):
        start = pl.multiple_of(start, 16)
        rows, picked = parts if parts is not None else window(kt, start)
        xl_ref[pl.ds(start, MOE_WINDOW), :] += rows
        gl_ref[pl.ds(start, MOE_WINDOW), :] += jnp.broadcast_to(picked, (MOE_WINDOW, 128))

    first_windows = [window(kt, firsts[kt]) for kt in tiles]
    for kt in tiles:
        add_window(kt, firsts[kt], first_windows[kt])
    for kt in tiles:
        def more(w, carry, kt=kt):
            add_window(kt, firsts[kt] + (w + 1) * MOE_WINDOW)
            return carry

        n_win = (bnd_ref[ib, ie, kt + 1] - firsts[kt] + MOE_WINDOW - 1) // MOE_WINDOW
        lax.fori_loop(0, jnp.maximum(n_win - 1, 0), more, 0)
    iota = lax.broadcasted_iota(jnp.int32, (cap_ctx, n_ctx), 0)
    hit = slot[:, :n_ctx] == iota
    xc = _bdot(jnp.where(hit, 1.0, 0.0).astype(BF16), h_ref[0, :n_ctx, :])
    gc = jnp.sum(jnp.where(hit, aff[:, :n_ctx], 0.0), axis=1, keepdims=True)
    xs_ref[0, j, :cap_lat] = xl_ref[:cap_lat].astype(BF16)
    xs_ref[0, j, cap_lat:] = xc.astype(BF16)
    gate_ref[0, j, :cap_lat] = gl_ref[:cap_lat]
    gate_ref[0, j, cap_lat:] = jnp.broadcast_to(gc, (cap_ctx, 128))


def _gather_call(bounds, slot, aff, h2, n_ctx, cap_ctx, cap_lat):
    b, e, n = slot.shape
    d = h2.shape[-1]
    cap = cap_lat + cap_ctx
    ne = GATHER_EXPERTS
    row = pl.BlockSpec((1, ne, 1, n), lambda ib, ie: (ib, ie, 0, 0))
    return pl.pallas_call(
        functools.partial(_gather_kernel, n_ctx=n_ctx, cap_ctx=cap_ctx),
        grid=(b, e // ne),
        in_specs=[pl.BlockSpec(memory_space=pltpu.SMEM), row, row, pl.BlockSpec((1, n, d), lambda ib, ie: (ib, 0, 0))],
        out_specs=[pl.BlockSpec((1, ne, cap, d), lambda ib, ie: (ib, ie, 0, 0)),
                   pl.BlockSpec((1, ne, cap, 128), lambda ib, ie: (ib, ie, 0, 0))],
        out_shape=[jax.ShapeDtypeStruct((b, e, cap, d), BF16), jax.ShapeDtypeStruct((b, e, cap, 128), F32)],
        scratch_shapes=[pltpu.VMEM((ne, cap_lat + MOE_WINDOW, d), F32),
                        pltpu.VMEM((ne, cap_lat + MOE_WINDOW, 128), F32)],
        compiler_params=_cparams(2),
    )(bounds, slot.reshape(b, e, 1, n), aff.reshape(b, e, 1, n), h2)


def _ffn_kernel(xs_ref, gate_ref, wg_ref, wu_ref, wd_ref, yl_ref, yc_ref, wg_s, wu_s, wd_s):
    ns, _, cap, d = xs_ref.shape
    cap_ctx = yc_ref.shape[2]
    cap_lat = cap - cap_ctx

    @pl.when(pl.program_id(1) == 0)
    def _():
        wg_s[...] = wg_ref[0, 0].astype(BF16)
        wu_s[...] = wu_ref[0, 0].astype(BF16)
        wd_s[...] = wd_ref[0, 0].astype(BF16)

    au = [(_bdot(xs_ref[s, 0], wg_s[...]), _bdot(xs_ref[s, 0], wu_s[...])) for s in range(ns)]
    for s, (a, u) in enumerate(au):
        hm = (a * _sigmoid(a) * u).astype(BF16)
        y = _bdot(hm, wd_s[...]) * gate_ref[s, 0, :, 0:1]
        yl_ref[s, 0, :cap_lat] = y[:cap_lat].astype(yl_ref.dtype)
        yl_ref[s, 0, cap_lat:] = jnp.zeros((MOE_WINDOW, d), yl_ref.dtype)
        yc_ref[s, 0] = y[cap_lat:].astype(yc_ref.dtype)


def _ffn_call(layer, xs, gate, w_gate, w_up, w_down, cap_ctx):
    b, e, cap, d = xs.shape
    f = w_gate.shape[-1]
    ns = FFN_SAMPLES
    rows_lat = cap - cap_ctx + MOE_WINDOW
    tok = lambda rows, w: pl.BlockSpec((ns, 1, rows, w), lambda ie, j: (j, ie, 0, 0))
    return pl.pallas_call(
        _ffn_kernel,
        grid=(e, b // ns),
        in_specs=[tok(cap, d), tok(cap, 128),
                  pl.BlockSpec((1, 1, d, f), lambda ie, j: (layer, ie, 0, 0)),
                  pl.BlockSpec((1, 1, d, f), lambda ie, j: (layer, ie, 0, 0)),
                  pl.BlockSpec((1, 1, f, d), lambda ie, j: (layer, ie, 0, 0))],
        out_specs=[tok(rows_lat, d), tok(cap_ctx, d)],
        out_shape=[jax.ShapeDtypeStruct((b, e, rows_lat, d), BF16), jax.ShapeDtypeStruct((b, e, cap_ctx, d), BF16)],
        scratch_shapes=[pltpu.VMEM((d, f), BF16), pltpu.VMEM((d, f), BF16), pltpu.VMEM((f, d), BF16)],
        compiler_params=_cparams(2),
    )(xs, gate, w_gate, w_up, w_down)


def _combine_kernel(bnd_ref, slot_ref, yl_ref, yc_ref, x1_ref, mod_ref, g2_ref, b2_ref, o_ref, fl_ref,
                    *, n_ctx_tiles, alpha):
    ib, t = pl.program_id(0), pl.program_id(1)
    tm = x1_ref.shape[1]
    slot = slot_ref[0]
    win = MOE_WINDOW

    def finish(fl):
        x1 = x1_ref[0]
        o_ref[0] = _layer_norm(alpha * x1 + mod_ref[0, 0, 5:6, :] * fl) * g2_ref[...] + b2_ref[...]

    def onehot(e, first, width):
        iota = lax.broadcasted_iota(jnp.int32, (tm, width), 1)
        return jnp.where(slot[:, e:e + 1] - first == iota, 1.0, 0.0).astype(BF16)

    def ctx_path():
        cap = yc_ref.shape[2]
        fl = None
        for e in range(N_EXPERTS):
            term = _bdot(onehot(e, 0, cap), yc_ref[0, e])
            fl = term if fl is None else fl + term
        finish(fl)

    def lat_path():
        kt = t - n_ctx_tiles
        firsts = [pl.multiple_of((bnd_ref[ib, e, kt] // 16) * 16, 16) for e in range(N_EXPERTS)]
        lane = lax.broadcasted_iota(jnp.int32, (tm, 2 * win), 1)
        pieces, ywins = [], []
        for e in range(0, N_EXPERTS, 2):
            rel = jnp.where(lane < win, slot[:, e:e + 1] - firsts[e], slot[:, e + 1:e + 2] - firsts[e + 1] + win)
            pieces.append(jnp.where(rel == lane, 1.0, 0.0).astype(BF16))
            ywins += [yl_ref[0, e, pl.ds(firsts[e], win), :], yl_ref[0, e + 1, pl.ds(firsts[e + 1], win), :]]
        fl_ref[...] = _bdot(jnp.concatenate(pieces, axis=1), jnp.concatenate(ywins, axis=0))
        for e in range(N_EXPERTS):
            def window(w, carry, e=e):
                first = pl.multiple_of(firsts[e] + (w + 1) * win, 16)
                fl_ref[...] += _bdot(onehot(e, first, win), yl_ref[0, e, pl.ds(first, win), :])
                return carry

            n_win = (bnd_ref[ib, e, kt + 1] - firsts[e] + win - 1) // win
            lax.fori_loop(0, jnp.maximum(n_win - 1, 0), window, 0)
        finish(fl_ref[...])

    pl.when(t < n_ctx_tiles)(ctx_path)
    pl.when(t >= n_ctx_tiles)(lat_path)


def _combine_call(bounds, slot_t, yl, yc, x1, mod, g2, b2, n_ctx_tiles, alpha, latent_only):
    b, n, d = x1.shape
    tm = MOE_TILE
    e = N_EXPERTS
    full = lambda a: pl.BlockSpec(a.shape, lambda i, t: (0,) * a.ndim)
    skip = n_ctx_tiles if latent_only else 0
    return pl.pallas_call(
        functools.partial(_combine_kernel, n_ctx_tiles=n_ctx_tiles, alpha=alpha),
        grid=(b, n // tm),
        scratch_shapes=[pltpu.VMEM((tm, d), F32)],
        in_specs=[pl.BlockSpec(memory_space=pltpu.SMEM),
                  pl.BlockSpec((1, tm, e), lambda i, t: (i, t, 0)),
                  pl.BlockSpec((1,) + yl.shape[1:], lambda i, t: (i, 0, 0, 0)),
                  pl.BlockSpec((1,) + yc.shape[1:], lambda i, t: (i, 0, 0, 0)),
                  pl.BlockSpec((1, tm, d), lambda i, t: (i, t, 0)),
                  pl.BlockSpec((1, 1, 6, d), lambda i, t: (i, jnp.where(t < n_ctx_tiles, 0, 1), 0, 0)),
                  full(g2), full(b2)],
        out_specs=pl.BlockSpec((1, tm, d), lambda i, t: (i, jnp.maximum(t - skip, 0), 0)),
        out_shape=jax.ShapeDtypeStruct((b, n - skip * tm, d), F32),
        compiler_params=_cparams(2),
    )(bounds, slot_t, yl, yc, x1, mod, g2, b2)


def _rope_tables(n_ctx, seq, head_dim, lane_offset):
    half = head_dim // 2
    nf = head_dim // 4
    t = jnp.arange(seq, dtype=F32)
    row = jnp.floor(t / GRID_W)
    col = t - row * GRID_W
    freqs = ROPE_BASE ** (-jnp.arange(nf, dtype=F32) / nf)
    ang = jnp.concatenate([row[:, None] * freqs, col[:, None] * freqs], axis=-1)
    cos, sin = jnp.cos(ang), jnp.sin(ang)
    zeros = jnp.zeros_like(sin)
    n_heads = (128 - lane_offset) // head_dim if lane_offset == 0 else 1
    c = jnp.concatenate([jnp.ones((seq, lane_offset), F32)] + [cos, cos] * n_heads, axis=-1)
    sa = jnp.concatenate([jnp.zeros((seq, lane_offset), F32)] + [-sin, zeros] * n_heads, axis=-1)
    sb = jnp.concatenate([jnp.zeros((seq, lane_offset), F32)] + [zeros, sin] * n_heads, axis=-1)
    pad = 128 - c.shape[1]
    c = jnp.pad(c, ((n_ctx, 0), (0, pad)), constant_values=1.0)
    sa = jnp.pad(sa, ((n_ctx, 0), (0, pad)))
    sb = jnp.pad(sb, ((n_ctx, 0), (0, pad)))
    return c, sa, sb


def _in_weights(w_in):
    lo = S5_WIDTH + MLA_Q_RANK + MLA_KV_RANK
    hi = lo + MLA_ROPE
    zeros = lambda n: jnp.zeros(w_in.shape[:2] + (n,), w_in.dtype)
    return jnp.concatenate([w_in[..., :lo], zeros(MLA_NOPE), w_in[..., lo:hi],
                            zeros(MLA_PAD - MLA_NOPE - MLA_ROPE), w_in[..., hi:]], axis=-1).astype(BF16)


def _layer_weights(i, p):
    dq = MLA_NOPE + MLA_ROPE
    wq = p['mla_w_uq'][i].reshape(MLA_Q_RANK, MLA_HEADS, dq)
    wq = jnp.pad(wq, ((0, 0), (0, 0), (0, MLA_PAD - dq))).reshape(MLA_Q_RANK, MLA_HEADS * MLA_PAD)
    wkv = p['mla_w_ukv'][i].reshape(MLA_KV_RANK, MLA_HEADS, MLA_NOPE + MLA_V)
    wk = jnp.pad(wkv[:, :, :MLA_NOPE], ((0, 0), (0, 0), (0, MLA_PAD - MLA_NOPE)))
    wk = wk.reshape(MLA_KV_RANK, MLA_HEADS * MLA_PAD)
    wv = wkv[:, :, MLA_NOPE:].reshape(MLA_KV_RANK, MLA_HEADS * MLA_V)
    row = lambda a: a[i].astype(F32).reshape(1, -1)
    return dict(
        wq=wq.astype(BF16), wk=wk.astype(BF16), wv=wv.astype(BF16),
        qg=row(p['mla_q_norm']), kvg=row(p['mla_kv_norm']),
        wglu=p['s5_w_glu'][i].astype(BF16), bglu=row(p['s5_b_glu']),
        sink=p['win_sink'][i].astype(F32),
        wbr=p['w_branch'][i].astype(BF16), wout=p['w_out'][i].astype(BF16),
        g1=row(p['ln1_g']), b1=row(p['ln1_b']), g2=row(p['ln2_g']), b2=row(p['ln2_b']),
        wr_t=p['w_router'][i].T.astype(BF16),
    )


def _forward(p):
    x, c, ctx, c_ctx = p['x'], p['c'], p['ctx'], p['c_ctx']
    b, seq, d = x.shape
    n_ctx = ctx.shape[1]
    depth = p['w_ada'].shape[0]
    assert b == 8 and seq % TOKEN_TILE == 0 and n_ctx % TOKEN_TILE == 0 and seq % GRID_W == 0
    alpha = float((2 * depth) ** 0.25)
    n_ctx_tiles = n_ctx // TOKEN_TILE
    cap_lat = CAPACITY_FACTOR * seq // N_EXPERTS
    cap_ctx = CAPACITY_FACTOR * n_ctx // N_EXPERTS

    cond = jnp.concatenate([c, c_ctx[None], jnp.zeros((16 - b - 1, d), F32)], axis=0)
    mods = _ada_call(cond, p['w_ada'], p['b_ada'])
    mods = mods.reshape(depth, 16, 6, d)
    tabs_mla = _rope_tables(n_ctx, seq, MLA_ROPE, MLA_NOPE)
    tabs_win = _rope_tables(n_ctx, seq, WIN_HEAD_DIM, 0)
    s5w = _s5_param_call(p)
    w_cat = _in_weights(p['w_in'])

    xall = jnp.concatenate([ctx, x], axis=1)
    for i in range(depth):
        w = _layer_weights(i, p)
        mod = jnp.stack([jnp.broadcast_to(mods[i, b], (b, 6, d)), mods[i, :b]], axis=1)
        u, wq, wk, wv, gates, qp, qr, kk, vv = _in_call(i, xall, mod, w_cat, w['qg'], w['kvg'], w['wq'], w['wk'],
                                                        w['wv'], tabs_mla, n_ctx)
        s5y = _s5_call(i, u, *s5w, p['s5_d'], n_ctx)
        mla_o = _mla_attn_call(qp, qr, kk, vv, n_ctx)
        win_o = _win_call(w['sink'], wq, wk, wv, tabs_win, n_ctx)
        x1, h2, logits_t = _merge_call(xall, s5y, mla_o, win_o, gates, mod, w['wglu'], w['bglu'], w['wbr'],
                                       w['wout'], w['g1'], w['b1'], w['wr_t'], n_ctx_tiles, alpha)
        slot, aff, bounds = _route_call(logits_t, n_ctx, cap_ctx, cap_lat)
        xs, gate = _gather_call(bounds, slot, aff, h2, n_ctx, cap_ctx, cap_lat)
        yl, yc = _ffn_call(i, xs, gate, p['w_gate'], p['w_up'], p['w_down'], cap_ctx)
        slot_t = jnp.swapaxes(slot, 1, 2)
        xall = _combine_call(bounds, slot_t, yl, yc, x1, mod, w['g2'], w['b2'], n_ctx_tiles, alpha,
                             latent_only=(i == depth - 1))
    return xall


def kernel(x, c, ctx, c_ctx, w_ada, b_ada, w_in, s5_lam_re, s5_lam_im, s5_log_dt, s5_b_re, s5_b_im, s5_c_re, s5_c_im, s5_d, s5_w_glu, s5_b_glu, mla_q_norm, mla_w_uq, mla_kv_norm, mla_w_ukv, win_sink, w_branch, w_out, ln1_g, ln1_b, ln2_g, ln2_b, w_router, w_gate, w_up, w_down):
    return _forward(dict(
        x=x, c=c, ctx=ctx, c_ctx=c_ctx, w_ada=w_ada, b_ada=b_ada, w_in=w_in, s5_lam_re=s5_lam_re,
        s5_lam_im=s5_lam_im, s5_log_dt=s5_log_dt, s5_b_re=s5_b_re, s5_b_im=s5_b_im, s5_c_re=s5_c_re,
        s5_c_im=s5_c_im, s5_d=s5_d, s5_w_glu=s5_w_glu, s5_b_glu=s5_b_glu, mla_q_norm=mla_q_norm,
        mla_w_uq=mla_w_uq, mla_kv_norm=mla_kv_norm, mla_w_ukv=mla_w_ukv, win_sink=win_sink, w_branch=w_branch,
        w_out=w_out, ln1_g=ln1_g, ln1_b=ln1_b, ln2_g=ln2_g, ln2_b=ln2_b, w_router=w_router, w_gate=w_gate,
        w_up=w_up, w_down=w_down))
```

```python
import functools
import math

import jax
import jax.numpy as jnp
import numpy as np
from jax import lax
from jax.experimental import pallas as pl
from jax.experimental.pallas import tpu as pltpu

F32 = jnp.float32
BF16 = jnp.bfloat16
HIGHEST = lax.Precision.HIGHEST

GRID_W = 64
S5_WIDTH = 512
S5_GROUP = 16
S5_GROUPS = S5_WIDTH // S5_GROUP
S5_STATE = 64
S5_CHUNK = 16
MLA_HEADS = 8
MLA_NOPE = 64
MLA_ROPE = 32
MLA_V = 64
MLA_Q_RANK = 384
MLA_KV_RANK = 256
MLA_PAD = 128
MLA_HEADS_PER_STEP = 8
MLA_SCALE = (MLA_NOPE + MLA_ROPE) ** -0.5
WIN_Q_HEADS = 8
WIN_KV_HEADS = 2
WIN_HEAD_DIM = 64
WINDOW = 128
BLOCK = 128
WIN_SCALE = WIN_HEAD_DIM ** -0.5
N_BRANCH = 3
BRANCH_WIDTH = 512
N_EXPERTS = 16
CAPACITY_FACTOR = 2
ROPE_BASE = 10000.0
LN_EPS = 1e-6
NEG_INF = -1e30
LOG2E = math.log2(math.e)
TOKEN_TILE = 256
IN_TILE = 384
MOE_TILE = 256
MOE_WINDOW = 64
MOE_BOUNDS = 16
FFN_SAMPLES = 4
GATHER_EXPERTS = 4
WIN_BLOCKS_PER_STEP = 2
VMEM_LIMIT =56 * 1024 * 1024


def _cparams(n_axes):
    return pltpu.CompilerParams(dimension_semantics=("arbitrary",) * n_axes, vmem_limit_bytes=VMEM_LIMIT)


def _bdot(a, b):
    return jnp.dot(a, b, preferred_element_type=F32)


def _dot_nt(a, b):
    return lax.dot_general(a, b, (((1,), (1,)), ((), ())), preferred_element_type=F32)


def _layer_norm(x):
    mu = jnp.mean(x, axis=-1, keepdims=True)
    xc = x - mu
    var = jnp.mean(xc * xc, axis=-1, keepdims=True)
    return xc * lax.rsqrt(var + LN_EPS)


def _ada_kernel(cond_ref, w_ref, b_ref, o_ref):
    s = cond_ref[...]
    s = s * jax.nn.sigmoid(s)
    o_ref[0] = jnp.dot(s, w_ref[0], precision=HIGHEST, preferred_element_type=F32) + b_ref[0]


def _ada_call(cond, w_ada, b_ada):
    depth, d, d6 = w_ada.shape
    tn = 1536
    rows = cond.shape[0]
    return pl.pallas_call(
        _ada_kernel,
        grid=(depth, d6 // tn),
        in_specs=[
            pl.BlockSpec((rows, d), lambda i, j: (0, 0)),
            pl.BlockSpec((1, d, tn), lambda i, j: (i, 0, j)),
            pl.BlockSpec((1, 1, tn), lambda i, j: (i, 0, j)),
        ],
        out_specs=pl.BlockSpec((1, rows, tn), lambda i, j: (i, 0, j)),
        out_shape=jax.ShapeDtypeStruct((depth, rows, d6), F32),
        compiler_params=_cparams(2),
    )(cond, w_ada, b_ada.reshape(depth, 1, d6))


IN_WIDTHS = (S5_WIDTH, MLA_Q_RANK, MLA_KV_RANK, MLA_PAD, WIN_Q_HEADS * WIN_HEAD_DIM,
             WIN_KV_HEADS * WIN_HEAD_DIM, WIN_KV_HEADS * WIN_HEAD_DIM)
IN_OFFSETS = tuple(int(v) for v in np.cumsum((0,) + IN_WIDTHS))


def _mod_rows(mod_ref, r, tile, n_ctx):
    row = pl.program_id(1) * tile + lax.broadcasted_iota(jnp.int32, (tile, 1), 0)
    return jnp.where(row < n_ctx, mod_ref[0, 0, r:r + 1, :], mod_ref[0, 1, r:r + 1, :])


def _rope_lanes(x, cos, sa, sb, shift):
    return x * cos + pltpu.roll(x, 128 - shift, 1) * sa + pltpu.roll(x, shift, 1) * sb


def _mla_project(qa, kva, kr, qg, kvg, wq_ref, wk_ref, wv_ref, cos, sa, sb, qp_ref, qr_ref, k_ref, v_ref):
    def rms(x, gain):
        return (x * lax.rsqrt(jnp.mean(x * x, axis=-1, keepdims=True) + LN_EPS) * gain).astype(BF16)

    qn = rms(qa, qg)
    kvn = rms(kva, kvg)
    q = _bdot(qn, wq_ref[...]) * (MLA_SCALE * LOG2E)
    k = _bdot(kvn, wk_ref[...])
    v_ref[0] = _bdot(kvn, wv_ref[...]).astype(BF16)
    kr_rot = _rope_lanes(kr, cos, sa, sb, MLA_ROPE // 2)
    qp_ref[0] = q.astype(BF16)
    for h in range(MLA_HEADS):
        sl = slice(h * MLA_PAD, (h + 1) * MLA_PAD)
        qr_ref[0, :, sl] = _rope_lanes(q[:, sl], cos, sa, sb, MLA_ROPE // 2).astype(BF16)
        k_ref[0, :, sl] = (k[:, sl] + kr_rot).astype(BF16)


def _in_kernel(x_ref, mod_ref, w_ref, qg_ref, kvg_ref, wuq_ref, wuk_ref, wuv_ref, cos_ref, sa_ref, sb_ref,
               u_ref, wq_ref, wk_ref, wv_ref, gate_ref, qp_ref, qr_ref, k_ref, v_ref, *, n_ctx):
    tile = x_ref.shape[1]
    xn = _layer_norm(x_ref[0])
    h = (xn * (1.0 + _mod_rows(mod_ref, 1, tile, n_ctx)) + _mod_rows(mod_ref, 0, tile, n_ctx)).astype(BF16)
    widths = IN_WIDTHS + (gate_ref.shape[-1],)
    proj = lambda i: _bdot(h, w_ref[0, :, IN_OFFSETS[i]:IN_OFFSETS[i] + widths[i]])
    _mla_project(proj(1), proj(2), proj(3), qg_ref[...], kvg_ref[...], wuq_ref, wuk_ref, wuv_ref,
                 cos_ref[...], sa_ref[...], sb_ref[...], qp_ref, qr_ref, k_ref, v_ref)
    u_ref[0] = proj(0)
    wq_ref[0], wk_ref[0], wv_ref[0] = proj(4), proj(5), proj(6)
    gate_ref[0] = proj(7).astype(gate_ref.dtype)


_sigmoid = jax.nn.sigmoid


def _in_call(layer, xall, mod, w_cat, qg, kvg, wuq, wuk, wuv, tabs, n_ctx):
    b, n, d = xall.shape
    tm = IN_TILE
    gate_w = w_cat.shape[-1] - IN_OFFSETS[-1]
    hw = MLA_HEADS * MLA_PAD
    tok = lambda w: pl.BlockSpec((1, tm, w), lambda i, t: (i, t, 0))
    full = lambda a: pl.BlockSpec(a.shape, lambda i, t: (0,) * a.ndim)
    tab = pl.BlockSpec((tm, 128), lambda i, t: (t, 0))
    out_widths = (IN_WIDTHS[0],) + IN_WIDTHS[4:7] + (gate_w, hw, hw, hw, MLA_HEADS * MLA_V)
    out_dtypes = (F32,) * 4 + (BF16,) * 5
    return pl.pallas_call(
        functools.partial(_in_kernel, n_ctx=n_ctx),
        grid=(b, n // tm),
        in_specs=[tok(d), pl.BlockSpec((1, 2, 6, d), lambda i, t: (i, 0, 0, 0)),
                  pl.BlockSpec((1,) + w_cat.shape[1:], lambda i, t: (layer, 0, 0), pipeline_mode=pl.Buffered(1)),
                  full(qg), full(kvg), full(wuq), full(wuk), full(wuv), tab, tab, tab],
        out_specs=[tok(w) for w in out_widths],
        out_shape=[jax.ShapeDtypeStruct((b, n, w), dt) for w, dt in zip(out_widths, out_dtypes)],
        compiler_params=_cparams(2),
    )(xall, mod, w_cat, qg, kvg, wuq, wuk, wuv, *tabs)


S5_LANE_GROUPS = 128 // S5_GROUP
S5_SCAN_GROUPS = 4
S5_PITCH_PAD = 8


def _s5_param_kernel(*refs):
    for g in range(refs[0].shape[2]):
        _s5_param_group(g, *refs)


def _s5_param_group(g, cre_ref, cim_ref, bre_ref, bim_ref, pr_ref, pi_ref, tz_ref, bc_ref, cc_ref, coef_ref):
    t = S5_CHUNK
    w = t * S5_GROUP
    nt = (((1,), (1,)), ((), ()))
    tz = None
    bcs, ccs, coefs = [], [], []
    for d in range(2):
        cre, cim = cre_ref[0, d, g], cim_ref[0, d, g]
        bre, bim = bre_ref[0, d, g], bim_ref[0, d, g]
        power = lambda k: (pr_ref[0, d, g, k:k + 1, :], pi_ref[0, d, g, k:k + 1, :])
        rt = []
        for k in range(t + 1):
            prk, pik = power(k)
            rt.append(jnp.concatenate([cre * prk - cim * pik, -(cre * pik + cim * prk)], axis=1))
        bt = jnp.concatenate([bre, bim], axis=1)
        zeros = jnp.zeros((S5_GROUP, w), F32)
        if d == 0:
            kt = lax.dot_general(bt, jnp.concatenate(rt[:t], axis=0), nt, precision=HIGHEST,
                                 preferred_element_type=F32)
            pad = jnp.concatenate([zeros, kt], axis=1)
            rows = [kt] + [pltpu.roll(pad, S5_GROUP * s, 1)[:, w:] for s in range(1, t)]
        else:
            kt = lax.dot_general(bt, jnp.concatenate(rt[t - 1::-1], axis=0), nt, precision=HIGHEST,
                                 preferred_element_type=F32)
            pad = jnp.concatenate([kt, zeros], axis=1)
            rows = [pltpu.roll(pad, 2 * w - S5_GROUP * (t - 1 - s), 1)[:, :w] for s in range(t - 1)] + [kt]
        tz_d = jnp.concatenate(rows, axis=0)
        tz = tz_d if tz is None else tz + tz_d
        bc_rows = []
        for s in range(t):
            prk, pik = power(t - 1 - s if d == 0 else s)
            br = bre * prk - bim * pik
            bi = bim * prk + bre * pik
            bc_rows.append(jnp.concatenate([br, bi, bi, br], axis=1))
        bcs.append(jnp.concatenate(bc_rows, axis=0))
        ccs.append(jnp.concatenate(rt[1:] if d == 0 else rt[t:0:-1], axis=0))
        er, ei = power(t)
        coefs += [jnp.concatenate([er, er], axis=1), jnp.concatenate([-ei, ei], axis=1),
                  jnp.concatenate([ei, -ei], axis=1)]
    tz_ref[0, g] = tz.astype(BF16)
    bc_ref[0, g] = jnp.concatenate(bcs, axis=1).astype(BF16)
    cc_ref[0, g] = jnp.concatenate(ccs, axis=1).astype(BF16)
    coef_ref[0, g] = jnp.concatenate(coefs + [jnp.zeros((2, 2 * S5_STATE), F32)], axis=0)


def _s5_param_call(p):
    t = S5_CHUNK
    f = lambda name: p[name].astype(F32)
    lam_re, lam_im = f('s5_lam_re'), f('s5_lam_im')
    depth = lam_re.shape[0]
    dt = jnp.exp(f('s5_log_dt'))[..., None]
    k = jnp.arange(t + 1, dtype=F32)[:, None]
    mag = jnp.exp((lam_re * dt)[..., None, :] * k)
    ang = (lam_im * dt)[..., None, :] * k
    pr, pi = mag * jnp.cos(ang), mag * jnp.sin(ang)
    ar, ai = pr[..., 1, :], pi[..., 1, :]
    den = lam_re * lam_re + lam_im * lam_im
    qr = (((ar - 1) * lam_re + ai * lam_im) / den)[..., None, :]
    qi = ((ai * lam_re - (ar - 1) * lam_im) / den)[..., None, :]
    b_re = jnp.swapaxes(f('s5_b_re'), -1, -2)
    b_im = jnp.swapaxes(f('s5_b_im'), -1, -2)
    bbr = qr * b_re - qi * b_im
    bbi = qr * b_im + qi * b_re
    g, hg, ps = S5_GROUPS, S5_GROUP, S5_STATE
    w = t * hg
    gs = S5_LANE_GROUPS
    small = lambda rows: pl.BlockSpec((1, 2, gs, rows, ps), lambda i, j: (i, 0, j, 0, 0))
    out = lambda cols: pl.BlockSpec((1, gs, w, cols), lambda i, j: (i, j, 0, 0))
    return pl.pallas_call(
        _s5_param_kernel,
        grid=(depth, g // gs),
        in_specs=[small(hg)] * 4 + [small(t + 1)] * 2,
        out_specs=[out(w), out(2 * w), out(w), pl.BlockSpec((1, gs, 8, 2 * ps), lambda i, j: (i, j, 0, 0))],
        out_shape=[jax.ShapeDtypeStruct((depth, g, w, w), BF16), jax.ShapeDtypeStruct((depth, g, w, 2 * w), BF16),
                   jax.ShapeDtypeStruct((depth, g, w, w), BF16), jax.ShapeDtypeStruct((depth, g, 8, 2 * ps), F32)],
        compiler_params=_cparams(2),
    )(f('s5_c_re'), f('s5_c_im'), bbr, bbi, pr, pi)


def _s5_kernel(u_ref, tz_ref, bc_ref, cc_ref, coef_ref, d_ref, y_ref, uy_ref, loc_ref, sp_ref, slab_ref,
               *, n_ctx, n_batch):
    ph, b = pl.program_id(1), pl.program_id(2)
    t, hg, ng = S5_CHUNK, S5_GROUP, S5_LANE_GROUPS
    ncc = n_ctx // t
    ncl = (u_ref.shape[1] - n_ctx) // t
    nc = ncc + ncl
    pitch = nc + S5_PITCH_PAD

    def to_chunk_rows(slabs):
        tr = [s.T for s in slabs]
        return [jnp.concatenate([x[g * hg:(g + 1) * hg] for x in tr], axis=0).T for g in range(ng)]

    def to_token_slabs(rows):
        tr = [r.T for r in rows]
        return [jnp.concatenate([x[tau * hg:(tau + 1) * hg] for x in tr], axis=0).T for tau in range(t)]

    base = pl.multiple_of(b * pitch, 8)
    cbase = pl.multiple_of(b * ncc, 8)

    @pl.when(ph == 0)
    def _():
        rows = to_chunk_rows([u_ref[0, pl.ds(n_ctx + tau, ncl, stride=t), :] for tau in range(t)])
        for g in range(ng):
            uy_ref[g, pl.ds(base + ncc, ncl), :] = rows[g]
            uy_ref[g, pl.ds(base + nc, S5_PITCH_PAD), :] = jnp.zeros((S5_PITCH_PAD, t * hg), F32)
        for tau in range(t):
            slab_ref[tau, pl.ds(cbase, ncc), :] = u_ref[0, pl.ds(tau, ncc, stride=t), :]

    @pl.when((ph == 1) & (b == 0))
    def _():
        rows = to_chunk_rows([slab_ref[tau] for tau in range(t)])
        for g in range(ng):
            for s in range(n_batch):
                uy_ref[g, s * pitch:s * pitch + ncc, :] = rows[g][s * ncc:(s + 1) * ncc]
        for part in range(ng // S5_SCAN_GROUPS):
            gs = [part * S5_SCAN_GROUPS + gl for gl in range(S5_SCAN_GROUPS)]
            for gl, g in enumerate(gs):
                ub = uy_ref[g].astype(BF16)
                loc = _bdot(ub, bc_ref[0, g])
                for q in range(4):
                    loc_ref[gl, q] = loc[:, q * 128:(q + 1) * 128]
                uy_ref[g] = _bdot(ub, tz_ref[0, g])
                for s in range(n_batch):
                    for d in range(2):
                        sp_ref[gl, d, s * pitch + nc:(s + 1) * pitch, :] = jnp.zeros((S5_PITCH_PAD, 128), F32)

            def coef(g, r):
                return jnp.broadcast_to(coef_ref[0, g, r:r + 1, :], (n_batch, 128))

            def step(i, carry):
                cb = jnp.where(i < ncc, ncc - 1 - i, nc + ncc - 1 - i)
                fwd = pl.ds(i, n_batch, stride=pitch)
                bwd = pl.ds(cb, n_batch, stride=pitch)
                out = []
                for gl, g in enumerate(gs):
                    v0f, v1f, v0b, v1b = carry[gl]
                    sp_ref[gl, 0, fwd, :] = v0f
                    sp_ref[gl, 1, bwd, :] = v0b
                    n0f = coef(g, 0) * v0f + coef(g, 1) * v1f + loc_ref[gl, 0, fwd, :]
                    n1f = coef(g, 0) * v1f + coef(g, 2) * v0f + loc_ref[gl, 1, fwd, :]
                    n0b = coef(g, 3) * v0b + coef(g, 4) * v1b + loc_ref[gl, 2, bwd, :]
                    n1b = coef(g, 3) * v1b + coef(g, 5) * v0b + loc_ref[gl, 3, bwd, :]
                    out.append((n0f, n1f, n0b, n1b))
                return tuple(out)

            z = jnp.zeros((n_batch, 128), F32)
            lax.fori_loop(0, nc, step, tuple((z, z, z, z) for _ in gs))
            for gl, g in enumerate(gs):
                sp = jnp.concatenate([sp_ref[gl, 0], sp_ref[gl, 1]], axis=1).astype(BF16)
                uy_ref[g] = uy_ref[g] + _dot_nt(sp, cc_ref[0, g])
        rows = [jnp.concatenate([uy_ref[g, s * pitch:s * pitch + ncc, :] for s in range(n_batch)], axis=0)
                for g in range(ng)]
        for tau, slab in enumerate(to_token_slabs(rows)):
            slab_ref[tau] = slab

    @pl.when(ph == 1)
    def _():
        slabs = to_token_slabs([uy_ref[g, pl.ds(base + ncc, ncl), :] for g in range(ng)])
        for tau in range(t):
            y_ref[0, pl.ds(n_ctx + tau, ncl, stride=t), :] = slabs[tau]
            y_ref[0, pl.ds(tau, ncc, stride=t), :] = slab_ref[tau, pl.ds(cbase, ncc), :]
        y_ref[0] = y_ref[0] + d_ref[0] * u_ref[0]


def _s5_call(layer, u, tz, bc, cc, coef, s5_d, n_ctx):
    b, n, width = u.shape
    t, ng = S5_CHUNK, S5_LANE_GROUPS
    rows = b * (n // t + S5_PITCH_PAD)
    assert (n - n_ctx) // t == 128 and b * (n_ctx // t) == 128
    wspec = lambda a: pl.BlockSpec((1, ng) + a.shape[2:], lambda g, ph, i: (layer, g, 0, 0))
    return pl.pallas_call(
        functools.partial(_s5_kernel, n_ctx=n_ctx, n_batch=b),
        grid=(width // 128, 2, b),
        in_specs=[pl.BlockSpec((1, n, 128), lambda g, ph, i: (i, 0, g)),
                  wspec(tz), wspec(bc), wspec(cc), wspec(coef),
                  pl.BlockSpec((1, 1, 128), lambda g, ph, i: (layer, 0, g))],
        out_specs=pl.BlockSpec((1, n, 128), lambda g, ph, i: (i * ph, 0, g)),
        out_shape=jax.ShapeDtypeStruct((b, n, width), F32),
        scratch_shapes=[pltpu.VMEM((ng, rows, t * S5_GROUP), F32),
                        pltpu.VMEM((S5_SCAN_GROUPS, 4, rows, 128), F32),
                        pltpu.VMEM((S5_SCAN_GROUPS, 2, rows, 128), F32),
                        pltpu.VMEM((t, 128, 128), F32)],
        compiler_params=_cparams(3),
    )(u, tz, bc, cc, coef, s5_d.astype(F32).reshape(s5_d.shape[0], 1, width))


def _lane_chunks(xs):
    return [x[:, i * 128:(i + 1) * 128] for x in xs for i in range(x.shape[1] // 128)]


def _row_max(scores, floor=None):
    mm = functools.reduce(jnp.maximum, _lane_chunks(scores))
    if floor is not None:
        mm = jnp.maximum(mm, floor)
    return jnp.max(mm, axis=-1, keepdims=True)


def _softmax_av(scores, values, sink=None):
    m = _row_max(scores, sink)
    ps = [jnp.exp2(s - m) for s in scores]
    ll = functools.reduce(jnp.add, _lane_chunks(ps))
    if sink is not None:
        lane = lax.broadcasted_iota(jnp.int32, sink.shape, 1)
        ll = ll + jnp.where(lane == 0, jnp.exp2(sink - m), 0.0)
    l = jnp.sum(ll, axis=-1, keepdims=True)
    o = functools.reduce(jnp.add, [_bdot(p.astype(BF16), v) for p, v in zip(ps, values)])
    return o / l


def _mla_attn_paths(qp_ref, qr_ref, k_ref, v_ref, o_ref, *, n_ctx):
    heads = qp_ref.shape[-1] // MLA_PAD
    group = 256 // MLA_V
    lane = lax.broadcasted_iota(jnp.int32, (1, 256), 1)

    def run(latent):
        all_scores = []
        for h in range(heads):
            sl = slice(h * MLA_PAD, (h + 1) * MLA_PAD)
            scores = [_dot_nt(qp_ref[0, :, sl], k_ref[0, :n_ctx, sl])]
            if latent:
                scores.append(_dot_nt(qr_ref[0, :, sl], k_ref[0, n_ctx:, sl]))
            all_scores.append(scores)
        probs = []
        for scores in all_scores:
            m = _row_max(scores)
            ps = [jnp.exp2(s - m) for s in scores]
            l = jnp.sum(functools.reduce(jnp.add, _lane_chunks(ps)), axis=-1, keepdims=True)
            probs.append(([p.astype(BF16) for p in ps], l))
        acc = [None] * (heads // group)
        for h, (ps, l) in enumerate(probs):
            blk, hh = divmod(h, group)
            cols = slice(blk * 256, (blk + 1) * 256)
            own = (lane >= hh * MLA_V) & (lane < (hh + 1) * MLA_V)
            zero = jnp.zeros((), BF16)
            values = [jnp.where(own, v_ref[0, :n_ctx, cols], zero)]
            if latent:
                values.append(jnp.where(own, v_ref[0, n_ctx:, cols], zero))
            o = functools.reduce(jnp.add, [_bdot(p, v) for p, v in zip(ps, values)]) / l
            acc[blk] = o if acc[blk] is None else acc[blk] + o
        for blk, o in enumerate(acc):
            o_ref[0, :, blk * 256:(blk + 1) * 256] = o.astype(o_ref.dtype)

    return run


def _attn_kernel(qp_ref, qr_ref, k_ref, v_ref, sink_ref, wq_ref, wk_ref, wv_ref, cos_ref, sa_ref, sb_ref,
                 mla_o_ref, win_o_ref, *, n_ctx, n_blocks):
    t = pl.program_id(1)
    n_ctx_tiles = n_ctx // qp_ref.shape[1]
    mla = _mla_attn_paths(qp_ref, qr_ref, k_ref, v_ref, mla_o_ref, n_ctx=n_ctx)
    win_ctx, win_lat = _win_paths(sink_ref, wq_ref, wk_ref, wv_ref, cos_ref, sa_ref, sb_ref, win_o_ref,
                                  n_ctx_blocks=n_ctx // BLOCK, n_blocks=n_blocks)

    @pl.when(t < n_ctx_tiles)
    def _():
        mla(False)
        win_ctx()

    @pl.when(t >= n_ctx_tiles)
    def _():
        mla(True)
        win_lat()


def _attn_call(qp, qr, k, v, sink, wq, wk, wv, tabs, n_ctx):
    b, n, _ = qp.shape
    tq = TOKEN_TILE
    assert tq == WIN_BLOCKS_PER_STEP * BLOCK and MLA_HEADS_PER_STEP == MLA_HEADS and n_ctx % tq == 0
    tok = lambda w: pl.BlockSpec((1, tq, w), lambda i, t: (i, t, 0))
    whole = lambda w: pl.BlockSpec((1, n, w), lambda i, t: (i, 0, 0))
    tab = pl.BlockSpec((n, 128), lambda i, t: (0, 0))
    return pl.pallas_call(
        functools.partial(_attn_kernel, n_ctx=n_ctx, n_blocks=(n - n_ctx) // BLOCK),
        grid=(b, n // tq),
        in_specs=[tok(MLA_HEADS * MLA_PAD), tok(MLA_HEADS * MLA_PAD), whole(MLA_HEADS * MLA_PAD),
                  whole(MLA_HEADS * MLA_V), pl.BlockSpec(memory_space=pltpu.SMEM),
                  tok(WIN_Q_HEADS * WIN_HEAD_DIM), whole(WIN_KV_HEADS * WIN_HEAD_DIM),
                  whole(WIN_KV_HEADS * WIN_HEAD_DIM), tab, tab, tab],
        out_specs=[tok(MLA_HEADS * MLA_V), tok(WIN_Q_HEADS * WIN_HEAD_DIM)],
        out_shape=[jax.ShapeDtypeStruct((b, n, MLA_HEADS * MLA_V), BF16),
                   jax.ShapeDtypeStruct((b, n, WIN_Q_HEADS * WIN_HEAD_DIM), BF16)],
        compiler_params=_cparams(2),
    )(qp, qr, k, v, sink, wq, wk, wv, *tabs)


def _win_paths(sink_ref, q_ref, k_ref, v_ref, cos_ref, sa_ref, sb_ref, o_ref, *, n_ctx_blocks, n_blocks):
    j = pl.program_id(1)
    hd = WIN_HEAD_DIM
    half = hd // 2
    lane = lax.broadcasted_iota(jnp.int32, (1, 128), 1)
    lo = jnp.where(lane < hd, 1.0, 0.0)
    hi = 1.0 - lo
    upper_rows = lax.broadcasted_iota(jnp.int32, (2 * BLOCK, 128), 0) < BLOCK

    def lane_halves(x):
        xr = pltpu.roll(x, hd, 1)
        return {(0, 0): (x * lo).astype(BF16), (0, 1): (xr * hi).astype(BF16),
                (1, 0): (xr * lo).astype(BF16), (1, 1): (x * hi).astype(BF16)}

    def attend(out_rows, queries, keys, values, masks):
        all_scores = {}
        for kh in range(WIN_KV_HEADS):
            stacked = [jnp.concatenate([qs[:, (2 * kh) * 128:(2 * kh + 1) * 128],
                                        qs[:, (2 * kh + 1) * 128:(2 * kh + 2) * 128]], axis=0).astype(BF16)
                       for qs in queries]
            for par in range(2):
                scores = []
                for qst, ks, msk in zip(stacked, keys, masks):
                    s = _dot_nt(qst, ks[(kh, par)])
                    scores.append(s if msk is None else jnp.where(msk, s, NEG_INF))
                all_scores[(kh, par)] = scores
        for kh in range(WIN_KV_HEADS):
            acc = None
            for par in range(2):
                sink = jnp.where(upper_rows, sink_ref[4 * kh + par], sink_ref[4 * kh + 2 + par]) * LOG2E
                o = _softmax_av(all_scores[(kh, par)], [vs[(kh, par)] for vs in values], sink)
                acc = o if acc is None else acc + o
            o_ref[0, out_rows, (2 * kh) * 128:(2 * kh + 1) * 128] = acc[:BLOCK].astype(o_ref.dtype)
            o_ref[0, out_rows, (2 * kh + 1) * 128:(2 * kh + 2) * 128] = acc[BLOCK:].astype(o_ref.dtype)

    n_ctx = n_ctx_blocks * BLOCK
    kctx = lane_halves(k_ref[0, :n_ctx, :])
    vctx = lane_halves(v_ref[0, :n_ctx, :])
    subs = [slice(s * BLOCK, (s + 1) * BLOCK) for s in range(q_ref.shape[1] // BLOCK)]

    def ctx_path():
        for rows in subs:
            attend(rows, [q_ref[0, rows, :] * (WIN_SCALE * LOG2E)], [kctx], [vctx], [None])

    def lat_path():
        for s, rows in enumerate(subs):
            lat_block(j * len(subs) + s - n_ctx_blocks, rows)

    def lat_block(blk, out_rows):
        q = q_ref[0, out_rows, :] * (WIN_SCALE * LOG2E)
        band = [pl.ds(pl.multiple_of((n_ctx_blocks + jnp.clip(blk + d, 0, n_blocks - 1)) * BLOCK, BLOCK), BLOCK)
                for d in (-1, 0, 1)]
        rope = lambda x, rows: _rope_lanes(x, cos_ref[rows, :], sa_ref[rows, :], sb_ref[rows, :], half)
        q_rot = jnp.concatenate([rope(q[:, c * 128:(c + 1) * 128], band[1]) for c in range(q.shape[1] // 128)],
                                axis=-1)
        kband = jnp.concatenate([rope(k_ref[0, rows, :], rows) for rows in band], axis=0)
        vband = jnp.concatenate([v_ref[0, rows, :] for rows in band], axis=0)
        r = lax.broadcasted_iota(jnp.int32, (2 * BLOCK, 3 * BLOCK), 0) % BLOCK
        c = lax.broadcasted_iota(jnp.int32, (2 * BLOCK, 3 * BLOCK), 1)
        first = jnp.where(blk > 0, 0, BLOCK)
        last = jnp.where(blk < n_blocks - 1, 3 * BLOCK, 2 * BLOCK)
        valid = (jnp.abs(c - BLOCK - r) <= WINDOW) & (c >= first) & (c < last)
        attend(out_rows, [q_rot, q], [lane_halves(kband), kctx], [lane_halves(vband), vctx], [valid, None])

    return ctx_path, lat_path


def _merge_kernel(x_ref, s5_ref, mla_ref, win_ref, gate_ref, mod_ref, wglu_ref, bglu_ref, wbr_ref, wout_ref,
                  g1_ref, b1_ref, wr_ref, x1_ref, h2_ref, lg_ref, *, alpha):
    d = x_ref.shape[-1]
    proj = {1: _bdot(mla_ref[0], wbr_ref[1]), 2: _bdot(win_ref[0], wbr_ref[2])}
    g = jax.nn.gelu(s5_ref[0])
    s5o = g * _sigmoid(_bdot(g.astype(BF16), wglu_ref[...]) + bglu_ref[...])
    proj[0] = _bdot(s5o.astype(BF16), wbr_ref[0])
    mix = None
    for kk in (1, 2, 0):
        term = _sigmoid(gate_ref[0, :, kk * d:(kk + 1) * d].astype(F32)) * proj[kk]
        mix = term if mix is None else mix + term
    y = _bdot(mix.astype(BF16), wout_ref[...])
    mod = lambda r: mod_ref[0, 0, r:r + 1, :]
    x1 = _layer_norm(alpha * x_ref[0] + mod(2) * y) * g1_ref[...] + b1_ref[...]
    x1_ref[0] = x1
    h2 = (_layer_norm(x1) * (1.0 + mod(4)) + mod(3)).astype(BF16)
    h2_ref[0] = h2
    lg_ref[0] = _dot_nt(wr_ref[...], h2)


def _merge_call(xall, s5y, mla_o, win_o, gates, mod, wglu, bglu, wbr, wout, g1, b1, wr_t, n_ctx_tiles, alpha):
    b, n, d = xall.shape
    tm = TOKEN_TILE
    tok = lambda w: pl.BlockSpec((1, tm, w), lambda i, t: (i, t, 0))
    full = lambda a: pl.BlockSpec(a.shape, lambda i, t: (0,) * a.ndim)
    return pl.pallas_call(
        functools.partial(_merge_kernel, alpha=alpha),
        grid=(b, n // tm),
        in_specs=[tok(d), tok(BRANCH_WIDTH), tok(BRANCH_WIDTH), tok(BRANCH_WIDTH), tok(N_BRANCH * d),
                  pl.BlockSpec((1, 1, 6, d), lambda i, t: (i, jnp.where(t < n_ctx_tiles, 0, 1), 0, 0)),
                  full(wglu), full(bglu), full(wbr), full(wout), full(g1), full(b1), full(wr_t)],
        out_specs=[tok(d), tok(d), pl.BlockSpec((1, N_EXPERTS, tm), lambda i, t: (i, 0, t))],
        out_shape=[jax.ShapeDtypeStruct((b, n, d), F32), jax.ShapeDtypeStruct((b, n, d), BF16),
                   jax.ShapeDtypeStruct((b, N_EXPERTS, n), F32)],
        compiler_params=_cparams(2),
    )(xall, s5y, mla_o, win_o, gates, mod, wglu, bglu, wbr, wout, g1, b1, wr_t)


def _excl_cumsum_lanes(m):
    rows, n = m.shape
    r = lax.broadcasted_iota(jnp.int32, (128, 128), 0)
    c = lax.broadcasted_iota(jnp.int32, (128, 128), 1)
    tri = jnp.where(r < c, 1.0, 0.0).astype(BF16)
    off = jnp.zeros((rows, 1), F32)
    outs, offs = [], []
    for jb in range(n // 128):
        blk = m[:, jb * 128:(jb + 1) * 128]
        offs.append(off)
        outs.append(_bdot(blk.astype(BF16), tri) + off)
        off = off + jnp.sum(blk, axis=1, keepdims=True)
    return jnp.concatenate(outs, axis=1), offs + [off]


def _topk_slots(affs, caps):
    bits = [pltpu.bitcast(aff, jnp.int32) for aff in affs]

    def body(i, thrs):
        out = []
        for b, cap, thr in zip(bits, caps, thrs):
            cand = thr | (jnp.int32(1) << (30 - i))
            cnt = jnp.sum(jnp.where(b >= cand, 1.0, 0.0), axis=1, keepdims=True)
            out.append(jnp.where(cnt >= cap, cand, thr))
        return tuple(out)

    zero = jnp.zeros((affs[0].shape[0], 1), jnp.int32)
    thrs = lax.fori_loop(0, 31, body, tuple(zero for _ in affs))
    results = []
    for b, cap, thr in zip(bits, caps, thrs):
        gt = jnp.where(b > thr, 1.0, 0.0)
        eq = jnp.where(b == thr, 1.0, 0.0)
        need = cap - jnp.sum(gt, axis=1, keepdims=True)
        sel = gt + eq * jnp.where(_excl_cumsum_lanes(eq)[0] < need, 1.0, 0.0)
        rank, offs = _excl_cumsum_lanes(sel)
        results.append((jnp.where(sel > 0.5, rank, -1.0).astype(jnp.int32), offs[::MOE_TILE // 128]))
    return results


def _route_kernel(lg_ref, slot_ref, aff_ref, bnd_ref, *, n_ctx, cap_ctx, cap_lat):
    lg = lg_ref[0]
    m = jnp.max(lg, axis=0, keepdims=True)
    ex = jnp.exp(lg - m)
    aff = ex / jnp.sum(ex, axis=0, keepdims=True)
    aff_ref[0] = aff
    (slots_ctx, _), (slots, counts) = _topk_slots([aff[:, :n_ctx], aff[:, n_ctx:]], [cap_ctx, cap_lat])
    slot_ref[0, :, :n_ctx] = slots_ctx
    slot_ref[0, :, n_ctx:] = slots
    lane = lax.broadcasted_iota(jnp.int32, bnd_ref.shape[1:], 1)
    bnd = jnp.zeros(bnd_ref.shape[1:], F32)
    for k, cnt in enumerate(counts):
        bnd = jnp.where(lane == k, cnt, bnd)
    bnd_ref[0] = bnd.astype(jnp.int32)


def _route_call(logits_t, n_ctx, cap_ctx, cap_lat):
    b, e, n = logits_t.shape
    assert (n - n_ctx) % MOE_TILE == 0 and (n - n_ctx) // MOE_TILE < MOE_BOUNDS
    spec = pl.BlockSpec((1, e, n), lambda i: (i, 0, 0))
    return pl.pallas_call(
        functools.partial(_route_kernel, n_ctx=n_ctx, cap_ctx=cap_ctx, cap_lat=cap_lat),
        grid=(b,),
        in_specs=[spec],
        out_specs=[spec, spec, pl.BlockSpec((1, e, MOE_BOUNDS), lambda i: (i, 0, 0))],
        out_shape=[jax.ShapeDtypeStruct((b, e, n), jnp.int32), jax.ShapeDtypeStruct((b, e, n), F32),
                   jax.ShapeDtypeStruct((b, e, MOE_BOUNDS), jnp.int32)],
        compiler_params=_cparams(1),
    )(logits_t)


def _gather_kernel(bnd_ref, slot_ref, aff_ref, h_ref, xs_ref, gate_ref, xl_ref, gl_ref, *, n_ctx, cap_ctx):
    for j in range(xs_ref.shape[1]):
        _gather_expert(j, bnd_ref, slot_ref, aff_ref, h_ref, xs_ref, gate_ref, xl_ref.at[j], gl_ref.at[j],
                       n_ctx=n_ctx, cap_ctx=cap_ctx)


def _gather_expert(j, bnd_ref, slot_ref, aff_ref, h_ref, xs_ref, gate_ref, xl_ref, gl_ref, *, n_ctx, cap_ctx):
    cap_lat = xs_ref.shape[2] - cap_ctx
    slot = slot_ref[0, j]
    aff = aff_ref[0, j]
    n = h_ref.shape[1]
    ib, ie = pl.program_id(0), pl.program_id(1) * xs_ref.shape[1] + j
    xl_ref[...] = jnp.zeros(xl_ref.shape, F32)
    gl_ref[...] = jnp.zeros(gl_ref.shape, F32)
    tiles = range((n - n_ctx) // MOE_TILE)
    firsts = [(bnd_ref[ib, ie, kt] // 16) * 16 for kt in tiles]

    def window(kt, start):
        tok = slice(n_ctx + kt * MOE_TILE, n_ctx + (kt + 1) * MOE_TILE)
        rows = start + lax.broadcasted_iota(jnp.int32, (MOE_WINDOW, MOE_TILE), 0)
        hit = slot[:, tok] == rows
        picked = jnp.sum(jnp.where(hit, aff[:, tok], 0.0), axis=1, keepdims=True)
        return _bdot(jnp.where(hit, 1.0, 0.0).astype(BF16), h_ref[0, tok, :]), picked

    def add_window(kt, start, parts=None):
        start = pl.multiple_of(start, 16)
        rows, picked = parts if parts is not None else window(kt, start)
        xl_ref[pl.ds(start, MOE_WINDOW), :] += rows
        gl_ref[pl.ds(start, MOE_WINDOW), :] += jnp.broadcast_to(picked, (MOE_WINDOW, 128))

    first_windows = [window(kt, firsts[kt]) for kt in tiles]
    for kt in tiles:
        add_window(kt, firsts[kt], first_windows[kt])
    for kt in tiles:
        def more(w, carry, kt=kt):
            add_window(kt, firsts[kt] + (w + 1) * MOE_WINDOW)
            return carry

        n_win = (bnd_ref[ib, ie, kt + 1] - firsts[kt] + MOE_WINDOW - 1) // MOE_WINDOW
        lax.fori_loop(0, jnp.maximum(n_win - 1, 0), more, 0)
    iota = lax.broadcasted_iota(jnp.int32, (cap_ctx, n_ctx), 0)
    hit = slot[:, :n_ctx] == iota
    xc = _bdot(jnp.where(hit, 1.0, 0.0).astype(BF16), h_ref[0, :n_ctx, :])
    gc = jnp.sum(jnp.where(hit, aff[:, :n_ctx], 0.0), axis=1, keepdims=True)
    xs_ref[0, j, :cap_lat] = xl_ref[:cap_lat].astype(BF16)
    xs_ref[0, j, cap_lat:] = xc.astype(BF16)
    gate_ref[0, j, :cap_lat] = gl_ref[:cap_lat]
    gate_ref[0, j, cap_lat:] = jnp.broadcast_to(gc, (cap_ctx, 128))


def _gather_call(bounds, slot, aff, h2, n_ctx, cap_ctx, cap_lat):
    b, e, n = slot.shape
    d = h2.shape[-1]
    cap = cap_lat + cap_ctx
    ne = GATHER_EXPERTS
    row = pl.BlockSpec((1, ne, 1, n), lambda ib, ie: (ib, ie, 0, 0))
    return pl.pallas_call(
        functools.partial(_gather_kernel, n_ctx=n_ctx, cap_ctx=cap_ctx),
        grid=(b, e // ne),
        in_specs=[pl.BlockSpec(memory_space=pltpu.SMEM), row, row, pl.BlockSpec((1, n, d), lambda ib, ie: (ib, 0, 0))],
        out_specs=[pl.BlockSpec((1, ne, cap, d), lambda ib, ie: (ib, ie, 0, 0)),
                   pl.BlockSpec((1, ne, cap, 128), lambda ib, ie: (ib, ie, 0, 0))],
        out_shape=[jax.ShapeDtypeStruct((b, e, cap, d), BF16), jax.ShapeDtypeStruct((b, e, cap, 128), F32)],
        scratch_shapes=[pltpu.VMEM((ne, cap_lat + MOE_WINDOW, d), F32),
                        pltpu.VMEM((ne, cap_lat + MOE_WINDOW, 128), F32)],
        compiler_params=_cparams(2),
    )(bounds, slot.reshape(b, e, 1, n), aff.reshape(b, e, 1, n), h2)


def _ffn_kernel(xs_ref, gate_ref, wg_ref, wu_ref, wd_ref, yl_ref, yc_ref, wg_s, wu_s, wd_s):
    ns, _, cap, d = xs_ref.shape
    cap_ctx = yc_ref.shape[2]
    cap_lat = cap - cap_ctx

    @pl.when(pl.program_id(1) == 0)
    def _():
        wg_s[...] = wg_ref[0, 0].astype(BF16)
        wu_s[...] = wu_ref[0, 0].astype(BF16)
        wd_s[...] = wd_ref[0, 0].astype(BF16)

    au = [(_bdot(xs_ref[s, 0], wg_s[...]), _bdot(xs_ref[s, 0], wu_s[...])) for s in range(ns)]
    for s, (a, u) in enumerate(au):
        hm = (a * _sigmoid(a) * u).astype(BF16)
        y = _bdot(hm, wd_s[...]) * gate_ref[s, 0, :, 0:1]
        yl_ref[s, 0, :cap_lat] = y[:cap_lat].astype(yl_ref.dtype)
        yl_ref[s, 0, cap_lat:] = jnp.zeros((MOE_WINDOW, d), yl_ref.dtype)
        yc_ref[s, 0] = y[cap_lat:].astype(yc_ref.dtype)


def _ffn_call(layer, xs, gate, w_gate, w_up, w_down, cap_ctx):
    b, e, cap, d = xs.shape
    f = w_gate.shape[-1]
    ns = FFN_SAMPLES
    rows_lat = cap - cap_ctx + MOE_WINDOW
    tok = lambda rows, w: pl.BlockSpec((ns, 1, rows, w), lambda ie, j: (j, ie, 0, 0))
    return pl.pallas_call(
        _ffn_kernel,
        grid=(e, b // ns),
        in_specs=[tok(cap, d), tok(cap, 128),
                  pl.BlockSpec((1, 1, d, f), lambda ie, j: (layer, ie, 0, 0)),
                  pl.BlockSpec((1, 1, d, f), lambda ie, j: (layer, ie, 0, 0)),
                  pl.BlockSpec((1, 1, f, d), lambda ie, j: (layer, ie, 0, 0))],
        out_specs=[tok(rows_lat, d), tok(cap_ctx, d)],
        out_shape=[jax.ShapeDtypeStruct((b, e, rows_lat, d), BF16), jax.ShapeDtypeStruct((b, e, cap_ctx, d), BF16)],
        scratch_shapes=[pltpu.VMEM((d, f), BF16), pltpu.VMEM((d, f), BF16), pltpu.VMEM((f, d), BF16)],
        compiler_params=_cparams(2),
    )(xs, gate, w_gate, w_up, w_down)


def _combine_kernel(bnd_ref, slot_ref, yl_ref, yc_ref, x1_ref, mod_ref, g2_ref, b2_ref, o_ref, fl_ref,
                    *, n_ctx_tiles, alpha):
    ib, t = pl.program_id(0), pl.program_id(1)
    tm = x1_ref.shape[1]
    slot = slot_ref[0]
    win = MOE_WINDOW

    def finish(fl):
        x1 = x1_ref[0]
        o_ref[0] = _layer_norm(alpha * x1 + mod_ref[0, 0, 5:6, :] * fl) * g2_ref[...] + b2_ref[...]

    def onehot(e, first, width):
        iota = lax.broadcasted_iota(jnp.int32, (tm, width), 1)
        return jnp.where(slot[:, e:e + 1] - first == iota, 1.0, 0.0).astype(BF16)

    def ctx_path():
        cap = yc_ref.shape[2]
        fl = None
        for e in range(N_EXPERTS):
            term = _bdot(onehot(e, 0, cap), yc_ref[0, e])
            fl = term if fl is None else fl + term
        finish(fl)

    def lat_path():
        kt = t - n_ctx_tiles
        firsts = [pl.multiple_of((bnd_ref[ib, e, kt] // 16) * 16, 16) for e in range(N_EXPERTS)]
        lane = lax.broadcasted_iota(jnp.int32, (tm, 2 * win), 1)
        pieces, ywins = [], []
        for e in range(0, N_EXPERTS, 2):
            rel = jnp.where(lane < win, slot[:, e:e + 1] - firsts[e], slot[:, e + 1:e + 2] - firsts[e + 1] + win)
            pieces.append(jnp.where(rel == lane, 1.0, 0.0).astype(BF16))
            ywins += [yl_ref[0, e, pl.ds(firsts[e], win), :], yl_ref[0, e + 1, pl.ds(firsts[e + 1], win), :]]
        fl_ref[...] = _bdot(jnp.concatenate(pieces, axis=1), jnp.concatenate(ywins, axis=0))
        for e in range(N_EXPERTS):
            def window(w, carry, e=e):
                first = pl.multiple_of(firsts[e] + (w + 1) * win, 16)
                fl_ref[...] += _bdot(onehot(e, first, win), yl_ref[0, e, pl.ds(first, win), :])
                return carry

            n_win = (bnd_ref[ib, e, kt + 1] - firsts[e] + win - 1) // win
            lax.fori_loop(0, jnp.maximum(n_win - 1, 0), window, 0)
        finish(fl_ref[...])

    pl.when(t < n_ctx_tiles)(ctx_path)
    pl.when(t >= n_ctx_tiles)(lat_path)


def _combine_call(bounds, slot_t, yl, yc, x1, mod, g2, b2, n_ctx_tiles, alpha, latent_only):
    b, n, d = x1.shape
    tm = MOE_TILE
    e = N_EXPERTS
    full = lambda a: pl.BlockSpec(a.shape, lambda i, t: (0,) * a.ndim)
    skip = n_ctx_tiles if latent_only else 0
    return pl.pallas_call(
        functools.partial(_combine_kernel, n_ctx_tiles=n_ctx_tiles, alpha=alpha),
        grid=(b, n // tm),
        scratch_shapes=[pltpu.VMEM((tm, d), F32)],
        in_specs=[pl.BlockSpec(memory_space=pltpu.SMEM),
                  pl.BlockSpec((1, tm, e), lambda i, t: (i, t, 0)),
                  pl.BlockSpec((1,) + yl.shape[1:], lambda i, t: (i, 0, 0, 0)),
                  pl.BlockSpec((1,) + yc.shape[1:], lambda i, t: (i, 0, 0, 0)),
                  pl.BlockSpec((1, tm, d), lambda i, t: (i, t, 0)),
                  pl.BlockSpec((1, 1, 6, d), lambda i, t: (i, jnp.where(t < n_ctx_tiles, 0, 1), 0, 0)),
                  full(g2), full(b2)],
        out_specs=pl.BlockSpec((1, tm, d), lambda i, t: (i, jnp.maximum(t - skip, 0), 0)),
        out_shape=jax.ShapeDtypeStruct((b, n - skip * tm, d), F32),
        compiler_params=_cparams(2),
    )(bounds, slot_t, yl, yc, x1, mod, g2, b2)


def _rope_tables(n_ctx, seq, head_dim, lane_offset):
    half = head_dim // 2
    nf = head_dim // 4
    t = jnp.arange(seq, dtype=F32)
    row = jnp.floor(t / GRID_W)
    col = t - row * GRID_W
    freqs = ROPE_BASE ** (-jnp.arange(nf, dtype=F32) / nf)
    ang = jnp.concatenate([row[:, None] * freqs, col[:, None] * freqs], axis=-1)
    cos, sin = jnp.cos(ang), jnp.sin(ang)
    zeros = jnp.zeros_like(sin)
    n_heads = (128 - lane_offset) // head_dim if lane_offset == 0 else 1
    c = jnp.concatenate([jnp.ones((seq, lane_offset), F32)] + [cos, cos] * n_heads, axis=-1)
    sa = jnp.concatenate([jnp.zeros((seq, lane_offset), F32)] + [-sin, zeros] * n_heads, axis=-1)
    sb = jnp.concatenate([jnp.zeros((seq, lane_offset), F32)] + [zeros, sin] * n_heads, axis=-1)
    pad = 128 - c.shape[1]
    c = jnp.pad(c, ((n_ctx, 0), (0, pad)), constant_values=1.0)
    sa = jnp.pad(sa, ((n_ctx, 0), (0, pad)))
    sb = jnp.pad(sb, ((n_ctx, 0), (0, pad)))
    return c, sa, sb


def _in_weights(w_in):
    lo = S5_WIDTH + MLA_Q_RANK + MLA_KV_RANK
    hi = lo + MLA_ROPE
    zeros = lambda n: jnp.zeros(w_in.shape[:2] + (n,), w_in.dtype)
    return jnp.concatenate([w_in[..., :lo], zeros(MLA_NOPE), w_in[..., lo:hi],
                            zeros(MLA_PAD - MLA_NOPE - MLA_ROPE), w_in[..., hi:]], axis=-1).astype(BF16)


def _layer_weights(i, p):
    dq = MLA_NOPE + MLA_ROPE
    wq = p['mla_w_uq'][i].reshape(MLA_Q_RANK, MLA_HEADS, dq)
    wq = jnp.pad(wq, ((0, 0), (0, 0), (0, MLA_PAD - dq))).reshape(MLA_Q_RANK, MLA_HEADS * MLA_PAD)
    wkv = p['mla_w_ukv'][i].reshape(MLA_KV_RANK, MLA_HEADS, MLA_NOPE + MLA_V)
    wk = jnp.pad(wkv[:, :, :MLA_NOPE], ((0, 0), (0, 0), (0, MLA_PAD - MLA_NOPE)))
    wk = wk.reshape(MLA_KV_RANK, MLA_HEADS * MLA_PAD)
    wv = wkv[:, :, MLA_NOPE:].reshape(MLA_KV_RANK, MLA_HEADS * MLA_V)
    row = lambda a: a[i].astype(F32).reshape(1, -1)
    return dict(
        wq=wq.astype(BF16), wk=wk.astype(BF16), wv=wv.astype(BF16),
        qg=row(p['mla_q_norm']), kvg=row(p['mla_kv_norm']),
        wglu=p['s5_w_glu'][i].astype(BF16), bglu=row(p['s5_b_glu']),
        sink=p['win_sink'][i].astype(F32),
        wbr=p['w_branch'][i].astype(BF16), wout=p['w_out'][i].astype(BF16),
        g1=row(p['ln1_g']), b1=row(p['ln1_b']), g2=row(p['ln2_g']), b2=row(p['ln2_b']),
        wr_t=p['w_router'][i].T.astype(BF16),
    )


def _forward(p):
    x, c, ctx, c_ctx = p['x'], p['c'], p['ctx'], p['c_ctx']
    b, seq, d = x.shape
    n_ctx = ctx.shape[1]
    depth = p['w_ada'].shape[0]
    assert b == 8 and seq % TOKEN_TILE == 0 and n_ctx % TOKEN_TILE == 0 and seq % GRID_W == 0
    alpha = float((2 * depth) ** 0.25)
    n_ctx_tiles = n_ctx // TOKEN_TILE
    cap_lat = CAPACITY_FACTOR * seq // N_EXPERTS
    cap_ctx = CAPACITY_FACTOR * n_ctx // N_EXPERTS

    cond = jnp.concatenate([c, c_ctx[None], jnp.zeros((16 - b - 1, d), F32)], axis=0)
    mods = _ada_call(cond, p['w_ada'], p['b_ada'])
    mods = mods.reshape(depth, 16, 6, d)
    tabs_mla = _rope_tables(n_ctx, seq, MLA_ROPE, MLA_NOPE)
    tabs_win = _rope_tables(n_ctx, seq, WIN_HEAD_DIM, 0)
    s5w = _s5_param_call(p)
    w_cat = _in_weights(p['w_in'])

    xall = jnp.concatenate([ctx, x], axis=1)
    for i in range(depth):
        w = _layer_weights(i, p)
        mod = jnp.stack([jnp.broadcast_to(mods[i, b], (b, 6, d)), mods[i, :b]], axis=1)
        u, wq, wk, wv, gates, qp, qr, kk, vv = _in_call(i, xall, mod, w_cat, w['qg'], w['kvg'], w['wq'], w['wk'],
                                                        w['wv'], tabs_mla, n_ctx)
        s5y = _s5_call(i, u, *s5w, p['s5_d'], n_ctx)
        mla_o, win_o = _attn_call(qp, qr, kk, vv, w['sink'], wq, wk, wv, tabs_win, n_ctx)
        x1, h2, logits_t = _merge_call(xall, s5y, mla_o, win_o, gates, mod, w['wglu'], w['bglu'], w['wbr'],
                                       w['wout'], w['g1'], w['b1'], w['wr_t'], n_ctx_tiles, alpha)
        slot, aff, bounds = _route_call(logits_t, n_ctx, cap_ctx, cap_lat)
        xs, gate = _gather_call(bounds, slot, aff, h2, n_ctx, cap_ctx, cap_lat)
        yl, yc = _ffn_call(i, xs, gate, p['w_gate'], p['w_up'], p['w_down'], cap_ctx)
        slot_t = jnp.swapaxes(slot, 1, 2)
        xall = _combine_call(bounds, slot_t, yl, yc, x1, mod, w['g2'], w['b2'], n_ctx_tiles, alpha,
                             latent_only=(i == depth - 1))
    return xall


def kernel(x, c, ctx, c_ctx, w_ada, b_ada, w_in, s5_lam_re, s5_lam_im, s5_log_dt, s5_b_re, s5_b_im, s5_c_re, s5_c_im, s5_d, s5_w_glu, s5_b_glu, mla_q_norm, mla_w_uq, mla_kv_norm, mla_w_ukv, win_sink, w_branch, w_out, ln1_g, ln1_b, ln2_g, ln2_b, w_router, w_gate, w_up, w_down):
    return _forward(dict(
        x=x, c=c, ctx=ctx, c_ctx=c_ctx, w_ada=w_ada, b_ada=b_ada, w_in=w_in, s5_lam_re=s5_lam_re,
        s5_lam_im=s5_lam_im, s5_log_dt=s5_log_dt, s5_b_re=s5_b_re, s5_b_im=s5_b_im, s5_c_re=s5_c_re,
        s5_c_im=s5_c_im, s5_d=s5_d, s5_w_glu=s5_w_glu, s5_b_glu=s5_b_glu, mla_q_norm=mla_q_norm,
        mla_w_uq=mla_w_uq, mla_kv_norm=mla_kv_norm, mla_w_ukv=mla_w_ukv, win_sink=win_sink, w_branch=w_branch,
        w_out=w_out, ln1_g=ln1_g, ln1_b=ln1_b, ln2_g=ln2_g, ln2_b=ln2_b, w_router=w_router, w_gate=w_gate,
        w_up=w_up, w_down=w_down))
```
